```python
import jax, jax.numpy as jnp
from jax import lax
import numpy as np

D_MODEL = 1024
BATCH = 8
SEQ = 16384
DEPTH = 4

D_MIX = D_MODEL
D_RNN = D_MIX // 2
RNN_HEADS = 8
RNN_BLOCK = D_RNN // RNN_HEADS
CONV_WIDTH = 4
CONV_PAD = (2, 1)
LRU_C = 8.0
GLA_HEADS = 4
GLA_DV_TOTAL = D_MIX - D_RNN
GLA_DK_TOTAL = GLA_DV_TOTAL // 2
GLA_DV = GLA_DV_TOTAL // GLA_HEADS
GLA_DK = GLA_DK_TOTAL // GLA_HEADS
GLA_RANK = 16
GLA_TAU = 16.0
GLA_CHUNK = 64
D_FF = ((8 * D_MODEL + 2) // 3 + 255) // 256 * 256
D_IN = 2 * D_RNN + 2 * GLA_DK_TOTAL + 2 * GLA_DV_TOTAL + 2 * GLA_RANK
RMS_EPS = 1e-6

kernel_name = 'hymba_rglru_gla_encoder'


def rms_norm(x, gain):
    xf = x.astype(jnp.float32)
    y = xf * lax.rsqrt(jnp.mean(xf * xf, axis=-1, keepdims=True) + RMS_EPS)
    return (y * gain.astype(jnp.float32)).astype(x.dtype)


def centred_depthwise_conv(x, w, b):
    y = lax.conv_general_dilated(
        x, w[:, None, :], window_strides=(1,), padding=[CONV_PAD],
        dimension_numbers=('NWC', 'WIO', 'NWC'), feature_group_count=x.shape[-1])
    return y + b


def linear_scan(a, u, reverse):
    def combine(c1, c2):
        a1, b1 = c1
        a2, b2 = c2
        return a1 * a2, a2 * b1 + b2
    _, h = lax.associative_scan(combine, (a, u), axis=1, reverse=reverse)
    return h


def rg_lru(x, w_a, b_a, w_x, b_x, lam, reverse):
    B, S, _ = x.shape
    xf = x.astype(jnp.float32)
    xh = xf.reshape(B, S, RNN_HEADS, RNN_BLOCK)
    r = jax.nn.sigmoid(jnp.einsum('bshi,hij->bshj', xh, w_a.astype(jnp.float32)).reshape(B, S, D_RNN) + b_a)
    i = jax.nn.sigmoid(jnp.einsum('bshi,hij->bshj', xh, w_x.astype(jnp.float32)).reshape(B, S, D_RNN) + b_x)
    log_a = -LRU_C * r * jax.nn.softplus(-lam.astype(jnp.float32))
    a = jnp.exp(log_a)
    u = xf * i * jnp.sqrt(-jnp.expm1(2.0 * log_a))
    return linear_scan(a, u, reverse)


def gla_chunked(q, k, v, log_a):
    B, S, H, DK = q.shape
    DV = v.shape[-1]
    N = S // GLA_CHUNK
    q = q.astype(jnp.float32).reshape(B, N, GLA_CHUNK, H, DK)
    k = k.astype(jnp.float32).reshape(B, N, GLA_CHUNK, H, DK)
    v = v.astype(jnp.float32).reshape(B, N, GLA_CHUNK, H, DV)
    b = jnp.cumsum(log_a.astype(jnp.float32).reshape(B, N, GLA_CHUNK, H, DK), axis=2)
    b_last = b[:, :, -1]
    q_e = q * jnp.exp(b)
    k_e = k * jnp.exp(-b)
    scores = jnp.einsum('bnihd,bnjhd->bnhij', q_e, k_e)
    mask = jnp.tril(jnp.ones((GLA_CHUNK, GLA_CHUNK), dtype=bool))
    scores = jnp.where(mask, scores, 0.0)
    o_intra = jnp.einsum('bnhij,bnjhv->bnihv', scores, v)
    k_dec = k * jnp.exp(b_last[:, :, None] - b)
    u = jnp.einsum('bnjhd,bnjhv->bnhdv', k_dec, v)
    decay = jnp.exp(b_last)

    def step(state, inp):
        d, u_n = inp
        return d[..., None] * state + u_n, state

    _, s_prev = lax.scan(step, jnp.zeros((B, H, DK, DV), jnp.float32),
                         (jnp.moveaxis(decay, 1, 0), jnp.moveaxis(u, 1, 0)))
    s_prev = jnp.moveaxis(s_prev, 0, 1)
    o_inter = jnp.einsum('bnihd,bnhdv->bnihv', q_e, s_prev)
    return (o_intra + o_inter).reshape(B, S, H, DV)


def hybrid_mixer(h, w_in, conv_w, conv_b, lru_w_a, lru_b_a, lru_w_x, lru_b_x, lru_lambda,
                 rnn_out_norm, gla_w_gate, gla_b_gate, gla_out_norm, w_out):
    B, S, _ = h.shape
    proj = jnp.einsum('bsd,de->bse', h, w_in)
    p1 = D_RNN
    p2 = p1 + D_RNN
    p3 = p2 + GLA_DK_TOTAL
    p4 = p3 + GLA_DK_TOTAL
    p5 = p4 + GLA_DV_TOTAL
    p6 = p5 + GLA_DV_TOTAL
    p7 = p6 + GLA_RANK
    x_r, gate_r, q, k, v, g, lr_f, lr_b = jnp.split(proj, [p1, p2, p3, p4, p5, p6, p7], axis=-1)

    xc = centred_depthwise_conv(x_r, conv_w, conv_b)
    h_f = rg_lru(xc, lru_w_a[0], lru_b_a[0], lru_w_x[0], lru_b_x[0], lru_lambda[0], False)
    h_b = rg_lru(xc, lru_w_a[1], lru_b_a[1], lru_w_x[1], lru_b_x[1], lru_lambda[1], True)
    y_rnn = (h_f + h_b) * jax.nn.gelu(gate_r.astype(jnp.float32))
    y_rnn = rms_norm(y_rnn, rnn_out_norm).astype(h.dtype)

    qh = q.reshape(B, S, GLA_HEADS, GLA_DK) * (GLA_DK ** -0.5)
    kh = k.reshape(B, S, GLA_HEADS, GLA_DK)
    vh = v.reshape(B, S, GLA_HEADS, GLA_DV)
    la_f = jax.nn.log_sigmoid((jnp.einsum('bsr,re->bse', lr_f, gla_w_gate[0]) + gla_b_gate[0]).astype(jnp.float32)) / GLA_TAU
    la_b = jax.nn.log_sigmoid((jnp.einsum('bsr,re->bse', lr_b, gla_w_gate[1]) + gla_b_gate[1]).astype(jnp.float32)) / GLA_TAU
    la_f = la_f.reshape(B, S, GLA_HEADS, GLA_DK)
    la_b = la_b.reshape(B, S, GLA_HEADS, GLA_DK)
    o_f = gla_chunked(qh, kh, vh, la_f)
    o_b = jnp.flip(gla_chunked(jnp.flip(qh, 1), jnp.flip(kh, 1), jnp.flip(vh, 1), jnp.flip(la_b, 1)), 1)
    o = rms_norm(o_f + o_b, gla_out_norm)
    o = o * jax.nn.silu(g.astype(jnp.float32).reshape(B, S, GLA_HEADS, GLA_DV))
    y_gla = o.reshape(B, S, GLA_DV_TOTAL).astype(h.dtype)

    y = jnp.concatenate([y_rnn, y_gla], axis=-1)
    return jnp.einsum('bse,ed->bsd', y, w_out)


def swiglu(h, w_gate, w_up, w_down):
    a = jnp.einsum('bsd,df->bsf', h, w_gate)
    u = jnp.einsum('bsd,df->bsf', h, w_up)
    return jnp.einsum('bsf,fd->bsd', jax.nn.silu(a) * u, w_down)


def _fwd_setup_inputs(seed: int = 0) -> dict:
    key = jax.random.key(seed)
    ks = jax.random.split(key, 24)
    L = DEPTH

    def nrm(k, shape, scale):
        return jax.random.normal(k, shape, jnp.float32) * scale

    def gain(k, shape):
        return 1.0 + nrm(k, shape, 0.02)

    u = jax.random.uniform(ks[10], (L, 2, D_RNN), jnp.float32, minval=0.9, maxval=0.999)
    a = u ** (1.0 / LRU_C)
    lam = jnp.log(a) - jnp.log1p(-a)
    return {
        'x': nrm(ks[0], (BATCH, SEQ, D_MODEL), 1.0),
        'mix_norm_pre': gain(ks[1], (L, D_MODEL)),
        'mix_norm_post': gain(ks[2], (L, D_MODEL)),
        'w_in': nrm(ks[3], (L, D_MODEL, D_IN), D_MODEL ** -0.5),
        'conv_w': nrm(ks[4], (L, CONV_WIDTH, D_RNN), CONV_WIDTH ** -0.5),
        'conv_b': nrm(ks[5], (L, D_RNN), 0.01),
        'lru_w_a': nrm(ks[6], (L, 2, RNN_HEADS, RNN_BLOCK, RNN_BLOCK), RNN_BLOCK ** -0.5),
        'lru_b_a': nrm(ks[7], (L, 2, D_RNN), 0.01),
        'lru_w_x': nrm(ks[8], (L, 2, RNN_HEADS, RNN_BLOCK, RNN_BLOCK), RNN_BLOCK ** -0.5),
        'lru_b_x': nrm(ks[9], (L, 2, D_RNN), 0.01),
        'lru_lambda': lam,
        'rnn_out_norm': gain(ks[11], (L, D_RNN)),
        'gla_w_gate': nrm(ks[12], (L, 2, GLA_RANK, GLA_DK_TOTAL), GLA_RANK ** -0.5),
        'gla_b_gate': nrm(ks[13], (L, 2, GLA_DK_TOTAL), 0.1),
        'gla_out_norm': gain(ks[14], (L, GLA_DV)),
        'w_out': nrm(ks[15], (L, D_MIX, D_MODEL), D_MIX ** -0.5),
        'ffn_norm_pre': gain(ks[16], (L, D_MODEL)),
        'ffn_norm_post': gain(ks[17], (L, D_MODEL)),
        'w_ffn_gate': nrm(ks[18], (L, D_MODEL, D_FF), D_MODEL ** -0.5),
        'w_ffn_up': nrm(ks[19], (L, D_MODEL, D_FF), D_MODEL ** -0.5),
        'w_ffn_down': nrm(ks[20], (L, D_FF, D_MODEL), D_FF ** -0.5),
    }


def _fwd_reference(x, mix_norm_pre, mix_norm_post, w_in, conv_w, conv_b, lru_w_a, lru_b_a,
              lru_w_x, lru_b_x, lru_lambda, rnn_out_norm, gla_w_gate, gla_b_gate,
              gla_out_norm, w_out, ffn_norm_pre, ffn_norm_post, w_ffn_gate, w_ffn_up,
              w_ffn_down):
    for l in range(DEPTH):
        h = rms_norm(x, mix_norm_pre[l])
        m = hybrid_mixer(h, w_in[l], conv_w[l], conv_b[l], lru_w_a[l], lru_b_a[l],
                         lru_w_x[l], lru_b_x[l], lru_lambda[l], rnn_out_norm[l],
                         gla_w_gate[l], gla_b_gate[l], gla_out_norm[l], w_out[l])
        x = x + rms_norm(m, mix_norm_post[l])
        h = rms_norm(x, ffn_norm_pre[l])
        f = swiglu(h, w_ffn_gate[l], w_ffn_up[l], w_ffn_down[l])
        x = x + rms_norm(f, ffn_norm_post[l])
    return x


import jax as _jax
import jax.numpy as _jnp

TWIN_FORMAT = 'train_step'
FWD_PARAMS = ['x', 'mix_norm_pre', 'mix_norm_post', 'w_in', 'conv_w', 'conv_b', 'lru_w_a', 'lru_b_a', 'lru_w_x', 'lru_b_x', 'lru_lambda', 'rnn_out_norm', 'gla_w_gate', 'gla_b_gate', 'gla_out_norm', 'w_out', 'ffn_norm_pre', 'ffn_norm_post', 'w_ffn_gate', 'w_ffn_up', 'w_ffn_down']
TWIN_WEIGHTS = ['mix_norm_pre', 'mix_norm_post', 'w_in', 'conv_w', 'conv_b', 'lru_w_a', 'lru_b_a', 'lru_w_x', 'lru_b_x', 'lru_lambda', 'rnn_out_norm', 'gla_w_gate', 'gla_b_gate', 'gla_out_norm', 'w_out', 'ffn_norm_pre', 'ffn_norm_post', 'w_ffn_gate', 'w_ffn_up', 'w_ffn_down']
TWIN_DIFF_INPUT = 'x'
TWIN_INPUTS = ['x', 'mix_norm_pre', 'mix_norm_post', 'w_in', 'conv_w', 'conv_b', 'lru_w_a', 'lru_b_a', 'lru_w_x', 'lru_b_x', 'lru_lambda', 'rnn_out_norm', 'gla_w_gate', 'gla_b_gate', 'gla_out_norm', 'w_out', 'ffn_norm_pre', 'ffn_norm_post', 'w_ffn_gate', 'w_ffn_up', 'w_ffn_down', 'loss_target', 'm_mix_norm_pre', 'm_mix_norm_post', 'm_w_in', 'm_conv_w', 'm_conv_b', 'm_lru_w_a', 'm_lru_b_a', 'm_lru_w_x', 'm_lru_b_x', 'm_lru_lambda', 'm_rnn_out_norm', 'm_gla_w_gate', 'm_gla_b_gate', 'm_gla_out_norm', 'm_w_out', 'm_ffn_norm_pre', 'm_ffn_norm_post', 'm_w_ffn_gate', 'm_w_ffn_up', 'm_w_ffn_down', 'v_mix_norm_pre', 'v_mix_norm_post', 'v_w_in', 'v_conv_w', 'v_conv_b', 'v_lru_w_a', 'v_lru_b_a', 'v_lru_w_x', 'v_lru_b_x', 'v_lru_lambda', 'v_rnn_out_norm', 'v_gla_w_gate', 'v_gla_b_gate', 'v_gla_out_norm', 'v_w_out', 'v_ffn_norm_pre', 'v_ffn_norm_post', 'v_w_ffn_gate', 'v_w_ffn_up', 'v_w_ffn_down']
TWIN_OUTPUTS = ['loss', 'grad_x', 'grad_mix_norm_pre', 'grad_mix_norm_post', 'grad_w_in', 'grad_conv_w', 'grad_conv_b', 'grad_lru_w_a', 'grad_lru_b_a', 'grad_lru_w_x', 'grad_lru_b_x', 'grad_lru_lambda', 'grad_rnn_out_norm', 'grad_gla_w_gate', 'grad_gla_b_gate', 'grad_gla_out_norm', 'grad_w_out', 'grad_ffn_norm_pre', 'grad_ffn_norm_post', 'grad_w_ffn_gate', 'grad_w_ffn_up', 'grad_w_ffn_down', 'delta_mix_norm_pre', 'delta_mix_norm_post', 'delta_w_in', 'delta_conv_w', 'delta_conv_b', 'delta_lru_w_a', 'delta_lru_b_a', 'delta_lru_w_x', 'delta_lru_b_x', 'delta_lru_lambda', 'delta_rnn_out_norm', 'delta_gla_w_gate', 'delta_gla_b_gate', 'delta_gla_out_norm', 'delta_w_out', 'delta_ffn_norm_pre', 'delta_ffn_norm_post', 'delta_w_ffn_gate', 'delta_w_ffn_up', 'delta_w_ffn_down', 'new_m_mix_norm_pre', 'new_m_mix_norm_post', 'new_m_w_in', 'new_m_conv_w', 'new_m_conv_b', 'new_m_lru_w_a', 'new_m_lru_b_a', 'new_m_lru_w_x', 'new_m_lru_b_x', 'new_m_lru_lambda', 'new_m_rnn_out_norm', 'new_m_gla_w_gate', 'new_m_gla_b_gate', 'new_m_gla_out_norm', 'new_m_w_out', 'new_m_ffn_norm_pre', 'new_m_ffn_norm_post', 'new_m_w_ffn_gate', 'new_m_w_ffn_up', 'new_m_w_ffn_down', 'new_v_mix_norm_pre', 'new_v_mix_norm_post', 'new_v_w_in', 'new_v_conv_w', 'new_v_conv_b', 'new_v_lru_w_a', 'new_v_lru_b_a', 'new_v_lru_w_x', 'new_v_lru_b_x', 'new_v_lru_lambda', 'new_v_rnn_out_norm', 'new_v_gla_w_gate', 'new_v_gla_b_gate', 'new_v_gla_out_norm', 'new_v_w_out', 'new_v_ffn_norm_pre', 'new_v_ffn_norm_post', 'new_v_w_ffn_gate', 'new_v_w_ffn_up', 'new_v_w_ffn_down']
TWIN_LEAF_KINDS = {'loss': 'loss', 'grad_x': 'grad_x', 'grad_mix_norm_pre': 'grad_w', 'grad_mix_norm_post': 'grad_w', 'grad_w_in': 'grad_w', 'grad_conv_w': 'grad_w', 'grad_conv_b': 'grad_w', 'grad_lru_w_a': 'grad_w', 'grad_lru_b_a': 'grad_w', 'grad_lru_w_x': 'grad_w', 'grad_lru_b_x': 'grad_w', 'grad_lru_lambda': 'grad_w', 'grad_rnn_out_norm': 'grad_w', 'grad_gla_w_gate': 'grad_w', 'grad_gla_b_gate': 'grad_w', 'grad_gla_out_norm': 'grad_w', 'grad_w_out': 'grad_w', 'grad_ffn_norm_pre': 'grad_w', 'grad_ffn_norm_post': 'grad_w', 'grad_w_ffn_gate': 'grad_w', 'grad_w_ffn_up': 'grad_w', 'grad_w_ffn_down': 'grad_w', 'delta_mix_norm_pre': 'delta_w', 'delta_mix_norm_post': 'delta_w', 'delta_w_in': 'delta_w', 'delta_conv_w': 'delta_w', 'delta_conv_b': 'delta_w', 'delta_lru_w_a': 'delta_w', 'delta_lru_b_a': 'delta_w', 'delta_lru_w_x': 'delta_w', 'delta_lru_b_x': 'delta_w', 'delta_lru_lambda': 'delta_w', 'delta_rnn_out_norm': 'delta_w', 'delta_gla_w_gate': 'delta_w', 'delta_gla_b_gate': 'delta_w', 'delta_gla_out_norm': 'delta_w', 'delta_w_out': 'delta_w', 'delta_ffn_norm_pre': 'delta_w', 'delta_ffn_norm_post': 'delta_w', 'delta_w_ffn_gate': 'delta_w', 'delta_w_ffn_up': 'delta_w', 'delta_w_ffn_down': 'delta_w', 'new_m_mix_norm_pre': 'new_m', 'new_m_mix_norm_post': 'new_m', 'new_m_w_in': 'new_m', 'new_m_conv_w': 'new_m', 'new_m_conv_b': 'new_m', 'new_m_lru_w_a': 'new_m', 'new_m_lru_b_a': 'new_m', 'new_m_lru_w_x': 'new_m', 'new_m_lru_b_x': 'new_m', 'new_m_lru_lambda': 'new_m', 'new_m_rnn_out_norm': 'new_m', 'new_m_gla_w_gate': 'new_m', 'new_m_gla_b_gate': 'new_m', 'new_m_gla_out_norm': 'new_m', 'new_m_w_out': 'new_m', 'new_m_ffn_norm_pre': 'new_m', 'new_m_ffn_norm_post': 'new_m', 'new_m_w_ffn_gate': 'new_m', 'new_m_w_ffn_up': 'new_m', 'new_m_w_ffn_down': 'new_m', 'new_v_mix_norm_pre': 'new_v', 'new_v_mix_norm_post': 'new_v', 'new_v_w_in': 'new_v', 'new_v_conv_w': 'new_v', 'new_v_conv_b': 'new_v', 'new_v_lru_w_a': 'new_v', 'new_v_lru_b_a': 'new_v', 'new_v_lru_w_x': 'new_v', 'new_v_lru_b_x': 'new_v', 'new_v_lru_lambda': 'new_v', 'new_v_rnn_out_norm': 'new_v', 'new_v_gla_w_gate': 'new_v', 'new_v_gla_b_gate': 'new_v', 'new_v_gla_out_norm': 'new_v', 'new_v_w_out': 'new_v', 'new_v_ffn_norm_pre': 'new_v', 'new_v_ffn_norm_post': 'new_v', 'new_v_w_ffn_gate': 'new_v', 'new_v_w_ffn_up': 'new_v', 'new_v_w_ffn_down': 'new_v'}


def _forward(args):
    return _fwd_reference(*[args[k] for k in FWD_PARAMS])


def _output_shape():
    def fwd():
        inp = _fwd_setup_inputs(0)
        return _fwd_reference(*[inp[k] for k in FWD_PARAMS])
    out = _jax.eval_shape(fwd)
    return out.shape, out.dtype

N_MICROBATCH = 1
ADAM_LR = 0.001
ADAM_B1 = 0.9
ADAM_B2 = 0.999
ADAM_EPS = 1e-08
ADAM_WD = 0.01
ADAM_STEP = 10
PER_EXAMPLE_BATCH_AXIS = {'x': 0, 'loss_target': 0}
SHARED_INPUTS = []
_WEIGHT_DTYPES = {'mix_norm_pre': _jnp.float32, 'mix_norm_post': _jnp.float32, 'w_in': _jnp.float32, 'conv_w': _jnp.float32, 'conv_b': _jnp.float32, 'lru_w_a': _jnp.float32, 'lru_b_a': _jnp.float32, 'lru_w_x': _jnp.float32, 'lru_b_x': _jnp.float32, 'lru_lambda': _jnp.float32, 'rnn_out_norm': _jnp.float32, 'gla_w_gate': _jnp.float32, 'gla_b_gate': _jnp.float32, 'gla_out_norm': _jnp.float32, 'w_out': _jnp.float32, 'ffn_norm_pre': _jnp.float32, 'ffn_norm_post': _jnp.float32, 'w_ffn_gate': _jnp.float32, 'w_ffn_up': _jnp.float32, 'w_ffn_down': _jnp.float32}
MOMENT_SCALE = {'mix_norm_pre': 1.800847e+01, 'mix_norm_post': 1.331825e+02, 'w_in': 1.122199e+01, 'conv_w': 3.937748e+01, 'conv_b': 4.278322e+02, 'lru_w_a': 8.210335e+00, 'lru_b_a': 5.779256e+00, 'lru_w_x': 1.505705e+01, 'lru_b_x': 7.907904e+00, 'lru_lambda': 1.013511e+01, 'rnn_out_norm': 4.301569e+01, 'gla_w_gate': 2.403402e-01, 'gla_b_gate': 8.719887e-01, 'gla_out_norm': 4.523546e+00, 'w_out': 3.325449e+01, 'ffn_norm_pre': 8.803675e+00, 'ffn_norm_post': 1.275006e+02, 'w_ffn_gate': 2.403388e+00, 'w_ffn_up': 4.605531e+00, 'w_ffn_down': 7.921722e+00}


def _to_microbatches(a, axis):
    t = _jnp.moveaxis(a, axis, 0)
    t = t.reshape((N_MICROBATCH, t.shape[0] // N_MICROBATCH) + t.shape[1:])
    return _jnp.moveaxis(t, 1, axis + 1)


def setup_inputs(seed: int = 0) -> dict:
    inp = _fwd_setup_inputs(seed)
    key = _jax.random.fold_in(_jax.random.key(seed), 7919)
    shape, _ = _output_shape()
    out = dict(inp)
    out["loss_target"] = _jax.random.normal(_jax.random.fold_in(key, 0), shape, _jnp.float32)
    for i, name in enumerate(TWIN_WEIGHTS):
        w = inp[name].astype(_jnp.float32)
        if MOMENT_SCALE is None:
            s = _jnp.sqrt(_jnp.mean(_jnp.square(w)) + 1e-30)
        else:
            s = MOMENT_SCALE[name]
        km, kv = _jax.random.split(_jax.random.fold_in(key, i + 1))
        out[name] = w
        out["m_" + name] = s * _jax.random.normal(km, w.shape, _jnp.float32)
        out["v_" + name] = (s * s) * _jax.random.uniform(kv, w.shape, _jnp.float32, 0.5, 1.5)
    if N_MICROBATCH > 1:
        for name, axis in PER_EXAMPLE_BATCH_AXIS.items():
            out[name] = _to_microbatches(out[name], axis)
    return {'x': out['x'], 'mix_norm_pre': out['mix_norm_pre'], 'mix_norm_post': out['mix_norm_post'], 'w_in': out['w_in'], 'conv_w': out['conv_w'], 'conv_b': out['conv_b'], 'lru_w_a': out['lru_w_a'], 'lru_b_a': out['lru_b_a'], 'lru_w_x': out['lru_w_x'], 'lru_b_x': out['lru_b_x'], 'lru_lambda': out['lru_lambda'], 'rnn_out_norm': out['rnn_out_norm'], 'gla_w_gate': out['gla_w_gate'], 'gla_b_gate': out['gla_b_gate'], 'gla_out_norm': out['gla_out_norm'], 'w_out': out['w_out'], 'ffn_norm_pre': out['ffn_norm_pre'], 'ffn_norm_post': out['ffn_norm_post'], 'w_ffn_gate': out['w_ffn_gate'], 'w_ffn_up': out['w_ffn_up'], 'w_ffn_down': out['w_ffn_down'], 'loss_target': out['loss_target'], 'm_mix_norm_pre': out['m_mix_norm_pre'], 'm_mix_norm_post': out['m_mix_norm_post'], 'm_w_in': out['m_w_in'], 'm_conv_w': out['m_conv_w'], 'm_conv_b': out['m_conv_b'], 'm_lru_w_a': out['m_lru_w_a'], 'm_lru_b_a': out['m_lru_b_a'], 'm_lru_w_x': out['m_lru_w_x'], 'm_lru_b_x': out['m_lru_b_x'], 'm_lru_lambda': out['m_lru_lambda'], 'm_rnn_out_norm': out['m_rnn_out_norm'], 'm_gla_w_gate': out['m_gla_w_gate'], 'm_gla_b_gate': out['m_gla_b_gate'], 'm_gla_out_norm': out['m_gla_out_norm'], 'm_w_out': out['m_w_out'], 'm_ffn_norm_pre': out['m_ffn_norm_pre'], 'm_ffn_norm_post': out['m_ffn_norm_post'], 'm_w_ffn_gate': out['m_w_ffn_gate'], 'm_w_ffn_up': out['m_w_ffn_up'], 'm_w_ffn_down': out['m_w_ffn_down'], 'v_mix_norm_pre': out['v_mix_norm_pre'], 'v_mix_norm_post': out['v_mix_norm_post'], 'v_w_in': out['v_w_in'], 'v_conv_w': out['v_conv_w'], 'v_conv_b': out['v_conv_b'], 'v_lru_w_a': out['v_lru_w_a'], 'v_lru_b_a': out['v_lru_b_a'], 'v_lru_w_x': out['v_lru_w_x'], 'v_lru_b_x': out['v_lru_b_x'], 'v_lru_lambda': out['v_lru_lambda'], 'v_rnn_out_norm': out['v_rnn_out_norm'], 'v_gla_w_gate': out['v_gla_w_gate'], 'v_gla_b_gate': out['v_gla_b_gate'], 'v_gla_out_norm': out['v_gla_out_norm'], 'v_w_out': out['v_w_out'], 'v_ffn_norm_pre': out['v_ffn_norm_pre'], 'v_ffn_norm_post': out['v_ffn_norm_post'], 'v_w_ffn_gate': out['v_w_ffn_gate'], 'v_w_ffn_up': out['v_w_ffn_up'], 'v_w_ffn_down': out['v_w_ffn_down']}


def _loss(weights, diff, rest, loss_target):
    with _jax.named_scope("forward"):
        args = {**rest, TWIN_DIFF_INPUT: diff, **{k: w.astype(_WEIGHT_DTYPES[k]) for k, w in weights.items()}}
        y = _forward(args)
    with _jax.named_scope("loss_head"):
        err = _jnp.square(y.astype(_jnp.float32) - loss_target)
        return 0.5 * _jnp.sum(_jnp.mean(err, axis=-1)) if err.ndim else 0.5 * err


def _adamw(w, g, m, v):
    m = ADAM_B1 * m + (1.0 - ADAM_B1) * g
    v = ADAM_B2 * v + (1.0 - ADAM_B2) * _jnp.square(g)
    m_hat = m / (1.0 - ADAM_B1 ** ADAM_STEP)
    v_hat = v / (1.0 - ADAM_B2 ** ADAM_STEP)
    delta = -ADAM_LR * (m_hat / (_jnp.sqrt(v_hat) + ADAM_EPS) + ADAM_WD * w)
    return delta, m, v


def reference(x, mix_norm_pre, mix_norm_post, w_in, conv_w, conv_b, lru_w_a, lru_b_a, lru_w_x, lru_b_x, lru_lambda, rnn_out_norm, gla_w_gate, gla_b_gate, gla_out_norm, w_out, ffn_norm_pre, ffn_norm_post, w_ffn_gate, w_ffn_up, w_ffn_down, loss_target, m_mix_norm_pre, m_mix_norm_post, m_w_in, m_conv_w, m_conv_b, m_lru_w_a, m_lru_b_a, m_lru_w_x, m_lru_b_x, m_lru_lambda, m_rnn_out_norm, m_gla_w_gate, m_gla_b_gate, m_gla_out_norm, m_w_out, m_ffn_norm_pre, m_ffn_norm_post, m_w_ffn_gate, m_w_ffn_up, m_w_ffn_down, v_mix_norm_pre, v_mix_norm_post, v_w_in, v_conv_w, v_conv_b, v_lru_w_a, v_lru_b_a, v_lru_w_x, v_lru_b_x, v_lru_lambda, v_rnn_out_norm, v_gla_w_gate, v_gla_b_gate, v_gla_out_norm, v_w_out, v_ffn_norm_pre, v_ffn_norm_post, v_w_ffn_gate, v_w_ffn_up, v_w_ffn_down):
    given = dict(x=x, mix_norm_pre=mix_norm_pre, mix_norm_post=mix_norm_post, w_in=w_in, conv_w=conv_w, conv_b=conv_b, lru_w_a=lru_w_a, lru_b_a=lru_b_a, lru_w_x=lru_w_x, lru_b_x=lru_b_x, lru_lambda=lru_lambda, rnn_out_norm=rnn_out_norm, gla_w_gate=gla_w_gate, gla_b_gate=gla_b_gate, gla_out_norm=gla_out_norm, w_out=w_out, ffn_norm_pre=ffn_norm_pre, ffn_norm_post=ffn_norm_post, w_ffn_gate=w_ffn_gate, w_ffn_up=w_ffn_up, w_ffn_down=w_ffn_down, loss_target=loss_target, m_mix_norm_pre=m_mix_norm_pre, m_mix_norm_post=m_mix_norm_post, m_w_in=m_w_in, m_conv_w=m_conv_w, m_conv_b=m_conv_b, m_lru_w_a=m_lru_w_a, m_lru_b_a=m_lru_b_a, m_lru_w_x=m_lru_w_x, m_lru_b_x=m_lru_b_x, m_lru_lambda=m_lru_lambda, m_rnn_out_norm=m_rnn_out_norm, m_gla_w_gate=m_gla_w_gate, m_gla_b_gate=m_gla_b_gate, m_gla_out_norm=m_gla_out_norm, m_w_out=m_w_out, m_ffn_norm_pre=m_ffn_norm_pre, m_ffn_norm_post=m_ffn_norm_post, m_w_ffn_gate=m_w_ffn_gate, m_w_ffn_up=m_w_ffn_up, m_w_ffn_down=m_w_ffn_down, v_mix_norm_pre=v_mix_norm_pre, v_mix_norm_post=v_mix_norm_post, v_w_in=v_w_in, v_conv_w=v_conv_w, v_conv_b=v_conv_b, v_lru_w_a=v_lru_w_a, v_lru_b_a=v_lru_b_a, v_lru_w_x=v_lru_w_x, v_lru_b_x=v_lru_b_x, v_lru_lambda=v_lru_lambda, v_rnn_out_norm=v_rnn_out_norm, v_gla_w_gate=v_gla_w_gate, v_gla_b_gate=v_gla_b_gate, v_gla_out_norm=v_gla_out_norm, v_w_out=v_w_out, v_ffn_norm_pre=v_ffn_norm_pre, v_ffn_norm_post=v_ffn_norm_post, v_w_ffn_gate=v_w_ffn_gate, v_w_ffn_up=v_w_ffn_up, v_w_ffn_down=v_w_ffn_down)
    weights = {n: given[n] for n in TWIN_WEIGHTS}
    shared = {n: given[n] for n in SHARED_INPUTS}
    per_example = {n: given[n] for n in ['x']}
    grad_fn = _jax.value_and_grad(_loss, argnums=(0, 1))

    def one_microbatch(ex, loss_target):
        ex = dict(ex)
        diff = ex.pop(TWIN_DIFF_INPUT)
        return grad_fn(weights, diff, {**shared, **ex}, loss_target)

    if N_MICROBATCH == 1:
        loss, (grad_w, grad_x) = one_microbatch(per_example, given["loss_target"])
    else:
        def body(carry, xs):
            loss_sum, grad_sum = carry
            l_k, (gw_k, gx_k) = one_microbatch(xs[0], xs[1])
            with _jax.named_scope("update"):
                return (loss_sum + l_k, _jax.tree.map(_jnp.add, grad_sum, gw_k)), gx_k

        init = (_jnp.zeros((), _jnp.float32), _jax.tree.map(_jnp.zeros_like, weights))
        (loss, grad_w), grad_x = _jax.lax.scan(body, init, (per_example, given["loss_target"]))
    with _jax.named_scope("update"):
        delta_w, new_m, new_v = {}, {}, {}
        for n in TWIN_WEIGHTS:
            delta_w[n], new_m[n], new_v[n] = _adamw(weights[n], grad_w[n], given["m_" + n], given["v_" + n])
    return (loss, grad_x, *[grad_w[n] for n in TWIN_WEIGHTS], *[delta_w[n] for n in TWIN_WEIGHTS],
            *[new_m[n] for n in TWIN_WEIGHTS], *[new_v[n] for n in TWIN_WEIGHTS])
```

```python
import functools

import jax
import jax.numpy as jnp
from jax import lax
from jax.experimental import pallas as pl
from jax.experimental.pallas import tpu as pltpu

F32 = jnp.float32
MXU_DTYPE = jnp.bfloat16

N_DEV = 8
D_MODEL = 1024
D_RNN = 512
CONV_WIDTH = 4
LRU_C = 8.0
GLA_HEADS = 4
GLA_DK = 64
GLA_DKP = 128
GLA_DV = 128
GLA_RANK = 16
GLA_TAU = 16.0
GLA_CHUNK = 64
D_FF = 2816
RMS_EPS = 1e-6
DEPTH = 4

PW = 3200
COL_LR = 3072
LANES = 128

TILE_S = 512
TILE_F = 256
F_CHUNK = 1408
VMEM_LIMIT = 56 * 1024 * 1024

ADAM_LR = 0.001
ADAM_B1 = 0.9
ADAM_B2 = 0.999
ADAM_EPS = 1e-08
ADAM_WD = 0.01
ADAM_STEP = 10

ADAM_TILE_ROWS = 256
ADAM_COLS = 1024


def _mm(a, b):
    return jnp.dot(a.astype(MXU_DTYPE), b.astype(MXU_DTYPE), preferred_element_type=F32)


def _mm_nt(a, b):
    return lax.dot_general(a.astype(MXU_DTYPE), b.astype(MXU_DTYPE), (((1,), (1,)), ((), ())),
                           preferred_element_type=F32)


def _mm_tn(a, b):
    return lax.dot_general(a.astype(MXU_DTYPE), b.astype(MXU_DTYPE), (((0,), (0,)), ((), ())),
                           preferred_element_type=F32)


def _mm_exact(a, b):
    return jnp.dot(a, b, precision=lax.Precision.HIGHEST, preferred_element_type=F32)


def _sigmoid(x):
    return 1.0 / (1.0 + jnp.exp(-x))


def _log1p_pos(e):
    series = e * (1.0 - e * (0.5 - e * (1.0 / 3.0 - e * 0.25)))
    return jnp.where(e < 0.01, series, jnp.log(1.0 + e))


def _softplus(x):
    return jnp.maximum(x, 0.0) + _log1p_pos(jnp.exp(-jnp.abs(x)))


def _expm1(x):
    series = x * (1.0 + x * (0.5 + x * (1.0 / 6.0 + x * (1.0 / 24.0))))
    return jnp.where(jnp.abs(x) < 0.05, series, jnp.exp(x) - 1.0)


GELU_C = 0.7978845608028654
GELU_K = 0.044715


def _gelu_and_grad(x):
    t = jnp.tanh(GELU_C * (x + GELU_K * x * x * x))
    y = 0.5 * x * (1.0 + t)
    dy = 0.5 * (1.0 + t) + 0.5 * x * (1.0 - t * t) * GELU_C * (1.0 + 3.0 * GELU_K * x * x)
    return y, dy


def _rms_fwd(x, g):
    rs = lax.rsqrt(jnp.mean(x * x, axis=-1, keepdims=True) + RMS_EPS)
    n = x * rs
    return n * g, n, rs


def _rms_bwd(dy, n, rs, g):
    dn = dy * g
    dx = rs * (dn - n * jnp.mean(dn * n, axis=-1, keepdims=True))
    dg = jnp.sum(dy * n, axis=0, keepdims=True)
    return dx, dg


def _cparams(sem=None):
    kw = dict(vmem_limit_bytes=VMEM_LIMIT)
    if sem is not None:
        kw["dimension_semantics"] = sem
    return pltpu.CompilerParams(**kw)


def _tile(n, pref):
    return pref if n % pref == 0 else n


def _const(shape):
    nd = len(shape)
    return pl.BlockSpec(shape, lambda *_: (0,) * nd, pipeline_mode=pl.Buffered(1))


def _acc(shape):
    nd = len(shape)
    return pl.BlockSpec(shape, lambda *_: (0,) * nd)


def _rows(ts, w, col=0, order=None):
    if order is None:
        return pl.BlockSpec((ts, w), lambda i: (i, col))
    return pl.BlockSpec((ts, w), lambda i: (order(i), col))


def _sds(shape, dtype=F32):
    return jax.ShapeDtypeStruct(shape, dtype)


def _mix_in_fwd(x, gpre, w_in_p):
    s = x.shape[0]
    ts = _tile(s, TILE_S)

    def body(x_ref, g_ref, w_ref, o_ref):
        h, _, _ = _rms_fwd(x_ref[...], g_ref[...])
        o_ref[...] = _mm(h, w_ref[...])

    return pl.pallas_call(
        body, grid=(s // ts,),
        in_specs=[_rows(ts, D_MODEL), _const((1, D_MODEL)), _const((D_MODEL, PW))],
        out_specs=_rows(ts, PW), out_shape=_sds((s, PW)),
        name="mix_in_fwd", compiler_params=_cparams(("parallel",)))(x, gpre, w_in_p)


def _conv_taps(xr, hp, hn, first, last):
    ts = xr.shape[0]
    hp = jnp.where(first, 0.0, hp)
    hn = jnp.where(last, 0.0, hn)
    xe = jnp.concatenate([hp, xr, hn], axis=0)
    return xe[6:6 + ts], xe[7:7 + ts], xr, xe[9:9 + ts]


def _conv_fwd(xr, hp, hn, cw, cb, first, last):
    t0, t1, t2, t3 = _conv_taps(xr, hp, hn, first, last)
    return cw[0:1] * t0 + cw[1:2] * t1 + cw[2:3] * t2 + cw[3:4] * t3 + cb


def _rnn_gates(xc, wa, ba, wx, bx, lam):
    r = _sigmoid(_mm(xc, wa) + ba)
    i = _sigmoid(_mm(xc, wx) + bx)
    sp = _softplus(-lam)
    la = (-LRU_C) * r * sp
    a = jnp.exp(la)
    mult = jnp.sqrt(-_expm1(2.0 * la))
    return r, i, sp, a, mult


def _scan_tile(a_scr, u_scr, h_ref, c0, reverse):
    ts = a_scr.shape[0]
    a = a_scr[...]
    u = u_scr[...]
    row = lax.broadcasted_iota(jnp.int32, a.shape, 0) % 8
    for k in (1, 2, 4):
        if reverse:
            a_sh = pltpu.roll(a, ts - k, 0)
            u_sh = pltpu.roll(u, ts - k, 0)
            ok = row < 8 - k
        else:
            a_sh = pltpu.roll(a, k, 0)
            u_sh = pltpu.roll(u, k, 0)
            ok = row >= k
        u = jnp.where(ok, u + a * u_sh, u)
        a = jnp.where(ok, a * a_sh, a)
    a_scr[...] = a
    u_scr[...] = u
    ng = ts // 8

    def body(j, c):
        g = (ng - 1 - j) if reverse else j
        sl = pl.ds(pl.multiple_of(g * 8, 8), 8)
        hh = u_scr[sl, :] + a_scr[sl, :] * c
        h_ref[sl, :] = hh
        return hh[0:1, :] if reverse else hh[7:8, :]

    return lax.fori_loop(0, ng, body, c0)


def _halo_specs(s, ts, w, col, order):
    n8 = s // 8
    per = ts // 8
    prev = pl.BlockSpec((8, w), lambda i: (jnp.maximum(order(i) * per - 1, 0), col))
    nxt = pl.BlockSpec((8, w), lambda i: (jnp.minimum((order(i) + 1) * per, n8 - 1), col))
    return prev, nxt


def _rnn_fwd(proj, cw, cb, wa, ba, wx, bx, lam, reverse):
    s = proj.shape[0]
    ts = _tile(s, TILE_S)
    nt = s // ts
    order = (lambda i: nt - 1 - i) if reverse else (lambda i: i)

    def body(xr_ref, hp_ref, hn_ref, cw_ref, cb_ref, wa_ref, ba_ref, wx_ref, bx_ref, lam_ref,
             h_ref, a_scr, u_scr, c_scr):
        i = pl.program_id(0)
        t = order(i)

        @pl.when(i == 0)
        def _():
            c_scr[...] = jnp.zeros_like(c_scr)

        xc = _conv_fwd(xr_ref[...], hp_ref[...], hn_ref[...], cw_ref[...], cb_ref[...], t == 0, t == nt - 1)
        _, gi, _, a, mult = _rnn_gates(xc, wa_ref[...], ba_ref[...], wx_ref[...], bx_ref[...], lam_ref[...])
        a_scr[...] = a
        u_scr[...] = xc * gi * mult
        c_scr[0:1, :] = _scan_tile(a_scr, u_scr, h_ref, c_scr[0:1, :], reverse)

    hp, hn = _halo_specs(s, ts, D_RNN, 0, order)
    return pl.pallas_call(
        body, grid=(nt,),
        in_specs=[_rows(ts, D_RNN, 0, order), hp, hn, _const((CONV_WIDTH, D_RNN)), _const((1, D_RNN)),
                  _const((D_RNN, D_RNN)), _const((1, D_RNN)), _const((D_RNN, D_RNN)), _const((1, D_RNN)),
                  _const((1, D_RNN))],
        out_specs=_rows(ts, D_RNN, 0, order), out_shape=_sds((s, D_RNN)),
        scratch_shapes=[pltpu.VMEM((ts, D_RNN), F32), pltpu.VMEM((ts, D_RNN), F32), pltpu.VMEM((8, D_RNN), F32)],
        name="rnn_fwd_rev" if reverse else "rnn_fwd",
        compiler_params=_cparams(("arbitrary",)))(proj, proj, proj, cw, cb, wa, ba, wx, bx, lam)


def _tri(reverse, transpose=False):
    r = lax.broadcasted_iota(jnp.int32, (GLA_CHUNK, GLA_CHUNK), 0)
    c = lax.broadcasted_iota(jnp.int32, (GLA_CHUNK, GLA_CHUNK), 1)
    if transpose:
        r, c = c, r
    return ((r <= c) if reverse else (r >= c)).astype(F32)


def _gla_chunk_terms(q, k, la, tri, reverse):
    b = _mm_exact(tri, la)
    bl = b[0:1] if reverse else b[GLA_CHUNK - 1:GLA_CHUNK]
    eb = jnp.exp(b)
    enb = jnp.exp(-b)
    ebl = jnp.exp(bl - b)
    d = jnp.exp(bl)
    return eb, enb, ebl, d, q * (GLA_DK ** -0.5) * eb, k * enb, k * ebl


def _gla_gate(lr, wg, bg):
    z = _mm(lr, wg) + bg
    return z, -_softplus(-z) * (1.0 / GLA_TAU)


def _gla_fwd(proj, wg, bg, reverse):
    s = proj.shape[0]
    ts = _tile(s, TILE_S)
    nt = s // ts
    ch = ts // GLA_CHUNK
    hw = GLA_HEADS * LANES
    order = (lambda i: nt - 1 - i) if reverse else (lambda i: i)

    def body(q_ref, k_ref, v_ref, lr_ref, wg_ref, bg_ref, o_ref, st_ref, s_scr):
        @pl.when(pl.program_id(0) == 0)
        def _():
            s_scr[...] = jnp.zeros_like(s_scr)

        _, la = _gla_gate(lr_ref[...], wg_ref[...], bg_ref[...])
        tri = _tri(reverse)
        keep = tri > 0.5
        for cc in range(ch):
            c = (ch - 1 - cc) if reverse else cc
            rows = slice(c * GLA_CHUNK, (c + 1) * GLA_CHUNK)
            _, _, _, d, qe, ke, kd = _gla_chunk_terms(q_ref[rows, :], k_ref[rows, :], la[rows], tri, reverse)
            outs = []
            for h in range(GLA_HEADS):
                ln = slice(h * LANES, (h + 1) * LANES)
                st = s_scr[h]
                st_ref[c, h] = st
                v_h = v_ref[rows, ln]
                a_m = jnp.where(keep, _mm_nt(qe[:, ln], ke[:, ln]), 0.0)
                outs.append(_mm(a_m, v_h) + _mm_nt(qe[:, ln], st))
                s_scr[h] = d[:, ln] * st + _mm_tn(v_h, kd[:, ln])
            o_ref[rows, :] = jnp.concatenate(outs, axis=1)

    return pl.pallas_call(
        body, grid=(nt,),
        in_specs=[_rows(ts, hw, 2, order), _rows(ts, hw, 3, order), _rows(ts, hw, 4, order),
                  _rows(ts, LANES, COL_LR // LANES, order), _const((LANES, hw)), _const((1, hw))],
        out_specs=[_rows(ts, hw, 0, order),
                   pl.BlockSpec((ch, GLA_HEADS, LANES, LANES), lambda i: (order(i), 0, 0, 0))],
        out_shape=[_sds((s, hw)), _sds((s // GLA_CHUNK, GLA_HEADS, LANES, LANES))],
        scratch_shapes=[pltpu.VMEM((GLA_HEADS, LANES, LANES), F32)],
        name="gla_fwd_rev" if reverse else "gla_fwd",
        compiler_params=_cparams(("arbitrary",)))(proj, proj, proj, proj, wg, bg)


def _mix_out_terms(hf, hb, gate_r, of, ob, g, g_rnn, g_gla):
    hs = hf + hb
    gl, dgl = _gelu_and_grad(gate_r)
    z = hs * gl
    y_rnn, n_rnn, rs_rnn = _rms_fwd(z, g_rnn)
    osum = of + ob
    sg_lin = _sigmoid(g)
    sg = g * sg_lin
    dsg = sg_lin * (1.0 + g * (1.0 - sg_lin))
    ons, ns, rss = [], [], []
    for h in range(GLA_HEADS):
        ln = slice(h * LANES, (h + 1) * LANES)
        on, n, rs = _rms_fwd(osum[:, ln], g_gla)
        ons.append(on)
        ns.append(n)
        rss.append(rs)
    on = jnp.concatenate(ons, axis=1)
    return hs, gl, dgl, y_rnn, n_rnn, rs_rnn, sg, dsg, on, ns, rss


def _mix_out_fwd(x, hf, hb, of, ob, proj, g_rnn, g_gla, w_out, gpost):
    s = x.shape[0]
    ts = _tile(s, TILE_S)

    def body(x_ref, hf_ref, hb_ref, gr_ref, of_ref, ob_ref, g_ref, grnn_ref, ggla_ref, w_ref, gp_ref,
             x1_ref, y_ref, m_ref):
        _, _, _, y_rnn, _, _, sg, _, on, _, _ = _mix_out_terms(
            hf_ref[...], hb_ref[...], gr_ref[...], of_ref[...], ob_ref[...], g_ref[...], grnn_ref[...], ggla_ref[...])
        y = jnp.concatenate([y_rnn, on * sg], axis=1).astype(MXU_DTYPE)
        y_ref[...] = y
        m = _mm(y, w_ref[...])
        m_ref[...] = m
        out, _, _ = _rms_fwd(m, gp_ref[...])
        x1_ref[...] = x_ref[...] + out

    return pl.pallas_call(
        body, grid=(s // ts,),
        in_specs=[_rows(ts, D_MODEL), _rows(ts, D_RNN), _rows(ts, D_RNN), _rows(ts, D_RNN, 1), _rows(ts, 512),
                  _rows(ts, 512), _rows(ts, 512, 5), _const((1, D_RNN)), _const((1, GLA_DV)),
                  _const((D_MODEL, D_MODEL)), _const((1, D_MODEL))],
        out_specs=[_rows(ts, D_MODEL), _rows(ts, D_MODEL), _rows(ts, D_MODEL)],
        out_shape=[_sds((s, D_MODEL)), _sds((s, D_MODEL), MXU_DTYPE), _sds((s, D_MODEL))],
        name="mix_out_fwd", compiler_params=_cparams(("parallel",)))(
            x, hf, hb, proj, of, ob, proj, g_rnn, g_gla, w_out, gpost)


def _f_chunks():
    return [(c0, min(c0 + F_CHUNK, D_FF)) for c0 in range(0, D_FF, F_CHUNK)]


def _ffn_fwd(x1, gpre, wg, wu, wd, gpost):
    s = x1.shape[0]
    ts = _tile(s, TILE_F)

    def body(x_ref, gpre_ref, wg_ref, wu_ref, wd_ref, gpost_ref, x2_ref, a_ref, u_ref, f_ref):
        x = x_ref[...]
        h, _, _ = _rms_fwd(x, gpre_ref[...])
        h = h.astype(MXU_DTYPE)
        f = jnp.zeros((ts, D_MODEL), F32)
        for c0, c1 in _f_chunks():
            a = _mm(h, wg_ref[:, c0:c1])
            u = _mm(h, wu_ref[:, c0:c1])
            a_ref[:, c0:c1] = a.astype(MXU_DTYPE)
            u_ref[:, c0:c1] = u.astype(MXU_DTYPE)
            f = f + _mm(a * _sigmoid(a) * u, wd_ref[c0:c1, :])
        f_ref[...] = f
        out, _, _ = _rms_fwd(f, gpost_ref[...])
        x2_ref[...] = x + out

    return pl.pallas_call(
        body, grid=(s // ts,),
        in_specs=[_rows(ts, D_MODEL), _const((1, D_MODEL)), _const((D_MODEL, D_FF)), _const((D_MODEL, D_FF)),
                  _const((D_FF, D_MODEL)), _const((1, D_MODEL))],
        out_specs=[_rows(ts, D_MODEL), _rows(ts, D_FF), _rows(ts, D_FF), _rows(ts, D_MODEL)],
        out_shape=[_sds((s, D_MODEL)), _sds((s, D_FF), MXU_DTYPE), _sds((s, D_FF), MXU_DTYPE), _sds((s, D_MODEL))],
        name="ffn_fwd", compiler_params=_cparams(("parallel",)))(x1, gpre, wg, wu, wd, gpost)


def _loss_fwd_bwd(y, target):
    s = y.shape[0]
    ts = _tile(s, TILE_S)

    def body(y_ref, t_ref, loss_ref, dy_ref):
        @pl.when(pl.program_id(0) == 0)
        def _():
            loss_ref[...] = jnp.zeros_like(loss_ref)

        e = y_ref[...] - t_ref[...]
        dy_ref[...] = e * (1.0 / D_MODEL)
        part = jnp.sum(jnp.sum(e * e, axis=1, keepdims=True), axis=0, keepdims=True) * (0.5 / D_MODEL)
        loss_ref[...] += jnp.broadcast_to(part, loss_ref.shape)

    return pl.pallas_call(
        body, grid=(s // ts,),
        in_specs=[_rows(ts, D_MODEL), _rows(ts, D_MODEL)],
        out_specs=[_acc((8, LANES)), _rows(ts, D_MODEL)],
        out_shape=[_sds((8, LANES)), _sds((s, D_MODEL))],
        name="loss", compiler_params=_cparams(("arbitrary",)))(y, target)


def _tn_matmul(a, b, name):
    s, k = a.shape
    n = b.shape[1]
    ts = _tile(s, TILE_S)
    tn = n
    while k * tn * 4 > 6 * 1024 * 1024 and tn % 256 == 0:
        tn //= 2

    def body(a_ref, b_ref, o_ref):
        @pl.when(pl.program_id(1) == 0)
        def _():
            o_ref[...] = jnp.zeros_like(o_ref)

        o_ref[...] += _mm_tn(a_ref[...], b_ref[...])

    return pl.pallas_call(
        body, grid=(n // tn, s // ts),
        in_specs=[pl.BlockSpec((ts, k), lambda j, i: (i, 0)), pl.BlockSpec((ts, tn), lambda j, i: (i, j))],
        out_specs=pl.BlockSpec((k, tn), lambda j, i: (0, j)), out_shape=_sds((k, n)),
        name=name, compiler_params=_cparams(("parallel", "arbitrary")))(a, b)


def _ffn_bwd(dx2, f, x1, a, u, gpre, wg, wu, wd, gpost):
    s = x1.shape[0]
    ts = _tile(s, TILE_F)

    def body(dx2_ref, f_ref, x1_ref, a_ref, u_ref, gpre_ref, wg_ref, wu_ref, wd_ref, gpost_ref,
             dx1_ref, df_ref, h_ref, p_ref, da_ref, du_ref, dgpost_ref, dgpre_ref):
        @pl.when(pl.program_id(0) == 0)
        def _():
            dgpost_ref[...] = jnp.zeros_like(dgpost_ref)
            dgpre_ref[...] = jnp.zeros_like(dgpre_ref)

        dx2 = dx2_ref[...]
        _, nf, rsf = _rms_fwd(f_ref[...], gpost_ref[...])
        df, dgpost = _rms_bwd(dx2, nf, rsf, gpost_ref[...])
        dgpost_ref[...] += dgpost
        df = df.astype(MXU_DTYPE)
        df_ref[...] = df
        h, n1, rs1 = _rms_fwd(x1_ref[...], gpre_ref[...])
        h_ref[...] = h.astype(MXU_DTYPE)
        dh = jnp.zeros((ts, D_MODEL), F32)
        for c0, c1 in _f_chunks():
            av = a_ref[:, c0:c1].astype(F32)
            uv = u_ref[:, c0:c1].astype(F32)
            sg = _sigmoid(av)
            dp = _mm_nt(df, wd_ref[c0:c1, :])
            p_ref[:, c0:c1] = (av * sg * uv).astype(MXU_DTYPE)
            da = (dp * uv * sg * (1.0 + av * (1.0 - sg))).astype(MXU_DTYPE)
            du = (dp * av * sg).astype(MXU_DTYPE)
            da_ref[:, c0:c1] = da
            du_ref[:, c0:c1] = du
            dh = dh + _mm_nt(da, wg_ref[:, c0:c1]) + _mm_nt(du, wu_ref[:, c0:c1])
        dx, dgpre = _rms_bwd(dh, n1, rs1, gpre_ref[...])
        dgpre_ref[...] += dgpre
        dx1_ref[...] = dx2 + dx

    return pl.pallas_call(
        body, grid=(s // ts,),
        in_specs=[_rows(ts, D_MODEL), _rows(ts, D_MODEL), _rows(ts, D_MODEL), _rows(ts, D_FF), _rows(ts, D_FF),
                  _const((1, D_MODEL)), _const((D_MODEL, D_FF)), _const((D_MODEL, D_FF)), _const((D_FF, D_MODEL)),
                  _const((1, D_MODEL))],
        out_specs=[_rows(ts, D_MODEL), _rows(ts, D_MODEL), _rows(ts, D_MODEL), _rows(ts, D_FF), _rows(ts, D_FF),
                   _rows(ts, D_FF), _acc((1, D_MODEL)), _acc((1, D_MODEL))],
        out_shape=[_sds((s, D_MODEL)), _sds((s, D_MODEL), MXU_DTYPE), _sds((s, D_MODEL), MXU_DTYPE),
                   _sds((s, D_FF), MXU_DTYPE), _sds((s, D_FF), MXU_DTYPE), _sds((s, D_FF), MXU_DTYPE),
                   _sds((1, D_MODEL)), _sds((1, D_MODEL))],
        name="ffn_bwd", compiler_params=_cparams(("arbitrary",)))(dx2, f, x1, a, u, gpre, wg, wu, wd, gpost)


def _mix_out_bwd(dx1, m, hf, hb, of, ob, proj, g_rnn, g_gla, w_out, gpost):
    s = m.shape[0]
    ts = _tile(s, TILE_S)

    def body(dx1_ref, m_ref, hf_ref, hb_ref, gr_ref, of_ref, ob_ref, g_ref, grnn_ref, ggla_ref, w_ref, gp_ref,
             dm_ref, dhs_ref, dgr_ref, dos_ref, dg_ref, dgpost_ref, dgrnn_ref, dggla_ref):
        @pl.when(pl.program_id(0) == 0)
        def _():
            dgpost_ref[...] = jnp.zeros_like(dgpost_ref)
            dgrnn_ref[...] = jnp.zeros_like(dgrnn_ref)
            dggla_ref[...] = jnp.zeros_like(dggla_ref)

        _, nm, rsm = _rms_fwd(m_ref[...], gp_ref[...])
        dm, dgpost = _rms_bwd(dx1_ref[...], nm, rsm, gp_ref[...])
        dgpost_ref[...] += dgpost
        dm = dm.astype(MXU_DTYPE)
        dm_ref[...] = dm
        dy = _mm_nt(dm, w_ref[...])
        hs, gl, dgl, _, n_rnn, rs_rnn, sg, dsg, on, ns, rss = _mix_out_terms(
            hf_ref[...], hb_ref[...], gr_ref[...], of_ref[...], ob_ref[...], g_ref[...], grnn_ref[...], ggla_ref[...])
        dz, dgrnn = _rms_bwd(dy[:, :D_RNN], n_rnn, rs_rnn, grnn_ref[...])
        dgrnn_ref[...] += dgrnn
        dhs_ref[...] = dz * gl
        dgr_ref[...] = dz * hs * dgl
        dyg = dy[:, D_RNN:]
        dg_ref[...] = dyg * on * dsg
        don = dyg * sg
        dggla = jnp.zeros((1, GLA_DV), F32)
        for h in range(GLA_HEADS):
            ln = slice(h * LANES, (h + 1) * LANES)
            dos, dgh = _rms_bwd(don[:, ln], ns[h], rss[h], ggla_ref[...])
            dos_ref[:, ln] = dos
            dggla = dggla + dgh
        dggla_ref[...] += dggla

    return pl.pallas_call(
        body, grid=(s // ts,),
        in_specs=[_rows(ts, D_MODEL), _rows(ts, D_MODEL), _rows(ts, D_RNN), _rows(ts, D_RNN), _rows(ts, D_RNN, 1),
                  _rows(ts, 512), _rows(ts, 512), _rows(ts, 512, 5), _const((1, D_RNN)), _const((1, GLA_DV)),
                  _const((D_MODEL, D_MODEL)), _const((1, D_MODEL))],
        out_specs=[_rows(ts, D_MODEL), _rows(ts, D_RNN), _rows(ts, D_RNN), _rows(ts, 512), _rows(ts, 512),
                   _acc((1, D_MODEL)), _acc((1, D_RNN)), _acc((1, GLA_DV))],
        out_shape=[_sds((s, D_MODEL), MXU_DTYPE), _sds((s, D_RNN)), _sds((s, D_RNN)), _sds((s, 512)), _sds((s, 512)),
                   _sds((1, D_MODEL)), _sds((1, D_RNN)), _sds((1, GLA_DV))],
        name="mix_out_bwd", compiler_params=_cparams(("arbitrary",)))(
            dx1, m, hf, hb, proj, of, ob, proj, g_rnn, g_gla, w_out, gpost)


def _gla_bwd(dos, proj, st, wg, bg, reverse):
    s = proj.shape[0]
    ts = _tile(s, TILE_S)
    nt = s // ts
    ch = ts // GLA_CHUNK
    hw = GLA_HEADS * LANES
    order = (lambda i: i) if reverse else (lambda i: nt - 1 - i)

    def body(do_ref, q_ref, k_ref, v_ref, lr_ref, st_ref, wg_ref, bg_ref,
             dq_ref, dk_ref, dv_ref, dlr_ref, dwg_ref, dbg_ref, ds_scr, dz_scr):
        @pl.when(pl.program_id(0) == 0)
        def _():
            ds_scr[...] = jnp.zeros_like(ds_scr)
            dwg_ref[...] = jnp.zeros_like(dwg_ref)
            dbg_ref[...] = jnp.zeros_like(dbg_ref)

        z, la = _gla_gate(lr_ref[...], wg_ref[...], bg_ref[...])
        tri = _tri(reverse)
        tri_t = _tri(reverse, transpose=True)
        keep = tri > 0.5
        last_row = 0 if reverse else GLA_CHUNK - 1
        is_last = lax.broadcasted_iota(jnp.int32, (GLA_CHUNK, hw), 0) == last_row
        for cc in range(ch):
            c = cc if reverse else (ch - 1 - cc)
            rows = slice(c * GLA_CHUNK, (c + 1) * GLA_CHUNK)
            eb, enb, ebl, d, qe, ke, kd = _gla_chunk_terms(q_ref[rows, :], k_ref[rows, :], la[rows], tri, reverse)
            dqe, dke, dkd, dd = [], [], [], []
            for h in range(GLA_HEADS):
                ln = slice(h * LANES, (h + 1) * LANES)
                st_h = st_ref[c, h]
                dst = ds_scr[h]
                v_h = v_ref[rows, ln]
                do_h = do_ref[rows, ln]
                qe_h, ke_h, kd_h = qe[:, ln], ke[:, ln], kd[:, ln]
                a_m = jnp.where(keep, _mm_nt(qe_h, ke_h), 0.0)
                da_m = jnp.where(keep, _mm_nt(do_h, v_h), 0.0)
                dv_ref[rows, ln] = _mm_tn(a_m, do_h) + _mm_nt(kd_h, dst)
                dqe.append(_mm(da_m, ke_h) + _mm(do_h, st_h))
                dke.append(_mm_tn(da_m, qe_h))
                dkd.append(_mm(v_h, dst))
                dd.append(jnp.sum(dst * st_h, axis=0, keepdims=True))
                ds_scr[h] = _mm_tn(do_h, qe_h) + d[:, ln] * dst
            dqe = jnp.concatenate(dqe, axis=1)
            dke = jnp.concatenate(dke, axis=1)
            dkd = jnp.concatenate(dkd, axis=1)
            dd = jnp.concatenate(dd, axis=1)
            dbl = dd * d + jnp.sum(dkd * kd, axis=0, keepdims=True)
            db = dqe * qe - dke * ke - dkd * kd
            db = jnp.where(is_last, db + dbl, db)
            dq_ref[rows, :] = dqe * eb * (GLA_DK ** -0.5)
            dk_ref[rows, :] = dke * enb + dkd * ebl
            dla = _mm_exact(tri_t, db)
            dz_scr[rows, :] = dla * (1.0 / GLA_TAU) * _sigmoid(-z[rows])
        dz = dz_scr[...]
        dlr_ref[...] = _mm_nt(dz, wg_ref[...])
        dwg_ref[...] += _mm_tn(lr_ref[...], dz)
        dbg_ref[...] += jnp.sum(dz, axis=0, keepdims=True)

    return pl.pallas_call(
        body, grid=(nt,),
        in_specs=[_rows(ts, hw, 0, order), _rows(ts, hw, 2, order), _rows(ts, hw, 3, order), _rows(ts, hw, 4, order),
                  _rows(ts, LANES, COL_LR // LANES, order),
                  pl.BlockSpec((ch, GLA_HEADS, LANES, LANES), lambda i: (order(i), 0, 0, 0)),
                  _const((LANES, hw)), _const((1, hw))],
        out_specs=[_rows(ts, hw, 0, order), _rows(ts, hw, 0, order), _rows(ts, hw, 0, order),
                   _rows(ts, LANES, 0, order), _acc((LANES, hw)), _acc((1, hw))],
        out_shape=[_sds((s, hw)), _sds((s, hw)), _sds((s, hw)), _sds((s, LANES)), _sds((LANES, hw)), _sds((1, hw))],
        scratch_shapes=[pltpu.VMEM((GLA_HEADS, LANES, LANES), F32), pltpu.VMEM((ts, hw), F32)],
        name="gla_bwd_rev" if reverse else "gla_bwd",
        compiler_params=_cparams(("arbitrary",)))(dos, proj, proj, proj, proj, st, wg, bg)


def _rnn_bwd(dhs, h, proj, cw, cb, wa, ba, wx, bx, lam, reverse):
    s = proj.shape[0]
    ts = _tile(s, TILE_S)
    nt = s // ts
    order = (lambda i: i) if reverse else (lambda i: nt - 1 - i)
    back = not reverse

    def body(dh_ref, h_ref, hh_ref, xr_ref, hp_ref, hn_ref, cw_ref, cb_ref, wa_ref, ba_ref, wx_ref, bx_ref, lam_ref,
             dxc_ref, dwa_ref, dba_ref, dwx_ref, dbx_ref, dlam_ref, a_scr, u_scr, g_scr, c_scr):
        i = pl.program_id(0)
        t = order(i)

        @pl.when(i == 0)
        def _():
            c_scr[...] = jnp.zeros_like(c_scr)
            dwa_ref[...] = jnp.zeros_like(dwa_ref)
            dba_ref[...] = jnp.zeros_like(dba_ref)
            dwx_ref[...] = jnp.zeros_like(dwx_ref)
            dbx_ref[...] = jnp.zeros_like(dbx_ref)
            dlam_ref[...] = jnp.zeros_like(dlam_ref)

        xc = _conv_fwd(xr_ref[...], hp_ref[...], hn_ref[...], cw_ref[...], cb_ref[...], t == 0, t == nt - 1)
        r, gi, sp, a, mult = _rnn_gates(xc, wa_ref[...], ba_ref[...], wx_ref[...], bx_ref[...], lam_ref[...])
        row = lax.broadcasted_iota(jnp.int32, (ts, D_RNN), 0)
        hv = h_ref[...]
        if reverse:
            edge = jnp.where(t == nt - 1, 0.0, hh_ref[0:1, :])
            h_prev = jnp.where(row == ts - 1, edge, pltpu.roll(hv, ts - 1, 0))
            a_nxt = jnp.where(row == 0, 1.0, pltpu.roll(a, 1, 0))
        else:
            edge = jnp.where(t == 0, 0.0, hh_ref[7:8, :])
            h_prev = jnp.where(row == 0, edge, pltpu.roll(hv, 1, 0))
            a_nxt = jnp.where(row == ts - 1, 1.0, pltpu.roll(a, ts - 1, 0))
        a_scr[...] = a_nxt
        u_scr[...] = dh_ref[...]
        _scan_tile(a_scr, u_scr, g_scr, c_scr[0:1, :], back)
        dh = g_scr[...]
        if reverse:
            c_scr[0:1, :] = a[ts - 1:ts, :] * dh[ts - 1:ts, :]
        else:
            c_scr[0:1, :] = a[0:1, :] * dh[0:1, :]
        dmult = dh * xc * gi
        dla = dh * h_prev * a - dmult * a * a / mult
        dza = dla * (-LRU_C) * sp * r * (1.0 - r)
        dzx = dh * xc * mult * gi * (1.0 - gi)
        dsp = jnp.sum(dla * (-LRU_C) * r, axis=0, keepdims=True)
        dlam_ref[...] += dsp * (-_sigmoid(-lam_ref[...]))
        dxc_ref[...] = dh * gi * mult + _mm_nt(dza, wa_ref[...]) + _mm_nt(dzx, wx_ref[...])
        dwa_ref[...] += _mm_tn(xc, dza)
        dwx_ref[...] += _mm_tn(xc, dzx)
        dba_ref[...] += jnp.sum(dza, axis=0, keepdims=True)
        dbx_ref[...] += jnp.sum(dzx, axis=0, keepdims=True)

    hp, hn = _halo_specs(s, ts, D_RNN, 0, order)
    hhp, hhn = _halo_specs(s, ts, D_RNN, 0, order)
    sq = (D_RNN, D_RNN)
    vec = (1, D_RNN)
    return pl.pallas_call(
        body, grid=(nt,),
        in_specs=[_rows(ts, D_RNN, 0, order), _rows(ts, D_RNN, 0, order), hhn if reverse else hhp,
                  _rows(ts, D_RNN, 0, order), hp, hn, _const((CONV_WIDTH, D_RNN)), _const(vec),
                  _const(sq), _const(vec), _const(sq), _const(vec), _const(vec)],
        out_specs=[_rows(ts, D_RNN, 0, order), _acc(sq), _acc(vec), _acc(sq), _acc(vec), _acc(vec)],
        out_shape=[_sds((s, D_RNN)), _sds(sq), _sds(vec), _sds(sq), _sds(vec), _sds(vec)],
        scratch_shapes=[pltpu.VMEM((ts, D_RNN), F32), pltpu.VMEM((ts, D_RNN), F32), pltpu.VMEM((ts, D_RNN), F32),
                        pltpu.VMEM((8, D_RNN), F32)],
        name="rnn_bwd_rev" if reverse else "rnn_bwd",
        compiler_params=_cparams(("arbitrary",)))(dhs, h, h, proj, proj, proj, cw, cb, wa, ba, wx, bx, lam)


def _conv_bwd(dxc_f, dxc_b, proj, cw):
    s = proj.shape[0]
    ts = _tile(s, TILE_S)
    nt = s // ts
    ident = lambda i: i

    def body(df_ref, dfp_ref, dfn_ref, db_ref, dbp_ref, dbn_ref, xr_ref, xp_ref, xn_ref, cw_ref,
             dxr_ref, dcw_ref, dcb_ref):
        t = pl.program_id(0)

        @pl.when(t == 0)
        def _():
            dcw_ref[...] = jnp.zeros_like(dcw_ref)
            dcb_ref[...] = jnp.zeros_like(dcb_ref)

        first = t == 0
        last = t == nt - 1
        d = df_ref[...] + db_ref[...]
        d_m2, d_m1, _, d_p1 = _conv_taps(d, dfp_ref[...] + dbp_ref[...], dfn_ref[...] + dbn_ref[...], first, last)
        dn = jnp.where(last, 0.0, dfn_ref[...] + dbn_ref[...])
        d_p2 = jnp.concatenate([d, dn], axis=0)[2:2 + ts]
        del d_m2
        cw = cw_ref[...]
        dxr_ref[...] = cw[0:1] * d_p2 + cw[1:2] * d_p1 + cw[2:3] * d + cw[3:4] * d_m1
        taps = _conv_taps(xr_ref[...], xp_ref[...], xn_ref[...], first, last)
        dcw_ref[...] += jnp.concatenate([jnp.sum(d * tp, axis=0, keepdims=True) for tp in taps], axis=0)
        dcb_ref[...] += jnp.sum(d, axis=0, keepdims=True)

    hp, hn = _halo_specs(s, ts, D_RNN, 0, ident)
    return pl.pallas_call(
        body, grid=(nt,),
        in_specs=[_rows(ts, D_RNN), hp, hn, _rows(ts, D_RNN), hp, hn, _rows(ts, D_RNN), hp, hn,
                  _const((CONV_WIDTH, D_RNN))],
        out_specs=[_rows(ts, D_RNN), _acc((CONV_WIDTH, D_RNN)), _acc((1, D_RNN))],
        out_shape=[_sds((s, D_RNN)), _sds((CONV_WIDTH, D_RNN)), _sds((1, D_RNN))],
        name="conv_bwd", compiler_params=_cparams(("arbitrary",)))(
            dxc_f, dxc_f, dxc_f, dxc_b, dxc_b, dxc_b, proj, proj, proj, cw)


def _mix_in_bwd(parts, lr_parts, x, dx1, gpre, w_in_p):
    s = x.shape[0]
    ts = _tile(s, TILE_S)
    dxr, dgr, dqf, dqb, dkf, dkb, dvf, dvb, dg = parts
    dlf, dlb = lr_parts

    def body(dxr_ref, dgr_ref, dqf_ref, dqb_ref, dkf_ref, dkb_ref, dvf_ref, dvb_ref, dg_ref, dlf_ref, dlb_ref,
             x_ref, dx1_ref, g_ref, w_ref, dx_ref, dp_ref, h_ref, dgpre_ref):
        @pl.when(pl.program_id(0) == 0)
        def _():
            dgpre_ref[...] = jnp.zeros_like(dgpre_ref)

        dp = jnp.concatenate(
            [dxr_ref[...], dgr_ref[...], dqf_ref[...] + dqb_ref[...], dkf_ref[...] + dkb_ref[...],
             dvf_ref[...] + dvb_ref[...], dg_ref[...], dlf_ref[...] + dlb_ref[...]], axis=1).astype(MXU_DTYPE)
        dp_ref[...] = dp
        h, n, rs = _rms_fwd(x_ref[...], g_ref[...])
        h_ref[...] = h.astype(MXU_DTYPE)
        dh = _mm_nt(dp, w_ref[...])
        dx, dgpre = _rms_bwd(dh, n, rs, g_ref[...])
        dgpre_ref[...] += dgpre
        dx_ref[...] = dx1_ref[...] + dx

    return pl.pallas_call(
        body, grid=(s // ts,),
        in_specs=[_rows(ts, 512)] * 9 + [_rows(ts, LANES)] * 2 + [_rows(ts, D_MODEL), _rows(ts, D_MODEL),
                                                                   _const((1, D_MODEL)), _const((D_MODEL, PW))],
        out_specs=[_rows(ts, D_MODEL), _rows(ts, PW), _rows(ts, D_MODEL), _acc((1, D_MODEL))],
        out_shape=[_sds((s, D_MODEL)), _sds((s, PW), MXU_DTYPE), _sds((s, D_MODEL), MXU_DTYPE), _sds((1, D_MODEL))],
        name="mix_in_bwd", compiler_params=_cparams(("arbitrary",)))(
            dxr, dgr, dqf, dqb, dkf, dkb, dvf, dvb, dg, dlf, dlb, x, dx1, gpre, w_in_p)


def _pad_heads(w):
    sh = w.shape[:-1]
    w = w.reshape(sh + (GLA_HEADS, GLA_DK))
    w = jnp.pad(w, [(0, 0)] * (len(sh) + 1) + [(0, GLA_DKP - GLA_DK)])
    return w.reshape(sh + (GLA_HEADS * GLA_DKP,))


def _unpad_heads(w):
    sh = w.shape[:-1]
    return w.reshape(sh + (GLA_HEADS, GLA_DKP))[..., :GLA_DK].reshape(sh + (GLA_HEADS * GLA_DK,))


def _pad_w_in(w):
    lr = jnp.pad(w[:, 2560:2592], ((0, 0), (0, LANES - 2 * GLA_RANK)))
    return jnp.concatenate([w[:, :1024], _pad_heads(w[:, 1024:1280]), _pad_heads(w[:, 1280:1536]),
                            w[:, 1536:2560], lr], axis=1)


def _unpad_w_in(w):
    return jnp.concatenate([w[:, :1024], _unpad_heads(w[:, 1024:1536]), _unpad_heads(w[:, 1536:2048]),
                            w[:, 2048:3072], w[:, COL_LR:COL_LR + 2 * GLA_RANK]], axis=1)


def _block_diag(w):
    n, b, _ = w.shape
    eye = jnp.eye(n, dtype=w.dtype)
    return (w[:, :, None, :] * eye[:, None, :, None]).reshape(n * b, n * b)


def _block_diag_of(w):
    n = D_RNN // 64
    eye = jnp.eye(n, dtype=w.dtype)
    return (w.reshape(n, 64, n, 64) * eye[:, None, :, None]).sum(axis=2)


def _gate_weight(wg, direction):
    w = _pad_heads(wg)
    lo = direction * GLA_RANK
    return jnp.pad(w, ((lo, LANES - GLA_RANK - lo), (0, 0)))


def _layer_weights(full, l):
    row = lambda v: v.reshape(1, -1)
    lw = dict(
        gpre=row(full["mix_norm_pre"][l]), gpost=row(full["mix_norm_post"][l]),
        w_in=_pad_w_in(full["w_in"][l]).astype(MXU_DTYPE),
        cw=full["conv_w"][l], cb=row(full["conv_b"][l]),
        g_rnn=row(full["rnn_out_norm"][l]), g_gla=row(full["gla_out_norm"][l]),
        w_out=full["w_out"][l].astype(MXU_DTYPE),
        fpre=row(full["ffn_norm_pre"][l]), fpost=row(full["ffn_norm_post"][l]),
        wg=full["w_ffn_gate"][l].astype(MXU_DTYPE), wu=full["w_ffn_up"][l].astype(MXU_DTYPE),
        wd=full["w_ffn_down"][l].astype(MXU_DTYPE))
    for d in (0, 1):
        lw[f"wa{d}"] = _block_diag(full["lru_w_a"][l, d]).astype(MXU_DTYPE)
        lw[f"wx{d}"] = _block_diag(full["lru_w_x"][l, d]).astype(MXU_DTYPE)
        lw[f"ba{d}"] = row(full["lru_b_a"][l, d])
        lw[f"bx{d}"] = row(full["lru_b_x"][l, d])
        lw[f"lam{d}"] = row(full["lru_lambda"][l, d])
        lw[f"gw{d}"] = _gate_weight(full["gla_w_gate"][l, d], d).astype(MXU_DTYPE)
        lw[f"gb{d}"] = row(_pad_heads(full["gla_b_gate"][l, d]))
    return lw


def _layer_fwd(x, lw):
    proj = _mix_in_fwd(x, lw["gpre"], lw["w_in"])
    hs, os_, sts = [], [], []
    for d in (0, 1):
        hs.append(_rnn_fwd(proj, lw["cw"], lw["cb"], lw[f"wa{d}"], lw[f"ba{d}"], lw[f"wx{d}"], lw[f"bx{d}"],
                           lw[f"lam{d}"], bool(d)))
        o, st = _gla_fwd(proj, lw[f"gw{d}"], lw[f"gb{d}"], bool(d))
        os_.append(o)
        sts.append(st)
    x1, y, m = _mix_out_fwd(x, hs[0], hs[1], os_[0], os_[1], proj, lw["g_rnn"], lw["g_gla"], lw["w_out"], lw["gpost"])
    x2, a, u, f = _ffn_fwd(x1, lw["fpre"], lw["wg"], lw["wu"], lw["wd"], lw["fpost"])
    saved = dict(x=x, proj=proj, hs=hs, os=os_, sts=sts, y=y, m=m, x1=x1, a=a, u=u, f=f)
    return x2, saved


def _layer_bwd(dx2, sv, lw):
    g = {}
    dx1, df, h2, p, da, du, dfpost, dfpre = _ffn_bwd(dx2, sv["f"], sv["x1"], sv["a"], sv["u"], lw["fpre"], lw["wg"],
                                                     lw["wu"], lw["wd"], lw["fpost"])
    g["ffn_norm_post"], g["ffn_norm_pre"] = dfpost[0], dfpre[0]
    g["w_ffn_gate"] = _tn_matmul(h2, da, "dw_ffn_gate")
    g["w_ffn_up"] = _tn_matmul(h2, du, "dw_ffn_up")
    g["w_ffn_down"] = _tn_matmul(p, df, "dw_ffn_down")
    proj = sv["proj"]
    dm, dhs, dgr, dos, dg, dgpost, dgrnn, dggla = _mix_out_bwd(
        dx1, sv["m"], sv["hs"][0], sv["hs"][1], sv["os"][0], sv["os"][1], proj, lw["g_rnn"], lw["g_gla"],
        lw["w_out"], lw["gpost"])
    g["mix_norm_post"], g["rnn_out_norm"], g["gla_out_norm"] = dgpost[0], dgrnn[0], dggla[0]
    g["w_out"] = _tn_matmul(sv["y"], dm, "dw_out")
    dq, dk, dv, dlr, dxc = [], [], [], [], []
    gw, gb, wa, ba, wx, bx, lam = [], [], [], [], [], [], []
    for d in (0, 1):
        r = _gla_bwd(dos, proj, sv["sts"][d], lw[f"gw{d}"], lw[f"gb{d}"], bool(d))
        dq.append(r[0]); dk.append(r[1]); dv.append(r[2]); dlr.append(r[3])
        lo = d * GLA_RANK
        gw.append(_unpad_heads(r[4][lo:lo + GLA_RANK]))
        gb.append(_unpad_heads(r[5][0]))
        r = _rnn_bwd(dhs, sv["hs"][d], proj, lw["cw"], lw["cb"], lw[f"wa{d}"], lw[f"ba{d}"], lw[f"wx{d}"],
                     lw[f"bx{d}"], lw[f"lam{d}"], bool(d))
        dxc.append(r[0])
        wa.append(_block_diag_of(r[1])); ba.append(r[2][0]); wx.append(_block_diag_of(r[3])); bx.append(r[4][0])
        lam.append(r[5][0])
    g["gla_w_gate"], g["gla_b_gate"] = jnp.stack(gw), jnp.stack(gb)
    g["lru_w_a"], g["lru_b_a"] = jnp.stack(wa), jnp.stack(ba)
    g["lru_w_x"], g["lru_b_x"], g["lru_lambda"] = jnp.stack(wx), jnp.stack(bx), jnp.stack(lam)
    dxr, dcw, dcb = _conv_bwd(dxc[0], dxc[1], proj, lw["cw"])
    g["conv_w"], g["conv_b"] = dcw, dcb[0]
    dx, dproj, h, dgpre = _mix_in_bwd((dxr, dgr, dq[0], dq[1], dk[0], dk[1], dv[0], dv[1], dg), (dlr[0], dlr[1]),
                                      sv["x"], dx1, lw["gpre"], lw["w_in"])
    g["mix_norm_pre"] = dgpre[0]
    g["w_in"] = _unpad_w_in(_tn_matmul(h, dproj, "dw_in"))
    return dx, g


WEIGHT_NAMES = ["mix_norm_pre", "mix_norm_post", "w_in", "conv_w", "conv_b", "lru_w_a", "lru_b_a", "lru_w_x", "lru_b_x",
                "lru_lambda", "rnn_out_norm", "gla_w_gate", "gla_b_gate", "gla_out_norm", "w_out", "ffn_norm_pre",
                "ffn_norm_post", "w_ffn_gate", "w_ffn_up", "w_ffn_down"]


def _local_step(x, target, full):
    lws = [_layer_weights(full, l) for l in range(DEPTH)]
    saved = []
    for l in range(DEPTH):
        x, sv = _layer_fwd(x, lws[l])
        saved.append(sv)
    loss, dx = _loss_fwd_bwd(x, target)
    grads = [None] * DEPTH
    for l in reversed(range(DEPTH)):
        dx, grads[l] = _layer_bwd(dx, saved[l], lws[l])
    g = {n: jnp.stack([grads[l][n] for l in range(DEPTH)]) for n in WEIGHT_NAMES}
    return loss[0, 0], dx, g


MESH_ID = pl.DeviceIdType.MESH
ANY = pl.BlockSpec(memory_space=pl.ANY)
MESH_AXES = ("x", "y", "c")


def _all_gather(x, name):
    def body(x_ref, out_ref, send_sems, recv_sems, local_sem):
        mx, my, mc = lax.axis_index("x"), lax.axis_index("y"), lax.axis_index("c")
        me, sibling = (mx, my, mc), (mx, my, 1 - mc)
        chips = [(1 - mx, my), (mx, 1 - my), (1 - mx, 1 - my)]

        def slot(px, py, pc):
            return out_ref.at[4 * px + 2 * py + pc]

        def copy(k, block, to, src=None):
            return pltpu.make_async_remote_copy(
                src_ref=slot(*block) if src is None else src, dst_ref=slot(*block),
                send_sem=send_sems.at[k], recv_sem=recv_sems.at[k], device_id=to, device_id_type=MESH_ID)

        mine = pltpu.make_async_copy(x_ref, slot(*me), local_sem)
        mine.start()
        first = [copy(0, me, sibling, src=x_ref)]
        first += [copy(1 + j, me, (*chip, mc), src=x_ref) for j, chip in enumerate(chips)]
        for cp in first:
            cp.start()
        passed = [copy(4 + j, (*chip, mc), sibling) for j, chip in enumerate(chips)]
        for j, chip in enumerate(chips):
            copy(1 + j, (*chip, mc), me).wait_recv()
            passed[j].start()
        copy(0, sibling, me).wait_recv()
        for j, chip in enumerate(chips):
            copy(4 + j, (*chip, 1 - mc), me).wait_recv()
        for cp in first + passed:
            cp.wait_send()
        mine.wait()

    return pl.pallas_call(
        body, out_shape=_sds((N_DEV,) + x.shape, x.dtype), in_specs=[ANY], out_specs=ANY,
        scratch_shapes=[pltpu.SemaphoreType.DMA((7,)), pltpu.SemaphoreType.DMA((7,)), pltpu.SemaphoreType.DMA],
        name=name)(x)


def _all_to_all(g, name):
    def body(g_ref, out_ref, send_sems, recv_sems, local_sem):
        mx, my, mc = lax.axis_index("x"), lax.axis_index("y"), lax.axis_index("c")
        me = 4 * mx + 2 * my + mc
        mine = pltpu.make_async_copy(g_ref.at[me], out_ref.at[me], local_sem)
        mine.start()
        copies = []
        for r in range(1, N_DEV):
            px = 1 - mx if r & 4 else mx
            py = 1 - my if r & 2 else my
            pc = 1 - mc if r & 1 else mc
            cp = pltpu.make_async_remote_copy(
                src_ref=g_ref.at[4 * px + 2 * py + pc], dst_ref=out_ref.at[me],
                send_sem=send_sems.at[r - 1], recv_sem=recv_sems.at[r - 1],
                device_id=(px, py, pc), device_id_type=MESH_ID)
            cp.start()
            copies.append(cp)
        for cp in copies:
            cp.wait()
        mine.wait()

    return pl.pallas_call(
        body, out_shape=_sds(g.shape, g.dtype), in_specs=[ANY], out_specs=ANY,
        scratch_shapes=[pltpu.SemaphoreType.DMA((7,)), pltpu.SemaphoreType.DMA((7,)), pltpu.SemaphoreType.DMA],
        name=name)(g)


def _sum_adamw(parts, w, m, v, name):
    _, r, c = parts.shape
    tr = _tile(r, ADAM_TILE_ROWS)

    def body(p_ref, w_ref, m_ref, v_ref, g_ref, d_ref, m2_ref, v2_ref):
        g = p_ref[0]
        for k in range(1, N_DEV):
            g = g + p_ref[k]
        g_ref[...] = g
        m2 = ADAM_B1 * m_ref[...] + (1.0 - ADAM_B1) * g
        v2 = ADAM_B2 * v_ref[...] + (1.0 - ADAM_B2) * (g * g)
        m2_ref[...] = m2
        v2_ref[...] = v2
        m_hat = m2 / (1.0 - ADAM_B1 ** ADAM_STEP)
        v_hat = v2 / (1.0 - ADAM_B2 ** ADAM_STEP)
        d_ref[...] = -ADAM_LR * (m_hat / (jnp.sqrt(v_hat) + ADAM_EPS) + ADAM_WD * w_ref[...])

    flat = pl.BlockSpec((tr, c), lambda i: (i, 0))
    return pl.pallas_call(
        body, grid=(r // tr,),
        in_specs=[pl.BlockSpec((N_DEV, tr, c), lambda i: (0, i, 0)), flat, flat, flat],
        out_specs=[flat] * 4, out_shape=[_sds((r, c))] * 4,
        name=name, compiler_params=_cparams(("parallel",)))(parts, w, m, v)


SHARDED = [("w_in", 2), ("w_out", 1), ("w_ffn_gate", 2), ("w_ffn_up", 2), ("w_ffn_down", 1),
           ("conv_w", 2), ("lru_b_a", 2), ("lru_b_x", 2), ("lru_lambda", 2), ("gla_w_gate", 3), ("gla_b_gate", 2)]
N_MATMUL_WEIGHTS = 5
REPLICATED = ["mix_norm_pre", "mix_norm_post", "conv_b", "lru_w_a", "lru_w_x", "rnn_out_norm", "gla_out_norm",
              "ffn_norm_pre", "ffn_norm_post"]


def _pack(arrays, cols, row_mult):
    flat = jnp.concatenate([a.reshape(-1) for a in arrays])
    unit = cols * row_mult
    total = -(-flat.shape[0] // unit) * unit
    return jnp.pad(flat, (0, total - flat.shape[0])).reshape(total // cols, cols)


def _pack_slots(arrays, cols, row_mult):
    flat = jnp.concatenate([a.reshape(N_DEV, -1) for a in arrays], axis=1)
    unit = cols * row_mult
    total = -(-flat.shape[1] // unit) * unit
    return jnp.pad(flat, ((0, 0), (0, total - flat.shape[1]))).reshape(N_DEV, total // cols, cols)


def _unpack(flat, shapes):
    flat = flat.reshape(-1)
    out, off = [], 0
    for sh in shapes:
        n = 1
        for d in sh:
            n *= d
        out.append(flat[off:off + n].reshape(sh))
        off += n
    return out


def _unpack_slots(flat, shapes):
    flat = flat.reshape(N_DEV, -1)
    out, off = [], 0
    for sh in shapes:
        n = 1
        for d in sh:
            n *= d
        out.append(flat[:, off:off + n].reshape((N_DEV,) + tuple(sh)))
        off += n
    return out


def _merge_shards(a, axis):
    a = jnp.moveaxis(a, 0, axis)
    sh = a.shape
    return a.reshape(sh[:axis] + (sh[axis] * sh[axis + 1],) + sh[axis + 2:])


def _split_shards(a, axis):
    sh = a.shape
    a = a.reshape(sh[:axis] + (N_DEV, sh[axis] // N_DEV) + sh[axis + 1:])
    return jnp.moveaxis(a, axis, 0)


def kernel(x, mix_norm_pre, mix_norm_post, w_in, conv_w, conv_b, lru_w_a, lru_b_a, lru_w_x, lru_b_x, lru_lambda, rnn_out_norm, gla_w_gate, gla_b_gate, gla_out_norm, w_out, ffn_norm_pre, ffn_norm_post, w_ffn_gate, w_ffn_up, w_ffn_down, loss_target, m_mix_norm_pre, m_mix_norm_post, m_w_in, m_conv_w, m_conv_b, m_lru_w_a, m_lru_b_a, m_lru_w_x, m_lru_b_x, m_lru_lambda, m_rnn_out_norm, m_gla_w_gate, m_gla_b_gate, m_gla_out_norm, m_w_out, m_ffn_norm_pre, m_ffn_norm_post, m_w_ffn_gate, m_w_ffn_up, m_w_ffn_down, v_mix_norm_pre, v_mix_norm_post, v_w_in, v_conv_w, v_conv_b, v_lru_w_a, v_lru_b_a, v_lru_w_x, v_lru_b_x, v_lru_lambda, v_rnn_out_norm, v_gla_w_gate, v_gla_b_gate, v_gla_out_norm, v_w_out, v_ffn_norm_pre, v_ffn_norm_post, v_w_ffn_gate, v_w_ffn_up, v_w_ffn_down):
    args = dict(locals())
    w = {n: args[n] for n in WEIGHT_NAMES}
    m = {n: args["m_" + n] for n in WEIGHT_NAMES}
    v = {n: args["v_" + n] for n in WEIGHT_NAMES}
    names_s = [n for n, _ in SHARDED]
    axis_s = dict(SHARDED)
    shapes_s = [w[n].shape for n in names_s]

    big = _pack([w[n].astype(MXU_DTYPE) for n in names_s[:N_MATMUL_WEIGHTS]], ADAM_COLS, 16)
    small = _pack([w[n] for n in names_s[N_MATMUL_WEIGHTS:]], LANES, 8)
    big_all = _unpack_slots(_all_gather(big, "gather_matmul_weights"), shapes_s[:N_MATMUL_WEIGHTS])
    small_all = _unpack_slots(_all_gather(small, "gather_small_weights"), shapes_s[N_MATMUL_WEIGHTS:])
    full = {n: w[n] for n in REPLICATED}
    for n, a in zip(names_s, big_all + small_all):
        full[n] = _merge_shards(a, axis_s[n])

    loss, dx, g = _local_step(x[0], loss_target[0], full)
    loss = lax.psum(loss, MESH_AXES)

    g_slots = _pack_slots([_split_shards(g[n], axis_s[n]) for n in names_s], ADAM_COLS, ADAM_TILE_ROWS)
    g_recv = _all_to_all(g_slots, "exchange_sharded_grads")
    packed = [_pack([t[n] for n in names_s], ADAM_COLS, ADAM_TILE_ROWS) for t in (w, m, v)]
    res_s = [_unpack(r, shapes_s) for r in _sum_adamw(g_recv, *packed, "adamw_sharded")]

    shapes_r = [w[n].shape for n in REPLICATED]
    g_rep = _all_gather(_pack([g[n] for n in REPLICATED], ADAM_COLS, ADAM_TILE_ROWS), "gather_replicated_grads")
    packed = [_pack([t[n] for n in REPLICATED], ADAM_COLS, ADAM_TILE_ROWS) for t in (w, m, v)]
    res_r = [_unpack(r, shapes_r) for r in _sum_adamw(g_rep, *packed, "adamw_replicated")]

    outs = []
    for k in range(4):
        by_name = dict(zip(names_s, res_s[k]))
        by_name.update(zip(REPLICATED, res_r[k]))
        outs.append([by_name[n] for n in WEIGHT_NAMES])
    return (loss, dx[None], *outs[0], *outs[1], *outs[2], *outs[3])
```

```python
import functools

import jax
import jax.numpy as jnp
from jax import lax
from jax.experimental import pallas as pl
from jax.experimental.pallas import tpu as pltpu

F32 = jnp.float32
MXU_DTYPE = jnp.bfloat16

N_DEV = 8
D_MODEL = 1024
D_RNN = 512
CONV_WIDTH = 4
LRU_C = 8.0
GLA_HEADS = 4
GLA_DK = 64
GLA_DKP = 128
GLA_DV = 128
GLA_RANK = 16
GLA_TAU = 16.0
GLA_CHUNK = 64
D_FF = 2816
RMS_EPS = 1e-6
DEPTH = 4

PW = 3200
COL_LR = 3072
LANES = 128

TILE_S = 512
TILE_F = 256
F_CHUNK = 1408
VMEM_LIMIT = 56 * 1024 * 1024

ADAM_LR = 0.001
ADAM_B1 = 0.9
ADAM_B2 = 0.999
ADAM_EPS = 1e-08
ADAM_WD = 0.01
ADAM_STEP = 10

ADAM_TILE_ROWS = 256
ADAM_COLS = 1024


def _mm(a, b):
    return jnp.dot(a.astype(MXU_DTYPE), b.astype(MXU_DTYPE), preferred_element_type=F32)


def _mm_nt(a, b):
    return lax.dot_general(a.astype(MXU_DTYPE), b.astype(MXU_DTYPE), (((1,), (1,)), ((), ())),
                           preferred_element_type=F32)


def _mm_tn(a, b):
    return lax.dot_general(a.astype(MXU_DTYPE), b.astype(MXU_DTYPE), (((0,), (0,)), ((), ())),
                           preferred_element_type=F32)


def _mm_exact(a, b):
    return jnp.dot(a, b, precision=lax.Precision.HIGHEST, preferred_element_type=F32)


def _sigmoid(x):
    return 1.0 / (1.0 + jnp.exp(-x))


def _log1p_pos(e):
    series = e * (1.0 - e * (0.5 - e * (1.0 / 3.0 - e * 0.25)))
    return jnp.where(e < 0.01, series, jnp.log(1.0 + e))


def _softplus(x):
    return jnp.maximum(x, 0.0) + _log1p_pos(jnp.exp(-jnp.abs(x)))


def _expm1(x):
    series = x * (1.0 + x * (0.5 + x * (1.0 / 6.0 + x * (1.0 / 24.0))))
    return jnp.where(jnp.abs(x) < 0.05, series, jnp.exp(x) - 1.0)


GELU_C = 0.7978845608028654
GELU_K = 0.044715


def _gelu_and_grad(x):
    t = jnp.tanh(GELU_C * (x + GELU_K * x * x * x))
    y = 0.5 * x * (1.0 + t)
    dy = 0.5 * (1.0 + t) + 0.5 * x * (1.0 - t * t) * GELU_C * (1.0 + 3.0 * GELU_K * x * x)
    return y, dy


def _rms_fwd(x, g):
    rs = lax.rsqrt(jnp.mean(x * x, axis=-1, keepdims=True) + RMS_EPS)
    n = x * rs
    return n * g, n, rs


def _rms_bwd(dy, n, rs, g):
    dn = dy * g
    dx = rs * (dn - n * jnp.mean(dn * n, axis=-1, keepdims=True))
    dg = jnp.sum(dy * n, axis=0, keepdims=True)
    return dx, dg


def _cparams(sem=None):
    kw = dict(vmem_limit_bytes=VMEM_LIMIT)
    if sem is not None:
        kw["dimension_semantics"] = sem
    return pltpu.CompilerParams(**kw)


def _tile(n, pref):
    return pref if n % pref == 0 else n


def _const(shape):
    nd = len(shape)
    return pl.BlockSpec(shape, lambda *_: (0,) * nd, pipeline_mode=pl.Buffered(1))


def _acc(shape):
    nd = len(shape)
    return pl.BlockSpec(shape, lambda *_: (0,) * nd)


def _rows(ts, w, col=0, order=None):
    if order is None:
        return pl.BlockSpec((ts, w), lambda i: (i, col))
    return pl.BlockSpec((ts, w), lambda i: (order(i), col))


def _sds(shape, dtype=F32):
    return jax.ShapeDtypeStruct(shape, dtype)


def _mix_in_fwd(x, gpre, w_in_p):
    s = x.shape[0]
    ts = _tile(s, TILE_S)

    def body(x_ref, g_ref, w_ref, o_ref):
        h, _, _ = _rms_fwd(x_ref[...], g_ref[...])
        o_ref[...] = _mm(h, w_ref[...])

    return pl.pallas_call(
        body, grid=(s // ts,),
        in_specs=[_rows(ts, D_MODEL), _const((1, D_MODEL)), _const((D_MODEL, PW))],
        out_specs=_rows(ts, PW), out_shape=_sds((s, PW)),
        name="mix_in_fwd", compiler_params=_cparams(("parallel",)))(x, gpre, w_in_p)


def _conv_taps(xr, hp, hn, first, last):
    ts = xr.shape[0]
    hp = jnp.where(first, 0.0, hp)
    hn = jnp.where(last, 0.0, hn)
    xe = jnp.concatenate([hp, xr, hn], axis=0)
    return xe[6:6 + ts], xe[7:7 + ts], xr, xe[9:9 + ts]


def _conv_fwd(xr, hp, hn, cw, cb, first, last):
    t0, t1, t2, t3 = _conv_taps(xr, hp, hn, first, last)
    return cw[0:1] * t0 + cw[1:2] * t1 + cw[2:3] * t2 + cw[3:4] * t3 + cb


def _rnn_gates(xc, wa, ba, wx, bx, lam):
    r = _sigmoid(_mm(xc, wa) + ba)
    i = _sigmoid(_mm(xc, wx) + bx)
    sp = _softplus(-lam)
    la = (-LRU_C) * r * sp
    a = jnp.exp(la)
    mult = jnp.sqrt(-_expm1(2.0 * la))
    return r, i, sp, a, mult


def _scan_tile(a_scr, u_scr, h_ref, c0, reverse):
    ts = a_scr.shape[0]
    a = a_scr[...]
    u = u_scr[...]
    row = lax.broadcasted_iota(jnp.int32, a.shape, 0) % 8
    for k in (1, 2, 4):
        if reverse:
            a_sh = pltpu.roll(a, ts - k, 0)
            u_sh = pltpu.roll(u, ts - k, 0)
            ok = row < 8 - k
        else:
            a_sh = pltpu.roll(a, k, 0)
            u_sh = pltpu.roll(u, k, 0)
            ok = row >= k
        u = jnp.where(ok, u + a * u_sh, u)
        a = jnp.where(ok, a * a_sh, a)
    a_scr[...] = a
    u_scr[...] = u
    ng = ts // 8

    def body(j, c):
        g = (ng - 1 - j) if reverse else j
        sl = pl.ds(pl.multiple_of(g * 8, 8), 8)
        hh = u_scr[sl, :] + a_scr[sl, :] * c
        h_ref[sl, :] = hh
        return hh[0:1, :] if reverse else hh[7:8, :]

    return lax.fori_loop(0, ng, body, c0)


def _halo_specs(s, ts, w, col, order):
    n8 = s // 8
    per = ts // 8
    prev = pl.BlockSpec((8, w), lambda i: (jnp.maximum(order(i) * per - 1, 0), col))
    nxt = pl.BlockSpec((8, w), lambda i: (jnp.minimum((order(i) + 1) * per, n8 - 1), col))
    return prev, nxt


def _rnn_fwd(proj, cw, cb, wa, ba, wx, bx, lam, reverse):
    s = proj.shape[0]
    ts = _tile(s, TILE_S)
    nt = s // ts
    order = (lambda i: nt - 1 - i) if reverse else (lambda i: i)

    def body(xr_ref, hp_ref, hn_ref, cw_ref, cb_ref, wa_ref, ba_ref, wx_ref, bx_ref, lam_ref,
             h_ref, a_scr, u_scr, c_scr):
        i = pl.program_id(0)
        t = order(i)

        @pl.when(i == 0)
        def _():
            c_scr[...] = jnp.zeros_like(c_scr)

        xc = _conv_fwd(xr_ref[...], hp_ref[...], hn_ref[...], cw_ref[...], cb_ref[...], t == 0, t == nt - 1)
        _, gi, _, a, mult = _rnn_gates(xc, wa_ref[...], ba_ref[...], wx_ref[...], bx_ref[...], lam_ref[...])
        a_scr[...] = a
        u_scr[...] = xc * gi * mult
        c_scr[0:1, :] = _scan_tile(a_scr, u_scr, h_ref, c_scr[0:1, :], reverse)

    hp, hn = _halo_specs(s, ts, D_RNN, 0, order)
    return pl.pallas_call(
        body, grid=(nt,),
        in_specs=[_rows(ts, D_RNN, 0, order), hp, hn, _const((CONV_WIDTH, D_RNN)), _const((1, D_RNN)),
                  _const((D_RNN, D_RNN)), _const((1, D_RNN)), _const((D_RNN, D_RNN)), _const((1, D_RNN)),
                  _const((1, D_RNN))],
        out_specs=_rows(ts, D_RNN, 0, order), out_shape=_sds((s, D_RNN)),
        scratch_shapes=[pltpu.VMEM((ts, D_RNN), F32), pltpu.VMEM((ts, D_RNN), F32), pltpu.VMEM((8, D_RNN), F32)],
        name="rnn_fwd_rev" if reverse else "rnn_fwd",
        compiler_params=_cparams(("arbitrary",)))(proj, proj, proj, cw, cb, wa, ba, wx, bx, lam)


def _tri(reverse, transpose=False):
    r = lax.broadcasted_iota(jnp.int32, (GLA_CHUNK, GLA_CHUNK), 0)
    c = lax.broadcasted_iota(jnp.int32, (GLA_CHUNK, GLA_CHUNK), 1)
    if transpose:
        r, c = c, r
    return ((r <= c) if reverse else (r >= c)).astype(F32)


def _gla_chunk_terms(q, k, la, tri, reverse):
    b = _mm_exact(tri, la)
    bl = b[0:1] if reverse else b[GLA_CHUNK - 1:GLA_CHUNK]
    eb = jnp.exp(b)
    enb = jnp.exp(-b)
    ebl = jnp.exp(bl - b)
    d = jnp.exp(bl)
    return eb, enb, ebl, d, q * (GLA_DK ** -0.5) * eb, k * enb, k * ebl


def _gla_gate(lr, wg, bg):
    z = _mm(lr, wg) + bg
    return z, -_softplus(-z) * (1.0 / GLA_TAU)


def _gla_fwd(proj, wg, bg, reverse):
    s = proj.shape[0]
    ts = _tile(s, TILE_S)
    nt = s // ts
    ch = ts // GLA_CHUNK
    hw = GLA_HEADS * LANES
    order = (lambda i: nt - 1 - i) if reverse else (lambda i: i)

    def body(q_ref, k_ref, v_ref, lr_ref, wg_ref, bg_ref, o_ref, st_ref, s_scr):
        @pl.when(pl.program_id(0) == 0)
        def _():
            s_scr[...] = jnp.zeros_like(s_scr)

        _, la = _gla_gate(lr_ref[...], wg_ref[...], bg_ref[...])
        tri = _tri(reverse)
        keep = tri > 0.5
        for cc in range(ch):
            c = (ch - 1 - cc) if reverse else cc
            rows = slice(c * GLA_CHUNK, (c + 1) * GLA_CHUNK)
            _, _, _, d, qe, ke, kd = _gla_chunk_terms(q_ref[rows, :], k_ref[rows, :], la[rows], tri, reverse)
            outs = []
            for h in range(GLA_HEADS):
                ln = slice(h * LANES, (h + 1) * LANES)
                st = s_scr[h]
                st_ref[c, h] = st
                v_h = v_ref[rows, ln]
                a_m = jnp.where(keep, _mm_nt(qe[:, ln], ke[:, ln]), 0.0)
                outs.append(_mm(a_m, v_h) + _mm_nt(qe[:, ln], st))
                s_scr[h] = d[:, ln] * st + _mm_tn(v_h, kd[:, ln])
            o_ref[rows, :] = jnp.concatenate(outs, axis=1)

    return pl.pallas_call(
        body, grid=(nt,),
        in_specs=[_rows(ts, hw, 2, order), _rows(ts, hw, 3, order), _rows(ts, hw, 4, order),
                  _rows(ts, LANES, COL_LR // LANES, order), _const((LANES, hw)), _const((1, hw))],
        out_specs=[_rows(ts, hw, 0, order),
                   pl.BlockSpec((ch, GLA_HEADS, LANES, LANES), lambda i: (order(i), 0, 0, 0))],
        out_shape=[_sds((s, hw)), _sds((s // GLA_CHUNK, GLA_HEADS, LANES, LANES))],
        scratch_shapes=[pltpu.VMEM((GLA_HEADS, LANES, LANES), F32)],
        name="gla_fwd_rev" if reverse else "gla_fwd",
        compiler_params=_cparams(("arbitrary",)))(proj, proj, proj, proj, wg, bg)


def _mix_out_terms(hf, hb, gate_r, of, ob, g, g_rnn, g_gla):
    hs = hf + hb
    gl, dgl = _gelu_and_grad(gate_r)
    z = hs * gl
    y_rnn, n_rnn, rs_rnn = _rms_fwd(z, g_rnn)
    osum = of + ob
    sg_lin = _sigmoid(g)
    sg = g * sg_lin
    dsg = sg_lin * (1.0 + g * (1.0 - sg_lin))
    ons, ns, rss = [], [], []
    for h in range(GLA_HEADS):
        ln = slice(h * LANES, (h + 1) * LANES)
        on, n, rs = _rms_fwd(osum[:, ln], g_gla)
        ons.append(on)
        ns.append(n)
        rss.append(rs)
    on = jnp.concatenate(ons, axis=1)
    return hs, gl, dgl, y_rnn, n_rnn, rs_rnn, sg, dsg, on, ns, rss


def _mix_out_fwd(x, hf, hb, of, ob, proj, g_rnn, g_gla, w_out, gpost):
    s = x.shape[0]
    ts = _tile(s, TILE_S)

    def body(x_ref, hf_ref, hb_ref, gr_ref, of_ref, ob_ref, g_ref, grnn_ref, ggla_ref, w_ref, gp_ref,
             x1_ref, y_ref, m_ref):
        _, _, _, y_rnn, _, _, sg, _, on, _, _ = _mix_out_terms(
            hf_ref[...], hb_ref[...], gr_ref[...], of_ref[...], ob_ref[...], g_ref[...], grnn_ref[...], ggla_ref[...])
        y = jnp.concatenate([y_rnn, on * sg], axis=1).astype(MXU_DTYPE)
        y_ref[...] = y
        m = _mm(y, w_ref[...])
        m_ref[...] = m
        out, _, _ = _rms_fwd(m, gp_ref[...])
        x1_ref[...] = x_ref[...] + out

    return pl.pallas_call(
        body, grid=(s // ts,),
        in_specs=[_rows(ts, D_MODEL), _rows(ts, D_RNN), _rows(ts, D_RNN), _rows(ts, D_RNN, 1), _rows(ts, 512),
                  _rows(ts, 512), _rows(ts, 512, 5), _const((1, D_RNN)), _const((1, GLA_DV)),
                  _const((D_MODEL, D_MODEL)), _const((1, D_MODEL))],
        out_specs=[_rows(ts, D_MODEL), _rows(ts, D_MODEL), _rows(ts, D_MODEL)],
        out_shape=[_sds((s, D_MODEL)), _sds((s, D_MODEL), MXU_DTYPE), _sds((s, D_MODEL))],
        name="mix_out_fwd", compiler_params=_cparams(("parallel",)))(
            x, hf, hb, proj, of, ob, proj, g_rnn, g_gla, w_out, gpost)


def _f_chunks():
    return [(c0, min(c0 + F_CHUNK, D_FF)) for c0 in range(0, D_FF, F_CHUNK)]


def _ffn_fwd(x1, gpre, wg, wu, wd, gpost):
    s = x1.shape[0]
    ts = _tile(s, TILE_F)

    def body(x_ref, gpre_ref, wg_ref, wu_ref, wd_ref, gpost_ref, x2_ref, a_ref, u_ref, f_ref):
        x = x_ref[...]
        h, _, _ = _rms_fwd(x, gpre_ref[...])
        h = h.astype(MXU_DTYPE)
        f = jnp.zeros((ts, D_MODEL), F32)
        for c0, c1 in _f_chunks():
            a = _mm(h, wg_ref[:, c0:c1])
            u = _mm(h, wu_ref[:, c0:c1])
            a_ref[:, c0:c1] = a.astype(MXU_DTYPE)
            u_ref[:, c0:c1] = u.astype(MXU_DTYPE)
            f = f + _mm(a * _sigmoid(a) * u, wd_ref[c0:c1, :])
        f_ref[...] = f
        out, _, _ = _rms_fwd(f, gpost_ref[...])
        x2_ref[...] = x + out

    return pl.pallas_call(
        body, grid=(s // ts,),
        in_specs=[_rows(ts, D_MODEL), _const((1, D_MODEL)), _const((D_MODEL, D_FF)), _const((D_MODEL, D_FF)),
                  _const((D_FF, D_MODEL)), _const((1, D_MODEL))],
        out_specs=[_rows(ts, D_MODEL), _rows(ts, D_FF), _rows(ts, D_FF), _rows(ts, D_MODEL)],
        out_shape=[_sds((s, D_MODEL)), _sds((s, D_FF), MXU_DTYPE), _sds((s, D_FF), MXU_DTYPE), _sds((s, D_MODEL))],
        name="ffn_fwd", compiler_params=_cparams(("parallel",)))(x1, gpre, wg, wu, wd, gpost)


def _loss_fwd_bwd(y, target):
    s = y.shape[0]
    ts = _tile(s, TILE_S)

    def body(y_ref, t_ref, loss_ref, dy_ref):
        @pl.when(pl.program_id(0) == 0)
        def _():
            loss_ref[...] = jnp.zeros_like(loss_ref)

        e = y_ref[...] - t_ref[...]
        dy_ref[...] = e * (1.0 / D_MODEL)
        part = jnp.sum(jnp.sum(e * e, axis=1, keepdims=True), axis=0, keepdims=True) * (0.5 / D_MODEL)
        loss_ref[...] += jnp.broadcast_to(part, loss_ref.shape)

    return pl.pallas_call(
        body, grid=(s // ts,),
        in_specs=[_rows(ts, D_MODEL), _rows(ts, D_MODEL)],
        out_specs=[_acc((8, LANES)), _rows(ts, D_MODEL)],
        out_shape=[_sds((8, LANES)), _sds((s, D_MODEL))],
        name="loss", compiler_params=_cparams(("arbitrary",)))(y, target)


def _tn_matmul(a, b, name):
    s, k = a.shape
    n = b.shape[1]
    ts = _tile(s, TILE_S)
    tn = n
    while k * tn * 4 > 6 * 1024 * 1024 and tn % 256 == 0:
        tn //= 2
    ns = s // ts

    def body(a_ref, b_ref, o_ref, acc):
        i = pl.program_id(1)

        @pl.when(i == 0)
        def _():
            acc[...] = jnp.zeros_like(acc)

        acc[...] += _mm_tn(a_ref[...], b_ref[...])

        @pl.when(i == ns - 1)
        def _():
            o_ref[...] = acc[...].astype(o_ref.dtype)

    return pl.pallas_call(
        body, grid=(n // tn, ns),
        in_specs=[pl.BlockSpec((ts, k), lambda j, i: (i, 0)), pl.BlockSpec((ts, tn), lambda j, i: (i, j))],
        out_specs=pl.BlockSpec((k, tn), lambda j, i: (0, j)), out_shape=_sds((k, n), MXU_DTYPE),
        scratch_shapes=[pltpu.VMEM((k, tn), F32)],
        name=name, compiler_params=_cparams(("parallel", "arbitrary")))(a, b)


def _ffn_bwd(dx2, f, x1, a, u, gpre, wg, wu, wd, gpost):
    s = x1.shape[0]
    ts = _tile(s, TILE_F)

    def body(dx2_ref, f_ref, x1_ref, a_ref, u_ref, gpre_ref, wg_ref, wu_ref, wd_ref, gpost_ref,
             dx1_ref, df_ref, h_ref, p_ref, da_ref, du_ref, dgpost_ref, dgpre_ref):
        @pl.when(pl.program_id(0) == 0)
        def _():
            dgpost_ref[...] = jnp.zeros_like(dgpost_ref)
            dgpre_ref[...] = jnp.zeros_like(dgpre_ref)

        dx2 = dx2_ref[...]
        _, nf, rsf = _rms_fwd(f_ref[...], gpost_ref[...])
        df, dgpost = _rms_bwd(dx2, nf, rsf, gpost_ref[...])
        dgpost_ref[...] += dgpost
        df = df.astype(MXU_DTYPE)
        df_ref[...] = df
        h, n1, rs1 = _rms_fwd(x1_ref[...], gpre_ref[...])
        h_ref[...] = h.astype(MXU_DTYPE)
        dh = jnp.zeros((ts, D_MODEL), F32)
        for c0, c1 in _f_chunks():
            av = a_ref[:, c0:c1].astype(F32)
            uv = u_ref[:, c0:c1].astype(F32)
            sg = _sigmoid(av)
            dp = _mm_nt(df, wd_ref[c0:c1, :])
            p_ref[:, c0:c1] = (av * sg * uv).astype(MXU_DTYPE)
            da = (dp * uv * sg * (1.0 + av * (1.0 - sg))).astype(MXU_DTYPE)
            du = (dp * av * sg).astype(MXU_DTYPE)
            da_ref[:, c0:c1] = da
            du_ref[:, c0:c1] = du
            dh = dh + _mm_nt(da, wg_ref[:, c0:c1]) + _mm_nt(du, wu_ref[:, c0:c1])
        dx, dgpre = _rms_bwd(dh, n1, rs1, gpre_ref[...])
        dgpre_ref[...] += dgpre
        dx1_ref[...] = dx2 + dx

    return pl.pallas_call(
        body, grid=(s // ts,),
        in_specs=[_rows(ts, D_MODEL), _rows(ts, D_MODEL), _rows(ts, D_MODEL), _rows(ts, D_FF), _rows(ts, D_FF),
                  _const((1, D_MODEL)), _const((D_MODEL, D_FF)), _const((D_MODEL, D_FF)), _const((D_FF, D_MODEL)),
                  _const((1, D_MODEL))],
        out_specs=[_rows(ts, D_MODEL), _rows(ts, D_MODEL), _rows(ts, D_MODEL), _rows(ts, D_FF), _rows(ts, D_FF),
                   _rows(ts, D_FF), _acc((1, D_MODEL)), _acc((1, D_MODEL))],
        out_shape=[_sds((s, D_MODEL)), _sds((s, D_MODEL), MXU_DTYPE), _sds((s, D_MODEL), MXU_DTYPE),
                   _sds((s, D_FF), MXU_DTYPE), _sds((s, D_FF), MXU_DTYPE), _sds((s, D_FF), MXU_DTYPE),
                   _sds((1, D_MODEL)), _sds((1, D_MODEL))],
        name="ffn_bwd", compiler_params=_cparams(("arbitrary",)))(dx2, f, x1, a, u, gpre, wg, wu, wd, gpost)


def _mix_out_bwd(dx1, m, hf, hb, of, ob, proj, g_rnn, g_gla, w_out, gpost):
    s = m.shape[0]
    ts = _tile(s, TILE_S)

    def body(dx1_ref, m_ref, hf_ref, hb_ref, gr_ref, of_ref, ob_ref, g_ref, grnn_ref, ggla_ref, w_ref, gp_ref,
             dm_ref, dhs_ref, dgr_ref, dos_ref, dg_ref, dgpost_ref, dgrnn_ref, dggla_ref):
        @pl.when(pl.program_id(0) == 0)
        def _():
            dgpost_ref[...] = jnp.zeros_like(dgpost_ref)
            dgrnn_ref[...] = jnp.zeros_like(dgrnn_ref)
            dggla_ref[...] = jnp.zeros_like(dggla_ref)

        _, nm, rsm = _rms_fwd(m_ref[...], gp_ref[...])
        dm, dgpost = _rms_bwd(dx1_ref[...], nm, rsm, gp_ref[...])
        dgpost_ref[...] += dgpost
        dm = dm.astype(MXU_DTYPE)
        dm_ref[...] = dm
        dy = _mm_nt(dm, w_ref[...])
        hs, gl, dgl, _, n_rnn, rs_rnn, sg, dsg, on, ns, rss = _mix_out_terms(
            hf_ref[...], hb_ref[...], gr_ref[...], of_ref[...], ob_ref[...], g_ref[...], grnn_ref[...], ggla_ref[...])
        dz, dgrnn = _rms_bwd(dy[:, :D_RNN], n_rnn, rs_rnn, grnn_ref[...])
        dgrnn_ref[...] += dgrnn
        dhs_ref[...] = dz * gl
        dgr_ref[...] = dz * hs * dgl
        dyg = dy[:, D_RNN:]
        dg_ref[...] = dyg * on * dsg
        don = dyg * sg
        dggla = jnp.zeros((1, GLA_DV), F32)
        for h in range(GLA_HEADS):
            ln = slice(h * LANES, (h + 1) * LANES)
            dos, dgh = _rms_bwd(don[:, ln], ns[h], rss[h], ggla_ref[...])
            dos_ref[:, ln] = dos
            dggla = dggla + dgh
        dggla_ref[...] += dggla

    return pl.pallas_call(
        body, grid=(s // ts,),
        in_specs=[_rows(ts, D_MODEL), _rows(ts, D_MODEL), _rows(ts, D_RNN), _rows(ts, D_RNN), _rows(ts, D_RNN, 1),
                  _rows(ts, 512), _rows(ts, 512), _rows(ts, 512, 5), _const((1, D_RNN)), _const((1, GLA_DV)),
                  _const((D_MODEL, D_MODEL)), _const((1, D_MODEL))],
        out_specs=[_rows(ts, D_MODEL), _rows(ts, D_RNN), _rows(ts, D_RNN), _rows(ts, 512), _rows(ts, 512),
                   _acc((1, D_MODEL)), _acc((1, D_RNN)), _acc((1, GLA_DV))],
        out_shape=[_sds((s, D_MODEL), MXU_DTYPE), _sds((s, D_RNN)), _sds((s, D_RNN)), _sds((s, 512)), _sds((s, 512)),
                   _sds((1, D_MODEL)), _sds((1, D_RNN)), _sds((1, GLA_DV))],
        name="mix_out_bwd", compiler_params=_cparams(("arbitrary",)))(
            dx1, m, hf, hb, proj, of, ob, proj, g_rnn, g_gla, w_out, gpost)


def _gla_bwd(dos, proj, st, wg, bg, reverse):
    s = proj.shape[0]
    ts = _tile(s, TILE_S)
    nt = s // ts
    ch = ts // GLA_CHUNK
    hw = GLA_HEADS * LANES
    order = (lambda i: i) if reverse else (lambda i: nt - 1 - i)

    def body(do_ref, q_ref, k_ref, v_ref, lr_ref, st_ref, wg_ref, bg_ref,
             dq_ref, dk_ref, dv_ref, dlr_ref, dwg_ref, dbg_ref, ds_scr, dz_scr):
        @pl.when(pl.program_id(0) == 0)
        def _():
            ds_scr[...] = jnp.zeros_like(ds_scr)
            dwg_ref[...] = jnp.zeros_like(dwg_ref)
            dbg_ref[...] = jnp.zeros_like(dbg_ref)

        z, la = _gla_gate(lr_ref[...], wg_ref[...], bg_ref[...])
        tri = _tri(reverse)
        tri_t = _tri(reverse, transpose=True)
        keep = tri > 0.5
        last_row = 0 if reverse else GLA_CHUNK - 1
        is_last = lax.broadcasted_iota(jnp.int32, (GLA_CHUNK, hw), 0) == last_row
        for cc in range(ch):
            c = cc if reverse else (ch - 1 - cc)
            rows = slice(c * GLA_CHUNK, (c + 1) * GLA_CHUNK)
            eb, enb, ebl, d, qe, ke, kd = _gla_chunk_terms(q_ref[rows, :], k_ref[rows, :], la[rows], tri, reverse)
            dqe, dke, dkd, dd = [], [], [], []
            for h in range(GLA_HEADS):
                ln = slice(h * LANES, (h + 1) * LANES)
                st_h = st_ref[c, h]
                dst = ds_scr[h]
                v_h = v_ref[rows, ln]
                do_h = do_ref[rows, ln]
                qe_h, ke_h, kd_h = qe[:, ln], ke[:, ln], kd[:, ln]
                a_m = jnp.where(keep, _mm_nt(qe_h, ke_h), 0.0)
                da_m = jnp.where(keep, _mm_nt(do_h, v_h), 0.0)
                dv_ref[rows, ln] = _mm_tn(a_m, do_h) + _mm_nt(kd_h, dst)
                dqe.append(_mm(da_m, ke_h) + _mm(do_h, st_h))
                dke.append(_mm_tn(da_m, qe_h))
                dkd.append(_mm(v_h, dst))
                dd.append(jnp.sum(dst * st_h, axis=0, keepdims=True))
                ds_scr[h] = _mm_tn(do_h, qe_h) + d[:, ln] * dst
            dqe = jnp.concatenate(dqe, axis=1)
            dke = jnp.concatenate(dke, axis=1)
            dkd = jnp.concatenate(dkd, axis=1)
            dd = jnp.concatenate(dd, axis=1)
            dbl = dd * d + jnp.sum(dkd * kd, axis=0, keepdims=True)
            db = dqe * qe - dke * ke - dkd * kd
            db = jnp.where(is_last, db + dbl, db)
            dq_ref[rows, :] = dqe * eb * (GLA_DK ** -0.5)
            dk_ref[rows, :] = dke * enb + dkd * ebl
            dla = _mm_exact(tri_t, db)
            dz_scr[rows, :] = dla * (1.0 / GLA_TAU) * _sigmoid(-z[rows])
        dz = dz_scr[...]
        dlr_ref[...] = _mm_nt(dz, wg_ref[...])
        dwg_ref[...] += _mm_tn(lr_ref[...], dz)
        dbg_ref[...] += jnp.sum(dz, axis=0, keepdims=True)

    return pl.pallas_call(
        body, grid=(nt,),
        in_specs=[_rows(ts, hw, 0, order), _rows(ts, hw, 2, order), _rows(ts, hw, 3, order), _rows(ts, hw, 4, order),
                  _rows(ts, LANES, COL_LR // LANES, order),
                  pl.BlockSpec((ch, GLA_HEADS, LANES, LANES), lambda i: (order(i), 0, 0, 0)),
                  _const((LANES, hw)), _const((1, hw))],
        out_specs=[_rows(ts, hw, 0, order), _rows(ts, hw, 0, order), _rows(ts, hw, 0, order),
                   _rows(ts, LANES, 0, order), _acc((LANES, hw)), _acc((1, hw))],
        out_shape=[_sds((s, hw)), _sds((s, hw)), _sds((s, hw)), _sds((s, LANES)), _sds((LANES, hw)), _sds((1, hw))],
        scratch_shapes=[pltpu.VMEM((GLA_HEADS, LANES, LANES), F32), pltpu.VMEM((ts, hw), F32)],
        name="gla_bwd_rev" if reverse else "gla_bwd",
        compiler_params=_cparams(("arbitrary",)))(dos, proj, proj, proj, proj, st, wg, bg)


def _rnn_bwd(dhs, h, proj, cw, cb, wa, ba, wx, bx, lam, reverse):
    s = proj.shape[0]
    ts = _tile(s, TILE_S)
    nt = s // ts
    order = (lambda i: i) if reverse else (lambda i: nt - 1 - i)
    back = not reverse

    def body(dh_ref, h_ref, hh_ref, xr_ref, hp_ref, hn_ref, cw_ref, cb_ref, wa_ref, ba_ref, wx_ref, bx_ref, lam_ref,
             dxc_ref, dwa_ref, dba_ref, dwx_ref, dbx_ref, dlam_ref, a_scr, u_scr, g_scr, c_scr):
        i = pl.program_id(0)
        t = order(i)

        @pl.when(i == 0)
        def _():
            c_scr[...] = jnp.zeros_like(c_scr)
            dwa_ref[...] = jnp.zeros_like(dwa_ref)
            dba_ref[...] = jnp.zeros_like(dba_ref)
            dwx_ref[...] = jnp.zeros_like(dwx_ref)
            dbx_ref[...] = jnp.zeros_like(dbx_ref)
            dlam_ref[...] = jnp.zeros_like(dlam_ref)

        xc = _conv_fwd(xr_ref[...], hp_ref[...], hn_ref[...], cw_ref[...], cb_ref[...], t == 0, t == nt - 1)
        r, gi, sp, a, mult = _rnn_gates(xc, wa_ref[...], ba_ref[...], wx_ref[...], bx_ref[...], lam_ref[...])
        row = lax.broadcasted_iota(jnp.int32, (ts, D_RNN), 0)
        hv = h_ref[...]
        if reverse:
            edge = jnp.where(t == nt - 1, 0.0, hh_ref[0:1, :])
            h_prev = jnp.where(row == ts - 1, edge, pltpu.roll(hv, ts - 1, 0))
            a_nxt = jnp.where(row == 0, 1.0, pltpu.roll(a, 1, 0))
        else:
            edge = jnp.where(t == 0, 0.0, hh_ref[7:8, :])
            h_prev = jnp.where(row == 0, edge, pltpu.roll(hv, 1, 0))
            a_nxt = jnp.where(row == ts - 1, 1.0, pltpu.roll(a, ts - 1, 0))
        a_scr[...] = a_nxt
        u_scr[...] = dh_ref[...]
        _scan_tile(a_scr, u_scr, g_scr, c_scr[0:1, :], back)
        dh = g_scr[...]
        if reverse:
            c_scr[0:1, :] = a[ts - 1:ts, :] * dh[ts - 1:ts, :]
        else:
            c_scr[0:1, :] = a[0:1, :] * dh[0:1, :]
        dmult = dh * xc * gi
        dla = dh * h_prev * a - dmult * a * a / mult
        dza = dla * (-LRU_C) * sp * r * (1.0 - r)
        dzx = dh * xc * mult * gi * (1.0 - gi)
        dsp = jnp.sum(dla * (-LRU_C) * r, axis=0, keepdims=True)
        dlam_ref[...] += dsp * (-_sigmoid(-lam_ref[...]))
        dxc_ref[...] = dh * gi * mult + _mm_nt(dza, wa_ref[...]) + _mm_nt(dzx, wx_ref[...])
        dwa_ref[...] += _mm_tn(xc, dza)
        dwx_ref[...] += _mm_tn(xc, dzx)
        dba_ref[...] += jnp.sum(dza, axis=0, keepdims=True)
        dbx_ref[...] += jnp.sum(dzx, axis=0, keepdims=True)

    hp, hn = _halo_specs(s, ts, D_RNN, 0, order)
    hhp, hhn = _halo_specs(s, ts, D_RNN, 0, order)
    sq = (D_RNN, D_RNN)
    vec = (1, D_RNN)
    return pl.pallas_call(
        body, grid=(nt,),
        in_specs=[_rows(ts, D_RNN, 0, order), _rows(ts, D_RNN, 0, order), hhn if reverse else hhp,
                  _rows(ts, D_RNN, 0, order), hp, hn, _const((CONV_WIDTH, D_RNN)), _const(vec),
                  _const(sq), _const(vec), _const(sq), _const(vec), _const(vec)],
        out_specs=[_rows(ts, D_RNN, 0, order), _acc(sq), _acc(vec), _acc(sq), _acc(vec), _acc(vec)],
        out_shape=[_sds((s, D_RNN)), _sds(sq), _sds(vec), _sds(sq), _sds(vec), _sds(vec)],
        scratch_shapes=[pltpu.VMEM((ts, D_RNN), F32), pltpu.VMEM((ts, D_RNN), F32), pltpu.VMEM((ts, D_RNN), F32),
                        pltpu.VMEM((8, D_RNN), F32)],
        name="rnn_bwd_rev" if reverse else "rnn_bwd",
        compiler_params=_cparams(("arbitrary",)))(dhs, h, h, proj, proj, proj, cw, cb, wa, ba, wx, bx, lam)


def _conv_bwd(dxc_f, dxc_b, proj, cw):
    s = proj.shape[0]
    ts = _tile(s, TILE_S)
    nt = s // ts
    ident = lambda i: i

    def body(df_ref, dfp_ref, dfn_ref, db_ref, dbp_ref, dbn_ref, xr_ref, xp_ref, xn_ref, cw_ref,
             dxr_ref, dcw_ref, dcb_ref):
        t = pl.program_id(0)

        @pl.when(t == 0)
        def _():
            dcw_ref[...] = jnp.zeros_like(dcw_ref)
            dcb_ref[...] = jnp.zeros_like(dcb_ref)

        first = t == 0
        last = t == nt - 1
        d = df_ref[...] + db_ref[...]
        d_m2, d_m1, _, d_p1 = _conv_taps(d, dfp_ref[...] + dbp_ref[...], dfn_ref[...] + dbn_ref[...], first, last)
        dn = jnp.where(last, 0.0, dfn_ref[...] + dbn_ref[...])
        d_p2 = jnp.concatenate([d, dn], axis=0)[2:2 + ts]
        del d_m2
        cw = cw_ref[...]
        dxr_ref[...] = cw[0:1] * d_p2 + cw[1:2] * d_p1 + cw[2:3] * d + cw[3:4] * d_m1
        taps = _conv_taps(xr_ref[...], xp_ref[...], xn_ref[...], first, last)
        dcw_ref[...] += jnp.concatenate([jnp.sum(d * tp, axis=0, keepdims=True) for tp in taps], axis=0)
        dcb_ref[...] += jnp.sum(d, axis=0, keepdims=True)

    hp, hn = _halo_specs(s, ts, D_RNN, 0, ident)
    return pl.pallas_call(
        body, grid=(nt,),
        in_specs=[_rows(ts, D_RNN), hp, hn, _rows(ts, D_RNN), hp, hn, _rows(ts, D_RNN), hp, hn,
                  _const((CONV_WIDTH, D_RNN))],
        out_specs=[_rows(ts, D_RNN), _acc((CONV_WIDTH, D_RNN)), _acc((1, D_RNN))],
        out_shape=[_sds((s, D_RNN)), _sds((CONV_WIDTH, D_RNN)), _sds((1, D_RNN))],
        name="conv_bwd", compiler_params=_cparams(("arbitrary",)))(
            dxc_f, dxc_f, dxc_f, dxc_b, dxc_b, dxc_b, proj, proj, proj, cw)


def _mix_in_bwd(parts, lr_parts, x, dx1, gpre, w_in_p):
    s = x.shape[0]
    ts = _tile(s, TILE_S)
    dxr, dgr, dqf, dqb, dkf, dkb, dvf, dvb, dg = parts
    dlf, dlb = lr_parts

    def body(dxr_ref, dgr_ref, dqf_ref, dqb_ref, dkf_ref, dkb_ref, dvf_ref, dvb_ref, dg_ref, dlf_ref, dlb_ref,
             x_ref, dx1_ref, g_ref, w_ref, dx_ref, dp_ref, h_ref, dgpre_ref):
        @pl.when(pl.program_id(0) == 0)
        def _():
            dgpre_ref[...] = jnp.zeros_like(dgpre_ref)

        dp = jnp.concatenate(
            [dxr_ref[...], dgr_ref[...], dqf_ref[...] + dqb_ref[...], dkf_ref[...] + dkb_ref[...],
             dvf_ref[...] + dvb_ref[...], dg_ref[...], dlf_ref[...] + dlb_ref[...]], axis=1).astype(MXU_DTYPE)
        dp_ref[...] = dp
        h, n, rs = _rms_fwd(x_ref[...], g_ref[...])
        h_ref[...] = h.astype(MXU_DTYPE)
        dh = _mm_nt(dp, w_ref[...])
        dx, dgpre = _rms_bwd(dh, n, rs, g_ref[...])
        dgpre_ref[...] += dgpre
        dx_ref[...] = dx1_ref[...] + dx

    return pl.pallas_call(
        body, grid=(s // ts,),
        in_specs=[_rows(ts, 512)] * 9 + [_rows(ts, LANES)] * 2 + [_rows(ts, D_MODEL), _rows(ts, D_MODEL),
                                                                   _const((1, D_MODEL)), _const((D_MODEL, PW))],
        out_specs=[_rows(ts, D_MODEL), _rows(ts, PW), _rows(ts, D_MODEL), _acc((1, D_MODEL))],
        out_shape=[_sds((s, D_MODEL)), _sds((s, PW), MXU_DTYPE), _sds((s, D_MODEL), MXU_DTYPE), _sds((1, D_MODEL))],
        name="mix_in_bwd", compiler_params=_cparams(("arbitrary",)))(
            dxr, dgr, dqf, dqb, dkf, dkb, dvf, dvb, dg, dlf, dlb, x, dx1, gpre, w_in_p)


def _pad_heads(w):
    sh = w.shape[:-1]
    w = w.reshape(sh + (GLA_HEADS, GLA_DK))
    w = jnp.pad(w, [(0, 0)] * (len(sh) + 1) + [(0, GLA_DKP - GLA_DK)])
    return w.reshape(sh + (GLA_HEADS * GLA_DKP,))


def _unpad_heads(w):
    sh = w.shape[:-1]
    return w.reshape(sh + (GLA_HEADS, GLA_DKP))[..., :GLA_DK].reshape(sh + (GLA_HEADS * GLA_DK,))


W_IN_COLS = 2592
W_IN_SHARD = W_IN_COLS // N_DEV
FF_SHARD = D_FF // N_DEV


def _w_in_pieces():
    segments = [(0, 1024, 0)]
    segments += [(1024 + GLA_DK * h, 1024 + GLA_DK * (h + 1), 1024 + GLA_DKP * h) for h in range(GLA_HEADS)]
    segments += [(1280 + GLA_DK * h, 1280 + GLA_DK * (h + 1), 1536 + GLA_DKP * h) for h in range(GLA_HEADS)]
    segments += [(1536, 2560, 2048), (2560, W_IN_COLS, COL_LR)]
    pieces = []
    for lo, hi, dst in segments:
        while lo < hi:
            j = lo // W_IN_SHARD
            end = min(hi, (j + 1) * W_IN_SHARD)
            pieces.append((j, lo - j * W_IN_SHARD, end - lo, dst))
            dst += end - lo
            lo = end
    return pieces


def _w_in_from_shards(w):
    nl = w.shape[0]
    tr = 256

    def body(w_ref, o_ref):
        o_ref[...] = jnp.zeros_like(o_ref)
        for j, src, width, dst in _w_in_pieces():
            o_ref[:, dst:dst + width] = w_ref[j, :, src:src + width]

    return pl.pallas_call(
        body, grid=(nl, D_MODEL // tr),
        in_specs=[pl.BlockSpec((None, N_DEV, tr, W_IN_SHARD), lambda l, i: (l, 0, i, 0))],
        out_specs=pl.BlockSpec((None, tr, PW), lambda l, i: (l, i, 0)), out_shape=_sds((nl, D_MODEL, PW), w.dtype),
        name="w_in_from_shards", compiler_params=_cparams(("parallel", "parallel")))(w)


def _w_in_to_shards(g):
    tr = 256

    def body(g_ref, o_ref):
        for j, src, width, dst in _w_in_pieces():
            o_ref[j, :, src:src + width] = g_ref[:, dst:dst + width]

    return pl.pallas_call(
        body, grid=(D_MODEL // tr,),
        in_specs=[pl.BlockSpec((tr, PW), lambda i: (i, 0))],
        out_specs=pl.BlockSpec((N_DEV, tr, W_IN_SHARD), lambda i: (0, i, 0)),
        out_shape=_sds((N_DEV, D_MODEL, W_IN_SHARD), g.dtype),
        name="w_in_to_shards", compiler_params=_cparams(("parallel",)))(g)


def _cols_from_shards(w, name):
    nl, _, d, c = w.shape
    tr = 256

    def body(w_ref, o_ref):
        for j in range(N_DEV):
            o_ref[:, j * c:(j + 1) * c] = w_ref[j]

    return pl.pallas_call(
        body, grid=(nl, d // tr),
        in_specs=[pl.BlockSpec((None, N_DEV, tr, c), lambda l, i: (l, 0, i, 0))],
        out_specs=pl.BlockSpec((None, tr, N_DEV * c), lambda l, i: (l, i, 0)),
        out_shape=_sds((nl, d, N_DEV * c), w.dtype),
        name=name, compiler_params=_cparams(("parallel", "parallel")))(w)


def _cols_to_shards(g, name):
    d, n = g.shape
    c = n // N_DEV
    tr = 256

    def body(g_ref, o_ref):
        for j in range(N_DEV):
            o_ref[j] = g_ref[:, j * c:(j + 1) * c]

    return pl.pallas_call(
        body, grid=(d // tr,),
        in_specs=[pl.BlockSpec((tr, n), lambda i: (i, 0))],
        out_specs=pl.BlockSpec((N_DEV, tr, c), lambda i: (0, i, 0)), out_shape=_sds((N_DEV, d, c), g.dtype),
        name=name, compiler_params=_cparams(("parallel",)))(g)


def _block_diag(w):
    n, b, _ = w.shape
    eye = jnp.eye(n, dtype=w.dtype)
    return (w[:, :, None, :] * eye[:, None, :, None]).reshape(n * b, n * b)


def _block_diag_of(w):
    n = D_RNN // 64
    eye = jnp.eye(n, dtype=w.dtype)
    return (w.reshape(n, 64, n, 64) * eye[:, None, :, None]).sum(axis=2)


def _gate_weight(wg, direction):
    w = _pad_heads(wg)
    lo = direction * GLA_RANK
    return jnp.pad(w, ((lo, LANES - GLA_RANK - lo), (0, 0)))


def _layer_weights(full, l):
    row = lambda v: v.reshape(1, -1)
    lw = dict(
        gpre=row(full["mix_norm_pre"][l]), gpost=row(full["mix_norm_post"][l]),
        w_in=full["w_in_p"][l].astype(MXU_DTYPE),
        cw=full["conv_w"][l], cb=row(full["conv_b"][l]),
        g_rnn=row(full["rnn_out_norm"][l]), g_gla=row(full["gla_out_norm"][l]),
        w_out=full["w_out"][l].astype(MXU_DTYPE),
        fpre=row(full["ffn_norm_pre"][l]), fpost=row(full["ffn_norm_post"][l]),
        wg=full["w_ffn_gate"][l].astype(MXU_DTYPE), wu=full["w_ffn_up"][l].astype(MXU_DTYPE),
        wd=full["w_ffn_down"][l].astype(MXU_DTYPE))
    for d in (0, 1):
        lw[f"wa{d}"] = _block_diag(full["lru_w_a"][l, d]).astype(MXU_DTYPE)
        lw[f"wx{d}"] = _block_diag(full["lru_w_x"][l, d]).astype(MXU_DTYPE)
        lw[f"ba{d}"] = row(full["lru_b_a"][l, d])
        lw[f"bx{d}"] = row(full["lru_b_x"][l, d])
        lw[f"lam{d}"] = row(full["lru_lambda"][l, d])
        lw[f"gw{d}"] = _gate_weight(full["gla_w_gate"][l, d], d).astype(MXU_DTYPE)
        lw[f"gb{d}"] = row(_pad_heads(full["gla_b_gate"][l, d]))
    return lw


def _layer_fwd(x, lw):
    proj = _mix_in_fwd(x, lw["gpre"], lw["w_in"])
    hs, os_, sts = [], [], []
    for d in (0, 1):
        hs.append(_rnn_fwd(proj, lw["cw"], lw["cb"], lw[f"wa{d}"], lw[f"ba{d}"], lw[f"wx{d}"], lw[f"bx{d}"],
                           lw[f"lam{d}"], bool(d)))
        o, st = _gla_fwd(proj, lw[f"gw{d}"], lw[f"gb{d}"], bool(d))
        os_.append(o)
        sts.append(st)
    x1, y, m = _mix_out_fwd(x, hs[0], hs[1], os_[0], os_[1], proj, lw["g_rnn"], lw["g_gla"], lw["w_out"], lw["gpost"])
    x2, a, u, f = _ffn_fwd(x1, lw["fpre"], lw["wg"], lw["wu"], lw["wd"], lw["fpost"])
    saved = dict(x=x, proj=proj, hs=hs, os=os_, sts=sts, y=y, m=m, x1=x1, a=a, u=u, f=f)
    return x2, saved


def _layer_bwd(dx2, sv, lw):
    g = {}
    dx1, df, h2, p, da, du, dfpost, dfpre = _ffn_bwd(dx2, sv["f"], sv["x1"], sv["a"], sv["u"], lw["fpre"], lw["wg"],
                                                     lw["wu"], lw["wd"], lw["fpost"])
    g["ffn_norm_post"], g["ffn_norm_pre"] = dfpost[0], dfpre[0]
    big = {}
    big["w_ffn_gate"] = _cols_to_shards(_tn_matmul(h2, da, "dw_ffn_gate"), "dw_ffn_gate_to_shards")
    big["w_ffn_up"] = _cols_to_shards(_tn_matmul(h2, du, "dw_ffn_up"), "dw_ffn_up_to_shards")
    big["w_ffn_down"] = _tn_matmul(p, df, "dw_ffn_down").reshape(N_DEV, FF_SHARD, D_MODEL)
    proj = sv["proj"]
    dm, dhs, dgr, dos, dg, dgpost, dgrnn, dggla = _mix_out_bwd(
        dx1, sv["m"], sv["hs"][0], sv["hs"][1], sv["os"][0], sv["os"][1], proj, lw["g_rnn"], lw["g_gla"],
        lw["w_out"], lw["gpost"])
    g["mix_norm_post"], g["rnn_out_norm"], g["gla_out_norm"] = dgpost[0], dgrnn[0], dggla[0]
    big["w_out"] = _tn_matmul(sv["y"], dm, "dw_out").reshape(N_DEV, D_MODEL // N_DEV, D_MODEL)
    dq, dk, dv, dlr, dxc = [], [], [], [], []
    gw, gb, wa, ba, wx, bx, lam = [], [], [], [], [], [], []
    for d in (0, 1):
        r = _gla_bwd(dos, proj, sv["sts"][d], lw[f"gw{d}"], lw[f"gb{d}"], bool(d))
        dq.append(r[0]); dk.append(r[1]); dv.append(r[2]); dlr.append(r[3])
        lo = d * GLA_RANK
        gw.append(_unpad_heads(r[4][lo:lo + GLA_RANK]))
        gb.append(_unpad_heads(r[5][0]))
        r = _rnn_bwd(dhs, sv["hs"][d], proj, lw["cw"], lw["cb"], lw[f"wa{d}"], lw[f"ba{d}"], lw[f"wx{d}"],
                     lw[f"bx{d}"], lw[f"lam{d}"], bool(d))
        dxc.append(r[0])
        wa.append(_block_diag_of(r[1])); ba.append(r[2][0]); wx.append(_block_diag_of(r[3])); bx.append(r[4][0])
        lam.append(r[5][0])
    g["gla_w_gate"], g["gla_b_gate"] = jnp.stack(gw), jnp.stack(gb)
    g["lru_w_a"], g["lru_b_a"] = jnp.stack(wa), jnp.stack(ba)
    g["lru_w_x"], g["lru_b_x"], g["lru_lambda"] = jnp.stack(wx), jnp.stack(bx), jnp.stack(lam)
    dxr, dcw, dcb = _conv_bwd(dxc[0], dxc[1], proj, lw["cw"])
    g["conv_w"], g["conv_b"] = dcw, dcb[0]
    dx, dproj, h, dgpre = _mix_in_bwd((dxr, dgr, dq[0], dq[1], dk[0], dk[1], dv[0], dv[1], dg), (dlr[0], dlr[1]),
                                      sv["x"], dx1, lw["gpre"], lw["w_in"])
    g["mix_norm_pre"] = dgpre[0]
    big["w_in"] = _w_in_to_shards(_tn_matmul(h, dproj, "dw_in"))
    return dx, g, [big[n] for n in BIG_WEIGHTS]


WEIGHT_NAMES = ["mix_norm_pre", "mix_norm_post", "w_in", "conv_w", "conv_b", "lru_w_a", "lru_b_a", "lru_w_x", "lru_b_x",
                "lru_lambda", "rnn_out_norm", "gla_w_gate", "gla_b_gate", "gla_out_norm", "w_out", "ffn_norm_pre",
                "ffn_norm_post", "w_ffn_gate", "w_ffn_up", "w_ffn_down"]
BIG_WEIGHTS = ["w_in", "w_out", "w_ffn_gate", "w_ffn_up", "w_ffn_down"]
BIG_AXIS = {"w_in": 2, "w_out": 1, "w_ffn_gate": 2, "w_ffn_up": 2, "w_ffn_down": 1}


def _local_step(x, target, full, on_big_grads):
    lws = [_layer_weights(full, l) for l in range(DEPTH)]
    saved = []
    for l in range(DEPTH):
        x, sv = _layer_fwd(x, lws[l])
        saved.append(sv)
    loss, dx = _loss_fwd_bwd(x, target)
    grads = [None] * DEPTH
    for l in reversed(range(DEPTH)):
        dx, grads[l], big = _layer_bwd(dx, saved[l], lws[l])
        on_big_grads(l, big)
    g = {n: jnp.stack([grads[l][n] for l in range(DEPTH)]) for n in WEIGHT_NAMES if n not in BIG_WEIGHTS}
    return loss[0, 0], dx, g


MESH_ID = pl.DeviceIdType.MESH
ANY = pl.BlockSpec(memory_space=pl.ANY)
MESH_AXES = ("x", "y", "c")


def _all_gather(x, name):
    def body(x_ref, out_ref, send_sems, recv_sems, local_sem):
        mx, my, mc = lax.axis_index("x"), lax.axis_index("y"), lax.axis_index("c")
        me, sibling = (mx, my, mc), (mx, my, 1 - mc)
        chips = [(1 - mx, my), (mx, 1 - my), (1 - mx, 1 - my)]

        def slot(px, py, pc):
            return out_ref.at[4 * px + 2 * py + pc]

        def copy(k, block, to, src=None):
            return pltpu.make_async_remote_copy(
                src_ref=slot(*block) if src is None else src, dst_ref=slot(*block),
                send_sem=send_sems.at[k], recv_sem=recv_sems.at[k], device_id=to, device_id_type=MESH_ID)

        mine = pltpu.make_async_copy(x_ref, slot(*me), local_sem)
        mine.start()
        first = [copy(0, me, sibling, src=x_ref)]
        first += [copy(1 + j, me, (*chip, mc), src=x_ref) for j, chip in enumerate(chips)]
        for cp in first:
            cp.start()
        passed = [copy(4 + j, (*chip, mc), sibling) for j, chip in enumerate(chips)]
        for j, chip in enumerate(chips):
            copy(1 + j, (*chip, mc), me).wait_recv()
            passed[j].start()
        copy(0, sibling, me).wait_recv()
        for j, chip in enumerate(chips):
            copy(4 + j, (*chip, 1 - mc), me).wait_recv()
        for cp in first + passed:
            cp.wait_send()
        mine.wait()

    return pl.pallas_call(
        body, out_shape=_sds((N_DEV,) + x.shape, x.dtype), in_specs=[ANY], out_specs=ANY,
        scratch_shapes=[pltpu.SemaphoreType.DMA((7,)), pltpu.SemaphoreType.DMA((7,)), pltpu.SemaphoreType.DMA],
        name=name)(x)


def _all_to_all(g, name):
    def body(g_ref, out_ref, send_sems, recv_sems, local_sem):
        mx, my, mc = lax.axis_index("x"), lax.axis_index("y"), lax.axis_index("c")
        me = 4 * mx + 2 * my + mc
        mine = pltpu.make_async_copy(g_ref.at[me], out_ref.at[me], local_sem)
        mine.start()
        copies = []
        for r in range(1, N_DEV):
            px = 1 - mx if r & 4 else mx
            py = 1 - my if r & 2 else my
            pc = 1 - mc if r & 1 else mc
            cp = pltpu.make_async_remote_copy(
                src_ref=g_ref.at[4 * px + 2 * py + pc], dst_ref=out_ref.at[me],
                send_sem=send_sems.at[r - 1], recv_sem=recv_sems.at[r - 1],
                device_id=(px, py, pc), device_id_type=MESH_ID)
            cp.start()
            copies.append(cp)
        for cp in copies:
            cp.wait()
        mine.wait()

    return pl.pallas_call(
        body, out_shape=_sds(g.shape, g.dtype), in_specs=[ANY], out_specs=ANY,
        scratch_shapes=[pltpu.SemaphoreType.DMA((7,)), pltpu.SemaphoreType.DMA((7,)), pltpu.SemaphoreType.DMA],
        name=name)(g)


def _sum_adamw(parts, w, m, v, name):
    _, r, c = parts.shape
    tr = _tile(r, ADAM_TILE_ROWS)

    def body(p_ref, w_ref, m_ref, v_ref, g_ref, d_ref, m2_ref, v2_ref):
        g = p_ref[0]
        for k in range(1, N_DEV):
            g = g + p_ref[k]
        g_ref[...] = g
        m2 = ADAM_B1 * m_ref[...] + (1.0 - ADAM_B1) * g
        v2 = ADAM_B2 * v_ref[...] + (1.0 - ADAM_B2) * (g * g)
        m2_ref[...] = m2
        v2_ref[...] = v2
        m_hat = m2 / (1.0 - ADAM_B1 ** ADAM_STEP)
        v_hat = v2 / (1.0 - ADAM_B2 ** ADAM_STEP)
        d_ref[...] = -ADAM_LR * (m_hat / (jnp.sqrt(v_hat) + ADAM_EPS) + ADAM_WD * w_ref[...])

    flat = pl.BlockSpec((tr, c), lambda i: (i, 0))
    return pl.pallas_call(
        body, grid=(r // tr,),
        in_specs=[pl.BlockSpec((N_DEV, tr, c), lambda i: (0, i, 0)), flat, flat, flat],
        out_specs=[flat] * 4, out_shape=[_sds((r, c))] * 4,
        name=name, compiler_params=_cparams(("parallel",)))(parts, w, m, v)


def _gather_big_weights(shards):
    n = len(shards)

    def body(*refs):
        srcs, outs = refs[:n], refs[n:2 * n]
        send_sems, recv_sems, local_sems = refs[2 * n:]
        mx, my, mc = lax.axis_index("x"), lax.axis_index("y"), lax.axis_index("c")
        me, sibling = (mx, my, mc), (mx, my, 1 - mc)
        chips = [(1 - mx, my), (mx, 1 - my), (1 - mx, 1 - my)]

        def slot(a, px, py, pc):
            return outs[a].at[:, 4 * px + 2 * py + pc]

        def copy(k, a, block, to, own=False):
            return pltpu.make_async_remote_copy(
                src_ref=srcs[a] if own else slot(a, *block), dst_ref=slot(a, *block),
                send_sem=send_sems.at[k * n + a], recv_sem=recv_sems.at[k * n + a],
                device_id=to, device_id_type=MESH_ID)

        mine = [pltpu.make_async_copy(srcs[a], slot(a, *me), local_sems.at[a]) for a in range(n)]
        for cp in mine:
            cp.start()
        first = [copy(0, a, me, sibling, own=True) for a in range(n)]
        first += [copy(1 + j, a, me, (*chip, mc), own=True) for j, chip in enumerate(chips) for a in range(n)]
        for cp in first:
            cp.start()
        passed = [[copy(4 + j, a, (*chip, mc), sibling) for a in range(n)] for j, chip in enumerate(chips)]
        for j, chip in enumerate(chips):
            for a in range(n):
                copy(1 + j, a, (*chip, mc), me).wait_recv()
                passed[j][a].start()
        for a in range(n):
            copy(0, a, sibling, me).wait_recv()
        for j, chip in enumerate(chips):
            for a in range(n):
                copy(4 + j, a, (*chip, 1 - mc), me).wait_recv()
        for cp in first + [cp for row in passed for cp in row]:
            cp.wait_send()
        for cp in mine:
            cp.wait()

    return pl.pallas_call(
        body, out_shape=[_sds((s.shape[0], N_DEV) + s.shape[1:], s.dtype) for s in shards],
        in_specs=[ANY] * n, out_specs=[ANY] * n,
        scratch_shapes=[pltpu.SemaphoreType.DMA((7 * n,)), pltpu.SemaphoreType.DMA((7 * n,)),
                        pltpu.SemaphoreType.DMA((n,))],
        name="gather_matmul_weights")(*shards)


def _exchange_big_grads(srcs, recvs, layer):
    n = len(srcs)

    def body(*refs):
        src, out = refs[:n], refs[2 * n:3 * n]
        send_sems, recv_sems, local_sems = refs[3 * n:]
        mx, my, mc = lax.axis_index("x"), lax.axis_index("y"), lax.axis_index("c")
        me = 4 * mx + 2 * my + mc
        mine = [pltpu.make_async_copy(src[a].at[me], out[a].at[me, layer], local_sems.at[a]) for a in range(n)]
        for cp in mine:
            cp.start()
        copies = []
        for r in range(1, N_DEV):
            px = 1 - mx if r & 4 else mx
            py = 1 - my if r & 2 else my
            pc = 1 - mc if r & 1 else mc
            for a in range(n):
                cp = pltpu.make_async_remote_copy(
                    src_ref=src[a].at[4 * px + 2 * py + pc], dst_ref=out[a].at[me, layer],
                    send_sem=send_sems.at[(r - 1) * n + a], recv_sem=recv_sems.at[(r - 1) * n + a],
                    device_id=(px, py, pc), device_id_type=MESH_ID)
                cp.start()
                copies.append(cp)
        for cp in copies:
            cp.wait()
        for cp in mine:
            cp.wait()

    return pl.pallas_call(
        body, out_shape=[_sds(r.shape, r.dtype) for r in recvs],
        in_specs=[ANY] * (2 * n), out_specs=[ANY] * n, input_output_aliases={n + a: a for a in range(n)},
        scratch_shapes=[pltpu.SemaphoreType.DMA((7 * n,)), pltpu.SemaphoreType.DMA((7 * n,)),
                        pltpu.SemaphoreType.DMA((n,))],
        name=f"exchange_grads_layer{layer}")(*srcs, *recvs)


def _sum_adamw_big(parts, w, m, v, name):
    _, nl, a, b = parts.shape
    ta = _tile(a, 256)

    def body(p_ref, w_ref, m_ref, v_ref, g_ref, d_ref, m2_ref, v2_ref):
        g = p_ref[0].astype(F32)
        for k in range(1, N_DEV):
            g = g + p_ref[k].astype(F32)
        g_ref[...] = g
        m2 = ADAM_B1 * m_ref[...] + (1.0 - ADAM_B1) * g
        v2 = ADAM_B2 * v_ref[...] + (1.0 - ADAM_B2) * (g * g)
        m2_ref[...] = m2
        v2_ref[...] = v2
        m_hat = m2 / (1.0 - ADAM_B1 ** ADAM_STEP)
        v_hat = v2 / (1.0 - ADAM_B2 ** ADAM_STEP)
        d_ref[...] = -ADAM_LR * (m_hat / (jnp.sqrt(v_hat) + ADAM_EPS) + ADAM_WD * w_ref[...])

    blk = pl.BlockSpec((None, ta, b), lambda l, i: (l, i, 0))
    return pl.pallas_call(
        body, grid=(nl, a // ta),
        in_specs=[pl.BlockSpec((N_DEV, None, ta, b), lambda l, i: (0, l, i, 0)), blk, blk, blk],
        out_specs=[blk] * 4, out_shape=[_sds((nl, a, b))] * 4,
        name=name, compiler_params=_cparams(("parallel", "parallel")))(parts, w, m, v)


SMALL_SHARDED = [("conv_w", 2), ("lru_b_a", 2), ("lru_b_x", 2), ("lru_lambda", 2), ("gla_w_gate", 3), ("gla_b_gate", 2)]
REPLICATED = ["mix_norm_pre", "mix_norm_post", "conv_b", "lru_w_a", "lru_w_x", "rnn_out_norm", "gla_out_norm",
              "ffn_norm_pre", "ffn_norm_post"]


def _pack(arrays, cols, row_mult):
    flat = jnp.concatenate([a.reshape(-1) for a in arrays])
    unit = cols * row_mult
    total = -(-flat.shape[0] // unit) * unit
    return jnp.pad(flat, (0, total - flat.shape[0])).reshape(total // cols, cols)


def _pack_slots(arrays, cols, row_mult):
    flat = jnp.concatenate([a.reshape(N_DEV, -1) for a in arrays], axis=1)
    unit = cols * row_mult
    total = -(-flat.shape[1] // unit) * unit
    return jnp.pad(flat, ((0, 0), (0, total - flat.shape[1]))).reshape(N_DEV, total // cols, cols)


def _unpack(flat, shapes):
    flat = flat.reshape(-1)
    out, off = [], 0
    for sh in shapes:
        n = 1
        for d in sh:
            n *= d
        out.append(flat[off:off + n].reshape(sh))
        off += n
    return out


def _unpack_slots(flat, shapes):
    flat = flat.reshape(N_DEV, -1)
    out, off = [], 0
    for sh in shapes:
        n = 1
        for d in sh:
            n *= d
        out.append(flat[:, off:off + n].reshape((N_DEV,) + tuple(sh)))
        off += n
    return out


def _merge_shards(a, axis):
    a = jnp.moveaxis(a, 0, axis)
    sh = a.shape
    return a.reshape(sh[:axis] + (sh[axis] * sh[axis + 1],) + sh[axis + 2:])


def _split_shards(a, axis):
    sh = a.shape
    a = a.reshape(sh[:axis] + (N_DEV, sh[axis] // N_DEV) + sh[axis + 1:])
    return jnp.moveaxis(a, axis, 0)


def kernel(x, mix_norm_pre, mix_norm_post, w_in, conv_w, conv_b, lru_w_a, lru_b_a, lru_w_x, lru_b_x, lru_lambda, rnn_out_norm, gla_w_gate, gla_b_gate, gla_out_norm, w_out, ffn_norm_pre, ffn_norm_post, w_ffn_gate, w_ffn_up, w_ffn_down, loss_target, m_mix_norm_pre, m_mix_norm_post, m_w_in, m_conv_w, m_conv_b, m_lru_w_a, m_lru_b_a, m_lru_w_x, m_lru_b_x, m_lru_lambda, m_rnn_out_norm, m_gla_w_gate, m_gla_b_gate, m_gla_out_norm, m_w_out, m_ffn_norm_pre, m_ffn_norm_post, m_w_ffn_gate, m_w_ffn_up, m_w_ffn_down, v_mix_norm_pre, v_mix_norm_post, v_w_in, v_conv_w, v_conv_b, v_lru_w_a, v_lru_b_a, v_lru_w_x, v_lru_b_x, v_lru_lambda, v_rnn_out_norm, v_gla_w_gate, v_gla_b_gate, v_gla_out_norm, v_w_out, v_ffn_norm_pre, v_ffn_norm_post, v_w_ffn_gate, v_w_ffn_up, v_w_ffn_down):
    args = dict(locals())
    w = {n: args[n] for n in WEIGHT_NAMES}
    m = {n: args["m_" + n] for n in WEIGHT_NAMES}
    v = {n: args["v_" + n] for n in WEIGHT_NAMES}
    names_s = [n for n, _ in SMALL_SHARDED]
    axis_s = dict(SMALL_SHARDED)
    shapes_s = [w[n].shape for n in names_s]

    gathered = dict(zip(BIG_WEIGHTS, _gather_big_weights([w[n].astype(MXU_DTYPE) for n in BIG_WEIGHTS])))
    small = _pack([w[n] for n in names_s], LANES, 8)
    small_all = _unpack_slots(_all_gather(small, "gather_small_weights"), shapes_s)
    full = {n: w[n] for n in REPLICATED}
    for n, a in zip(names_s, small_all):
        full[n] = _merge_shards(a, axis_s[n])
    full["w_in_p"] = _w_in_from_shards(gathered["w_in"])
    full["w_ffn_gate"] = _cols_from_shards(gathered["w_ffn_gate"], "w_ffn_gate_from_shards")
    full["w_ffn_up"] = _cols_from_shards(gathered["w_ffn_up"], "w_ffn_up_from_shards")
    full["w_out"] = gathered["w_out"].reshape(DEPTH, D_MODEL, D_MODEL)
    full["w_ffn_down"] = gathered["w_ffn_down"].reshape(DEPTH, D_FF, D_MODEL)

    recv = [jnp.zeros((N_DEV,) + w[n].shape, MXU_DTYPE) for n in BIG_WEIGHTS]

    def on_big_grads(layer, arrays):
        recv[:] = _exchange_big_grads(arrays, recv, layer)

    loss, dx, g = _local_step(x[0], loss_target[0], full, on_big_grads)
    loss = lax.psum(loss, MESH_AXES)
    res = {}
    for n, parts in zip(BIG_WEIGHTS, recv):
        res[n] = _sum_adamw_big(parts, w[n], m[n], v[n], "adamw_" + n)

    g_slots = _pack_slots([_split_shards(g[n], axis_s[n]) for n in names_s], LANES, 8)
    g_recv = _all_to_all(g_slots, "exchange_small_grads")
    packed = [_pack([t[n] for n in names_s], LANES, 8) for t in (w, m, v)]
    res_s = [_unpack(r, shapes_s) for r in _sum_adamw(g_recv, *packed, "adamw_small")]
    for i, n in enumerate(names_s):
        res[n] = [res_s[k][i] for k in range(4)]

    shapes_r = [w[n].shape for n in REPLICATED]
    g_rep = _all_gather(_pack([g[n] for n in REPLICATED], ADAM_COLS, ADAM_TILE_ROWS), "gather_replicated_grads")
    packed = [_pack([t[n] for n in REPLICATED], ADAM_COLS, ADAM_TILE_ROWS) for t in (w, m, v)]
    res_r = [_unpack(r, shapes_r) for r in _sum_adamw(g_rep, *packed, "adamw_replicated")]
    for i, n in enumerate(REPLICATED):
        res[n] = [res_r[k][i] for k in range(4)]

    outs = [[res[n][k] for n in WEIGHT_NAMES] for k in range(4)]
    return (loss, dx[None], *outs[0], *outs[1], *outs[2], *outs[3])
```

```python
import functools

import jax
import jax.numpy as jnp
from jax import lax
from jax.experimental import pallas as pl
from jax.experimental.pallas import tpu as pltpu

F32 = jnp.float32
MXU_DTYPE = jnp.bfloat16

N_DEV = 8
D_MODEL = 1024
D_RNN = 512
CONV_WIDTH = 4
LRU_C = 8.0
GLA_HEADS = 4
GLA_DK = 64
GLA_DKP = 128
GLA_DV = 128
GLA_RANK = 16
GLA_TAU = 16.0
GLA_CHUNK = 64
D_FF = 2816
RMS_EPS = 1e-6
DEPTH = 4

PW = 3200
COL_LR = 3072
LANES = 128

TILE_S = 512
TILE_F = 256
TILE_TN = 2048
TN_ACC_BYTES = 6 * 1024 * 1024
F_CHUNK = 1408
VMEM_LIMIT = 56 * 1024 * 1024

ADAM_LR = 0.001
ADAM_B1 = 0.9
ADAM_B2 = 0.999
ADAM_EPS = 1e-08
ADAM_WD = 0.01
ADAM_STEP = 10

ADAM_TILE_ROWS = 256
ADAM_COLS = 1024


def _mm(a, b):
    return jnp.dot(a.astype(MXU_DTYPE), b.astype(MXU_DTYPE), preferred_element_type=F32)


def _mm_nt(a, b):
    return lax.dot_general(a.astype(MXU_DTYPE), b.astype(MXU_DTYPE), (((1,), (1,)), ((), ())),
                           preferred_element_type=F32)


def _mm_tn(a, b):
    return lax.dot_general(a.astype(MXU_DTYPE), b.astype(MXU_DTYPE), (((0,), (0,)), ((), ())),
                           preferred_element_type=F32)


def _mm_tri(tri, x):
    t = tri.astype(jnp.bfloat16)
    hi = x.astype(jnp.bfloat16)
    r1 = x - hi.astype(F32)
    mid = r1.astype(jnp.bfloat16)
    lo = (r1 - mid.astype(F32)).astype(jnp.bfloat16)
    dot = lambda v: jnp.dot(t, v, preferred_element_type=F32)
    return dot(hi) + dot(mid) + dot(lo)


def _sigmoid(x):
    return 0.5 * jnp.tanh(0.5 * x) + 0.5


def _log1p_pos(e):
    series = e * (1.0 - e * (0.5 - e * (1.0 / 3.0 - e * 0.25)))
    return jnp.where(e < 0.01, series, jnp.log(1.0 + e))


def _softplus(x):
    return jnp.maximum(x, 0.0) + _log1p_pos(jnp.exp(-jnp.abs(x)))


def _softplus_coarse(x):
    return jnp.maximum(x, 0.0) + jnp.log(1.0 + jnp.exp(-jnp.abs(x)))


GELU_C = 0.7978845608028654
GELU_K = 0.044715


def _gelu_and_grad(x):
    t = jnp.tanh(GELU_C * (x + GELU_K * x * x * x))
    y = 0.5 * x * (1.0 + t)
    dy = 0.5 * (1.0 + t) + 0.5 * x * (1.0 - t * t) * GELU_C * (1.0 + 3.0 * GELU_K * x * x)
    return y, dy


def _rms_fwd(x, g):
    rs = lax.rsqrt(jnp.mean(x * x, axis=-1, keepdims=True) + RMS_EPS)
    n = x * rs
    return n * g, n, rs


def _rms_bwd(dy, n, rs, g):
    dn = dy * g
    dx = rs * (dn - n * jnp.mean(dn * n, axis=-1, keepdims=True))
    dg = jnp.sum(dy * n, axis=0, keepdims=True)
    return dx, dg


def _cparams(sem=None):
    kw = dict(vmem_limit_bytes=VMEM_LIMIT)
    if sem is not None:
        kw["dimension_semantics"] = sem
    return pltpu.CompilerParams(**kw)


def _tile(n, pref):
    return pref if n % pref == 0 else n


def _const(shape):
    nd = len(shape)
    return pl.BlockSpec(shape, lambda *_: (0,) * nd, pipeline_mode=pl.Buffered(1))


def _acc(shape):
    nd = len(shape)
    return pl.BlockSpec(shape, lambda *_: (0,) * nd)


def _rows(ts, w, col=0, order=None):
    if order is None:
        return pl.BlockSpec((ts, w), lambda i: (i, col))
    return pl.BlockSpec((ts, w), lambda i: (order(i), col))


def _sds(shape, dtype=F32):
    return jax.ShapeDtypeStruct(shape, dtype)


def _mix_in_fwd(x, gpre, w_in_p):
    s = x.shape[0]
    ts = _tile(s, TILE_S)

    def body(x_ref, g_ref, w_ref, o_ref):
        h, _, _ = _rms_fwd(x_ref[...], g_ref[...])
        o_ref[...] = _mm(h, w_ref[...])

    return pl.pallas_call(
        body, grid=(s // ts,),
        in_specs=[_rows(ts, D_MODEL), _const((1, D_MODEL)), _const((D_MODEL, PW))],
        out_specs=_rows(ts, PW), out_shape=_sds((s, PW)),
        name="mix_in_fwd", compiler_params=_cparams(("parallel",)))(x, gpre, w_in_p)


def _conv_taps(xr, hp, hn, first, last):
    ts = xr.shape[0]
    hp = jnp.where(first, 0.0, hp)
    hn = jnp.where(last, 0.0, hn)
    xe = jnp.concatenate([hp, xr, hn], axis=0)
    return xe[6:6 + ts], xe[7:7 + ts], xr, xe[9:9 + ts]


def _conv_fwd(xr, hp, hn, cw, cb, first, last):
    t0, t1, t2, t3 = _conv_taps(xr, hp, hn, first, last)
    return cw[0:1] * t0 + cw[1:2] * t1 + cw[2:3] * t2 + cw[3:4] * t3 + cb


def _rnn_gates(xc, wa, ba, wx, bx, lam):
    r = _sigmoid(_mm(xc, wa) + ba)
    i = _sigmoid(_mm(xc, wx) + bx)
    sp = _softplus(-lam)
    la = (-LRU_C) * r * sp
    a = jnp.exp(la)
    mult = jnp.sqrt(-jnp.tanh(la) * (a * a + 1.0))
    return r, i, sp, a, mult


def _scan_tile(a_scr, u_scr, h_ref, c0, reverse):
    ts = a_scr.shape[0]
    a = a_scr[...]
    u = u_scr[...]
    row = lax.broadcasted_iota(jnp.int32, a.shape, 0) % 8
    for k in (1, 2, 4):
        if reverse:
            a_sh = pltpu.roll(a, ts - k, 0)
            u_sh = pltpu.roll(u, ts - k, 0)
            ok = row < 8 - k
        else:
            a_sh = pltpu.roll(a, k, 0)
            u_sh = pltpu.roll(u, k, 0)
            ok = row >= k
        u = jnp.where(ok, u + a * u_sh, u)
        a = jnp.where(ok, a * a_sh, a)
    a_scr[...] = a
    u_scr[...] = u
    ng = ts // 8

    def body(j, c):
        g = (ng - 1 - j) if reverse else j
        sl = pl.ds(pl.multiple_of(g * 8, 8), 8)
        hh = u_scr[sl, :] + a_scr[sl, :] * c
        h_ref[sl, :] = hh
        return hh[0:1, :] if reverse else hh[7:8, :]

    return lax.fori_loop(0, ng, body, c0)


def _halo_specs(s, ts, w, col, order):
    n8 = s // 8
    per = ts // 8
    prev = pl.BlockSpec((8, w), lambda i: (jnp.maximum(order(i) * per - 1, 0), col))
    nxt = pl.BlockSpec((8, w), lambda i: (jnp.minimum((order(i) + 1) * per, n8 - 1), col))
    return prev, nxt


def _rnn_fwd(proj, cw, cb, wa, ba, wx, bx, lam, reverse):
    s = proj.shape[0]
    ts = _tile(s, TILE_S)
    nt = s // ts
    order = (lambda i: nt - 1 - i) if reverse else (lambda i: i)

    def body(xr_ref, hp_ref, hn_ref, cw_ref, cb_ref, wa_ref, ba_ref, wx_ref, bx_ref, lam_ref,
             h_ref, a_scr, u_scr, c_scr):
        i = pl.program_id(0)
        t = order(i)

        @pl.when(i == 0)
        def _():
            c_scr[...] = jnp.zeros_like(c_scr)

        xc = _conv_fwd(xr_ref[...], hp_ref[...], hn_ref[...], cw_ref[...], cb_ref[...], t == 0, t == nt - 1)
        _, gi, _, a, mult = _rnn_gates(xc, wa_ref[...], ba_ref[...], wx_ref[...], bx_ref[...], lam_ref[...])
        a_scr[...] = a
        u_scr[...] = xc * gi * mult
        c_scr[0:1, :] = _scan_tile(a_scr, u_scr, h_ref, c_scr[0:1, :], reverse)

    hp, hn = _halo_specs(s, ts, D_RNN, 0, order)
    return pl.pallas_call(
        body, grid=(nt,),
        in_specs=[_rows(ts, D_RNN, 0, order), hp, hn, _const((CONV_WIDTH, D_RNN)), _const((1, D_RNN)),
                  _const((D_RNN, D_RNN)), _const((1, D_RNN)), _const((D_RNN, D_RNN)), _const((1, D_RNN)),
                  _const((1, D_RNN))],
        out_specs=_rows(ts, D_RNN, 0, order), out_shape=_sds((s, D_RNN)),
        scratch_shapes=[pltpu.VMEM((ts, D_RNN), F32), pltpu.VMEM((ts, D_RNN), F32), pltpu.VMEM((8, D_RNN), F32)],
        name="rnn_fwd_rev" if reverse else "rnn_fwd",
        compiler_params=_cparams(("arbitrary",)))(proj, proj, proj, cw, cb, wa, ba, wx, bx, lam)


def _tri(reverse, transpose=False):
    r = lax.broadcasted_iota(jnp.int32, (GLA_CHUNK, GLA_CHUNK), 0)
    c = lax.broadcasted_iota(jnp.int32, (GLA_CHUNK, GLA_CHUNK), 1)
    if transpose:
        r, c = c, r
    return ((r <= c) if reverse else (r >= c)).astype(F32)


def _gla_chunk_terms(q, k, la, tri, reverse):
    b = _mm_tri(tri, la)
    bl = b[0:1] if reverse else b[GLA_CHUNK - 1:GLA_CHUNK]
    eb = jnp.exp(b)
    enb = jnp.exp(-b)
    ebl = jnp.exp(bl - b)
    d = jnp.exp(bl)
    return eb, enb, ebl, d, q * (GLA_DK ** -0.5) * eb, k * enb, k * ebl


def _gla_gate(lr, wg, bg):
    z = _mm(lr, wg) + bg
    return z, -_softplus_coarse(-z) * (1.0 / GLA_TAU)


def _gla_fwd(proj, wg, bg, reverse, o_add=None):
    s = proj.shape[0]
    ts = _tile(s, TILE_S)
    nt = s // ts
    ch = ts // GLA_CHUNK
    hw = GLA_HEADS * LANES
    order = (lambda i: nt - 1 - i) if reverse else (lambda i: i)
    extra = [] if o_add is None else [o_add]

    def body(q_ref, k_ref, v_ref, lr_ref, wg_ref, bg_ref, *rest):
        add_ref = None if o_add is None else rest[0]
        o_ref, st_ref, s_scr = rest[len(extra):]

        @pl.when(pl.program_id(0) == 0)
        def _():
            s_scr[...] = jnp.zeros_like(s_scr)

        _, la = _gla_gate(lr_ref[...], wg_ref[...], bg_ref[...])
        tri = _tri(reverse)
        keep = tri > 0.5
        for cc in range(ch):
            c = (ch - 1 - cc) if reverse else cc
            rows = slice(c * GLA_CHUNK, (c + 1) * GLA_CHUNK)
            _, _, _, d, qe, ke, kd = _gla_chunk_terms(q_ref[rows, :], k_ref[rows, :], la[rows], tri, reverse)
            outs = []
            for h in range(GLA_HEADS):
                ln = slice(h * LANES, (h + 1) * LANES)
                st = s_scr[h]
                st_ref[c, h] = st
                v_h = v_ref[rows, ln]
                a_m = jnp.where(keep, _mm_nt(qe[:, ln], ke[:, ln]), 0.0)
                outs.append(_mm(a_m, v_h) + _mm_nt(qe[:, ln], st))
                s_scr[h] = d[:, ln] * st + _mm_tn(v_h, kd[:, ln])
            o = jnp.concatenate(outs, axis=1)
            o_ref[rows, :] = o if add_ref is None else o + add_ref[rows, :]

    return pl.pallas_call(
        body, grid=(nt,),
        in_specs=[_rows(ts, hw, 2, order), _rows(ts, hw, 3, order), _rows(ts, hw, 4, order),
                  _rows(ts, LANES, COL_LR // LANES, order), _const((LANES, hw)), _const((1, hw))]
                 + [_rows(ts, hw, 0, order)] * len(extra),
        out_specs=[_rows(ts, hw, 0, order),
                   pl.BlockSpec((ch, GLA_HEADS, LANES, LANES), lambda i: (order(i), 0, 0, 0))],
        out_shape=[_sds((s, hw)), _sds((s // GLA_CHUNK, GLA_HEADS, LANES, LANES))],
        scratch_shapes=[pltpu.VMEM((GLA_HEADS, LANES, LANES), F32)],
        name="gla_fwd_rev" if reverse else "gla_fwd",
        compiler_params=_cparams(("arbitrary",)))(proj, proj, proj, proj, wg, bg, *extra)


def _mix_out_terms(hf, hb, gate_r, osum, g, g_rnn, g_gla):
    hs = hf + hb
    gl, dgl = _gelu_and_grad(gate_r)
    z = hs * gl
    y_rnn, n_rnn, rs_rnn = _rms_fwd(z, g_rnn)
    sg_lin = _sigmoid(g)
    sg = g * sg_lin
    dsg = sg_lin * (1.0 + g * (1.0 - sg_lin))
    ons, ns, rss = [], [], []
    for h in range(GLA_HEADS):
        ln = slice(h * LANES, (h + 1) * LANES)
        on, n, rs = _rms_fwd(osum[:, ln], g_gla)
        ons.append(on)
        ns.append(n)
        rss.append(rs)
    on = jnp.concatenate(ons, axis=1)
    return hs, gl, dgl, y_rnn, n_rnn, rs_rnn, sg, dsg, on, ns, rss


def _mix_out_fwd(x, hf, hb, osum, proj, g_rnn, g_gla, w_out, gpost):
    s = x.shape[0]
    ts = _tile(s, TILE_S)

    def body(x_ref, hf_ref, hb_ref, gr_ref, os_ref, g_ref, grnn_ref, ggla_ref, w_ref, gp_ref, x1_ref, y_ref):
        _, _, _, y_rnn, _, _, sg, _, on, _, _ = _mix_out_terms(
            hf_ref[...], hb_ref[...], gr_ref[...], os_ref[...], g_ref[...], grnn_ref[...], ggla_ref[...])
        y = jnp.concatenate([y_rnn, on * sg], axis=1).astype(MXU_DTYPE)
        y_ref[...] = y
        out, _, _ = _rms_fwd(_mm(y, w_ref[...]), gp_ref[...])
        x1_ref[...] = x_ref[...] + out

    return pl.pallas_call(
        body, grid=(s // ts,),
        in_specs=[_rows(ts, D_MODEL), _rows(ts, D_RNN), _rows(ts, D_RNN), _rows(ts, D_RNN, 1), _rows(ts, 512),
                  _rows(ts, 512, 5), _const((1, D_RNN)), _const((1, GLA_DV)),
                  _const((D_MODEL, D_MODEL)), _const((1, D_MODEL))],
        out_specs=[_rows(ts, D_MODEL), _rows(ts, D_MODEL)],
        out_shape=[_sds((s, D_MODEL)), _sds((s, D_MODEL), MXU_DTYPE)],
        name="mix_out_fwd", compiler_params=_cparams(("parallel",)))(
            x, hf, hb, proj, osum, proj, g_rnn, g_gla, w_out, gpost)


def _f_chunks():
    return [(c0, min(c0 + F_CHUNK, D_FF)) for c0 in range(0, D_FF, F_CHUNK)]


def _ffn_fwd(x1, gpre, wg, wu, wd, gpost):
    s = x1.shape[0]
    ts = _tile(s, TILE_F)

    def body(x_ref, gpre_ref, wg_ref, wu_ref, wd_ref, gpost_ref, x2_ref, a_ref, u_ref, f_ref):
        x = x_ref[...]
        h, _, _ = _rms_fwd(x, gpre_ref[...])
        h = h.astype(MXU_DTYPE)
        f = jnp.zeros((ts, D_MODEL), F32)
        for c0, c1 in _f_chunks():
            a = _mm(h, wg_ref[:, c0:c1])
            u = _mm(h, wu_ref[:, c0:c1])
            a_ref[:, c0:c1] = a.astype(MXU_DTYPE)
            u_ref[:, c0:c1] = u.astype(MXU_DTYPE)
            f = f + _mm(a * _sigmoid(a) * u, wd_ref[c0:c1, :])
        f_ref[...] = f
        out, _, _ = _rms_fwd(f, gpost_ref[...])
        x2_ref[...] = x + out

    return pl.pallas_call(
        body, grid=(s // ts,),
        in_specs=[_rows(ts, D_MODEL), _const((1, D_MODEL)), _const((D_MODEL, D_FF)), _const((D_MODEL, D_FF)),
                  _const((D_FF, D_MODEL)), _const((1, D_MODEL))],
        out_specs=[_rows(ts, D_MODEL), _rows(ts, D_FF), _rows(ts, D_FF), _rows(ts, D_MODEL)],
        out_shape=[_sds((s, D_MODEL)), _sds((s, D_FF), MXU_DTYPE), _sds((s, D_FF), MXU_DTYPE), _sds((s, D_MODEL))],
        name="ffn_fwd", compiler_params=_cparams(("parallel",)))(x1, gpre, wg, wu, wd, gpost)


def _loss_fwd_bwd(y, target):
    s = y.shape[0]
    ts = _tile(s, TILE_S)

    def body(y_ref, t_ref, loss_ref, dy_ref):
        @pl.when(pl.program_id(0) == 0)
        def _():
            loss_ref[...] = jnp.zeros_like(loss_ref)

        e = y_ref[...] - t_ref[...]
        dy_ref[...] = e * (1.0 / D_MODEL)
        part = jnp.sum(jnp.sum(e * e, axis=1, keepdims=True), axis=0, keepdims=True) * (0.5 / D_MODEL)
        loss_ref[...] += jnp.broadcast_to(part, loss_ref.shape)

    return pl.pallas_call(
        body, grid=(s // ts,),
        in_specs=[_rows(ts, D_MODEL), _rows(ts, D_MODEL)],
        out_specs=[_acc((8, LANES)), _rows(ts, D_MODEL)],
        out_shape=[_sds((8, LANES)), _sds((s, D_MODEL))],
        name="loss", compiler_params=_cparams(("arbitrary",)))(y, target)


def _tn_matmul(a, b, name):
    s, k = a.shape
    n = b.shape[1]
    ts = _tile(s, TILE_TN)
    tn = max(t for t in range(LANES, n + 1, LANES) if n % t == 0 and (k * t * 4 <= TN_ACC_BYTES or t == LANES))
    ns = s // ts

    def body(a_ref, b_ref, o_ref, acc):
        i = pl.program_id(1)

        @pl.when(i == 0)
        def _():
            acc[...] = jnp.zeros_like(acc)

        acc[...] += _mm_tn(a_ref[...], b_ref[...])

        @pl.when(i == ns - 1)
        def _():
            o_ref[...] = acc[...].astype(o_ref.dtype)

    return pl.pallas_call(
        body, grid=(n // tn, ns),
        in_specs=[pl.BlockSpec((ts, k), lambda j, i: (i, 0)), pl.BlockSpec((ts, tn), lambda j, i: (i, j))],
        out_specs=pl.BlockSpec((k, tn), lambda j, i: (0, j)), out_shape=_sds((k, n), MXU_DTYPE),
        scratch_shapes=[pltpu.VMEM((k, tn), F32)],
        name=name, compiler_params=_cparams(("parallel", "arbitrary")))(a, b)


def _ffn_bwd(dx2, f, x1, a, u, gpre, wg, wu, wd, gpost):
    s = x1.shape[0]
    ts = _tile(s, TILE_F)

    def body(dx2_ref, f_ref, x1_ref, a_ref, u_ref, gpre_ref, wg_ref, wu_ref, wd_ref, gpost_ref,
             dx1_ref, df_ref, h_ref, p_ref, da_ref, du_ref, dgpost_ref, dgpre_ref):
        @pl.when(pl.program_id(0) == 0)
        def _():
            dgpost_ref[...] = jnp.zeros_like(dgpost_ref)
            dgpre_ref[...] = jnp.zeros_like(dgpre_ref)

        dx2 = dx2_ref[...]
        _, nf, rsf = _rms_fwd(f_ref[...], gpost_ref[...])
        df, dgpost = _rms_bwd(dx2, nf, rsf, gpost_ref[...])
        dgpost_ref[...] += dgpost
        df = df.astype(MXU_DTYPE)
        df_ref[...] = df
        h, n1, rs1 = _rms_fwd(x1_ref[...], gpre_ref[...])
        h_ref[...] = h.astype(MXU_DTYPE)
        dh = jnp.zeros((ts, D_MODEL), F32)
        for c0, c1 in _f_chunks():
            av = a_ref[:, c0:c1].astype(F32)
            uv = u_ref[:, c0:c1].astype(F32)
            sg = _sigmoid(av)
            dp = _mm_nt(df, wd_ref[c0:c1, :])
            p_ref[:, c0:c1] = (av * sg * uv).astype(MXU_DTYPE)
            da = (dp * uv * sg * (1.0 + av * (1.0 - sg))).astype(MXU_DTYPE)
            du = (dp * av * sg).astype(MXU_DTYPE)
            da_ref[:, c0:c1] = da
            du_ref[:, c0:c1] = du
            dh = dh + _mm_nt(da, wg_ref[:, c0:c1]) + _mm_nt(du, wu_ref[:, c0:c1])
        dx, dgpre = _rms_bwd(dh, n1, rs1, gpre_ref[...])
        dgpre_ref[...] += dgpre
        dx1_ref[...] = dx2 + dx

    return pl.pallas_call(
        body, grid=(s // ts,),
        in_specs=[_rows(ts, D_MODEL), _rows(ts, D_MODEL), _rows(ts, D_MODEL), _rows(ts, D_FF), _rows(ts, D_FF),
                  _const((1, D_MODEL)), _const((D_MODEL, D_FF)), _const((D_MODEL, D_FF)), _const((D_FF, D_MODEL)),
                  _const((1, D_MODEL))],
        out_specs=[_rows(ts, D_MODEL), _rows(ts, D_MODEL), _rows(ts, D_MODEL), _rows(ts, D_FF), _rows(ts, D_FF),
                   _rows(ts, D_FF), _acc((1, D_MODEL)), _acc((1, D_MODEL))],
        out_shape=[_sds((s, D_MODEL)), _sds((s, D_MODEL), MXU_DTYPE), _sds((s, D_MODEL), MXU_DTYPE),
                   _sds((s, D_FF), MXU_DTYPE), _sds((s, D_FF), MXU_DTYPE), _sds((s, D_FF), MXU_DTYPE),
                   _sds((1, D_MODEL)), _sds((1, D_MODEL))],
        name="ffn_bwd", compiler_params=_cparams(("arbitrary",)))(dx2, f, x1, a, u, gpre, wg, wu, wd, gpost)


def _mix_out_bwd(dx1, y, hf, hb, osum, proj, g_rnn, g_gla, w_out, gpost):
    s = y.shape[0]
    ts = _tile(s, TILE_S)

    def body(dx1_ref, y_ref, hf_ref, hb_ref, gr_ref, os_ref, g_ref, grnn_ref, ggla_ref, w_ref, gp_ref,
             dm_ref, dhs_ref, dgr_ref, dos_ref, dg_ref, dgpost_ref, dgrnn_ref, dggla_ref):
        @pl.when(pl.program_id(0) == 0)
        def _():
            dgpost_ref[...] = jnp.zeros_like(dgpost_ref)
            dgrnn_ref[...] = jnp.zeros_like(dgrnn_ref)
            dggla_ref[...] = jnp.zeros_like(dggla_ref)

        _, nm, rsm = _rms_fwd(_mm(y_ref[...], w_ref[...]), gp_ref[...])
        dm, dgpost = _rms_bwd(dx1_ref[...], nm, rsm, gp_ref[...])
        dgpost_ref[...] += dgpost
        dm = dm.astype(MXU_DTYPE)
        dm_ref[...] = dm
        dy = _mm_nt(dm, w_ref[...])
        hs, gl, dgl, _, n_rnn, rs_rnn, sg, dsg, on, ns, rss = _mix_out_terms(
            hf_ref[...], hb_ref[...], gr_ref[...], os_ref[...], g_ref[...], grnn_ref[...], ggla_ref[...])
        dz, dgrnn = _rms_bwd(dy[:, :D_RNN], n_rnn, rs_rnn, grnn_ref[...])
        dgrnn_ref[...] += dgrnn
        dhs_ref[...] = dz * gl
        dgr_ref[...] = (dz * hs * dgl).astype(MXU_DTYPE)
        dyg = dy[:, D_RNN:]
        dg_ref[...] = (dyg * on * dsg).astype(MXU_DTYPE)
        don = dyg * sg
        dggla = jnp.zeros((1, GLA_DV), F32)
        for h in range(GLA_HEADS):
            ln = slice(h * LANES, (h + 1) * LANES)
            dos, dgh = _rms_bwd(don[:, ln], ns[h], rss[h], ggla_ref[...])
            dos_ref[:, ln] = dos.astype(MXU_DTYPE)
            dggla = dggla + dgh
        dggla_ref[...] += dggla

    return pl.pallas_call(
        body, grid=(s // ts,),
        in_specs=[_rows(ts, D_MODEL), _rows(ts, D_MODEL), _rows(ts, D_RNN), _rows(ts, D_RNN), _rows(ts, D_RNN, 1),
                  _rows(ts, 512), _rows(ts, 512, 5), _const((1, D_RNN)), _const((1, GLA_DV)),
                  _const((D_MODEL, D_MODEL)), _const((1, D_MODEL))],
        out_specs=[_rows(ts, D_MODEL), _rows(ts, D_RNN), _rows(ts, D_RNN), _rows(ts, 512), _rows(ts, 512),
                   _acc((1, D_MODEL)), _acc((1, D_RNN)), _acc((1, GLA_DV))],
        out_shape=[_sds((s, D_MODEL), MXU_DTYPE), _sds((s, D_RNN)), _sds((s, D_RNN), MXU_DTYPE),
                   _sds((s, 512), MXU_DTYPE), _sds((s, 512), MXU_DTYPE),
                   _sds((1, D_MODEL)), _sds((1, D_RNN)), _sds((1, GLA_DV))],
        name="mix_out_bwd", compiler_params=_cparams(("arbitrary",)))(
            dx1, y, hf, hb, proj, osum, proj, g_rnn, g_gla, w_out, gpost)


def _gla_bwd(dos, proj, st, wg, bg, reverse, prev=None):
    s = proj.shape[0]
    ts = _tile(s, TILE_S)
    nt = s // ts
    ch = ts // GLA_CHUNK
    hw = GLA_HEADS * LANES
    order = (lambda i: i) if reverse else (lambda i: nt - 1 - i)

    n_prev = 0 if prev is None else 4

    def body(do_ref, q_ref, k_ref, v_ref, lr_ref, st_ref, wg_ref, bg_ref, *rest):
        pq_ref, pk_ref, pv_ref, plr_ref = rest[:n_prev] if n_prev else (None,) * 4
        dq_ref, dk_ref, dv_ref, dlr_ref, dwg_ref, dbg_ref, ds_scr, dz_scr = rest[n_prev:]

        def put(ref, p_ref, rows, cols, val):
            if p_ref is not None:
                val = val + p_ref[rows, cols].astype(F32)
            ref[rows, cols] = val.astype(ref.dtype)

        @pl.when(pl.program_id(0) == 0)
        def _():
            ds_scr[...] = jnp.zeros_like(ds_scr)
            dwg_ref[...] = jnp.zeros_like(dwg_ref)
            dbg_ref[...] = jnp.zeros_like(dbg_ref)

        z, la = _gla_gate(lr_ref[...], wg_ref[...], bg_ref[...])
        tri = _tri(reverse)
        tri_t = _tri(reverse, transpose=True)
        keep = tri > 0.5
        last_row = 0 if reverse else GLA_CHUNK - 1
        is_last = lax.broadcasted_iota(jnp.int32, (GLA_CHUNK, hw), 0) == last_row
        for cc in range(ch):
            c = cc if reverse else (ch - 1 - cc)
            rows = slice(c * GLA_CHUNK, (c + 1) * GLA_CHUNK)
            eb, enb, ebl, d, qe, ke, kd = _gla_chunk_terms(q_ref[rows, :], k_ref[rows, :], la[rows], tri, reverse)
            dqe, dke, dkd, dd = [], [], [], []
            for h in range(GLA_HEADS):
                ln = slice(h * LANES, (h + 1) * LANES)
                st_h = st_ref[c, h]
                dst = ds_scr[h]
                v_h = v_ref[rows, ln]
                do_h = do_ref[rows, ln]
                qe_h, ke_h, kd_h = qe[:, ln], ke[:, ln], kd[:, ln]
                a_m = jnp.where(keep, _mm_nt(qe_h, ke_h), 0.0)
                da_m = jnp.where(keep, _mm_nt(do_h, v_h), 0.0)
                put(dv_ref, pv_ref, rows, ln, _mm_tn(a_m, do_h) + _mm_nt(kd_h, dst))
                dqe.append(_mm(da_m, ke_h) + _mm(do_h, st_h))
                dke.append(_mm_tn(da_m, qe_h))
                dkd.append(_mm(v_h, dst))
                dd.append(jnp.sum(dst * st_h, axis=0, keepdims=True))
                ds_scr[h] = _mm_tn(do_h, qe_h) + d[:, ln] * dst
            dqe = jnp.concatenate(dqe, axis=1)
            dke = jnp.concatenate(dke, axis=1)
            dkd = jnp.concatenate(dkd, axis=1)
            dd = jnp.concatenate(dd, axis=1)
            dbl = dd * d + jnp.sum(dkd * kd, axis=0, keepdims=True)
            db = dqe * qe - dke * ke - dkd * kd
            db = jnp.where(is_last, db + dbl, db)
            put(dq_ref, pq_ref, rows, slice(None), dqe * eb * (GLA_DK ** -0.5))
            put(dk_ref, pk_ref, rows, slice(None), dke * enb + dkd * ebl)
            dla = _mm_tri(tri_t, db)
            dz_scr[rows, :] = dla * (1.0 / GLA_TAU) * _sigmoid(-z[rows])
        dz = dz_scr[...]
        put(dlr_ref, plr_ref, slice(None), slice(None), _mm_nt(dz, wg_ref[...]))
        dwg_ref[...] += _mm_tn(lr_ref[...], dz)
        dbg_ref[...] += jnp.sum(dz, axis=0, keepdims=True)

    wide, narrow = _rows(ts, hw, 0, order), _rows(ts, LANES, 0, order)
    return pl.pallas_call(
        body, grid=(nt,),
        in_specs=[wide, _rows(ts, hw, 2, order), _rows(ts, hw, 3, order), _rows(ts, hw, 4, order),
                  _rows(ts, LANES, COL_LR // LANES, order),
                  pl.BlockSpec((ch, GLA_HEADS, LANES, LANES), lambda i: (order(i), 0, 0, 0)),
                  _const((LANES, hw)), _const((1, hw))] + ([wide, wide, wide, narrow] if n_prev else []),
        out_specs=[wide, wide, wide, narrow, _acc((LANES, hw)), _acc((1, hw))],
        out_shape=[_sds((s, hw), MXU_DTYPE), _sds((s, hw), MXU_DTYPE), _sds((s, hw), MXU_DTYPE),
                   _sds((s, LANES), MXU_DTYPE), _sds((LANES, hw)), _sds((1, hw))],
        scratch_shapes=[pltpu.VMEM((GLA_HEADS, LANES, LANES), F32), pltpu.VMEM((ts, hw), F32)],
        name="gla_bwd_rev" if reverse else "gla_bwd",
        compiler_params=_cparams(("arbitrary",)))(dos, proj, proj, proj, proj, st, wg, bg, *(prev or ()))


def _rnn_bwd(dhs, h, proj, cw, cb, wa, ba, wx, bx, lam, reverse):
    s = proj.shape[0]
    ts = _tile(s, TILE_S)
    nt = s // ts
    order = (lambda i: i) if reverse else (lambda i: nt - 1 - i)
    back = not reverse

    def body(dh_ref, h_ref, hh_ref, xr_ref, hp_ref, hn_ref, cw_ref, cb_ref, wa_ref, ba_ref, wx_ref, bx_ref, lam_ref,
             dxc_ref, dwa_ref, dba_ref, dwx_ref, dbx_ref, dlam_ref, a_scr, u_scr, g_scr, c_scr):
        i = pl.program_id(0)
        t = order(i)

        @pl.when(i == 0)
        def _():
            c_scr[...] = jnp.zeros_like(c_scr)
            dwa_ref[...] = jnp.zeros_like(dwa_ref)
            dba_ref[...] = jnp.zeros_like(dba_ref)
            dwx_ref[...] = jnp.zeros_like(dwx_ref)
            dbx_ref[...] = jnp.zeros_like(dbx_ref)
            dlam_ref[...] = jnp.zeros_like(dlam_ref)

        xc = _conv_fwd(xr_ref[...], hp_ref[...], hn_ref[...], cw_ref[...], cb_ref[...], t == 0, t == nt - 1)
        r, gi, sp, a, mult = _rnn_gates(xc, wa_ref[...], ba_ref[...], wx_ref[...], bx_ref[...], lam_ref[...])
        row = lax.broadcasted_iota(jnp.int32, (ts, D_RNN), 0)
        hv = h_ref[...]
        if reverse:
            edge = jnp.where(t == nt - 1, 0.0, hh_ref[0:1, :])
            h_prev = jnp.where(row == ts - 1, edge, pltpu.roll(hv, ts - 1, 0))
            a_nxt = jnp.where(row == 0, 1.0, pltpu.roll(a, 1, 0))
        else:
            edge = jnp.where(t == 0, 0.0, hh_ref[7:8, :])
            h_prev = jnp.where(row == 0, edge, pltpu.roll(hv, 1, 0))
            a_nxt = jnp.where(row == ts - 1, 1.0, pltpu.roll(a, ts - 1, 0))
        a_scr[...] = a_nxt
        u_scr[...] = dh_ref[...]
        _scan_tile(a_scr, u_scr, g_scr, c_scr[0:1, :], back)
        dh = g_scr[...]
        if reverse:
            c_scr[0:1, :] = a[ts - 1:ts, :] * dh[ts - 1:ts, :]
        else:
            c_scr[0:1, :] = a[0:1, :] * dh[0:1, :]
        dmult = dh * xc * gi
        dla = dh * h_prev * a - dmult * a * a / mult
        dza = dla * (-LRU_C) * sp * r * (1.0 - r)
        dzx = dh * xc * mult * gi * (1.0 - gi)
        dsp = jnp.sum(dla * (-LRU_C) * r, axis=0, keepdims=True)
        dlam_ref[...] += dsp * (-_sigmoid(-lam_ref[...]))
        dxc_ref[...] = dh * gi * mult + _mm_nt(dza, wa_ref[...]) + _mm_nt(dzx, wx_ref[...])
        dwa_ref[...] += _mm_tn(xc, dza)
        dwx_ref[...] += _mm_tn(xc, dzx)
        dba_ref[...] += jnp.sum(dza, axis=0, keepdims=True)
        dbx_ref[...] += jnp.sum(dzx, axis=0, keepdims=True)

    hp, hn = _halo_specs(s, ts, D_RNN, 0, order)
    hhp, hhn = _halo_specs(s, ts, D_RNN, 0, order)
    sq = (D_RNN, D_RNN)
    vec = (1, D_RNN)
    return pl.pallas_call(
        body, grid=(nt,),
        in_specs=[_rows(ts, D_RNN, 0, order), _rows(ts, D_RNN, 0, order), hhn if reverse else hhp,
                  _rows(ts, D_RNN, 0, order), hp, hn, _const((CONV_WIDTH, D_RNN)), _const(vec),
                  _const(sq), _const(vec), _const(sq), _const(vec), _const(vec)],
        out_specs=[_rows(ts, D_RNN, 0, order), _acc(sq), _acc(vec), _acc(sq), _acc(vec), _acc(vec)],
        out_shape=[_sds((s, D_RNN)), _sds(sq), _sds(vec), _sds(sq), _sds(vec), _sds(vec)],
        scratch_shapes=[pltpu.VMEM((ts, D_RNN), F32), pltpu.VMEM((ts, D_RNN), F32), pltpu.VMEM((ts, D_RNN), F32),
                        pltpu.VMEM((8, D_RNN), F32)],
        name="rnn_bwd_rev" if reverse else "rnn_bwd",
        compiler_params=_cparams(("arbitrary",)))(dhs, h, h, proj, proj, proj, cw, cb, wa, ba, wx, bx, lam)


def _conv_bwd(dxc_f, dxc_b, proj, cw):
    s = proj.shape[0]
    ts = _tile(s, TILE_S)
    nt = s // ts
    ident = lambda i: i

    def body(df_ref, dfp_ref, dfn_ref, db_ref, dbp_ref, dbn_ref, xr_ref, xp_ref, xn_ref, cw_ref,
             dxr_ref, dcw_ref, dcb_ref):
        t = pl.program_id(0)

        @pl.when(t == 0)
        def _():
            dcw_ref[...] = jnp.zeros_like(dcw_ref)
            dcb_ref[...] = jnp.zeros_like(dcb_ref)

        first = t == 0
        last = t == nt - 1
        d = df_ref[...] + db_ref[...]
        d_m2, d_m1, _, d_p1 = _conv_taps(d, dfp_ref[...] + dbp_ref[...], dfn_ref[...] + dbn_ref[...], first, last)
        dn = jnp.where(last, 0.0, dfn_ref[...] + dbn_ref[...])
        d_p2 = jnp.concatenate([d, dn], axis=0)[2:2 + ts]
        del d_m2
        cw = cw_ref[...]
        dxr_ref[...] = (cw[0:1] * d_p2 + cw[1:2] * d_p1 + cw[2:3] * d + cw[3:4] * d_m1).astype(dxr_ref.dtype)
        taps = _conv_taps(xr_ref[...], xp_ref[...], xn_ref[...], first, last)
        dcw_ref[...] += jnp.concatenate([jnp.sum(d * tp, axis=0, keepdims=True) for tp in taps], axis=0)
        dcb_ref[...] += jnp.sum(d, axis=0, keepdims=True)

    hp, hn = _halo_specs(s, ts, D_RNN, 0, ident)
    return pl.pallas_call(
        body, grid=(nt,),
        in_specs=[_rows(ts, D_RNN), hp, hn, _rows(ts, D_RNN), hp, hn, _rows(ts, D_RNN), hp, hn,
                  _const((CONV_WIDTH, D_RNN))],
        out_specs=[_rows(ts, D_RNN), _acc((CONV_WIDTH, D_RNN)), _acc((1, D_RNN))],
        out_shape=[_sds((s, D_RNN), MXU_DTYPE), _sds((CONV_WIDTH, D_RNN)), _sds((1, D_RNN))],
        name="conv_bwd", compiler_params=_cparams(("arbitrary",)))(
            dxc_f, dxc_f, dxc_f, dxc_b, dxc_b, dxc_b, proj, proj, proj, cw)


def _mix_in_bwd(parts, dlr, x, dx1, gpre, w_in_p):
    s = x.shape[0]
    ts = _tile(s, TILE_S)
    n_parts = len(parts)

    def body(*refs):
        part_refs = refs[:n_parts + 1]
        x_ref, dx1_ref, g_ref, w_ref, dx_ref, dp_ref, h_ref, dgpre_ref = refs[n_parts + 1:]

        @pl.when(pl.program_id(0) == 0)
        def _():
            dgpre_ref[...] = jnp.zeros_like(dgpre_ref)

        dp = jnp.concatenate([r[...] for r in part_refs], axis=1)
        dp_ref[...] = dp
        h, n, rs = _rms_fwd(x_ref[...], g_ref[...])
        h_ref[...] = h.astype(MXU_DTYPE)
        dh = _mm_nt(dp, w_ref[...])
        dx, dgpre = _rms_bwd(dh, n, rs, g_ref[...])
        dgpre_ref[...] += dgpre
        dx_ref[...] = dx1_ref[...] + dx

    return pl.pallas_call(
        body, grid=(s // ts,),
        in_specs=[_rows(ts, 512)] * n_parts + [_rows(ts, LANES), _rows(ts, D_MODEL), _rows(ts, D_MODEL),
                                                _const((1, D_MODEL)), _const((D_MODEL, PW))],
        out_specs=[_rows(ts, D_MODEL), _rows(ts, PW), _rows(ts, D_MODEL), _acc((1, D_MODEL))],
        out_shape=[_sds((s, D_MODEL)), _sds((s, PW), MXU_DTYPE), _sds((s, D_MODEL), MXU_DTYPE), _sds((1, D_MODEL))],
        name="mix_in_bwd", compiler_params=_cparams(("arbitrary",)))(*parts, dlr, x, dx1, gpre, w_in_p)


def _pad_heads(w):
    sh = w.shape[:-1]
    w = w.reshape(sh + (GLA_HEADS, GLA_DK))
    w = jnp.pad(w, [(0, 0)] * (len(sh) + 1) + [(0, GLA_DKP - GLA_DK)])
    return w.reshape(sh + (GLA_HEADS * GLA_DKP,))


def _unpad_heads(w):
    sh = w.shape[:-1]
    return w.reshape(sh + (GLA_HEADS, GLA_DKP))[..., :GLA_DK].reshape(sh + (GLA_HEADS * GLA_DK,))


W_IN_COLS = 2592
W_IN_SHARD = W_IN_COLS // N_DEV
FF_SHARD = D_FF // N_DEV


def _w_in_pieces():
    segments = [(0, 1024, 0)]
    segments += [(1024 + GLA_DK * h, 1024 + GLA_DK * (h + 1), 1024 + GLA_DKP * h) for h in range(GLA_HEADS)]
    segments += [(1280 + GLA_DK * h, 1280 + GLA_DK * (h + 1), 1536 + GLA_DKP * h) for h in range(GLA_HEADS)]
    segments += [(1536, 2560, 2048), (2560, W_IN_COLS, COL_LR)]
    pieces = []
    for lo, hi, dst in segments:
        while lo < hi:
            j = lo // W_IN_SHARD
            end = min(hi, (j + 1) * W_IN_SHARD)
            pieces.append((j, lo - j * W_IN_SHARD, end - lo, dst))
            dst += end - lo
            lo = end
    return pieces


def _w_in_from_shards(w):
    nl = w.shape[0]
    tr = 256

    def body(w_ref, o_ref):
        o_ref[...] = jnp.zeros_like(o_ref)
        for j, src, width, dst in _w_in_pieces():
            o_ref[:, dst:dst + width] = w_ref[j, :, src:src + width]

    return pl.pallas_call(
        body, grid=(nl, D_MODEL // tr),
        in_specs=[pl.BlockSpec((None, N_DEV, tr, W_IN_SHARD), lambda l, i: (l, 0, i, 0))],
        out_specs=pl.BlockSpec((None, tr, PW), lambda l, i: (l, i, 0)), out_shape=_sds((nl, D_MODEL, PW), w.dtype),
        name="w_in_from_shards", compiler_params=_cparams(("parallel", "parallel")))(w)


def _w_in_to_shards(g):
    tr = 256

    def body(g_ref, o_ref):
        for j, src, width, dst in _w_in_pieces():
            o_ref[j, :, src:src + width] = g_ref[:, dst:dst + width]

    return pl.pallas_call(
        body, grid=(D_MODEL // tr,),
        in_specs=[pl.BlockSpec((tr, PW), lambda i: (i, 0))],
        out_specs=pl.BlockSpec((N_DEV, tr, W_IN_SHARD), lambda i: (0, i, 0)),
        out_shape=_sds((N_DEV, D_MODEL, W_IN_SHARD), g.dtype),
        name="w_in_to_shards", compiler_params=_cparams(("parallel",)))(g)


def _cols_from_shards(w, name):
    nl, _, d, c = w.shape
    tr = 256

    def body(w_ref, o_ref):
        for j in range(N_DEV):
            o_ref[:, j * c:(j + 1) * c] = w_ref[j]

    return pl.pallas_call(
        body, grid=(nl, d // tr),
        in_specs=[pl.BlockSpec((None, N_DEV, tr, c), lambda l, i: (l, 0, i, 0))],
        out_specs=pl.BlockSpec((None, tr, N_DEV * c), lambda l, i: (l, i, 0)),
        out_shape=_sds((nl, d, N_DEV * c), w.dtype),
        name=name, compiler_params=_cparams(("parallel", "parallel")))(w)


def _cols_to_shards(g, name):
    d, n = g.shape
    c = n // N_DEV
    tr = 256

    def body(g_ref, o_ref):
        for j in range(N_DEV):
            o_ref[j] = g_ref[:, j * c:(j + 1) * c]

    return pl.pallas_call(
        body, grid=(d // tr,),
        in_specs=[pl.BlockSpec((tr, n), lambda i: (i, 0))],
        out_specs=pl.BlockSpec((N_DEV, tr, c), lambda i: (0, i, 0)), out_shape=_sds((N_DEV, d, c), g.dtype),
        name=name, compiler_params=_cparams(("parallel",)))(g)


def _block_diag(w):
    n, b, _ = w.shape
    eye = jnp.eye(n, dtype=w.dtype)
    return (w[:, :, None, :] * eye[:, None, :, None]).reshape(n * b, n * b)


def _block_diag_of(w):
    n = D_RNN // 64
    eye = jnp.eye(n, dtype=w.dtype)
    return (w.reshape(n, 64, n, 64) * eye[:, None, :, None]).sum(axis=2)


def _gate_weight(wg, direction):
    w = _pad_heads(wg)
    lo = direction * GLA_RANK
    return jnp.pad(w, ((lo, LANES - GLA_RANK - lo), (0, 0)))


def _layer_weights(full, l):
    row = lambda v: v.reshape(1, -1)
    lw = dict(
        gpre=row(full["mix_norm_pre"][l]), gpost=row(full["mix_norm_post"][l]),
        w_in=full["w_in_p"][l].astype(MXU_DTYPE),
        cw=full["conv_w"][l], cb=row(full["conv_b"][l]),
        g_rnn=row(full["rnn_out_norm"][l]), g_gla=row(full["gla_out_norm"][l]),
        w_out=full["w_out"][l].astype(MXU_DTYPE),
        fpre=row(full["ffn_norm_pre"][l]), fpost=row(full["ffn_norm_post"][l]),
        wg=full["w_ffn_gate"][l].astype(MXU_DTYPE), wu=full["w_ffn_up"][l].astype(MXU_DTYPE),
        wd=full["w_ffn_down"][l].astype(MXU_DTYPE))
    for d in (0, 1):
        lw[f"wa{d}"] = _block_diag(full["lru_w_a"][l, d]).astype(MXU_DTYPE)
        lw[f"wx{d}"] = _block_diag(full["lru_w_x"][l, d]).astype(MXU_DTYPE)
        lw[f"ba{d}"] = row(full["lru_b_a"][l, d])
        lw[f"bx{d}"] = row(full["lru_b_x"][l, d])
        lw[f"lam{d}"] = row(full["lru_lambda"][l, d])
        lw[f"gw{d}"] = _gate_weight(full["gla_w_gate"][l, d], d).astype(MXU_DTYPE)
        lw[f"gb{d}"] = row(_pad_heads(full["gla_b_gate"][l, d]))
    return lw


def _layer_fwd(x, lw):
    proj = _mix_in_fwd(x, lw["gpre"], lw["w_in"])
    hs, sts = [], []
    osum = None
    for d in (0, 1):
        hs.append(_rnn_fwd(proj, lw["cw"], lw["cb"], lw[f"wa{d}"], lw[f"ba{d}"], lw[f"wx{d}"], lw[f"bx{d}"],
                           lw[f"lam{d}"], bool(d)))
        osum, st = _gla_fwd(proj, lw[f"gw{d}"], lw[f"gb{d}"], bool(d), osum)
        sts.append(st)
    x1, y = _mix_out_fwd(x, hs[0], hs[1], osum, proj, lw["g_rnn"], lw["g_gla"], lw["w_out"], lw["gpost"])
    x2, a, u, f = _ffn_fwd(x1, lw["fpre"], lw["wg"], lw["wu"], lw["wd"], lw["fpost"])
    saved = dict(x=x, proj=proj, hs=hs, osum=osum, sts=sts, y=y, x1=x1, a=a, u=u, f=f)
    return x2, saved


def _layer_bwd(dx2, sv, lw):
    g = {}
    dx1, df, h2, p, da, du, dfpost, dfpre = _ffn_bwd(dx2, sv["f"], sv["x1"], sv["a"], sv["u"], lw["fpre"], lw["wg"],
                                                     lw["wu"], lw["wd"], lw["fpost"])
    g["ffn_norm_post"], g["ffn_norm_pre"] = dfpost[0], dfpre[0]
    big = {}
    big["w_ffn_gate"] = _cols_to_shards(_tn_matmul(h2, da, "dw_ffn_gate"), "dw_ffn_gate_to_shards")
    big["w_ffn_up"] = _cols_to_shards(_tn_matmul(h2, du, "dw_ffn_up"), "dw_ffn_up_to_shards")
    big["w_ffn_down"] = _tn_matmul(p, df, "dw_ffn_down").reshape(N_DEV, FF_SHARD, D_MODEL)
    proj = sv["proj"]
    dm, dhs, dgr, dos, dg, dgpost, dgrnn, dggla = _mix_out_bwd(
        dx1, sv["y"], sv["hs"][0], sv["hs"][1], sv["osum"], proj, lw["g_rnn"], lw["g_gla"], lw["w_out"], lw["gpost"])
    g["mix_norm_post"], g["rnn_out_norm"], g["gla_out_norm"] = dgpost[0], dgrnn[0], dggla[0]
    big["w_out"] = _tn_matmul(sv["y"], dm, "dw_out").reshape(N_DEV, D_MODEL // N_DEV, D_MODEL)
    dxc = []
    gla = None
    gw, gb, wa, ba, wx, bx, lam = [], [], [], [], [], [], []
    for d in (0, 1):
        r = _gla_bwd(dos, proj, sv["sts"][d], lw[f"gw{d}"], lw[f"gb{d}"], bool(d), gla)
        gla = r[:4]
        lo = d * GLA_RANK
        gw.append(_unpad_heads(r[4][lo:lo + GLA_RANK]))
        gb.append(_unpad_heads(r[5][0]))
        r = _rnn_bwd(dhs, sv["hs"][d], proj, lw["cw"], lw["cb"], lw[f"wa{d}"], lw[f"ba{d}"], lw[f"wx{d}"],
                     lw[f"bx{d}"], lw[f"lam{d}"], bool(d))
        dxc.append(r[0])
        wa.append(_block_diag_of(r[1])); ba.append(r[2][0]); wx.append(_block_diag_of(r[3])); bx.append(r[4][0])
        lam.append(r[5][0])
    g["gla_w_gate"], g["gla_b_gate"] = jnp.stack(gw), jnp.stack(gb)
    g["lru_w_a"], g["lru_b_a"] = jnp.stack(wa), jnp.stack(ba)
    g["lru_w_x"], g["lru_b_x"], g["lru_lambda"] = jnp.stack(wx), jnp.stack(bx), jnp.stack(lam)
    dxr, dcw, dcb = _conv_bwd(dxc[0], dxc[1], proj, lw["cw"])
    g["conv_w"], g["conv_b"] = dcw, dcb[0]
    dx, dproj, h, dgpre = _mix_in_bwd((dxr, dgr, gla[0], gla[1], gla[2], dg), gla[3], sv["x"], dx1, lw["gpre"],
                                      lw["w_in"])
    g["mix_norm_pre"] = dgpre[0]
    big["w_in"] = _w_in_to_shards(_tn_matmul(h, dproj, "dw_in"))
    return dx, g, [big[n] for n in BIG_WEIGHTS]


WEIGHT_NAMES = ["mix_norm_pre", "mix_norm_post", "w_in", "conv_w", "conv_b", "lru_w_a", "lru_b_a", "lru_w_x", "lru_b_x",
                "lru_lambda", "rnn_out_norm", "gla_w_gate", "gla_b_gate", "gla_out_norm", "w_out", "ffn_norm_pre",
                "ffn_norm_post", "w_ffn_gate", "w_ffn_up", "w_ffn_down"]
BIG_WEIGHTS = ["w_in", "w_out", "w_ffn_gate", "w_ffn_up", "w_ffn_down"]
BIG_AXIS = {"w_in": 2, "w_out": 1, "w_ffn_gate": 2, "w_ffn_up": 2, "w_ffn_down": 1}


def _local_step(x, target, full, on_big_grads):
    lws = [_layer_weights(full, l) for l in range(DEPTH)]
    saved = []
    for l in range(DEPTH):
        x, sv = _layer_fwd(x, lws[l])
        saved.append(sv)
    loss, dx = _loss_fwd_bwd(x, target)
    grads = [None] * DEPTH
    for l in reversed(range(DEPTH)):
        dx, grads[l], big = _layer_bwd(dx, saved[l], lws[l])
        on_big_grads(l, big)
    g = {n: jnp.stack([grads[l][n] for l in range(DEPTH)]) for n in WEIGHT_NAMES if n not in BIG_WEIGHTS}
    return loss[0, 0], dx, g


MESH_ID = pl.DeviceIdType.MESH
ANY = pl.BlockSpec(memory_space=pl.ANY)
MESH_AXES = ("x", "y", "c")


def _all_gather(x, name):
    def body(x_ref, out_ref, send_sems, recv_sems, local_sem):
        mx, my, mc = lax.axis_index("x"), lax.axis_index("y"), lax.axis_index("c")
        me, sibling = (mx, my, mc), (mx, my, 1 - mc)
        chips = [(1 - mx, my), (mx, 1 - my), (1 - mx, 1 - my)]

        def slot(px, py, pc):
            return out_ref.at[4 * px + 2 * py + pc]

        def copy(k, block, to, src=None):
            return pltpu.make_async_remote_copy(
                src_ref=slot(*block) if src is None else src, dst_ref=slot(*block),
                send_sem=send_sems.at[k], recv_sem=recv_sems.at[k], device_id=to, device_id_type=MESH_ID)

        mine = pltpu.make_async_copy(x_ref, slot(*me), local_sem)
        mine.start()
        first = [copy(0, me, sibling, src=x_ref)]
        first += [copy(1 + j, me, (*chip, mc), src=x_ref) for j, chip in enumerate(chips)]
        for cp in first:
            cp.start()
        passed = [copy(4 + j, (*chip, mc), sibling) for j, chip in enumerate(chips)]
        for j, chip in enumerate(chips):
            copy(1 + j, (*chip, mc), me).wait_recv()
            passed[j].start()
        copy(0, sibling, me).wait_recv()
        for j, chip in enumerate(chips):
            copy(4 + j, (*chip, 1 - mc), me).wait_recv()
        for cp in first + passed:
            cp.wait_send()
        mine.wait()

    return pl.pallas_call(
        body, out_shape=_sds((N_DEV,) + x.shape, x.dtype), in_specs=[ANY], out_specs=ANY,
        scratch_shapes=[pltpu.SemaphoreType.DMA((7,)), pltpu.SemaphoreType.DMA((7,)), pltpu.SemaphoreType.DMA],
        name=name)(x)


def _all_to_all(g, name):
    def body(g_ref, out_ref, send_sems, recv_sems, local_sem):
        mx, my, mc = lax.axis_index("x"), lax.axis_index("y"), lax.axis_index("c")
        me = 4 * mx + 2 * my + mc
        mine = pltpu.make_async_copy(g_ref.at[me], out_ref.at[me], local_sem)
        mine.start()
        copies = []
        for r in range(1, N_DEV):
            px = 1 - mx if r & 4 else mx
            py = 1 - my if r & 2 else my
            pc = 1 - mc if r & 1 else mc
            cp = pltpu.make_async_remote_copy(
                src_ref=g_ref.at[4 * px + 2 * py + pc], dst_ref=out_ref.at[me],
                send_sem=send_sems.at[r - 1], recv_sem=recv_sems.at[r - 1],
                device_id=(px, py, pc), device_id_type=MESH_ID)
            cp.start()
            copies.append(cp)
        for cp in copies:
            cp.wait()
        mine.wait()

    return pl.pallas_call(
        body, out_shape=_sds(g.shape, g.dtype), in_specs=[ANY], out_specs=ANY,
        scratch_shapes=[pltpu.SemaphoreType.DMA((7,)), pltpu.SemaphoreType.DMA((7,)), pltpu.SemaphoreType.DMA],
        name=name)(g)


def _sum_adamw(parts, w, m, v, name):
    _, r, c = parts.shape
    tr = _tile(r, ADAM_TILE_ROWS)

    def body(p_ref, w_ref, m_ref, v_ref, g_ref, d_ref, m2_ref, v2_ref):
        g = p_ref[0]
        for k in range(1, N_DEV):
            g = g + p_ref[k]
        g_ref[...] = g
        m2 = ADAM_B1 * m_ref[...] + (1.0 - ADAM_B1) * g
        v2 = ADAM_B2 * v_ref[...] + (1.0 - ADAM_B2) * (g * g)
        m2_ref[...] = m2
        v2_ref[...] = v2
        m_hat = m2 / (1.0 - ADAM_B1 ** ADAM_STEP)
        v_hat = v2 / (1.0 - ADAM_B2 ** ADAM_STEP)
        d_ref[...] = -ADAM_LR * (m_hat / (jnp.sqrt(v_hat) + ADAM_EPS) + ADAM_WD * w_ref[...])

    flat = pl.BlockSpec((tr, c), lambda i: (i, 0))
    return pl.pallas_call(
        body, grid=(r // tr,),
        in_specs=[pl.BlockSpec((N_DEV, tr, c), lambda i: (0, i, 0)), flat, flat, flat],
        out_specs=[flat] * 4, out_shape=[_sds((r, c))] * 4,
        name=name, compiler_params=_cparams(("parallel",)))(parts, w, m, v)


def _gather_big_weights(shards):
    n = len(shards)

    def body(*refs):
        srcs, outs = refs[:n], refs[n:2 * n]
        send_sems, recv_sems, local_sems = refs[2 * n:]
        mx, my, mc = lax.axis_index("x"), lax.axis_index("y"), lax.axis_index("c")
        me, sibling = (mx, my, mc), (mx, my, 1 - mc)
        chips = [(1 - mx, my), (mx, 1 - my), (1 - mx, 1 - my)]

        def slot(a, px, py, pc):
            return outs[a].at[:, 4 * px + 2 * py + pc]

        def copy(k, a, block, to, own=False):
            return pltpu.make_async_remote_copy(
                src_ref=srcs[a] if own else slot(a, *block), dst_ref=slot(a, *block),
                send_sem=send_sems.at[k * n + a], recv_sem=recv_sems.at[k * n + a],
                device_id=to, device_id_type=MESH_ID)

        mine = [pltpu.make_async_copy(srcs[a], slot(a, *me), local_sems.at[a]) for a in range(n)]
        for cp in mine:
            cp.start()
        first = [copy(0, a, me, sibling, own=True) for a in range(n)]
        first += [copy(1 + j, a, me, (*chip, mc), own=True) for j, chip in enumerate(chips) for a in range(n)]
        for cp in first:
            cp.start()
        passed = [[copy(4 + j, a, (*chip, mc), sibling) for a in range(n)] for j, chip in enumerate(chips)]
        for j, chip in enumerate(chips):
            for a in range(n):
                copy(1 + j, a, (*chip, mc), me).wait_recv()
                passed[j][a].start()
        for a in range(n):
            copy(0, a, sibling, me).wait_recv()
        for j, chip in enumerate(chips):
            for a in range(n):
                copy(4 + j, a, (*chip, 1 - mc), me).wait_recv()
        for cp in first + [cp for row in passed for cp in row]:
            cp.wait_send()
        for cp in mine:
            cp.wait()

    return pl.pallas_call(
        body, out_shape=[_sds((s.shape[0], N_DEV) + s.shape[1:], s.dtype) for s in shards],
        in_specs=[ANY] * n, out_specs=[ANY] * n,
        scratch_shapes=[pltpu.SemaphoreType.DMA((7 * n,)), pltpu.SemaphoreType.DMA((7 * n,)),
                        pltpu.SemaphoreType.DMA((n,))],
        name="gather_matmul_weights")(*shards)


def _exchange_big_grads(srcs, recvs, layer):
    n = len(srcs)

    def body(*refs):
        src, out = refs[:n], refs[2 * n:3 * n]
        send_sems, recv_sems, local_sems = refs[3 * n:]
        mx, my, mc = lax.axis_index("x"), lax.axis_index("y"), lax.axis_index("c")
        me = 4 * mx + 2 * my + mc
        mine = [pltpu.make_async_copy(src[a].at[me], out[a].at[me, layer], local_sems.at[a]) for a in range(n)]
        for cp in mine:
            cp.start()
        copies = []
        for r in range(1, N_DEV):
            px = 1 - mx if r & 4 else mx
            py = 1 - my if r & 2 else my
            pc = 1 - mc if r & 1 else mc
            for a in range(n):
                cp = pltpu.make_async_remote_copy(
                    src_ref=src[a].at[4 * px + 2 * py + pc], dst_ref=out[a].at[me, layer],
                    send_sem=send_sems.at[(r - 1) * n + a], recv_sem=recv_sems.at[(r - 1) * n + a],
                    device_id=(px, py, pc), device_id_type=MESH_ID)
                cp.start()
                copies.append(cp)
        for cp in copies:
            cp.wait()
        for cp in mine:
            cp.wait()

    return pl.pallas_call(
        body, out_shape=[_sds(r.shape, r.dtype) for r in recvs],
        in_specs=[ANY] * (2 * n), out_specs=[ANY] * n, input_output_aliases={n + a: a for a in range(n)},
        scratch_shapes=[pltpu.SemaphoreType.DMA((7 * n,)), pltpu.SemaphoreType.DMA((7 * n,)),
                        pltpu.SemaphoreType.DMA((n,))],
        name=f"exchange_grads_layer{layer}")(*srcs, *recvs)


def _sum_adamw_big(parts, w, m, v, name):
    _, nl, a, b = parts.shape
    ta = _tile(a, 256)

    def body(p_ref, w_ref, m_ref, v_ref, g_ref, d_ref, m2_ref, v2_ref):
        g = p_ref[0].astype(F32)
        for k in range(1, N_DEV):
            g = g + p_ref[k].astype(F32)
        g_ref[...] = g
        m2 = ADAM_B1 * m_ref[...] + (1.0 - ADAM_B1) * g
        v2 = ADAM_B2 * v_ref[...] + (1.0 - ADAM_B2) * (g * g)
        m2_ref[...] = m2
        v2_ref[...] = v2
        m_hat = m2 / (1.0 - ADAM_B1 ** ADAM_STEP)
        v_hat = v2 / (1.0 - ADAM_B2 ** ADAM_STEP)
        d_ref[...] = -ADAM_LR * (m_hat / (jnp.sqrt(v_hat) + ADAM_EPS) + ADAM_WD * w_ref[...])

    blk = pl.BlockSpec((None, ta, b), lambda l, i: (l, i, 0))
    return pl.pallas_call(
        body, grid=(nl, a // ta),
        in_specs=[pl.BlockSpec((N_DEV, None, ta, b), lambda l, i: (0, l, i, 0)), blk, blk, blk],
        out_specs=[blk] * 4, out_shape=[_sds((nl, a, b))] * 4,
        name=name, compiler_params=_cparams(("parallel", "parallel")))(parts, w, m, v)


SMALL_SHARDED = [("conv_w", 2), ("lru_b_a", 2), ("lru_b_x", 2), ("lru_lambda", 2), ("gla_w_gate", 3), ("gla_b_gate", 2)]
REPLICATED = ["mix_norm_pre", "mix_norm_post", "conv_b", "lru_w_a", "lru_w_x", "rnn_out_norm", "gla_out_norm",
              "ffn_norm_pre", "ffn_norm_post"]


def _pack(arrays, cols, row_mult):
    flat = jnp.concatenate([a.reshape(-1) for a in arrays])
    unit = cols * row_mult
    total = -(-flat.shape[0] // unit) * unit
    return jnp.pad(flat, (0, total - flat.shape[0])).reshape(total // cols, cols)


def _pack_slots(arrays, cols, row_mult):
    flat = jnp.concatenate([a.reshape(N_DEV, -1) for a in arrays], axis=1)
    unit = cols * row_mult
    total = -(-flat.shape[1] // unit) * unit
    return jnp.pad(flat, ((0, 0), (0, total - flat.shape[1]))).reshape(N_DEV, total // cols, cols)


def _unpack(flat, shapes):
    flat = flat.reshape(-1)
    out, off = [], 0
    for sh in shapes:
        n = 1
        for d in sh:
            n *= d
        out.append(flat[off:off + n].reshape(sh))
        off += n
    return out


def _unpack_slots(flat, shapes):
    flat = flat.reshape(N_DEV, -1)
    out, off = [], 0
    for sh in shapes:
        n = 1
        for d in sh:
            n *= d
        out.append(flat[:, off:off + n].reshape((N_DEV,) + tuple(sh)))
        off += n
    return out


def _merge_shards(a, axis):
    a = jnp.moveaxis(a, 0, axis)
    sh = a.shape
    return a.reshape(sh[:axis] + (sh[axis] * sh[axis + 1],) + sh[axis + 2:])


def _split_shards(a, axis):
    sh = a.shape
    a = a.reshape(sh[:axis] + (N_DEV, sh[axis] // N_DEV) + sh[axis + 1:])
    return jnp.moveaxis(a, axis, 0)


def kernel(x, mix_norm_pre, mix_norm_post, w_in, conv_w, conv_b, lru_w_a, lru_b_a, lru_w_x, lru_b_x, lru_lambda, rnn_out_norm, gla_w_gate, gla_b_gate, gla_out_norm, w_out, ffn_norm_pre, ffn_norm_post, w_ffn_gate, w_ffn_up, w_ffn_down, loss_target, m_mix_norm_pre, m_mix_norm_post, m_w_in, m_conv_w, m_conv_b, m_lru_w_a, m_lru_b_a, m_lru_w_x, m_lru_b_x, m_lru_lambda, m_rnn_out_norm, m_gla_w_gate, m_gla_b_gate, m_gla_out_norm, m_w_out, m_ffn_norm_pre, m_ffn_norm_post, m_w_ffn_gate, m_w_ffn_up, m_w_ffn_down, v_mix_norm_pre, v_mix_norm_post, v_w_in, v_conv_w, v_conv_b, v_lru_w_a, v_lru_b_a, v_lru_w_x, v_lru_b_x, v_lru_lambda, v_rnn_out_norm, v_gla_w_gate, v_gla_b_gate, v_gla_out_norm, v_w_out, v_ffn_norm_pre, v_ffn_norm_post, v_w_ffn_gate, v_w_ffn_up, v_w_ffn_down):
    args = dict(locals())
    w = {n: args[n] for n in WEIGHT_NAMES}
    m = {n: args["m_" + n] for n in WEIGHT_NAMES}
    v = {n: args["v_" + n] for n in WEIGHT_NAMES}
    names_s = [n for n, _ in SMALL_SHARDED]
    axis_s = dict(SMALL_SHARDED)
    shapes_s = [w[n].shape for n in names_s]

    gathered = dict(zip(BIG_WEIGHTS, _gather_big_weights([w[n].astype(MXU_DTYPE) for n in BIG_WEIGHTS])))
    small = _pack([w[n] for n in names_s], LANES, 8)
    small_all = _unpack_slots(_all_gather(small, "gather_small_weights"), shapes_s)
    full = {n: w[n] for n in REPLICATED}
    for n, a in zip(names_s, small_all):
        full[n] = _merge_shards(a, axis_s[n])
    full["w_in_p"] = _w_in_from_shards(gathered["w_in"])
    full["w_ffn_gate"] = _cols_from_shards(gathered["w_ffn_gate"], "w_ffn_gate_from_shards")
    full["w_ffn_up"] = _cols_from_shards(gathered["w_ffn_up"], "w_ffn_up_from_shards")
    full["w_out"] = gathered["w_out"].reshape(DEPTH, D_MODEL, D_MODEL)
    full["w_ffn_down"] = gathered["w_ffn_down"].reshape(DEPTH, D_FF, D_MODEL)

    recv = [jnp.zeros((N_DEV,) + w[n].shape, MXU_DTYPE) for n in BIG_WEIGHTS]

    def on_big_grads(layer, arrays):
        recv[:] = _exchange_big_grads(arrays, recv, layer)

    loss, dx, g = _local_step(x[0], loss_target[0], full, on_big_grads)
    loss = lax.psum(loss, MESH_AXES)
    res = {}
    for n, parts in zip(BIG_WEIGHTS, recv):
        res[n] = _sum_adamw_big(parts, w[n], m[n], v[n], "adamw_" + n)

    g_slots = _pack_slots([_split_shards(g[n], axis_s[n]) for n in names_s], LANES, 8)
    g_recv = _all_to_all(g_slots, "exchange_small_grads")
    packed = [_pack([t[n] for n in names_s], LANES, 8) for t in (w, m, v)]
    res_s = [_unpack(r, shapes_s) for r in _sum_adamw(g_recv, *packed, "adamw_small")]
    for i, n in enumerate(names_s):
        res[n] = [res_s[k][i] for k in range(4)]

    shapes_r = [w[n].shape for n in REPLICATED]
    g_rep = _all_gather(_pack([g[n] for n in REPLICATED], ADAM_COLS, ADAM_TILE_ROWS), "gather_replicated_grads")
    packed = [_pack([t[n] for n in REPLICATED], ADAM_COLS, ADAM_TILE_ROWS) for t in (w, m, v)]
    res_r = [_unpack(r, shapes_r) for r in _sum_adamw(g_rep, *packed, "adamw_replicated")]
    for i, n in enumerate(REPLICATED):
        res[n] = [res_r[k][i] for k in range(4)]

    outs = [[res[n][k] for n in WEIGHT_NAMES] for k in range(4)]
    return (loss, dx[None], *outs[0], *outs[1], *outs[2], *outs[3])
```

```python
import functools

import jax
import jax.numpy as jnp
from jax import lax
from jax.experimental import pallas as pl
from jax.experimental.pallas import tpu as pltpu

F32 = jnp.float32
MXU_DTYPE = jnp.bfloat16

N_DEV = 8
D_MODEL = 1024
D_RNN = 512
CONV_WIDTH = 4
LRU_C = 8.0
GLA_HEADS = 4
GLA_DK = 64
GLA_DKP = 128
GLA_DV = 128
GLA_RANK = 16
GLA_TAU = 16.0
GLA_CHUNK = 64
D_FF = 2816
RMS_EPS = 1e-6
DEPTH = 4

PW = 3200
COL_LR = 3072
LANES = 128

TILE_S = 512
TILE_F = 256
TILE_TN = 2048
TN_ACC_BYTES = 6 * 1024 * 1024
F_CHUNK = 1408
VMEM_LIMIT = 56 * 1024 * 1024

ADAM_LR = 0.001
ADAM_B1 = 0.9
ADAM_B2 = 0.999
ADAM_EPS = 1e-08
ADAM_WD = 0.01
ADAM_STEP = 10

ADAM_TILE_ROWS = 256
ADAM_COLS = 1024


def _mm(a, b):
    return jnp.dot(a.astype(MXU_DTYPE), b.astype(MXU_DTYPE), preferred_element_type=F32)


def _mm_nt(a, b):
    return lax.dot_general(a.astype(MXU_DTYPE), b.astype(MXU_DTYPE), (((1,), (1,)), ((), ())),
                           preferred_element_type=F32)


def _mm_tn(a, b):
    return lax.dot_general(a.astype(MXU_DTYPE), b.astype(MXU_DTYPE), (((0,), (0,)), ((), ())),
                           preferred_element_type=F32)


def _mm_tri(tri, x):
    t = tri.astype(jnp.bfloat16)
    hi = x.astype(jnp.bfloat16)
    r1 = x - hi.astype(F32)
    mid = r1.astype(jnp.bfloat16)
    lo = (r1 - mid.astype(F32)).astype(jnp.bfloat16)
    dot = lambda v: jnp.dot(t, v, preferred_element_type=F32)
    return dot(hi) + dot(mid) + dot(lo)


def _sigmoid(x):
    return 0.5 * jnp.tanh(0.5 * x) + 0.5


def _log1p_pos(e):
    series = e * (1.0 - e * (0.5 - e * (1.0 / 3.0 - e * 0.25)))
    return jnp.where(e < 0.01, series, jnp.log(1.0 + e))


def _softplus(x):
    return jnp.maximum(x, 0.0) + _log1p_pos(jnp.exp(-jnp.abs(x)))


def _softplus_coarse(x):
    return jnp.maximum(x, 0.0) + jnp.log(1.0 + jnp.exp(-jnp.abs(x)))


GELU_C = 0.7978845608028654
GELU_K = 0.044715


def _gelu_and_grad(x):
    t = jnp.tanh(GELU_C * (x + GELU_K * x * x * x))
    y = 0.5 * x * (1.0 + t)
    dy = 0.5 * (1.0 + t) + 0.5 * x * (1.0 - t * t) * GELU_C * (1.0 + 3.0 * GELU_K * x * x)
    return y, dy


def _rms_fwd(x, g):
    rs = lax.rsqrt(jnp.mean(x * x, axis=-1, keepdims=True) + RMS_EPS)
    n = x * rs
    return n * g, n, rs


def _rms_bwd(dy, n, rs, g):
    dn = dy * g
    dx = rs * (dn - n * jnp.mean(dn * n, axis=-1, keepdims=True))
    dg = jnp.sum(dy * n, axis=0, keepdims=True)
    return dx, dg


def _cparams(sem=None):
    kw = dict(vmem_limit_bytes=VMEM_LIMIT)
    if sem is not None:
        kw["dimension_semantics"] = sem
    return pltpu.CompilerParams(**kw)


def _tile(n, pref):
    return pref if n % pref == 0 else n


def _const(shape):
    nd = len(shape)
    return pl.BlockSpec(shape, lambda *_: (0,) * nd, pipeline_mode=pl.Buffered(1))


def _acc(shape):
    nd = len(shape)
    return pl.BlockSpec(shape, lambda *_: (0,) * nd)


def _rows(ts, w, col=0, order=None):
    if order is None:
        return pl.BlockSpec((ts, w), lambda i: (i, col))
    return pl.BlockSpec((ts, w), lambda i: (order(i), col))


def _sds(shape, dtype=F32):
    return jax.ShapeDtypeStruct(shape, dtype)


MESH_ID = pl.DeviceIdType.MESH
ANY = pl.BlockSpec(memory_space=pl.ANY)
MESH_AXES = ("x", "y", "c")


def _peers():
    mx, my, mc = lax.axis_index("x"), lax.axis_index("y"), lax.axis_index("c")
    peers = []
    for r in range(1, N_DEV):
        px = 1 - mx if r & 4 else mx
        py = 1 - my if r & 2 else my
        pc = 1 - mc if r & 1 else mc
        peers.append((4 * px + 2 * py + pc, (px, py, pc)))
    return 4 * mx + 2 * my + mc, peers


class _GradExchange:
    def __init__(self, srcs, recvs, layer):
        self.n = len(srcs)
        self.layer = layer
        self.operands = list(srcs) + list(recvs)
        self.out_shape = [_sds(r.shape, r.dtype) for r in recvs]
        self.aliases = {self.n + a: a for a in range(self.n)}

    def copies(self, ins, outs, send_sems, recv_sems, local_sems):
        n, layer = self.n, self.layer
        me, peers = _peers()
        local = [pltpu.make_async_copy(ins[a].at[me], outs[a].at[me, layer], local_sems.at[a]) for a in range(n)]
        remote = [pltpu.make_async_remote_copy(
            src_ref=ins[a].at[slot], dst_ref=outs[a].at[me, layer],
            send_sem=send_sems.at[r * n + a], recv_sem=recv_sems.at[r * n + a],
            device_id=dev, device_id_type=MESH_ID) for r, (slot, dev) in enumerate(peers) for a in range(n)]
        return local, remote


class _WeightGather:
    def __init__(self, shards):
        self.n = len(shards)
        self.operands = list(shards)
        self.out_shape = [_sds((N_DEV,) + s.shape, s.dtype) for s in shards]
        self.aliases = {}

    def copies(self, ins, outs, send_sems, recv_sems, local_sems):
        n = self.n
        me, peers = _peers()
        local = [pltpu.make_async_copy(ins[a], outs[a].at[me], local_sems.at[a]) for a in range(n)]
        remote = [pltpu.make_async_remote_copy(
            src_ref=ins[a], dst_ref=outs[a].at[me],
            send_sem=send_sems.at[r * n + a], recv_sem=recv_sems.at[r * n + a],
            device_id=dev, device_id_type=MESH_ID) for r, (_, dev) in enumerate(peers) for a in range(n)]
        return local, remote


def _call_with_rider(body, rider, operands, *, steps, in_specs, out_specs, out_shape, scratch_shapes=(), name,
                     semantics):
    if rider is None:
        outs = pl.pallas_call(body, grid=(steps,), in_specs=in_specs, out_specs=out_specs, out_shape=out_shape,
                              scratch_shapes=list(scratch_shapes), name=name,
                              compiler_params=_cparams((semantics,)))(*operands)
        return outs, []
    n_in, n_out, n_scr = len(in_specs), len(out_specs), len(scratch_shapes)
    r_in, r_out = len(rider.operands), len(rider.out_shape)

    def riding(*refs):
        own_in, ride_in = refs[:n_in], refs[n_in:n_in + r_in]
        refs = refs[n_in + r_in:]
        own_out, ride_out = refs[:n_out], refs[n_out:n_out + r_out]
        refs = refs[n_out + r_out:]
        own_scr, sems = refs[:n_scr], refs[n_scr:]
        i = pl.program_id(0)

        @pl.when(i == 0)
        def _():
            local, remote = rider.copies(ride_in, ride_out, *sems)
            for cp in local + remote:
                cp.start()

        body(*own_in, *own_out, *own_scr)

        @pl.when(i == steps - 1)
        def _():
            local, remote = rider.copies(ride_in, ride_out, *sems)
            for cp in remote:
                cp.wait()
            for cp in local:
                cp.wait()

    n_remote = (N_DEV - 1) * rider.n
    outs = pl.pallas_call(
        riding, grid=(steps,), in_specs=list(in_specs) + [ANY] * r_in, out_specs=list(out_specs) + [ANY] * r_out,
        out_shape=list(out_shape) + rider.out_shape,
        input_output_aliases={n_in + i: n_out + o for i, o in rider.aliases.items()},
        scratch_shapes=list(scratch_shapes) + [pltpu.SemaphoreType.DMA((n_remote,)),
                                               pltpu.SemaphoreType.DMA((n_remote,)),
                                               pltpu.SemaphoreType.DMA((rider.n,))],
        name=name, compiler_params=_cparams(("arbitrary",)))(*operands, *rider.operands)
    return outs[:n_out], outs[n_out:]


def _mix_in_fwd(x, gpre, w_in_p):
    s = x.shape[0]
    ts = _tile(s, TILE_S)

    def body(x_ref, g_ref, w_ref, o_ref):
        h, _, _ = _rms_fwd(x_ref[...], g_ref[...])
        o_ref[...] = _mm(h, w_ref[...])

    return pl.pallas_call(
        body, grid=(s // ts,),
        in_specs=[_rows(ts, D_MODEL), _const((1, D_MODEL)), _const((D_MODEL, PW))],
        out_specs=_rows(ts, PW), out_shape=_sds((s, PW)),
        name="mix_in_fwd", compiler_params=_cparams(("parallel",)))(x, gpre, w_in_p)


def _conv_taps(xr, hp, hn, first, last):
    ts = xr.shape[0]
    hp = jnp.where(first, 0.0, hp)
    hn = jnp.where(last, 0.0, hn)
    xe = jnp.concatenate([hp, xr, hn], axis=0)
    return xe[6:6 + ts], xe[7:7 + ts], xr, xe[9:9 + ts]


def _conv_fwd(xr, hp, hn, cw, cb, first, last):
    t0, t1, t2, t3 = _conv_taps(xr, hp, hn, first, last)
    return cw[0:1] * t0 + cw[1:2] * t1 + cw[2:3] * t2 + cw[3:4] * t3 + cb


def _rnn_gates(xc, wa, ba, wx, bx, lam):
    r = _sigmoid(_mm(xc, wa) + ba)
    i = _sigmoid(_mm(xc, wx) + bx)
    sp = _softplus(-lam)
    la = (-LRU_C) * r * sp
    a = jnp.exp(la)
    mult = jnp.sqrt(-jnp.tanh(la) * (a * a + 1.0))
    return r, i, sp, a, mult


def _scan_tile(a_scr, u_scr, h_ref, c0, reverse):
    ts = a_scr.shape[0]
    a = a_scr[...]
    u = u_scr[...]
    row = lax.broadcasted_iota(jnp.int32, a.shape, 0) % 8
    for k in (1, 2, 4):
        if reverse:
            a_sh = pltpu.roll(a, ts - k, 0)
            u_sh = pltpu.roll(u, ts - k, 0)
            ok = row < 8 - k
        else:
            a_sh = pltpu.roll(a, k, 0)
            u_sh = pltpu.roll(u, k, 0)
            ok = row >= k
        u = jnp.where(ok, u + a * u_sh, u)
        a = jnp.where(ok, a * a_sh, a)
    a_scr[...] = a
    u_scr[...] = u
    ng = ts // 8

    def body(j, c):
        g = (ng - 1 - j) if reverse else j
        sl = pl.ds(pl.multiple_of(g * 8, 8), 8)
        hh = u_scr[sl, :] + a_scr[sl, :] * c
        h_ref[sl, :] = hh
        return hh[0:1, :] if reverse else hh[7:8, :]

    return lax.fori_loop(0, ng, body, c0)


def _halo_specs(s, ts, w, col, order):
    n8 = s // 8
    per = ts // 8
    prev = pl.BlockSpec((8, w), lambda i: (jnp.maximum(order(i) * per - 1, 0), col))
    nxt = pl.BlockSpec((8, w), lambda i: (jnp.minimum((order(i) + 1) * per, n8 - 1), col))
    return prev, nxt


def _rnn_fwd(proj, cw, cb, wa, ba, wx, bx, lam, reverse):
    s = proj.shape[0]
    ts = _tile(s, TILE_S)
    nt = s // ts
    order = (lambda i: nt - 1 - i) if reverse else (lambda i: i)

    def body(xr_ref, hp_ref, hn_ref, cw_ref, cb_ref, wa_ref, ba_ref, wx_ref, bx_ref, lam_ref,
             h_ref, a_scr, u_scr, c_scr):
        i = pl.program_id(0)
        t = order(i)

        @pl.when(i == 0)
        def _():
            c_scr[...] = jnp.zeros_like(c_scr)

        xc = _conv_fwd(xr_ref[...], hp_ref[...], hn_ref[...], cw_ref[...], cb_ref[...], t == 0, t == nt - 1)
        _, gi, _, a, mult = _rnn_gates(xc, wa_ref[...], ba_ref[...], wx_ref[...], bx_ref[...], lam_ref[...])
        a_scr[...] = a
        u_scr[...] = xc * gi * mult
        c_scr[0:1, :] = _scan_tile(a_scr, u_scr, h_ref, c_scr[0:1, :], reverse)

    hp, hn = _halo_specs(s, ts, D_RNN, 0, order)
    return pl.pallas_call(
        body, grid=(nt,),
        in_specs=[_rows(ts, D_RNN, 0, order), hp, hn, _const((CONV_WIDTH, D_RNN)), _const((1, D_RNN)),
                  _const((D_RNN, D_RNN)), _const((1, D_RNN)), _const((D_RNN, D_RNN)), _const((1, D_RNN)),
                  _const((1, D_RNN))],
        out_specs=_rows(ts, D_RNN, 0, order), out_shape=_sds((s, D_RNN)),
        scratch_shapes=[pltpu.VMEM((ts, D_RNN), F32), pltpu.VMEM((ts, D_RNN), F32), pltpu.VMEM((8, D_RNN), F32)],
        name="rnn_fwd_rev" if reverse else "rnn_fwd",
        compiler_params=_cparams(("arbitrary",)))(proj, proj, proj, cw, cb, wa, ba, wx, bx, lam)


def _tri(reverse, transpose=False):
    r = lax.broadcasted_iota(jnp.int32, (GLA_CHUNK, GLA_CHUNK), 0)
    c = lax.broadcasted_iota(jnp.int32, (GLA_CHUNK, GLA_CHUNK), 1)
    if transpose:
        r, c = c, r
    return ((r <= c) if reverse else (r >= c)).astype(F32)


def _gla_chunk_terms(q, k, la, tri, reverse):
    b = _mm_tri(tri, la)
    bl = b[0:1] if reverse else b[GLA_CHUNK - 1:GLA_CHUNK]
    eb = jnp.exp(b)
    enb = jnp.exp(-b)
    ebl = jnp.exp(bl - b)
    d = jnp.exp(bl)
    return eb, enb, ebl, d, q * (GLA_DK ** -0.5) * eb, k * enb, k * ebl


def _gla_gate(lr, wg, bg):
    z = _mm(lr, wg) + bg
    return z, -_softplus_coarse(-z) * (1.0 / GLA_TAU)


def _gla_fwd(proj, wg, bg, reverse, o_add=None):
    s = proj.shape[0]
    ts = _tile(s, TILE_S)
    nt = s // ts
    ch = ts // GLA_CHUNK
    hw = GLA_HEADS * LANES
    order = (lambda i: nt - 1 - i) if reverse else (lambda i: i)
    extra = [] if o_add is None else [o_add]

    def body(q_ref, k_ref, v_ref, lr_ref, wg_ref, bg_ref, *rest):
        add_ref = None if o_add is None else rest[0]
        o_ref, st_ref, s_scr = rest[len(extra):]

        @pl.when(pl.program_id(0) == 0)
        def _():
            s_scr[...] = jnp.zeros_like(s_scr)

        _, la = _gla_gate(lr_ref[...], wg_ref[...], bg_ref[...])
        tri = _tri(reverse)
        keep = tri > 0.5
        for cc in range(ch):
            c = (ch - 1 - cc) if reverse else cc
            rows = slice(c * GLA_CHUNK, (c + 1) * GLA_CHUNK)
            _, _, _, d, qe, ke, kd = _gla_chunk_terms(q_ref[rows, :], k_ref[rows, :], la[rows], tri, reverse)
            outs = []
            for h in range(GLA_HEADS):
                ln = slice(h * LANES, (h + 1) * LANES)
                st = s_scr[h]
                st_ref[c, h] = st
                v_h = v_ref[rows, ln]
                a_m = jnp.where(keep, _mm_nt(qe[:, ln], ke[:, ln]), 0.0)
                outs.append(_mm(a_m, v_h) + _mm_nt(qe[:, ln], st))
                s_scr[h] = d[:, ln] * st + _mm_tn(v_h, kd[:, ln])
            o = jnp.concatenate(outs, axis=1)
            o_ref[rows, :] = o if add_ref is None else o + add_ref[rows, :]

    return pl.pallas_call(
        body, grid=(nt,),
        in_specs=[_rows(ts, hw, 2, order), _rows(ts, hw, 3, order), _rows(ts, hw, 4, order),
                  _rows(ts, LANES, COL_LR // LANES, order), _const((LANES, hw)), _const((1, hw))]
                 + [_rows(ts, hw, 0, order)] * len(extra),
        out_specs=[_rows(ts, hw, 0, order),
                   pl.BlockSpec((ch, GLA_HEADS, LANES, LANES), lambda i: (order(i), 0, 0, 0))],
        out_shape=[_sds((s, hw)), _sds((s // GLA_CHUNK, GLA_HEADS, LANES, LANES))],
        scratch_shapes=[pltpu.VMEM((GLA_HEADS, LANES, LANES), F32)],
        name="gla_fwd_rev" if reverse else "gla_fwd",
        compiler_params=_cparams(("arbitrary",)))(proj, proj, proj, proj, wg, bg, *extra)


def _mix_out_terms(hf, hb, gate_r, osum, g, g_rnn, g_gla):
    hs = hf + hb
    gl, dgl = _gelu_and_grad(gate_r)
    z = hs * gl
    y_rnn, n_rnn, rs_rnn = _rms_fwd(z, g_rnn)
    sg_lin = _sigmoid(g)
    sg = g * sg_lin
    dsg = sg_lin * (1.0 + g * (1.0 - sg_lin))
    ons, ns, rss = [], [], []
    for h in range(GLA_HEADS):
        ln = slice(h * LANES, (h + 1) * LANES)
        on, n, rs = _rms_fwd(osum[:, ln], g_gla)
        ons.append(on)
        ns.append(n)
        rss.append(rs)
    on = jnp.concatenate(ons, axis=1)
    return hs, gl, dgl, y_rnn, n_rnn, rs_rnn, sg, dsg, on, ns, rss


def _mix_out_fwd(x, hf, hb, osum, proj, g_rnn, g_gla, w_out, gpost):
    s = x.shape[0]
    ts = _tile(s, TILE_S)

    def body(x_ref, hf_ref, hb_ref, gr_ref, os_ref, g_ref, grnn_ref, ggla_ref, w_ref, gp_ref, x1_ref, y_ref):
        _, _, _, y_rnn, _, _, sg, _, on, _, _ = _mix_out_terms(
            hf_ref[...], hb_ref[...], gr_ref[...], os_ref[...], g_ref[...], grnn_ref[...], ggla_ref[...])
        y = jnp.concatenate([y_rnn, on * sg], axis=1).astype(MXU_DTYPE)
        y_ref[...] = y
        out, _, _ = _rms_fwd(_mm(y, w_ref[...]), gp_ref[...])
        x1_ref[...] = x_ref[...] + out

    return pl.pallas_call(
        body, grid=(s // ts,),
        in_specs=[_rows(ts, D_MODEL), _rows(ts, D_RNN), _rows(ts, D_RNN), _rows(ts, D_RNN, 1), _rows(ts, 512),
                  _rows(ts, 512, 5), _const((1, D_RNN)), _const((1, GLA_DV)),
                  _const((D_MODEL, D_MODEL)), _const((1, D_MODEL))],
        out_specs=[_rows(ts, D_MODEL), _rows(ts, D_MODEL)],
        out_shape=[_sds((s, D_MODEL)), _sds((s, D_MODEL), MXU_DTYPE)],
        name="mix_out_fwd", compiler_params=_cparams(("parallel",)))(
            x, hf, hb, proj, osum, proj, g_rnn, g_gla, w_out, gpost)


def _f_chunks():
    return [(c0, min(c0 + F_CHUNK, D_FF)) for c0 in range(0, D_FF, F_CHUNK)]


def _ffn_fwd(x1, gpre, wg, wu, wd, gpost, rider=None):
    s = x1.shape[0]
    ts = _tile(s, TILE_F)

    def body(x_ref, gpre_ref, wg_ref, wu_ref, wd_ref, gpost_ref, x2_ref, a_ref, u_ref, f_ref):
        x = x_ref[...]
        h, _, _ = _rms_fwd(x, gpre_ref[...])
        h = h.astype(MXU_DTYPE)
        f = jnp.zeros((ts, D_MODEL), F32)
        for c0, c1 in _f_chunks():
            a = _mm(h, wg_ref[:, c0:c1])
            u = _mm(h, wu_ref[:, c0:c1])
            a_ref[:, c0:c1] = a.astype(MXU_DTYPE)
            u_ref[:, c0:c1] = u.astype(MXU_DTYPE)
            f = f + _mm(a * _sigmoid(a) * u, wd_ref[c0:c1, :])
        f_ref[...] = f
        out, _, _ = _rms_fwd(f, gpost_ref[...])
        x2_ref[...] = x + out

    return _call_with_rider(
        body, rider, (x1, gpre, wg, wu, wd, gpost), steps=s // ts,
        in_specs=[_rows(ts, D_MODEL), _const((1, D_MODEL)), _const((D_MODEL, D_FF)), _const((D_MODEL, D_FF)),
                  _const((D_FF, D_MODEL)), _const((1, D_MODEL))],
        out_specs=[_rows(ts, D_MODEL), _rows(ts, D_FF), _rows(ts, D_FF), _rows(ts, D_MODEL)],
        out_shape=[_sds((s, D_MODEL)), _sds((s, D_FF), MXU_DTYPE), _sds((s, D_FF), MXU_DTYPE), _sds((s, D_MODEL))],
        name="ffn_fwd", semantics="parallel")


def _loss_fwd_bwd(y, target):
    s = y.shape[0]
    ts = _tile(s, TILE_S)

    def body(y_ref, t_ref, loss_ref, dy_ref):
        @pl.when(pl.program_id(0) == 0)
        def _():
            loss_ref[...] = jnp.zeros_like(loss_ref)

        e = y_ref[...] - t_ref[...]
        dy_ref[...] = e * (1.0 / D_MODEL)
        part = jnp.sum(jnp.sum(e * e, axis=1, keepdims=True), axis=0, keepdims=True) * (0.5 / D_MODEL)
        loss_ref[...] += jnp.broadcast_to(part, loss_ref.shape)

    return pl.pallas_call(
        body, grid=(s // ts,),
        in_specs=[_rows(ts, D_MODEL), _rows(ts, D_MODEL)],
        out_specs=[_acc((8, LANES)), _rows(ts, D_MODEL)],
        out_shape=[_sds((8, LANES)), _sds((s, D_MODEL))],
        name="loss", compiler_params=_cparams(("arbitrary",)))(y, target)


def _tn_matmul(a, b, name, rows=TILE_TN, acc_bytes=TN_ACC_BYTES):
    s, k = a.shape
    n = b.shape[1]
    ts = _tile(s, rows)
    tn = max(t for t in range(LANES, n + 1, LANES) if n % t == 0 and (k * t * 4 <= acc_bytes or t == LANES))
    ns = s // ts

    def body(a_ref, b_ref, o_ref, acc):
        i = pl.program_id(1)

        @pl.when(i == 0)
        def _():
            acc[...] = jnp.zeros_like(acc)

        acc[...] += _mm_tn(a_ref[...], b_ref[...])

        @pl.when(i == ns - 1)
        def _():
            o_ref[...] = acc[...].astype(o_ref.dtype)

    return pl.pallas_call(
        body, grid=(n // tn, ns),
        in_specs=[pl.BlockSpec((ts, k), lambda j, i: (i, 0)), pl.BlockSpec((ts, tn), lambda j, i: (i, j))],
        out_specs=pl.BlockSpec((k, tn), lambda j, i: (0, j)), out_shape=_sds((k, n), MXU_DTYPE),
        scratch_shapes=[pltpu.VMEM((k, tn), F32)],
        name=name, compiler_params=_cparams(("parallel", "arbitrary")))(a, b)


def _ffn_bwd(dx2, f, x1, a, u, gpre, wg, wu, wd, gpost, rider=None):
    s = x1.shape[0]
    ts = _tile(s, TILE_F)

    def body(dx2_ref, f_ref, x1_ref, a_ref, u_ref, gpre_ref, wg_ref, wu_ref, wd_ref, gpost_ref,
             dx1_ref, df_ref, h_ref, p_ref, da_ref, du_ref, dgpost_ref, dgpre_ref):
        @pl.when(pl.program_id(0) == 0)
        def _():
            dgpost_ref[...] = jnp.zeros_like(dgpost_ref)
            dgpre_ref[...] = jnp.zeros_like(dgpre_ref)

        dx2 = dx2_ref[...]
        _, nf, rsf = _rms_fwd(f_ref[...], gpost_ref[...])
        df, dgpost = _rms_bwd(dx2, nf, rsf, gpost_ref[...])
        dgpost_ref[...] += dgpost
        df = df.astype(MXU_DTYPE)
        df_ref[...] = df
        h, n1, rs1 = _rms_fwd(x1_ref[...], gpre_ref[...])
        h_ref[...] = h.astype(MXU_DTYPE)
        dh = jnp.zeros((ts, D_MODEL), F32)
        for c0, c1 in _f_chunks():
            av = a_ref[:, c0:c1].astype(F32)
            uv = u_ref[:, c0:c1].astype(F32)
            sg = _sigmoid(av)
            dp = _mm_nt(df, wd_ref[c0:c1, :])
            p_ref[:, c0:c1] = (av * sg * uv).astype(MXU_DTYPE)
            da = (dp * uv * sg * (1.0 + av * (1.0 - sg))).astype(MXU_DTYPE)
            du = (dp * av * sg).astype(MXU_DTYPE)
            da_ref[:, c0:c1] = da
            du_ref[:, c0:c1] = du
            dh = dh + _mm_nt(da, wg_ref[:, c0:c1]) + _mm_nt(du, wu_ref[:, c0:c1])
        dx, dgpre = _rms_bwd(dh, n1, rs1, gpre_ref[...])
        dgpre_ref[...] += dgpre
        dx1_ref[...] = dx2 + dx

    return _call_with_rider(
        body, rider, (dx2, f, x1, a, u, gpre, wg, wu, wd, gpost), steps=s // ts,
        in_specs=[_rows(ts, D_MODEL), _rows(ts, D_MODEL), _rows(ts, D_MODEL), _rows(ts, D_FF), _rows(ts, D_FF),
                  _const((1, D_MODEL)), _const((D_MODEL, D_FF)), _const((D_MODEL, D_FF)), _const((D_FF, D_MODEL)),
                  _const((1, D_MODEL))],
        out_specs=[_rows(ts, D_MODEL), _rows(ts, D_MODEL), _rows(ts, D_MODEL), _rows(ts, D_FF), _rows(ts, D_FF),
                   _rows(ts, D_FF), _acc((1, D_MODEL)), _acc((1, D_MODEL))],
        out_shape=[_sds((s, D_MODEL)), _sds((s, D_MODEL), MXU_DTYPE), _sds((s, D_MODEL), MXU_DTYPE),
                   _sds((s, D_FF), MXU_DTYPE), _sds((s, D_FF), MXU_DTYPE), _sds((s, D_FF), MXU_DTYPE),
                   _sds((1, D_MODEL)), _sds((1, D_MODEL))],
        name="ffn_bwd", semantics="arbitrary")


def _mix_out_bwd(dx1, y, hf, hb, osum, proj, g_rnn, g_gla, w_out, gpost):
    s = y.shape[0]
    ts = _tile(s, TILE_S)

    def body(dx1_ref, y_ref, hf_ref, hb_ref, gr_ref, os_ref, g_ref, grnn_ref, ggla_ref, w_ref, gp_ref,
             dm_ref, dhs_ref, dgr_ref, dos_ref, dg_ref, dgpost_ref, dgrnn_ref, dggla_ref):
        @pl.when(pl.program_id(0) == 0)
        def _():
            dgpost_ref[...] = jnp.zeros_like(dgpost_ref)
            dgrnn_ref[...] = jnp.zeros_like(dgrnn_ref)
            dggla_ref[...] = jnp.zeros_like(dggla_ref)

        _, nm, rsm = _rms_fwd(_mm(y_ref[...], w_ref[...]), gp_ref[...])
        dm, dgpost = _rms_bwd(dx1_ref[...], nm, rsm, gp_ref[...])
        dgpost_ref[...] += dgpost
        dm = dm.astype(MXU_DTYPE)
        dm_ref[...] = dm
        dy = _mm_nt(dm, w_ref[...])
        hs, gl, dgl, _, n_rnn, rs_rnn, sg, dsg, on, ns, rss = _mix_out_terms(
            hf_ref[...], hb_ref[...], gr_ref[...], os_ref[...], g_ref[...], grnn_ref[...], ggla_ref[...])
        dz, dgrnn = _rms_bwd(dy[:, :D_RNN], n_rnn, rs_rnn, grnn_ref[...])
        dgrnn_ref[...] += dgrnn
        dhs_ref[...] = dz * gl
        dgr_ref[...] = (dz * hs * dgl).astype(MXU_DTYPE)
        dyg = dy[:, D_RNN:]
        dg_ref[...] = (dyg * on * dsg).astype(MXU_DTYPE)
        don = dyg * sg
        dggla = jnp.zeros((1, GLA_DV), F32)
        for h in range(GLA_HEADS):
            ln = slice(h * LANES, (h + 1) * LANES)
            dos, dgh = _rms_bwd(don[:, ln], ns[h], rss[h], ggla_ref[...])
            dos_ref[:, ln] = dos.astype(MXU_DTYPE)
            dggla = dggla + dgh
        dggla_ref[...] += dggla

    return pl.pallas_call(
        body, grid=(s // ts,),
        in_specs=[_rows(ts, D_MODEL), _rows(ts, D_MODEL), _rows(ts, D_RNN), _rows(ts, D_RNN), _rows(ts, D_RNN, 1),
                  _rows(ts, 512), _rows(ts, 512, 5), _const((1, D_RNN)), _const((1, GLA_DV)),
                  _const((D_MODEL, D_MODEL)), _const((1, D_MODEL))],
        out_specs=[_rows(ts, D_MODEL), _rows(ts, D_RNN), _rows(ts, D_RNN), _rows(ts, 512), _rows(ts, 512),
                   _acc((1, D_MODEL)), _acc((1, D_RNN)), _acc((1, GLA_DV))],
        out_shape=[_sds((s, D_MODEL), MXU_DTYPE), _sds((s, D_RNN)), _sds((s, D_RNN), MXU_DTYPE),
                   _sds((s, 512), MXU_DTYPE), _sds((s, 512), MXU_DTYPE),
                   _sds((1, D_MODEL)), _sds((1, D_RNN)), _sds((1, GLA_DV))],
        name="mix_out_bwd", compiler_params=_cparams(("arbitrary",)))(
            dx1, y, hf, hb, proj, osum, proj, g_rnn, g_gla, w_out, gpost)


def _gla_bwd(dos, proj, st, wg, bg, reverse, prev=None):
    s = proj.shape[0]
    ts = _tile(s, TILE_S)
    nt = s // ts
    ch = ts // GLA_CHUNK
    hw = GLA_HEADS * LANES
    order = (lambda i: i) if reverse else (lambda i: nt - 1 - i)

    n_prev = 0 if prev is None else 4

    def body(do_ref, q_ref, k_ref, v_ref, lr_ref, st_ref, wg_ref, bg_ref, *rest):
        pq_ref, pk_ref, pv_ref, plr_ref = rest[:n_prev] if n_prev else (None,) * 4
        dq_ref, dk_ref, dv_ref, dlr_ref, dwg_ref, dbg_ref, ds_scr, dz_scr = rest[n_prev:]

        def put(ref, p_ref, rows, cols, val):
            if p_ref is not None:
                val = val + p_ref[rows, cols].astype(F32)
            ref[rows, cols] = val.astype(ref.dtype)

        @pl.when(pl.program_id(0) == 0)
        def _():
            ds_scr[...] = jnp.zeros_like(ds_scr)
            dwg_ref[...] = jnp.zeros_like(dwg_ref)
            dbg_ref[...] = jnp.zeros_like(dbg_ref)

        z, la = _gla_gate(lr_ref[...], wg_ref[...], bg_ref[...])
        tri = _tri(reverse)
        tri_t = _tri(reverse, transpose=True)
        keep = tri > 0.5
        last_row = 0 if reverse else GLA_CHUNK - 1
        is_last = lax.broadcasted_iota(jnp.int32, (GLA_CHUNK, hw), 0) == last_row
        for cc in range(ch):
            c = cc if reverse else (ch - 1 - cc)
            rows = slice(c * GLA_CHUNK, (c + 1) * GLA_CHUNK)
            eb, enb, ebl, d, qe, ke, kd = _gla_chunk_terms(q_ref[rows, :], k_ref[rows, :], la[rows], tri, reverse)
            dqe, dke, dkd, dd = [], [], [], []
            for h in range(GLA_HEADS):
                ln = slice(h * LANES, (h + 1) * LANES)
                st_h = st_ref[c, h]
                dst = ds_scr[h]
                v_h = v_ref[rows, ln]
                do_h = do_ref[rows, ln]
                qe_h, ke_h, kd_h = qe[:, ln], ke[:, ln], kd[:, ln]
                a_m = jnp.where(keep, _mm_nt(qe_h, ke_h), 0.0)
                da_m = jnp.where(keep, _mm_nt(do_h, v_h), 0.0)
                put(dv_ref, pv_ref, rows, ln, _mm_tn(a_m, do_h) + _mm_nt(kd_h, dst))
                dqe.append(_mm(da_m, ke_h) + _mm(do_h, st_h))
                dke.append(_mm_tn(da_m, qe_h))
                dkd.append(_mm(v_h, dst))
                dd.append(jnp.sum(dst * st_h, axis=0, keepdims=True))
                ds_scr[h] = _mm_tn(do_h, qe_h) + d[:, ln] * dst
            dqe = jnp.concatenate(dqe, axis=1)
            dke = jnp.concatenate(dke, axis=1)
            dkd = jnp.concatenate(dkd, axis=1)
            dd = jnp.concatenate(dd, axis=1)
            dbl = dd * d + jnp.sum(dkd * kd, axis=0, keepdims=True)
            db = dqe * qe - dke * ke - dkd * kd
            db = jnp.where(is_last, db + dbl, db)
            put(dq_ref, pq_ref, rows, slice(None), dqe * eb * (GLA_DK ** -0.5))
            put(dk_ref, pk_ref, rows, slice(None), dke * enb + dkd * ebl)
            dla = _mm_tri(tri_t, db)
            dz_scr[rows, :] = dla * (1.0 / GLA_TAU) * _sigmoid(-z[rows])
        dz = dz_scr[...]
        put(dlr_ref, plr_ref, slice(None), slice(None), _mm_nt(dz, wg_ref[...]))
        dwg_ref[...] += _mm_tn(lr_ref[...], dz)
        dbg_ref[...] += jnp.sum(dz, axis=0, keepdims=True)

    wide, narrow = _rows(ts, hw, 0, order), _rows(ts, LANES, 0, order)
    return pl.pallas_call(
        body, grid=(nt,),
        in_specs=[wide, _rows(ts, hw, 2, order), _rows(ts, hw, 3, order), _rows(ts, hw, 4, order),
                  _rows(ts, LANES, COL_LR // LANES, order),
                  pl.BlockSpec((ch, GLA_HEADS, LANES, LANES), lambda i: (order(i), 0, 0, 0)),
                  _const((LANES, hw)), _const((1, hw))] + ([wide, wide, wide, narrow] if n_prev else []),
        out_specs=[wide, wide, wide, narrow, _acc((LANES, hw)), _acc((1, hw))],
        out_shape=[_sds((s, hw), MXU_DTYPE), _sds((s, hw), MXU_DTYPE), _sds((s, hw), MXU_DTYPE),
                   _sds((s, LANES), MXU_DTYPE), _sds((LANES, hw)), _sds((1, hw))],
        scratch_shapes=[pltpu.VMEM((GLA_HEADS, LANES, LANES), F32), pltpu.VMEM((ts, hw), F32)],
        name="gla_bwd_rev" if reverse else "gla_bwd",
        compiler_params=_cparams(("arbitrary",)))(dos, proj, proj, proj, proj, st, wg, bg, *(prev or ()))


def _rnn_bwd(dhs, h, proj, cw, cb, wa, ba, wx, bx, lam, reverse):
    s = proj.shape[0]
    ts = _tile(s, TILE_S)
    nt = s // ts
    order = (lambda i: i) if reverse else (lambda i: nt - 1 - i)
    back = not reverse

    def body(dh_ref, h_ref, hh_ref, xr_ref, hp_ref, hn_ref, cw_ref, cb_ref, wa_ref, ba_ref, wx_ref, bx_ref, lam_ref,
             dxc_ref, dwa_ref, dba_ref, dwx_ref, dbx_ref, dlam_ref, a_scr, u_scr, g_scr, c_scr):
        i = pl.program_id(0)
        t = order(i)

        @pl.when(i == 0)
        def _():
            c_scr[...] = jnp.zeros_like(c_scr)
            dwa_ref[...] = jnp.zeros_like(dwa_ref)
            dba_ref[...] = jnp.zeros_like(dba_ref)
            dwx_ref[...] = jnp.zeros_like(dwx_ref)
            dbx_ref[...] = jnp.zeros_like(dbx_ref)
            dlam_ref[...] = jnp.zeros_like(dlam_ref)

        xc = _conv_fwd(xr_ref[...], hp_ref[...], hn_ref[...], cw_ref[...], cb_ref[...], t == 0, t == nt - 1)
        r, gi, sp, a, mult = _rnn_gates(xc, wa_ref[...], ba_ref[...], wx_ref[...], bx_ref[...], lam_ref[...])
        row = lax.broadcasted_iota(jnp.int32, (ts, D_RNN), 0)
        hv = h_ref[...]
        if reverse:
            edge = jnp.where(t == nt - 1, 0.0, hh_ref[0:1, :])
            h_prev = jnp.where(row == ts - 1, edge, pltpu.roll(hv, ts - 1, 0))
            a_nxt = jnp.where(row == 0, 1.0, pltpu.roll(a, 1, 0))
        else:
            edge = jnp.where(t == 0, 0.0, hh_ref[7:8, :])
            h_prev = jnp.where(row == 0, edge, pltpu.roll(hv, 1, 0))
            a_nxt = jnp.where(row == ts - 1, 1.0, pltpu.roll(a, ts - 1, 0))
        a_scr[...] = a_nxt
        u_scr[...] = dh_ref[...]
        _scan_tile(a_scr, u_scr, g_scr, c_scr[0:1, :], back)
        dh = g_scr[...]
        if reverse:
            c_scr[0:1, :] = a[ts - 1:ts, :] * dh[ts - 1:ts, :]
        else:
            c_scr[0:1, :] = a[0:1, :] * dh[0:1, :]
        dmult = dh * xc * gi
        dla = dh * h_prev * a - dmult * a * a / mult
        dza = dla * (-LRU_C) * sp * r * (1.0 - r)
        dzx = dh * xc * mult * gi * (1.0 - gi)
        dsp = jnp.sum(dla * (-LRU_C) * r, axis=0, keepdims=True)
        dlam_ref[...] += dsp * (-_sigmoid(-lam_ref[...]))
        dxc_ref[...] = dh * gi * mult + _mm_nt(dza, wa_ref[...]) + _mm_nt(dzx, wx_ref[...])
        dwa_ref[...] += _mm_tn(xc, dza)
        dwx_ref[...] += _mm_tn(xc, dzx)
        dba_ref[...] += jnp.sum(dza, axis=0, keepdims=True)
        dbx_ref[...] += jnp.sum(dzx, axis=0, keepdims=True)

    hp, hn = _halo_specs(s, ts, D_RNN, 0, order)
    hhp, hhn = _halo_specs(s, ts, D_RNN, 0, order)
    sq = (D_RNN, D_RNN)
    vec = (1, D_RNN)
    return pl.pallas_call(
        body, grid=(nt,),
        in_specs=[_rows(ts, D_RNN, 0, order), _rows(ts, D_RNN, 0, order), hhn if reverse else hhp,
                  _rows(ts, D_RNN, 0, order), hp, hn, _const((CONV_WIDTH, D_RNN)), _const(vec),
                  _const(sq), _const(vec), _const(sq), _const(vec), _const(vec)],
        out_specs=[_rows(ts, D_RNN, 0, order), _acc(sq), _acc(vec), _acc(sq), _acc(vec), _acc(vec)],
        out_shape=[_sds((s, D_RNN)), _sds(sq), _sds(vec), _sds(sq), _sds(vec), _sds(vec)],
        scratch_shapes=[pltpu.VMEM((ts, D_RNN), F32), pltpu.VMEM((ts, D_RNN), F32), pltpu.VMEM((ts, D_RNN), F32),
                        pltpu.VMEM((8, D_RNN), F32)],
        name="rnn_bwd_rev" if reverse else "rnn_bwd",
        compiler_params=_cparams(("arbitrary",)))(dhs, h, h, proj, proj, proj, cw, cb, wa, ba, wx, bx, lam)


def _conv_bwd(dxc_f, dxc_b, proj, cw):
    s = proj.shape[0]
    ts = _tile(s, TILE_S)
    nt = s // ts
    ident = lambda i: i

    def body(df_ref, dfp_ref, dfn_ref, db_ref, dbp_ref, dbn_ref, xr_ref, xp_ref, xn_ref, cw_ref,
             dxr_ref, dcw_ref, dcb_ref):
        t = pl.program_id(0)

        @pl.when(t == 0)
        def _():
            dcw_ref[...] = jnp.zeros_like(dcw_ref)
            dcb_ref[...] = jnp.zeros_like(dcb_ref)

        first = t == 0
        last = t == nt - 1
        d = df_ref[...] + db_ref[...]
        d_m2, d_m1, _, d_p1 = _conv_taps(d, dfp_ref[...] + dbp_ref[...], dfn_ref[...] + dbn_ref[...], first, last)
        dn = jnp.where(last, 0.0, dfn_ref[...] + dbn_ref[...])
        d_p2 = jnp.concatenate([d, dn], axis=0)[2:2 + ts]
        del d_m2
        cw = cw_ref[...]
        dxr_ref[...] = (cw[0:1] * d_p2 + cw[1:2] * d_p1 + cw[2:3] * d + cw[3:4] * d_m1).astype(dxr_ref.dtype)
        taps = _conv_taps(xr_ref[...], xp_ref[...], xn_ref[...], first, last)
        dcw_ref[...] += jnp.concatenate([jnp.sum(d * tp, axis=0, keepdims=True) for tp in taps], axis=0)
        dcb_ref[...] += jnp.sum(d, axis=0, keepdims=True)

    hp, hn = _halo_specs(s, ts, D_RNN, 0, ident)
    return pl.pallas_call(
        body, grid=(nt,),
        in_specs=[_rows(ts, D_RNN), hp, hn, _rows(ts, D_RNN), hp, hn, _rows(ts, D_RNN), hp, hn,
                  _const((CONV_WIDTH, D_RNN))],
        out_specs=[_rows(ts, D_RNN), _acc((CONV_WIDTH, D_RNN)), _acc((1, D_RNN))],
        out_shape=[_sds((s, D_RNN), MXU_DTYPE), _sds((CONV_WIDTH, D_RNN)), _sds((1, D_RNN))],
        name="conv_bwd", compiler_params=_cparams(("arbitrary",)))(
            dxc_f, dxc_f, dxc_f, dxc_b, dxc_b, dxc_b, proj, proj, proj, cw)


def _mix_in_bwd(parts, dlr, x, dx1, gpre, w_in_p):
    s = x.shape[0]
    ts = _tile(s, TILE_S)
    n_parts = len(parts)

    def body(*refs):
        part_refs = refs[:n_parts + 1]
        x_ref, dx1_ref, g_ref, w_ref, dx_ref, dp_ref, h_ref, dgpre_ref = refs[n_parts + 1:]

        @pl.when(pl.program_id(0) == 0)
        def _():
            dgpre_ref[...] = jnp.zeros_like(dgpre_ref)

        dp = jnp.concatenate([r[...] for r in part_refs], axis=1)
        dp_ref[...] = dp
        h, n, rs = _rms_fwd(x_ref[...], g_ref[...])
        h_ref[...] = h.astype(MXU_DTYPE)
        dh = _mm_nt(dp, w_ref[...])
        dx, dgpre = _rms_bwd(dh, n, rs, g_ref[...])
        dgpre_ref[...] += dgpre
        dx_ref[...] = dx1_ref[...] + dx

    return pl.pallas_call(
        body, grid=(s // ts,),
        in_specs=[_rows(ts, 512)] * n_parts + [_rows(ts, LANES), _rows(ts, D_MODEL), _rows(ts, D_MODEL),
                                                _const((1, D_MODEL)), _const((D_MODEL, PW))],
        out_specs=[_rows(ts, D_MODEL), _rows(ts, PW), _rows(ts, D_MODEL), _acc((1, D_MODEL))],
        out_shape=[_sds((s, D_MODEL)), _sds((s, PW), MXU_DTYPE), _sds((s, D_MODEL), MXU_DTYPE), _sds((1, D_MODEL))],
        name="mix_in_bwd", compiler_params=_cparams(("arbitrary",)))(*parts, dlr, x, dx1, gpre, w_in_p)


def _pad_heads(w):
    sh = w.shape[:-1]
    w = w.reshape(sh + (GLA_HEADS, GLA_DK))
    w = jnp.pad(w, [(0, 0)] * (len(sh) + 1) + [(0, GLA_DKP - GLA_DK)])
    return w.reshape(sh + (GLA_HEADS * GLA_DKP,))


def _unpad_heads(w):
    sh = w.shape[:-1]
    return w.reshape(sh + (GLA_HEADS, GLA_DKP))[..., :GLA_DK].reshape(sh + (GLA_HEADS * GLA_DK,))


W_IN_COLS = 2592
W_IN_SHARD = W_IN_COLS // N_DEV
FF_SHARD = D_FF // N_DEV


def _w_in_pieces():
    segments = [(0, 1024, 0)]
    segments += [(1024 + GLA_DK * h, 1024 + GLA_DK * (h + 1), 1024 + GLA_DKP * h) for h in range(GLA_HEADS)]
    segments += [(1280 + GLA_DK * h, 1280 + GLA_DK * (h + 1), 1536 + GLA_DKP * h) for h in range(GLA_HEADS)]
    segments += [(1536, 2560, 2048), (2560, W_IN_COLS, COL_LR)]
    pieces = []
    for lo, hi, dst in segments:
        while lo < hi:
            j = lo // W_IN_SHARD
            end = min(hi, (j + 1) * W_IN_SHARD)
            pieces.append((j, lo - j * W_IN_SHARD, end - lo, dst))
            dst += end - lo
            lo = end
    return pieces


def _w_in_from_shards(w):
    tr = 256

    def body(w_ref, o_ref):
        o_ref[...] = jnp.zeros_like(o_ref)
        for j, src, width, dst in _w_in_pieces():
            o_ref[:, dst:dst + width] = w_ref[j, :, src:src + width]

    return pl.pallas_call(
        body, grid=(D_MODEL // tr,),
        in_specs=[pl.BlockSpec((N_DEV, tr, W_IN_SHARD), lambda i: (0, i, 0))],
        out_specs=pl.BlockSpec((tr, PW), lambda i: (i, 0)), out_shape=_sds((D_MODEL, PW), w.dtype),
        name="w_in_from_shards", compiler_params=_cparams(("parallel",)))(w)


def _w_in_to_shards(g):
    tr = 256

    def body(g_ref, o_ref):
        for j, src, width, dst in _w_in_pieces():
            o_ref[j, :, src:src + width] = g_ref[:, dst:dst + width]

    return pl.pallas_call(
        body, grid=(D_MODEL // tr,),
        in_specs=[pl.BlockSpec((tr, PW), lambda i: (i, 0))],
        out_specs=pl.BlockSpec((N_DEV, tr, W_IN_SHARD), lambda i: (0, i, 0)),
        out_shape=_sds((N_DEV, D_MODEL, W_IN_SHARD), g.dtype),
        name="w_in_to_shards", compiler_params=_cparams(("parallel",)))(g)


def _cols_from_shards(w, name):
    _, d, c = w.shape
    tr = 256

    def body(w_ref, o_ref):
        for j in range(N_DEV):
            o_ref[:, j * c:(j + 1) * c] = w_ref[j]

    return pl.pallas_call(
        body, grid=(d // tr,),
        in_specs=[pl.BlockSpec((N_DEV, tr, c), lambda i: (0, i, 0))],
        out_specs=pl.BlockSpec((tr, N_DEV * c), lambda i: (i, 0)), out_shape=_sds((d, N_DEV * c), w.dtype),
        name=name, compiler_params=_cparams(("parallel",)))(w)


def _cols_to_shards(g, name):
    d, n = g.shape
    c = n // N_DEV
    tr = 256

    def body(g_ref, o_ref):
        for j in range(N_DEV):
            o_ref[j] = g_ref[:, j * c:(j + 1) * c]

    return pl.pallas_call(
        body, grid=(d // tr,),
        in_specs=[pl.BlockSpec((tr, n), lambda i: (i, 0))],
        out_specs=pl.BlockSpec((N_DEV, tr, c), lambda i: (0, i, 0)), out_shape=_sds((N_DEV, d, c), g.dtype),
        name=name, compiler_params=_cparams(("parallel",)))(g)


def _block_diag(w):
    n, b, _ = w.shape
    eye = jnp.eye(n, dtype=w.dtype)
    return (w[:, :, None, :] * eye[:, None, :, None]).reshape(n * b, n * b)


def _block_diag_of(w):
    n = D_RNN // 64
    eye = jnp.eye(n, dtype=w.dtype)
    return (w.reshape(n, 64, n, 64) * eye[:, None, :, None]).sum(axis=2)


def _gate_weight(wg, direction):
    w = _pad_heads(wg)
    lo = direction * GLA_RANK
    return jnp.pad(w, ((lo, LANES - GLA_RANK - lo), (0, 0)))


def _layer_weights(full, big, l):
    row = lambda v: v.reshape(1, -1)
    lw = dict(
        gpre=row(full["mix_norm_pre"][l]), gpost=row(full["mix_norm_post"][l]),
        w_in=_w_in_from_shards(big["w_in"]),
        cw=full["conv_w"][l], cb=row(full["conv_b"][l]),
        g_rnn=row(full["rnn_out_norm"][l]), g_gla=row(full["gla_out_norm"][l]),
        w_out=big["w_out"].reshape(D_MODEL, D_MODEL),
        fpre=row(full["ffn_norm_pre"][l]), fpost=row(full["ffn_norm_post"][l]),
        wg=_cols_from_shards(big["w_ffn_gate"], "w_ffn_gate_from_shards"),
        wu=_cols_from_shards(big["w_ffn_up"], "w_ffn_up_from_shards"),
        wd=big["w_ffn_down"].reshape(D_FF, D_MODEL))
    for d in (0, 1):
        lw[f"wa{d}"] = _block_diag(full["lru_w_a"][l, d]).astype(MXU_DTYPE)
        lw[f"wx{d}"] = _block_diag(full["lru_w_x"][l, d]).astype(MXU_DTYPE)
        lw[f"ba{d}"] = row(full["lru_b_a"][l, d])
        lw[f"bx{d}"] = row(full["lru_b_x"][l, d])
        lw[f"lam{d}"] = row(full["lru_lambda"][l, d])
        lw[f"gw{d}"] = _gate_weight(full["gla_w_gate"][l, d], d).astype(MXU_DTYPE)
        lw[f"gb{d}"] = row(_pad_heads(full["gla_b_gate"][l, d]))
    return lw


def _layer_fwd(x, lw, rider=None):
    proj = _mix_in_fwd(x, lw["gpre"], lw["w_in"])
    hs, sts = [], []
    osum = None
    for d in (0, 1):
        hs.append(_rnn_fwd(proj, lw["cw"], lw["cb"], lw[f"wa{d}"], lw[f"ba{d}"], lw[f"wx{d}"], lw[f"bx{d}"],
                           lw[f"lam{d}"], bool(d)))
        osum, st = _gla_fwd(proj, lw[f"gw{d}"], lw[f"gb{d}"], bool(d), osum)
        sts.append(st)
    x1, y = _mix_out_fwd(x, hs[0], hs[1], osum, proj, lw["g_rnn"], lw["g_gla"], lw["w_out"], lw["gpost"])
    (x2, a, u, f), ridden = _ffn_fwd(x1, lw["fpre"], lw["wg"], lw["wu"], lw["wd"], lw["fpost"], rider)
    saved = dict(x=x, proj=proj, hs=hs, osum=osum, sts=sts, y=y, x1=x1, a=a, u=u, f=f)
    return x2, saved, ridden


def _layer_bwd(dx2, sv, lw, rider=None):
    g = {}
    (dx1, df, h2, p, da, du, dfpost, dfpre), ridden = _ffn_bwd(
        dx2, sv["f"], sv["x1"], sv["a"], sv["u"], lw["fpre"], lw["wg"], lw["wu"], lw["wd"], lw["fpost"], rider)
    g["ffn_norm_post"], g["ffn_norm_pre"] = dfpost[0], dfpre[0]
    big = {}
    big["w_ffn_gate"] = _cols_to_shards(_tn_matmul(h2, da, "dw_ffn_gate"), "dw_ffn_gate_to_shards")
    big["w_ffn_up"] = _cols_to_shards(_tn_matmul(h2, du, "dw_ffn_up"), "dw_ffn_up_to_shards")
    big["w_ffn_down"] = _tn_matmul(p, df, "dw_ffn_down").reshape(N_DEV, FF_SHARD, D_MODEL)
    proj = sv["proj"]
    dm, dhs, dgr, dos, dg, dgpost, dgrnn, dggla = _mix_out_bwd(
        dx1, sv["y"], sv["hs"][0], sv["hs"][1], sv["osum"], proj, lw["g_rnn"], lw["g_gla"], lw["w_out"], lw["gpost"])
    g["mix_norm_post"], g["rnn_out_norm"], g["gla_out_norm"] = dgpost[0], dgrnn[0], dggla[0]
    big["w_out"] = _tn_matmul(sv["y"], dm, "dw_out").reshape(N_DEV, D_MODEL // N_DEV, D_MODEL)
    dxc = []
    gla = None
    gw, gb, wa, ba, wx, bx, lam = [], [], [], [], [], [], []
    for d in (0, 1):
        r = _gla_bwd(dos, proj, sv["sts"][d], lw[f"gw{d}"], lw[f"gb{d}"], bool(d), gla)
        gla = r[:4]
        lo = d * GLA_RANK
        gw.append(_unpad_heads(r[4][lo:lo + GLA_RANK]))
        gb.append(_unpad_heads(r[5][0]))
        r = _rnn_bwd(dhs, sv["hs"][d], proj, lw["cw"], lw["cb"], lw[f"wa{d}"], lw[f"ba{d}"], lw[f"wx{d}"],
                     lw[f"bx{d}"], lw[f"lam{d}"], bool(d))
        dxc.append(r[0])
        wa.append(_block_diag_of(r[1])); ba.append(r[2][0]); wx.append(_block_diag_of(r[3])); bx.append(r[4][0])
        lam.append(r[5][0])
    g["gla_w_gate"], g["gla_b_gate"] = jnp.stack(gw), jnp.stack(gb)
    g["lru_w_a"], g["lru_b_a"] = jnp.stack(wa), jnp.stack(ba)
    g["lru_w_x"], g["lru_b_x"], g["lru_lambda"] = jnp.stack(wx), jnp.stack(bx), jnp.stack(lam)
    dxr, dcw, dcb = _conv_bwd(dxc[0], dxc[1], proj, lw["cw"])
    g["conv_w"], g["conv_b"] = dcw, dcb[0]
    dx, dproj, h, dgpre = _mix_in_bwd((dxr, dgr, gla[0], gla[1], gla[2], dg), gla[3], sv["x"], dx1, lw["gpre"],
                                      lw["w_in"])
    g["mix_norm_pre"] = dgpre[0]
    big["w_in"] = _w_in_to_shards(_tn_matmul(h, dproj, "dw_in", rows=1024, acc_bytes=14 * 1024 * 1024))
    return dx, g, [big[n] for n in BIG_WEIGHTS], ridden


WEIGHT_NAMES = ["mix_norm_pre", "mix_norm_post", "w_in", "conv_w", "conv_b", "lru_w_a", "lru_b_a", "lru_w_x", "lru_b_x",
                "lru_lambda", "rnn_out_norm", "gla_w_gate", "gla_b_gate", "gla_out_norm", "w_out", "ffn_norm_pre",
                "ffn_norm_post", "w_ffn_gate", "w_ffn_up", "w_ffn_down"]
BIG_WEIGHTS = ["w_in", "w_out", "w_ffn_gate", "w_ffn_up", "w_ffn_down"]


def _local_step(x, target, full, hooks):
    saved, lws = [], []
    for l in range(DEPTH):
        lws.append(_layer_weights(full, hooks.big_weights(l), l))
        x, sv, ridden = _layer_fwd(x, lws[l], hooks.fwd_rider(l))
        hooks.fwd_ridden(l, ridden)
        saved.append(sv)
    loss, dx = _loss_fwd_bwd(x, target)
    grads = [None] * DEPTH
    for l in reversed(range(DEPTH)):
        dx, grads[l], big, ridden = _layer_bwd(dx, saved[l], lws[l], hooks.bwd_rider(l))
        hooks.bwd_ridden(l, ridden)
        hooks.big_grads(l, big)
    g = {n: jnp.stack([grads[l][n] for l in range(DEPTH)]) for n in WEIGHT_NAMES if n not in BIG_WEIGHTS}
    return loss[0, 0], dx, g


def _all_gather(x, name):
    def body(x_ref, out_ref, send_sems, recv_sems, local_sem):
        mx, my, mc = lax.axis_index("x"), lax.axis_index("y"), lax.axis_index("c")
        me, sibling = (mx, my, mc), (mx, my, 1 - mc)
        chips = [(1 - mx, my), (mx, 1 - my), (1 - mx, 1 - my)]

        def slot(px, py, pc):
            return out_ref.at[4 * px + 2 * py + pc]

        def copy(k, block, to, src=None):
            return pltpu.make_async_remote_copy(
                src_ref=slot(*block) if src is None else src, dst_ref=slot(*block),
                send_sem=send_sems.at[k], recv_sem=recv_sems.at[k], device_id=to, device_id_type=MESH_ID)

        mine = pltpu.make_async_copy(x_ref, slot(*me), local_sem)
        mine.start()
        first = [copy(0, me, sibling, src=x_ref)]
        first += [copy(1 + j, me, (*chip, mc), src=x_ref) for j, chip in enumerate(chips)]
        for cp in first:
            cp.start()
        passed = [copy(4 + j, (*chip, mc), sibling) for j, chip in enumerate(chips)]
        for j, chip in enumerate(chips):
            copy(1 + j, (*chip, mc), me).wait_recv()
            passed[j].start()
        copy(0, sibling, me).wait_recv()
        for j, chip in enumerate(chips):
            copy(4 + j, (*chip, 1 - mc), me).wait_recv()
        for cp in first + passed:
            cp.wait_send()
        mine.wait()

    return pl.pallas_call(
        body, out_shape=_sds((N_DEV,) + x.shape, x.dtype), in_specs=[ANY], out_specs=ANY,
        scratch_shapes=[pltpu.SemaphoreType.DMA((7,)), pltpu.SemaphoreType.DMA((7,)), pltpu.SemaphoreType.DMA],
        name=name)(x)


def _all_to_all(g, name):
    def body(g_ref, out_ref, send_sems, recv_sems, local_sem):
        mx, my, mc = lax.axis_index("x"), lax.axis_index("y"), lax.axis_index("c")
        me = 4 * mx + 2 * my + mc
        mine = pltpu.make_async_copy(g_ref.at[me], out_ref.at[me], local_sem)
        mine.start()
        copies = []
        for r in range(1, N_DEV):
            px = 1 - mx if r & 4 else mx
            py = 1 - my if r & 2 else my
            pc = 1 - mc if r & 1 else mc
            cp = pltpu.make_async_remote_copy(
                src_ref=g_ref.at[4 * px + 2 * py + pc], dst_ref=out_ref.at[me],
                send_sem=send_sems.at[r - 1], recv_sem=recv_sems.at[r - 1],
                device_id=(px, py, pc), device_id_type=MESH_ID)
            cp.start()
            copies.append(cp)
        for cp in copies:
            cp.wait()
        mine.wait()

    return pl.pallas_call(
        body, out_shape=_sds(g.shape, g.dtype), in_specs=[ANY], out_specs=ANY,
        scratch_shapes=[pltpu.SemaphoreType.DMA((7,)), pltpu.SemaphoreType.DMA((7,)), pltpu.SemaphoreType.DMA],
        name=name)(g)


def _sum_adamw(parts, w, m, v, name):
    _, r, c = parts.shape
    tr = _tile(r, ADAM_TILE_ROWS)

    def body(p_ref, w_ref, m_ref, v_ref, g_ref, d_ref, m2_ref, v2_ref):
        g = p_ref[0]
        for k in range(1, N_DEV):
            g = g + p_ref[k]
        g_ref[...] = g
        m2 = ADAM_B1 * m_ref[...] + (1.0 - ADAM_B1) * g
        v2 = ADAM_B2 * v_ref[...] + (1.0 - ADAM_B2) * (g * g)
        m2_ref[...] = m2
        v2_ref[...] = v2
        m_hat = m2 / (1.0 - ADAM_B1 ** ADAM_STEP)
        v_hat = v2 / (1.0 - ADAM_B2 ** ADAM_STEP)
        d_ref[...] = -ADAM_LR * (m_hat / (jnp.sqrt(v_hat) + ADAM_EPS) + ADAM_WD * w_ref[...])

    flat = pl.BlockSpec((tr, c), lambda i: (i, 0))
    return pl.pallas_call(
        body, grid=(r // tr,),
        in_specs=[pl.BlockSpec((N_DEV, tr, c), lambda i: (0, i, 0)), flat, flat, flat],
        out_specs=[flat] * 4, out_shape=[_sds((r, c))] * 4,
        name=name, compiler_params=_cparams(("parallel",)))(parts, w, m, v)


def _gather_big_weights(shards):
    n = len(shards)

    def body(*refs):
        srcs, outs = refs[:n], refs[n:2 * n]
        send_sems, recv_sems, local_sems = refs[2 * n:]
        mx, my, mc = lax.axis_index("x"), lax.axis_index("y"), lax.axis_index("c")
        me, sibling = (mx, my, mc), (mx, my, 1 - mc)
        chips = [(1 - mx, my), (mx, 1 - my), (1 - mx, 1 - my)]

        def slot(a, px, py, pc):
            return outs[a].at[4 * px + 2 * py + pc]

        def copy(k, a, block, to, own=False):
            return pltpu.make_async_remote_copy(
                src_ref=srcs[a] if own else slot(a, *block), dst_ref=slot(a, *block),
                send_sem=send_sems.at[k * n + a], recv_sem=recv_sems.at[k * n + a],
                device_id=to, device_id_type=MESH_ID)

        mine = [pltpu.make_async_copy(srcs[a], slot(a, *me), local_sems.at[a]) for a in range(n)]
        for cp in mine:
            cp.start()
        first = [copy(0, a, me, sibling, own=True) for a in range(n)]
        first += [copy(1 + j, a, me, (*chip, mc), own=True) for j, chip in enumerate(chips) for a in range(n)]
        for cp in first:
            cp.start()
        passed = [[copy(4 + j, a, (*chip, mc), sibling) for a in range(n)] for j, chip in enumerate(chips)]
        for j, chip in enumerate(chips):
            for a in range(n):
                copy(1 + j, a, (*chip, mc), me).wait_recv()
                passed[j][a].start()
        for a in range(n):
            copy(0, a, sibling, me).wait_recv()
        for j, chip in enumerate(chips):
            for a in range(n):
                copy(4 + j, a, (*chip, 1 - mc), me).wait_recv()
        for cp in first + [cp for row in passed for cp in row]:
            cp.wait_send()
        for cp in mine:
            cp.wait()

    return pl.pallas_call(
        body, out_shape=[_sds((N_DEV,) + s.shape, s.dtype) for s in shards],
        in_specs=[ANY] * n, out_specs=[ANY] * n,
        scratch_shapes=[pltpu.SemaphoreType.DMA((7 * n,)), pltpu.SemaphoreType.DMA((7 * n,)),
                        pltpu.SemaphoreType.DMA((n,))],
        name="gather_matmul_weights")(*shards)


def _run_alone(rider, name):
    r_in = len(rider.operands)
    r_out = len(rider.out_shape)

    def body(*refs):
        local, remote = rider.copies(refs[:r_in], refs[r_in:r_in + r_out], *refs[r_in + r_out:])
        for cp in local + remote:
            cp.start()
        for cp in remote:
            cp.wait()
        for cp in local:
            cp.wait()

    n_remote = (N_DEV - 1) * rider.n
    return pl.pallas_call(
        body, out_shape=rider.out_shape, in_specs=[ANY] * r_in, out_specs=[ANY] * r_out,
        input_output_aliases=dict(rider.aliases),
        scratch_shapes=[pltpu.SemaphoreType.DMA((n_remote,)), pltpu.SemaphoreType.DMA((n_remote,)),
                        pltpu.SemaphoreType.DMA((rider.n,))],
        name=name)(*rider.operands)


def _sum_adamw_big(parts, w, m, v, name):
    _, nl, a, b = parts.shape
    ta = _tile(a, 256)

    def body(p_ref, w_ref, m_ref, v_ref, g_ref, d_ref, m2_ref, v2_ref):
        g = p_ref[0].astype(F32)
        for k in range(1, N_DEV):
            g = g + p_ref[k].astype(F32)
        g_ref[...] = g
        m2 = ADAM_B1 * m_ref[...] + (1.0 - ADAM_B1) * g
        v2 = ADAM_B2 * v_ref[...] + (1.0 - ADAM_B2) * (g * g)
        m2_ref[...] = m2
        v2_ref[...] = v2
        m_hat = m2 / (1.0 - ADAM_B1 ** ADAM_STEP)
        v_hat = v2 / (1.0 - ADAM_B2 ** ADAM_STEP)
        d_ref[...] = -ADAM_LR * (m_hat / (jnp.sqrt(v_hat) + ADAM_EPS) + ADAM_WD * w_ref[...])

    blk = pl.BlockSpec((None, ta, b), lambda l, i: (l, i, 0))
    return pl.pallas_call(
        body, grid=(nl, a // ta),
        in_specs=[pl.BlockSpec((N_DEV, None, ta, b), lambda l, i: (0, l, i, 0)), blk, blk, blk],
        out_specs=[blk] * 4, out_shape=[_sds((nl, a, b))] * 4,
        name=name, compiler_params=_cparams(("parallel", "parallel")))(parts, w, m, v)


SMALL_SHARDED = [("conv_w", 2), ("lru_b_a", 2), ("lru_b_x", 2), ("lru_lambda", 2), ("gla_w_gate", 3), ("gla_b_gate", 2)]
REPLICATED = ["mix_norm_pre", "mix_norm_post", "conv_b", "lru_w_a", "lru_w_x", "rnn_out_norm", "gla_out_norm",
              "ffn_norm_pre", "ffn_norm_post"]


def _pack(arrays, cols, row_mult):
    flat = jnp.concatenate([a.reshape(-1) for a in arrays])
    unit = cols * row_mult
    total = -(-flat.shape[0] // unit) * unit
    return jnp.pad(flat, (0, total - flat.shape[0])).reshape(total // cols, cols)


def _pack_slots(arrays, cols, row_mult):
    flat = jnp.concatenate([a.reshape(N_DEV, -1) for a in arrays], axis=1)
    unit = cols * row_mult
    total = -(-flat.shape[1] // unit) * unit
    return jnp.pad(flat, ((0, 0), (0, total - flat.shape[1]))).reshape(N_DEV, total // cols, cols)


def _unpack(flat, shapes):
    flat = flat.reshape(-1)
    out, off = [], 0
    for sh in shapes:
        n = 1
        for d in sh:
            n *= d
        out.append(flat[off:off + n].reshape(sh))
        off += n
    return out


def _unpack_slots(flat, shapes):
    flat = flat.reshape(N_DEV, -1)
    out, off = [], 0
    for sh in shapes:
        n = 1
        for d in sh:
            n *= d
        out.append(flat[:, off:off + n].reshape((N_DEV,) + tuple(sh)))
        off += n
    return out


def _merge_shards(a, axis):
    a = jnp.moveaxis(a, 0, axis)
    sh = a.shape
    return a.reshape(sh[:axis] + (sh[axis] * sh[axis + 1],) + sh[axis + 2:])


def _split_shards(a, axis):
    sh = a.shape
    a = a.reshape(sh[:axis] + (N_DEV, sh[axis] // N_DEV) + sh[axis + 1:])
    return jnp.moveaxis(a, axis, 0)


class _StepHooks:
    def __init__(self, shards, recvs):
        self.shards = shards
        self.recvs = recvs
        self.gathered = {0: _gather_big_weights(shards[0])}
        self.pending = None

    def big_weights(self, l):
        return dict(zip(BIG_WEIGHTS, self.gathered.pop(l)))

    def fwd_rider(self, l):
        return _WeightGather(self.shards[l + 1]) if l + 1 < DEPTH else None

    def fwd_ridden(self, l, outs):
        if outs:
            self.gathered[l + 1] = outs

    def bwd_rider(self, l):
        if self.pending is None:
            return None
        layer, arrays = self.pending
        self.pending = None
        return _GradExchange(arrays, self.recvs, layer)

    def bwd_ridden(self, l, outs):
        if outs:
            self.recvs = list(outs)

    def big_grads(self, l, arrays):
        self.pending = (l, arrays)

    def finish(self):
        layer, arrays = self.pending
        self.pending = None
        self.recvs = list(_run_alone(_GradExchange(arrays, self.recvs, layer), f"exchange_grads_layer{layer}"))
        return self.recvs


def kernel(x, mix_norm_pre, mix_norm_post, w_in, conv_w, conv_b, lru_w_a, lru_b_a, lru_w_x, lru_b_x, lru_lambda, rnn_out_norm, gla_w_gate, gla_b_gate, gla_out_norm, w_out, ffn_norm_pre, ffn_norm_post, w_ffn_gate, w_ffn_up, w_ffn_down, loss_target, m_mix_norm_pre, m_mix_norm_post, m_w_in, m_conv_w, m_conv_b, m_lru_w_a, m_lru_b_a, m_lru_w_x, m_lru_b_x, m_lru_lambda, m_rnn_out_norm, m_gla_w_gate, m_gla_b_gate, m_gla_out_norm, m_w_out, m_ffn_norm_pre, m_ffn_norm_post, m_w_ffn_gate, m_w_ffn_up, m_w_ffn_down, v_mix_norm_pre, v_mix_norm_post, v_w_in, v_conv_w, v_conv_b, v_lru_w_a, v_lru_b_a, v_lru_w_x, v_lru_b_x, v_lru_lambda, v_rnn_out_norm, v_gla_w_gate, v_gla_b_gate, v_gla_out_norm, v_w_out, v_ffn_norm_pre, v_ffn_norm_post, v_w_ffn_gate, v_w_ffn_up, v_w_ffn_down):
    args = dict(locals())
    w = {n: args[n] for n in WEIGHT_NAMES}
    m = {n: args["m_" + n] for n in WEIGHT_NAMES}
    v = {n: args["v_" + n] for n in WEIGHT_NAMES}
    names_s = [n for n, _ in SMALL_SHARDED]
    axis_s = dict(SMALL_SHARDED)
    shapes_s = [w[n].shape for n in names_s]

    small = _pack([w[n] for n in names_s], LANES, 8)
    small_all = _unpack_slots(_all_gather(small, "gather_small_weights"), shapes_s)
    full = {n: w[n] for n in REPLICATED}
    for n, a in zip(names_s, small_all):
        full[n] = _merge_shards(a, axis_s[n])

    hooks = _StepHooks([[w[n][l].astype(MXU_DTYPE) for n in BIG_WEIGHTS] for l in range(DEPTH)],
                       [jnp.zeros((N_DEV,) + w[n].shape, MXU_DTYPE) for n in BIG_WEIGHTS])
    loss, dx, g = _local_step(x[0], loss_target[0], full, hooks)
    loss = lax.psum(loss, MESH_AXES)
    res = {}
    for n, parts in zip(BIG_WEIGHTS, hooks.finish()):
        res[n] = _sum_adamw_big(parts, w[n], m[n], v[n], "adamw_" + n)

    g_slots = _pack_slots([_split_shards(g[n], axis_s[n]) for n in names_s], LANES, 8)
    g_recv = _all_to_all(g_slots, "exchange_small_grads")
    packed = [_pack([t[n] for n in names_s], LANES, 8) for t in (w, m, v)]
    res_s = [_unpack(r, shapes_s) for r in _sum_adamw(g_recv, *packed, "adamw_small")]
    for i, n in enumerate(names_s):
        res[n] = [res_s[k][i] for k in range(4)]

    shapes_r = [w[n].shape for n in REPLICATED]
    g_rep = _all_gather(_pack([g[n] for n in REPLICATED], ADAM_COLS, ADAM_TILE_ROWS), "gather_replicated_grads")
    packed = [_pack([t[n] for n in REPLICATED], ADAM_COLS, ADAM_TILE_ROWS) for t in (w, m, v)]
    res_r = [_unpack(r, shapes_r) for r in _sum_adamw(g_rep, *packed, "adamw_replicated")]
    for i, n in enumerate(REPLICATED):
        res[n] = [res_r[k][i] for k in range(4)]

    outs = [[res[n][k] for n in WEIGHT_NAMES] for k in range(4)]
    return (loss, dx[None], *outs[0], *outs[1], *outs[2], *outs[3])
```

```python
import functools

import jax
import jax.numpy as jnp
from jax import lax
from jax.experimental import pallas as pl
from jax.experimental.pallas import tpu as pltpu

F32 = jnp.float32
MXU_DTYPE = jnp.bfloat16

N_DEV = 8
D_MODEL = 1024
D_RNN = 512
CONV_WIDTH = 4
LRU_C = 8.0
GLA_HEADS = 4
GLA_DK = 64
GLA_PAIRS = 2
PAIR_K = 128
PAIR_V = 256
GLA_DV = 128
GLA_RANK = 16
GLA_TAU = 16.0
GLA_CHUNK = 64
D_FF = 2816
RMS_EPS = 1e-6
DEPTH = 4

PW = 2688
COL_Q, COL_K, COL_V, COL_G, COL_LR = 1024, 1280, 1536, 2048, 2560
QK_W = GLA_HEADS * GLA_DK
LANES = 128

TILE_S = 512
TILE_F = 256
TILE_TN = 2048
TN_ACC_BYTES = 6 * 1024 * 1024
F_CHUNK = 1408
VMEM_LIMIT = 56 * 1024 * 1024

ADAM_LR = 0.001
ADAM_B1 = 0.9
ADAM_B2 = 0.999
ADAM_EPS = 1e-08
ADAM_WD = 0.01
ADAM_STEP = 10

ADAM_TILE_ROWS = 256
ADAM_COLS = 1024


def _mm(a, b):
    return jnp.dot(a.astype(MXU_DTYPE), b.astype(MXU_DTYPE), preferred_element_type=F32)


def _mm_nt(a, b):
    return lax.dot_general(a.astype(MXU_DTYPE), b.astype(MXU_DTYPE), (((1,), (1,)), ((), ())),
                           preferred_element_type=F32)


def _mm_tn(a, b):
    return lax.dot_general(a.astype(MXU_DTYPE), b.astype(MXU_DTYPE), (((0,), (0,)), ((), ())),
                           preferred_element_type=F32)


def _bmm(a, b):
    return lax.dot_general(a.astype(MXU_DTYPE), b.astype(MXU_DTYPE), (((2,), (1,)), ((0,), (0,))),
                           preferred_element_type=F32)


def _bmm_nt(a, b):
    return lax.dot_general(a.astype(MXU_DTYPE), b.astype(MXU_DTYPE), (((2,), (2,)), ((0,), (0,))),
                           preferred_element_type=F32)


def _bmm_tn(a, b):
    return lax.dot_general(a.astype(MXU_DTYPE), b.astype(MXU_DTYPE), (((1,), (1,)), ((0,), (0,))),
                           preferred_element_type=F32)


def _bmm_tri(tri, x):
    t = jnp.broadcast_to(tri.astype(jnp.bfloat16)[None], (x.shape[0],) + tri.shape)
    hi = x.astype(jnp.bfloat16)
    r1 = x - hi.astype(F32)
    mid = r1.astype(jnp.bfloat16)
    lo = (r1 - mid.astype(F32)).astype(jnp.bfloat16)
    dot = lambda v: lax.dot_general(t, v, (((2,), (1,)), ((0,), (0,))), preferred_element_type=F32)
    return dot(hi) + dot(mid) + dot(lo)


def _sigmoid(x):
    return 0.5 * jnp.tanh(0.5 * x) + 0.5


def _log1p_pos(e):
    series = e * (1.0 - e * (0.5 - e * (1.0 / 3.0 - e * 0.25)))
    return jnp.where(e < 0.01, series, jnp.log(1.0 + e))


def _softplus(x):
    return jnp.maximum(x, 0.0) + _log1p_pos(jnp.exp(-jnp.abs(x)))


def _softplus_coarse(x):
    return jnp.maximum(x, 0.0) + jnp.log(1.0 + jnp.exp(-jnp.abs(x)))


GELU_C = 0.7978845608028654
GELU_K = 0.044715


def _gelu_and_grad(x):
    t = jnp.tanh(GELU_C * (x + GELU_K * x * x * x))
    y = 0.5 * x * (1.0 + t)
    dy = 0.5 * (1.0 + t) + 0.5 * x * (1.0 - t * t) * GELU_C * (1.0 + 3.0 * GELU_K * x * x)
    return y, dy


def _rms_fwd(x, g):
    rs = lax.rsqrt(jnp.mean(x * x, axis=-1, keepdims=True) + RMS_EPS)
    n = x * rs
    return n * g, n, rs


def _rms_bwd(dy, n, rs, g):
    dn = dy * g
    dx = rs * (dn - n * jnp.mean(dn * n, axis=-1, keepdims=True))
    dg = jnp.sum(dy * n, axis=0, keepdims=True)
    return dx, dg


def _cparams(sem=None):
    kw = dict(vmem_limit_bytes=VMEM_LIMIT)
    if sem is not None:
        kw["dimension_semantics"] = sem
    return pltpu.CompilerParams(**kw)


def _tile(n, pref):
    return pref if n % pref == 0 else n


def _const(shape):
    nd = len(shape)
    return pl.BlockSpec(shape, lambda *_: (0,) * nd, pipeline_mode=pl.Buffered(1))


def _acc(shape):
    nd = len(shape)
    return pl.BlockSpec(shape, lambda *_: (0,) * nd)


def _rows(ts, w, col=0, order=None):
    if order is None:
        return pl.BlockSpec((ts, w), lambda i: (i, col))
    return pl.BlockSpec((ts, w), lambda i: (order(i), col))


def _sds(shape, dtype=F32):
    return jax.ShapeDtypeStruct(shape, dtype)


MESH_ID = pl.DeviceIdType.MESH
ANY = pl.BlockSpec(memory_space=pl.ANY)
MESH_AXES = ("x", "y", "c")


def _peers():
    mx, my, mc = lax.axis_index("x"), lax.axis_index("y"), lax.axis_index("c")
    peers = []
    for r in range(1, N_DEV):
        px = 1 - mx if r & 4 else mx
        py = 1 - my if r & 2 else my
        pc = 1 - mc if r & 1 else mc
        peers.append((4 * px + 2 * py + pc, (px, py, pc)))
    return 4 * mx + 2 * my + mc, peers


class _GradExchange:
    def __init__(self, srcs, recvs, layer):
        self.n = len(srcs)
        self.layer = layer
        self.operands = list(srcs) + list(recvs)
        self.out_shape = [_sds(r.shape, r.dtype) for r in recvs]
        self.aliases = {self.n + a: a for a in range(self.n)}

    def copies(self, ins, outs, send_sems, recv_sems, local_sems):
        n, layer = self.n, self.layer
        me, peers = _peers()
        local = [pltpu.make_async_copy(ins[a].at[me], outs[a].at[me, layer], local_sems.at[a]) for a in range(n)]
        remote = [pltpu.make_async_remote_copy(
            src_ref=ins[a].at[slot], dst_ref=outs[a].at[me, layer],
            send_sem=send_sems.at[r * n + a], recv_sem=recv_sems.at[r * n + a],
            device_id=dev, device_id_type=MESH_ID) for r, (slot, dev) in enumerate(peers) for a in range(n)]
        return local, remote


class _WeightGather:
    def __init__(self, shards):
        self.n = len(shards)
        self.operands = list(shards)
        self.out_shape = [_sds((N_DEV,) + s.shape, s.dtype) for s in shards]
        self.aliases = {}

    def copies(self, ins, outs, send_sems, recv_sems, local_sems):
        n = self.n
        me, peers = _peers()
        local = [pltpu.make_async_copy(ins[a], outs[a].at[me], local_sems.at[a]) for a in range(n)]
        remote = [pltpu.make_async_remote_copy(
            src_ref=ins[a], dst_ref=outs[a].at[me],
            send_sem=send_sems.at[r * n + a], recv_sem=recv_sems.at[r * n + a],
            device_id=dev, device_id_type=MESH_ID) for r, (_, dev) in enumerate(peers) for a in range(n)]
        return local, remote


def _call_with_rider(body, rider, operands, *, steps, in_specs, out_specs, out_shape, scratch_shapes=(), name,
                     semantics):
    if rider is None:
        outs = pl.pallas_call(body, grid=(steps,), in_specs=in_specs, out_specs=out_specs, out_shape=out_shape,
                              scratch_shapes=list(scratch_shapes), name=name,
                              compiler_params=_cparams((semantics,)))(*operands)
        return outs, []
    n_in, n_out, n_scr = len(in_specs), len(out_specs), len(scratch_shapes)
    r_in, r_out = len(rider.operands), len(rider.out_shape)

    def riding(*refs):
        own_in, ride_in = refs[:n_in], refs[n_in:n_in + r_in]
        refs = refs[n_in + r_in:]
        own_out, ride_out = refs[:n_out], refs[n_out:n_out + r_out]
        refs = refs[n_out + r_out:]
        own_scr, sems = refs[:n_scr], refs[n_scr:]
        i = pl.program_id(0)

        @pl.when(i == 0)
        def _():
            local, remote = rider.copies(ride_in, ride_out, *sems)
            for cp in local + remote:
                cp.start()

        body(*own_in, *own_out, *own_scr)

        @pl.when(i == steps - 1)
        def _():
            local, remote = rider.copies(ride_in, ride_out, *sems)
            for cp in remote:
                cp.wait()
            for cp in local:
                cp.wait()

    n_remote = (N_DEV - 1) * rider.n
    outs = pl.pallas_call(
        riding, grid=(steps,), in_specs=list(in_specs) + [ANY] * r_in, out_specs=list(out_specs) + [ANY] * r_out,
        out_shape=list(out_shape) + rider.out_shape,
        input_output_aliases={n_in + i: n_out + o for i, o in rider.aliases.items()},
        scratch_shapes=list(scratch_shapes) + [pltpu.SemaphoreType.DMA((n_remote,)),
                                               pltpu.SemaphoreType.DMA((n_remote,)),
                                               pltpu.SemaphoreType.DMA((rider.n,))],
        name=name, compiler_params=_cparams(("arbitrary",)))(*operands, *rider.operands)
    return outs[:n_out], outs[n_out:]


def _mix_in_fwd(x, gpre, w_in_p):
    s = x.shape[0]
    ts = _tile(s, TILE_S)

    def body(x_ref, g_ref, w_ref, o_ref):
        h, _, _ = _rms_fwd(x_ref[...], g_ref[...])
        o_ref[...] = _mm(h, w_ref[...])

    return pl.pallas_call(
        body, grid=(s // ts,),
        in_specs=[_rows(ts, D_MODEL), _const((1, D_MODEL)), _const((D_MODEL, PW))],
        out_specs=_rows(ts, PW), out_shape=_sds((s, PW)),
        name="mix_in_fwd", compiler_params=_cparams(("parallel",)))(x, gpre, w_in_p)


def _conv_taps(xr, hp, hn, first, last):
    ts = xr.shape[0]
    hp = jnp.where(first, 0.0, hp)
    hn = jnp.where(last, 0.0, hn)
    xe = jnp.concatenate([hp, xr, hn], axis=0)
    return xe[6:6 + ts], xe[7:7 + ts], xr, xe[9:9 + ts]


def _conv_fwd(xr, hp, hn, cw, cb, first, last):
    t0, t1, t2, t3 = _conv_taps(xr, hp, hn, first, last)
    return cw[0:1] * t0 + cw[1:2] * t1 + cw[2:3] * t2 + cw[3:4] * t3 + cb


def _rnn_gates(xc, wa, ba, wx, bx, lam):
    r = _sigmoid(_mm(xc, wa) + ba)
    i = _sigmoid(_mm(xc, wx) + bx)
    sp = _softplus(-lam)
    la = (-LRU_C) * r * sp
    a = jnp.exp(la)
    mult = jnp.sqrt(-jnp.tanh(la) * (a * a + 1.0))
    return r, i, sp, a, mult


def _scan_tile(a_scr, u_scr, h_ref, c0, reverse):
    ts = a_scr.shape[0]
    a = a_scr[...]
    u = u_scr[...]
    row = lax.broadcasted_iota(jnp.int32, a.shape, 0) % 8
    for k in (1, 2, 4):
        if reverse:
            a_sh = pltpu.roll(a, ts - k, 0)
            u_sh = pltpu.roll(u, ts - k, 0)
            ok = row < 8 - k
        else:
            a_sh = pltpu.roll(a, k, 0)
            u_sh = pltpu.roll(u, k, 0)
            ok = row >= k
        u = jnp.where(ok, u + a * u_sh, u)
        a = jnp.where(ok, a * a_sh, a)
    a_scr[...] = a
    u_scr[...] = u
    ng = ts // 8

    def body(j, c):
        g = (ng - 1 - j) if reverse else j
        sl = pl.ds(pl.multiple_of(g * 8, 8), 8)
        hh = u_scr[sl, :] + a_scr[sl, :] * c
        h_ref[sl, :] = hh
        return hh[0:1, :] if reverse else hh[7:8, :]

    return lax.fori_loop(0, ng, body, c0)


def _halo_specs(s, ts, w, col, order):
    n8 = s // 8
    per = ts // 8
    prev = pl.BlockSpec((8, w), lambda i: (jnp.maximum(order(i) * per - 1, 0), col))
    nxt = pl.BlockSpec((8, w), lambda i: (jnp.minimum((order(i) + 1) * per, n8 - 1), col))
    return prev, nxt


def _rnn_fwd(proj, cw, cb, wa, ba, wx, bx, lam, reverse):
    s = proj.shape[0]
    ts = _tile(s, TILE_S)
    nt = s // ts
    order = (lambda i: nt - 1 - i) if reverse else (lambda i: i)

    def body(xr_ref, hp_ref, hn_ref, cw_ref, cb_ref, wa_ref, ba_ref, wx_ref, bx_ref, lam_ref,
             h_ref, a_scr, u_scr, c_scr):
        i = pl.program_id(0)
        t = order(i)

        @pl.when(i == 0)
        def _():
            c_scr[...] = jnp.zeros_like(c_scr)

        xc = _conv_fwd(xr_ref[...], hp_ref[...], hn_ref[...], cw_ref[...], cb_ref[...], t == 0, t == nt - 1)
        _, gi, _, a, mult = _rnn_gates(xc, wa_ref[...], ba_ref[...], wx_ref[...], bx_ref[...], lam_ref[...])
        a_scr[...] = a
        u_scr[...] = xc * gi * mult
        c_scr[0:1, :] = _scan_tile(a_scr, u_scr, h_ref, c_scr[0:1, :], reverse)

    hp, hn = _halo_specs(s, ts, D_RNN, 0, order)
    return pl.pallas_call(
        body, grid=(nt,),
        in_specs=[_rows(ts, D_RNN, 0, order), hp, hn, _const((CONV_WIDTH, D_RNN)), _const((1, D_RNN)),
                  _const((D_RNN, D_RNN)), _const((1, D_RNN)), _const((D_RNN, D_RNN)), _const((1, D_RNN)),
                  _const((1, D_RNN))],
        out_specs=_rows(ts, D_RNN, 0, order), out_shape=_sds((s, D_RNN)),
        scratch_shapes=[pltpu.VMEM((ts, D_RNN), F32), pltpu.VMEM((ts, D_RNN), F32), pltpu.VMEM((8, D_RNN), F32)],
        name="rnn_fwd_rev" if reverse else "rnn_fwd",
        compiler_params=_cparams(("arbitrary",)))(proj, proj, proj, cw, cb, wa, ba, wx, bx, lam)


def _tri(reverse, transpose=False):
    r = lax.broadcasted_iota(jnp.int32, (GLA_CHUNK, GLA_CHUNK), 0)
    c = lax.broadcasted_iota(jnp.int32, (GLA_CHUNK, GLA_CHUNK), 1)
    if transpose:
        r, c = c, r
    return ((r <= c) if reverse else (r >= c)).astype(F32)


def _gla_chunk_terms(q, k, la, tri, reverse):
    b = _bmm_tri(tri, la)
    bl = b[:, 0:1] if reverse else b[:, GLA_CHUNK - 1:GLA_CHUNK]
    eb = jnp.exp(b)
    enb = jnp.exp(-b)
    ebl = jnp.exp(bl - b)
    d = jnp.exp(bl)
    return eb, enb, ebl, d, q * (GLA_DK ** -0.5) * eb, k * enb, k * ebl


def _gla_gate(lr, wg, bg):
    z = _mm(lr, wg) + bg
    return z, -_softplus_coarse(-z) * (1.0 / GLA_TAU)


def _pair_masks():
    first_head = lax.broadcasted_iota(jnp.int32, (1, PAIR_K), 1) < GLA_DK
    row_first = lax.broadcasted_iota(jnp.int32, (PAIR_V, PAIR_K), 0) < GLA_DV
    lane_first = lax.broadcasted_iota(jnp.int32, (PAIR_V, PAIR_K), 1) < GLA_DK
    return first_head, row_first == lane_first


def _gla_specs(ts, order):
    return [_rows(ts, QK_W, COL_Q // QK_W, order), _rows(ts, QK_W, COL_K // QK_W, order),
            _rows(ts, GLA_HEADS * GLA_DV, COL_V // (GLA_HEADS * GLA_DV), order),
            _rows(ts, LANES, COL_LR // LANES, order)]


def _gla_fwd(proj, wg, bg, reverse, o_add=None):
    s = proj.shape[0]
    ts = _tile(s, TILE_S)
    nt = s // ts
    ch = ts // GLA_CHUNK
    vw = GLA_HEADS * GLA_DV
    order = (lambda i: nt - 1 - i) if reverse else (lambda i: i)
    extra = [] if o_add is None else [o_add]

    def body(q_ref, k_ref, v_ref, lr_ref, wg_ref, bg_ref, *rest):
        add_ref = None if o_add is None else rest[0]
        o_ref, st_ref, s_scr = rest[len(extra):]

        @pl.when(pl.program_id(0) == 0)
        def _():
            s_scr[...] = jnp.zeros_like(s_scr)

        _, la = _gla_gate(lr_ref[...], wg_ref[...], bg_ref[...])
        tri = _tri(reverse)
        keep = tri > 0.5
        first_head, own = _pair_masks()
        chunks = lambda a: a.reshape(ch, GLA_CHUNK, a.shape[-1])
        _, _, _, d, qe, ke, kd = _gla_chunk_terms(chunks(q_ref[...]), chunks(k_ref[...]), chunks(la), tri, reverse)
        v = chunks(v_ref[...])
        outs = []
        for p in range(GLA_PAIRS):
            ln = slice(p * PAIR_K, (p + 1) * PAIR_K)
            v_p = v[:, :, p * PAIR_V:(p + 1) * PAIR_V]
            qe_p, ke_p = qe[:, :, ln], ke[:, :, ln]
            grow = jnp.where(own, _bmm_tn(v_p, kd[:, :, ln]), 0.0)
            st = s_scr[p]
            for cc in range(ch):
                c = (ch - 1 - cc) if reverse else cc
                st_ref[c, p] = st
                st = d[c, :, ln] * st + grow[c]
            s_scr[p] = st
            intra = []
            for h in range(2):
                q_h = jnp.where(first_head if h == 0 else ~first_head, qe_p, 0.0)
                a_m = jnp.where(keep, _bmm_nt(q_h, ke_p), 0.0)
                intra.append(_bmm(a_m, v_p[:, :, h * GLA_DV:(h + 1) * GLA_DV]))
            outs.append(_bmm_nt(qe_p, st_ref[:, p]) + jnp.concatenate(intra, axis=2))
        o = jnp.concatenate(outs, axis=2).reshape(ts, vw)
        o_ref[...] = o if add_ref is None else o + add_ref[...]

    return pl.pallas_call(
        body, grid=(nt,),
        in_specs=_gla_specs(ts, order) + [_const((LANES, QK_W)), _const((1, QK_W))]
                 + [_rows(ts, vw, 0, order)] * len(extra),
        out_specs=[_rows(ts, vw, 0, order),
                   pl.BlockSpec((ch, GLA_PAIRS, PAIR_V, PAIR_K), lambda i: (order(i), 0, 0, 0))],
        out_shape=[_sds((s, vw)), _sds((s // GLA_CHUNK, GLA_PAIRS, PAIR_V, PAIR_K))],
        scratch_shapes=[pltpu.VMEM((GLA_PAIRS, PAIR_V, PAIR_K), F32)],
        name="gla_fwd_rev" if reverse else "gla_fwd",
        compiler_params=_cparams(("arbitrary",)))(proj, proj, proj, proj, wg, bg, *extra)


def _mix_out_terms(hf, hb, gate_r, osum, g, g_rnn, g_gla):
    hs = hf + hb
    gl, dgl = _gelu_and_grad(gate_r)
    z = hs * gl
    y_rnn, n_rnn, rs_rnn = _rms_fwd(z, g_rnn)
    sg_lin = _sigmoid(g)
    sg = g * sg_lin
    dsg = sg_lin * (1.0 + g * (1.0 - sg_lin))
    ons, ns, rss = [], [], []
    for h in range(GLA_HEADS):
        ln = slice(h * LANES, (h + 1) * LANES)
        on, n, rs = _rms_fwd(osum[:, ln], g_gla)
        ons.append(on)
        ns.append(n)
        rss.append(rs)
    on = jnp.concatenate(ons, axis=1)
    return hs, gl, dgl, y_rnn, n_rnn, rs_rnn, sg, dsg, on, ns, rss


def _mix_out_fwd(x, hf, hb, osum, proj, g_rnn, g_gla, w_out, gpost):
    s = x.shape[0]
    ts = _tile(s, TILE_S)

    def body(x_ref, hf_ref, hb_ref, gr_ref, os_ref, g_ref, grnn_ref, ggla_ref, w_ref, gp_ref, x1_ref, y_ref):
        _, _, _, y_rnn, _, _, sg, _, on, _, _ = _mix_out_terms(
            hf_ref[...], hb_ref[...], gr_ref[...], os_ref[...], g_ref[...], grnn_ref[...], ggla_ref[...])
        y = jnp.concatenate([y_rnn, on * sg], axis=1).astype(MXU_DTYPE)
        y_ref[...] = y
        out, _, _ = _rms_fwd(_mm(y, w_ref[...]), gp_ref[...])
        x1_ref[...] = x_ref[...] + out

    return pl.pallas_call(
        body, grid=(s // ts,),
        in_specs=[_rows(ts, D_MODEL), _rows(ts, D_RNN), _rows(ts, D_RNN), _rows(ts, D_RNN, 1), _rows(ts, 512),
                  _rows(ts, 512, COL_G // 512), _const((1, D_RNN)), _const((1, GLA_DV)),
                  _const((D_MODEL, D_MODEL)), _const((1, D_MODEL))],
        out_specs=[_rows(ts, D_MODEL), _rows(ts, D_MODEL)],
        out_shape=[_sds((s, D_MODEL)), _sds((s, D_MODEL), MXU_DTYPE)],
        name="mix_out_fwd", compiler_params=_cparams(("parallel",)))(
            x, hf, hb, proj, osum, proj, g_rnn, g_gla, w_out, gpost)


def _f_chunks():
    return [(c0, min(c0 + F_CHUNK, D_FF)) for c0 in range(0, D_FF, F_CHUNK)]


def _ffn_fwd(x1, gpre, wg, wu, wd, gpost, rider=None):
    s = x1.shape[0]
    ts = _tile(s, TILE_F)

    def body(x_ref, gpre_ref, wg_ref, wu_ref, wd_ref, gpost_ref, x2_ref, a_ref, u_ref, f_ref):
        x = x_ref[...]
        h, _, _ = _rms_fwd(x, gpre_ref[...])
        h = h.astype(MXU_DTYPE)
        f = jnp.zeros((ts, D_MODEL), F32)
        for c0, c1 in _f_chunks():
            a = _mm(h, wg_ref[:, c0:c1])
            u = _mm(h, wu_ref[:, c0:c1])
            a_ref[:, c0:c1] = a.astype(MXU_DTYPE)
            u_ref[:, c0:c1] = u.astype(MXU_DTYPE)
            f = f + _mm(a * _sigmoid(a) * u, wd_ref[c0:c1, :])
        f_ref[...] = f
        out, _, _ = _rms_fwd(f, gpost_ref[...])
        x2_ref[...] = x + out

    return _call_with_rider(
        body, rider, (x1, gpre, wg, wu, wd, gpost), steps=s // ts,
        in_specs=[_rows(ts, D_MODEL), _const((1, D_MODEL)), _const((D_MODEL, D_FF)), _const((D_MODEL, D_FF)),
                  _const((D_FF, D_MODEL)), _const((1, D_MODEL))],
        out_specs=[_rows(ts, D_MODEL), _rows(ts, D_FF), _rows(ts, D_FF), _rows(ts, D_MODEL)],
        out_shape=[_sds((s, D_MODEL)), _sds((s, D_FF), MXU_DTYPE), _sds((s, D_FF), MXU_DTYPE), _sds((s, D_MODEL))],
        name="ffn_fwd", semantics="parallel")


def _loss_fwd_bwd(y, target):
    s = y.shape[0]
    ts = _tile(s, TILE_S)

    def body(y_ref, t_ref, loss_ref, dy_ref):
        @pl.when(pl.program_id(0) == 0)
        def _():
            loss_ref[...] = jnp.zeros_like(loss_ref)

        e = y_ref[...] - t_ref[...]
        dy_ref[...] = e * (1.0 / D_MODEL)
        part = jnp.sum(jnp.sum(e * e, axis=1, keepdims=True), axis=0, keepdims=True) * (0.5 / D_MODEL)
        loss_ref[...] += jnp.broadcast_to(part, loss_ref.shape)

    return pl.pallas_call(
        body, grid=(s // ts,),
        in_specs=[_rows(ts, D_MODEL), _rows(ts, D_MODEL)],
        out_specs=[_acc((8, LANES)), _rows(ts, D_MODEL)],
        out_shape=[_sds((8, LANES)), _sds((s, D_MODEL))],
        name="loss", compiler_params=_cparams(("arbitrary",)))(y, target)


def _tn_matmul(a, b, name, rows=TILE_TN, acc_bytes=TN_ACC_BYTES):
    s, k = a.shape
    n = b.shape[1]
    ts = _tile(s, rows)
    tn = max(t for t in range(LANES, n + 1, LANES) if n % t == 0 and (k * t * 4 <= acc_bytes or t == LANES))
    ns = s // ts

    def body(a_ref, b_ref, o_ref, acc):
        i = pl.program_id(1)

        @pl.when(i == 0)
        def _():
            acc[...] = jnp.zeros_like(acc)

        acc[...] += _mm_tn(a_ref[...], b_ref[...])

        @pl.when(i == ns - 1)
        def _():
            o_ref[...] = acc[...].astype(o_ref.dtype)

    return pl.pallas_call(
        body, grid=(n // tn, ns),
        in_specs=[pl.BlockSpec((ts, k), lambda j, i: (i, 0)), pl.BlockSpec((ts, tn), lambda j, i: (i, j))],
        out_specs=pl.BlockSpec((k, tn), lambda j, i: (0, j)), out_shape=_sds((k, n), MXU_DTYPE),
        scratch_shapes=[pltpu.VMEM((k, tn), F32)],
        name=name, compiler_params=_cparams(("parallel", "arbitrary")))(a, b)


def _ffn_bwd(dx2, f, x1, a, u, gpre, wg, wu, wd, gpost, rider=None):
    s = x1.shape[0]
    ts = _tile(s, TILE_F)

    def body(dx2_ref, f_ref, x1_ref, a_ref, u_ref, gpre_ref, wg_ref, wu_ref, wd_ref, gpost_ref,
             dx1_ref, df_ref, h_ref, p_ref, da_ref, du_ref, dgpost_ref, dgpre_ref):
        @pl.when(pl.program_id(0) == 0)
        def _():
            dgpost_ref[...] = jnp.zeros_like(dgpost_ref)
            dgpre_ref[...] = jnp.zeros_like(dgpre_ref)

        dx2 = dx2_ref[...]
        _, nf, rsf = _rms_fwd(f_ref[...], gpost_ref[...])
        df, dgpost = _rms_bwd(dx2, nf, rsf, gpost_ref[...])
        dgpost_ref[...] += dgpost
        df = df.astype(MXU_DTYPE)
        df_ref[...] = df
        h, n1, rs1 = _rms_fwd(x1_ref[...], gpre_ref[...])
        h_ref[...] = h.astype(MXU_DTYPE)
        dh = jnp.zeros((ts, D_MODEL), F32)
        for c0, c1 in _f_chunks():
            av = a_ref[:, c0:c1].astype(F32)
            uv = u_ref[:, c0:c1].astype(F32)
            sg = _sigmoid(av)
            dp = _mm_nt(df, wd_ref[c0:c1, :])
            p_ref[:, c0:c1] = (av * sg * uv).astype(MXU_DTYPE)
            da = (dp * uv * sg * (1.0 + av * (1.0 - sg))).astype(MXU_DTYPE)
            du = (dp * av * sg).astype(MXU_DTYPE)
            da_ref[:, c0:c1] = da
            du_ref[:, c0:c1] = du
            dh = dh + _mm_nt(da, wg_ref[:, c0:c1]) + _mm_nt(du, wu_ref[:, c0:c1])
        dx, dgpre = _rms_bwd(dh, n1, rs1, gpre_ref[...])
        dgpre_ref[...] += dgpre
        dx1_ref[...] = dx2 + dx

    return _call_with_rider(
        body, rider, (dx2, f, x1, a, u, gpre, wg, wu, wd, gpost), steps=s // ts,
        in_specs=[_rows(ts, D_MODEL), _rows(ts, D_MODEL), _rows(ts, D_MODEL), _rows(ts, D_FF), _rows(ts, D_FF),
                  _const((1, D_MODEL)), _const((D_MODEL, D_FF)), _const((D_MODEL, D_FF)), _const((D_FF, D_MODEL)),
                  _const((1, D_MODEL))],
        out_specs=[_rows(ts, D_MODEL), _rows(ts, D_MODEL), _rows(ts, D_MODEL), _rows(ts, D_FF), _rows(ts, D_FF),
                   _rows(ts, D_FF), _acc((1, D_MODEL)), _acc((1, D_MODEL))],
        out_shape=[_sds((s, D_MODEL)), _sds((s, D_MODEL), MXU_DTYPE), _sds((s, D_MODEL), MXU_DTYPE),
                   _sds((s, D_FF), MXU_DTYPE), _sds((s, D_FF), MXU_DTYPE), _sds((s, D_FF), MXU_DTYPE),
                   _sds((1, D_MODEL)), _sds((1, D_MODEL))],
        name="ffn_bwd", semantics="arbitrary")


def _mix_out_bwd(dx1, y, hf, hb, osum, proj, g_rnn, g_gla, w_out, gpost):
    s = y.shape[0]
    ts = _tile(s, TILE_S)

    def body(dx1_ref, y_ref, hf_ref, hb_ref, gr_ref, os_ref, g_ref, grnn_ref, ggla_ref, w_ref, gp_ref,
             dm_ref, dhs_ref, dgr_ref, dos_ref, dg_ref, dgpost_ref, dgrnn_ref, dggla_ref):
        @pl.when(pl.program_id(0) == 0)
        def _():
            dgpost_ref[...] = jnp.zeros_like(dgpost_ref)
            dgrnn_ref[...] = jnp.zeros_like(dgrnn_ref)
            dggla_ref[...] = jnp.zeros_like(dggla_ref)

        _, nm, rsm = _rms_fwd(_mm(y_ref[...], w_ref[...]), gp_ref[...])
        dm, dgpost = _rms_bwd(dx1_ref[...], nm, rsm, gp_ref[...])
        dgpost_ref[...] += dgpost
        dm = dm.astype(MXU_DTYPE)
        dm_ref[...] = dm
        dy = _mm_nt(dm, w_ref[...])
        hs, gl, dgl, _, n_rnn, rs_rnn, sg, dsg, on, ns, rss = _mix_out_terms(
            hf_ref[...], hb_ref[...], gr_ref[...], os_ref[...], g_ref[...], grnn_ref[...], ggla_ref[...])
        dz, dgrnn = _rms_bwd(dy[:, :D_RNN], n_rnn, rs_rnn, grnn_ref[...])
        dgrnn_ref[...] += dgrnn
        dhs_ref[...] = dz * gl
        dgr_ref[...] = (dz * hs * dgl).astype(MXU_DTYPE)
        dyg = dy[:, D_RNN:]
        dg_ref[...] = (dyg * on * dsg).astype(MXU_DTYPE)
        don = dyg * sg
        dggla = jnp.zeros((1, GLA_DV), F32)
        for h in range(GLA_HEADS):
            ln = slice(h * LANES, (h + 1) * LANES)
            dos, dgh = _rms_bwd(don[:, ln], ns[h], rss[h], ggla_ref[...])
            dos_ref[:, ln] = dos.astype(MXU_DTYPE)
            dggla = dggla + dgh
        dggla_ref[...] += dggla

    return pl.pallas_call(
        body, grid=(s // ts,),
        in_specs=[_rows(ts, D_MODEL), _rows(ts, D_MODEL), _rows(ts, D_RNN), _rows(ts, D_RNN), _rows(ts, D_RNN, 1),
                  _rows(ts, 512), _rows(ts, 512, COL_G // 512), _const((1, D_RNN)), _const((1, GLA_DV)),
                  _const((D_MODEL, D_MODEL)), _const((1, D_MODEL))],
        out_specs=[_rows(ts, D_MODEL), _rows(ts, D_RNN), _rows(ts, D_RNN), _rows(ts, 512), _rows(ts, 512),
                   _acc((1, D_MODEL)), _acc((1, D_RNN)), _acc((1, GLA_DV))],
        out_shape=[_sds((s, D_MODEL), MXU_DTYPE), _sds((s, D_RNN)), _sds((s, D_RNN), MXU_DTYPE),
                   _sds((s, 512), MXU_DTYPE), _sds((s, 512), MXU_DTYPE),
                   _sds((1, D_MODEL)), _sds((1, D_RNN)), _sds((1, GLA_DV))],
        name="mix_out_bwd", compiler_params=_cparams(("arbitrary",)))(
            dx1, y, hf, hb, proj, osum, proj, g_rnn, g_gla, w_out, gpost)


def _gla_bwd(dos, proj, st, wg, bg, reverse, prev=None):
    s = proj.shape[0]
    ts = _tile(s, TILE_S)
    nt = s // ts
    ch = ts // GLA_CHUNK
    vw = GLA_HEADS * GLA_DV
    order = (lambda i: i) if reverse else (lambda i: nt - 1 - i)

    n_prev = 0 if prev is None else 4

    def body(do_ref, q_ref, k_ref, v_ref, lr_ref, st_ref, wg_ref, bg_ref, *rest):
        pq_ref, pk_ref, pv_ref, plr_ref = rest[:n_prev] if n_prev else (None,) * 4
        dq_ref, dk_ref, dv_ref, dlr_ref, dwg_ref, dbg_ref, ds_scr, dsa_scr = rest[n_prev:]

        def put(ref, p_ref, val):
            if p_ref is not None:
                val = val + p_ref[...].astype(F32)
            ref[...] = val.astype(ref.dtype)

        @pl.when(pl.program_id(0) == 0)
        def _():
            ds_scr[...] = jnp.zeros_like(ds_scr)
            dwg_ref[...] = jnp.zeros_like(dwg_ref)
            dbg_ref[...] = jnp.zeros_like(dbg_ref)

        z, la = _gla_gate(lr_ref[...], wg_ref[...], bg_ref[...])
        tri = _tri(reverse)
        tri_t = _tri(reverse, transpose=True)
        keep = tri > 0.5
        last_row = 0 if reverse else GLA_CHUNK - 1
        is_last = lax.broadcasted_iota(jnp.int32, (ch, GLA_CHUNK, QK_W), 1) == last_row
        first_head, own = _pair_masks()
        chunks = lambda a: a.reshape(ch, GLA_CHUNK, a.shape[-1])
        eb, enb, ebl, d, qe, ke, kd = _gla_chunk_terms(chunks(q_ref[...]), chunks(k_ref[...]), chunks(la), tri, reverse)
        v = chunks(v_ref[...])
        do = chunks(do_ref[...])
        dqe, dke, dkd, dd, dv = [], [], [], [], []
        for p in range(GLA_PAIRS):
            ln = slice(p * PAIR_K, (p + 1) * PAIR_K)
            lv = slice(p * PAIR_V, (p + 1) * PAIR_V)
            v_p, do_p = v[:, :, lv], do[:, :, lv]
            qe_p, ke_p, kd_p = qe[:, :, ln], ke[:, :, ln], kd[:, :, ln]
            grow = jnp.where(own, _bmm_tn(do_p, qe_p), 0.0)
            dst = ds_scr[p]
            for cc in range(ch):
                c = cc if reverse else (ch - 1 - cc)
                dsa_scr[c, p] = dst
                dst = grow[c] + d[c, :, ln] * dst
            ds_scr[p] = dst
            st_p = st_ref[:, p]
            dst_p = dsa_scr[:, p]
            dv_intra, dqe_intra, dke_intra = [], [], None
            for h in range(2):
                mine = first_head if h == 0 else ~first_head
                hv = slice(h * GLA_DV, (h + 1) * GLA_DV)
                q_h = jnp.where(mine, qe_p, 0.0)
                a_m = jnp.where(keep, _bmm_nt(q_h, ke_p), 0.0)
                da_m = jnp.where(keep, _bmm_nt(do_p[:, :, hv], v_p[:, :, hv]), 0.0)
                dv_intra.append(_bmm_tn(a_m, do_p[:, :, hv]))
                dqe_intra.append(_bmm(da_m, ke_p))
                dk_h = _bmm_tn(da_m, q_h)
                dke_intra = dk_h if dke_intra is None else dke_intra + dk_h
            dv.append(jnp.concatenate(dv_intra, axis=2) + _bmm_nt(kd_p, dst_p))
            dqe.append(jnp.where(first_head, dqe_intra[0], dqe_intra[1]) + _bmm(do_p, st_p))
            dke.append(dke_intra)
            dkd.append(_bmm(v_p, dst_p))
            dd.append(jnp.sum(dst_p * st_p, axis=1, keepdims=True))
        dqe = jnp.concatenate(dqe, axis=2)
        dke = jnp.concatenate(dke, axis=2)
        dkd = jnp.concatenate(dkd, axis=2)
        dd = jnp.concatenate(dd, axis=2)
        dbl = dd * d + jnp.sum(dkd * kd, axis=1, keepdims=True)
        db = dqe * qe - dke * ke - dkd * kd
        db = jnp.where(is_last, db + dbl, db)
        put(dv_ref, pv_ref, jnp.concatenate(dv, axis=2).reshape(ts, vw))
        put(dq_ref, pq_ref, (dqe * eb * (GLA_DK ** -0.5)).reshape(ts, QK_W))
        put(dk_ref, pk_ref, (dke * enb + dkd * ebl).reshape(ts, QK_W))
        dz = (_bmm_tri(tri_t, db) * (1.0 / GLA_TAU)).reshape(ts, QK_W) * _sigmoid(-z)
        put(dlr_ref, plr_ref, _mm_nt(dz, wg_ref[...]))
        dwg_ref[...] += _mm_tn(lr_ref[...], dz)
        dbg_ref[...] += jnp.sum(dz, axis=0, keepdims=True)

    wide, mid, narrow = _rows(ts, vw, 0, order), _rows(ts, QK_W, 0, order), _rows(ts, LANES, 0, order)
    return pl.pallas_call(
        body, grid=(nt,),
        in_specs=[wide] + _gla_specs(ts, order)
                 + [pl.BlockSpec((ch, GLA_PAIRS, PAIR_V, PAIR_K), lambda i: (order(i), 0, 0, 0)),
                    _const((LANES, QK_W)), _const((1, QK_W))] + ([mid, mid, wide, narrow] if n_prev else []),
        out_specs=[mid, mid, wide, narrow, _acc((LANES, QK_W)), _acc((1, QK_W))],
        out_shape=[_sds((s, QK_W), MXU_DTYPE), _sds((s, QK_W), MXU_DTYPE), _sds((s, vw), MXU_DTYPE),
                   _sds((s, LANES), MXU_DTYPE), _sds((LANES, QK_W)), _sds((1, QK_W))],
        scratch_shapes=[pltpu.VMEM((GLA_PAIRS, PAIR_V, PAIR_K), F32),
                        pltpu.VMEM((ch, GLA_PAIRS, PAIR_V, PAIR_K), F32)],
        name="gla_bwd_rev" if reverse else "gla_bwd",
        compiler_params=_cparams(("arbitrary",)))(dos, proj, proj, proj, proj, st, wg, bg, *(prev or ()))


def _rnn_bwd(dhs, h, proj, cw, cb, wa, ba, wx, bx, lam, reverse):
    s = proj.shape[0]
    ts = _tile(s, TILE_S)
    nt = s // ts
    order = (lambda i: i) if reverse else (lambda i: nt - 1 - i)
    back = not reverse

    def body(dh_ref, h_ref, hh_ref, xr_ref, hp_ref, hn_ref, cw_ref, cb_ref, wa_ref, ba_ref, wx_ref, bx_ref, lam_ref,
             dxc_ref, dwa_ref, dba_ref, dwx_ref, dbx_ref, dlam_ref, a_scr, u_scr, g_scr, c_scr):
        i = pl.program_id(0)
        t = order(i)

        @pl.when(i == 0)
        def _():
            c_scr[...] = jnp.zeros_like(c_scr)
            dwa_ref[...] = jnp.zeros_like(dwa_ref)
            dba_ref[...] = jnp.zeros_like(dba_ref)
            dwx_ref[...] = jnp.zeros_like(dwx_ref)
            dbx_ref[...] = jnp.zeros_like(dbx_ref)
            dlam_ref[...] = jnp.zeros_like(dlam_ref)

        xc = _conv_fwd(xr_ref[...], hp_ref[...], hn_ref[...], cw_ref[...], cb_ref[...], t == 0, t == nt - 1)
        r, gi, sp, a, mult = _rnn_gates(xc, wa_ref[...], ba_ref[...], wx_ref[...], bx_ref[...], lam_ref[...])
        row = lax.broadcasted_iota(jnp.int32, (ts, D_RNN), 0)
        hv = h_ref[...]
        if reverse:
            edge = jnp.where(t == nt - 1, 0.0, hh_ref[0:1, :])
            h_prev = jnp.where(row == ts - 1, edge, pltpu.roll(hv, ts - 1, 0))
            a_nxt = jnp.where(row == 0, 1.0, pltpu.roll(a, 1, 0))
        else:
            edge = jnp.where(t == 0, 0.0, hh_ref[7:8, :])
            h_prev = jnp.where(row == 0, edge, pltpu.roll(hv, 1, 0))
            a_nxt = jnp.where(row == ts - 1, 1.0, pltpu.roll(a, ts - 1, 0))
        a_scr[...] = a_nxt
        u_scr[...] = dh_ref[...]
        _scan_tile(a_scr, u_scr, g_scr, c_scr[0:1, :], back)
        dh = g_scr[...]
        if reverse:
            c_scr[0:1, :] = a[ts - 1:ts, :] * dh[ts - 1:ts, :]
        else:
            c_scr[0:1, :] = a[0:1, :] * dh[0:1, :]
        dmult = dh * xc * gi
        dla = dh * h_prev * a - dmult * a * a / mult
        dza = dla * (-LRU_C) * sp * r * (1.0 - r)
        dzx = dh * xc * mult * gi * (1.0 - gi)
        dsp = jnp.sum(dla * (-LRU_C) * r, axis=0, keepdims=True)
        dlam_ref[...] += dsp * (-_sigmoid(-lam_ref[...]))
        dxc_ref[...] = dh * gi * mult + _mm_nt(dza, wa_ref[...]) + _mm_nt(dzx, wx_ref[...])
        dwa_ref[...] += _mm_tn(xc, dza)
        dwx_ref[...] += _mm_tn(xc, dzx)
        dba_ref[...] += jnp.sum(dza, axis=0, keepdims=True)
        dbx_ref[...] += jnp.sum(dzx, axis=0, keepdims=True)

    hp, hn = _halo_specs(s, ts, D_RNN, 0, order)
    hhp, hhn = _halo_specs(s, ts, D_RNN, 0, order)
    sq = (D_RNN, D_RNN)
    vec = (1, D_RNN)
    return pl.pallas_call(
        body, grid=(nt,),
        in_specs=[_rows(ts, D_RNN, 0, order), _rows(ts, D_RNN, 0, order), hhn if reverse else hhp,
                  _rows(ts, D_RNN, 0, order), hp, hn, _const((CONV_WIDTH, D_RNN)), _const(vec),
                  _const(sq), _const(vec), _const(sq), _const(vec), _const(vec)],
        out_specs=[_rows(ts, D_RNN, 0, order), _acc(sq), _acc(vec), _acc(sq), _acc(vec), _acc(vec)],
        out_shape=[_sds((s, D_RNN)), _sds(sq), _sds(vec), _sds(sq), _sds(vec), _sds(vec)],
        scratch_shapes=[pltpu.VMEM((ts, D_RNN), F32), pltpu.VMEM((ts, D_RNN), F32), pltpu.VMEM((ts, D_RNN), F32),
                        pltpu.VMEM((8, D_RNN), F32)],
        name="rnn_bwd_rev" if reverse else "rnn_bwd",
        compiler_params=_cparams(("arbitrary",)))(dhs, h, h, proj, proj, proj, cw, cb, wa, ba, wx, bx, lam)


def _conv_bwd(dxc_f, dxc_b, proj, cw):
    s = proj.shape[0]
    ts = _tile(s, TILE_S)
    nt = s // ts
    ident = lambda i: i

    def body(df_ref, dfp_ref, dfn_ref, db_ref, dbp_ref, dbn_ref, xr_ref, xp_ref, xn_ref, cw_ref,
             dxr_ref, dcw_ref, dcb_ref):
        t = pl.program_id(0)

        @pl.when(t == 0)
        def _():
            dcw_ref[...] = jnp.zeros_like(dcw_ref)
            dcb_ref[...] = jnp.zeros_like(dcb_ref)

        first = t == 0
        last = t == nt - 1
        d = df_ref[...] + db_ref[...]
        d_m2, d_m1, _, d_p1 = _conv_taps(d, dfp_ref[...] + dbp_ref[...], dfn_ref[...] + dbn_ref[...], first, last)
        dn = jnp.where(last, 0.0, dfn_ref[...] + dbn_ref[...])
        d_p2 = jnp.concatenate([d, dn], axis=0)[2:2 + ts]
        del d_m2
        cw = cw_ref[...]
        dxr_ref[...] = (cw[0:1] * d_p2 + cw[1:2] * d_p1 + cw[2:3] * d + cw[3:4] * d_m1).astype(dxr_ref.dtype)
        taps = _conv_taps(xr_ref[...], xp_ref[...], xn_ref[...], first, last)
        dcw_ref[...] += jnp.concatenate([jnp.sum(d * tp, axis=0, keepdims=True) for tp in taps], axis=0)
        dcb_ref[...] += jnp.sum(d, axis=0, keepdims=True)

    hp, hn = _halo_specs(s, ts, D_RNN, 0, ident)
    return pl.pallas_call(
        body, grid=(nt,),
        in_specs=[_rows(ts, D_RNN), hp, hn, _rows(ts, D_RNN), hp, hn, _rows(ts, D_RNN), hp, hn,
                  _const((CONV_WIDTH, D_RNN))],
        out_specs=[_rows(ts, D_RNN), _acc((CONV_WIDTH, D_RNN)), _acc((1, D_RNN))],
        out_shape=[_sds((s, D_RNN), MXU_DTYPE), _sds((CONV_WIDTH, D_RNN)), _sds((1, D_RNN))],
        name="conv_bwd", compiler_params=_cparams(("arbitrary",)))(
            dxc_f, dxc_f, dxc_f, dxc_b, dxc_b, dxc_b, proj, proj, proj, cw)


def _mix_in_bwd(parts, dlr, x, dx1, gpre, w_in_p):
    s = x.shape[0]
    ts = _tile(s, TILE_S)
    n_parts = len(parts)
    assert sum(p.shape[1] for p in parts) + LANES == PW

    def body(*refs):
        part_refs = refs[:n_parts + 1]
        x_ref, dx1_ref, g_ref, w_ref, dx_ref, dp_ref, h_ref, dgpre_ref = refs[n_parts + 1:]

        @pl.when(pl.program_id(0) == 0)
        def _():
            dgpre_ref[...] = jnp.zeros_like(dgpre_ref)

        dp = jnp.concatenate([r[...] for r in part_refs], axis=1)
        dp_ref[...] = dp
        h, n, rs = _rms_fwd(x_ref[...], g_ref[...])
        h_ref[...] = h.astype(MXU_DTYPE)
        dh = _mm_nt(dp, w_ref[...])
        dx, dgpre = _rms_bwd(dh, n, rs, g_ref[...])
        dgpre_ref[...] += dgpre
        dx_ref[...] = dx1_ref[...] + dx

    return pl.pallas_call(
        body, grid=(s // ts,),
        in_specs=[_rows(ts, p.shape[1]) for p in parts] + [_rows(ts, LANES), _rows(ts, D_MODEL), _rows(ts, D_MODEL),
                                                             _const((1, D_MODEL)), _const((D_MODEL, PW))],
        out_specs=[_rows(ts, D_MODEL), _rows(ts, PW), _rows(ts, D_MODEL), _acc((1, D_MODEL))],
        out_shape=[_sds((s, D_MODEL)), _sds((s, PW), MXU_DTYPE), _sds((s, D_MODEL), MXU_DTYPE), _sds((1, D_MODEL))],
        name="mix_in_bwd", compiler_params=_cparams(("arbitrary",)))(*parts, dlr, x, dx1, gpre, w_in_p)


W_IN_COLS = 2592
W_IN_SHARD = W_IN_COLS // N_DEV
FF_SHARD = D_FF // N_DEV


def _w_in_pieces():
    return [(j, 0, W_IN_SHARD, j * W_IN_SHARD) for j in range(N_DEV)]


def _w_in_from_shards(w):
    tr = 256

    def body(w_ref, o_ref):
        o_ref[...] = jnp.zeros_like(o_ref)
        for j, src, width, dst in _w_in_pieces():
            o_ref[:, dst:dst + width] = w_ref[j, :, src:src + width]

    return pl.pallas_call(
        body, grid=(D_MODEL // tr,),
        in_specs=[pl.BlockSpec((N_DEV, tr, W_IN_SHARD), lambda i: (0, i, 0))],
        out_specs=pl.BlockSpec((tr, PW), lambda i: (i, 0)), out_shape=_sds((D_MODEL, PW), w.dtype),
        name="w_in_from_shards", compiler_params=_cparams(("parallel",)))(w)


def _w_in_to_shards(g):
    tr = 256

    def body(g_ref, o_ref):
        for j, src, width, dst in _w_in_pieces():
            o_ref[j, :, src:src + width] = g_ref[:, dst:dst + width]

    return pl.pallas_call(
        body, grid=(D_MODEL // tr,),
        in_specs=[pl.BlockSpec((tr, PW), lambda i: (i, 0))],
        out_specs=pl.BlockSpec((N_DEV, tr, W_IN_SHARD), lambda i: (0, i, 0)),
        out_shape=_sds((N_DEV, D_MODEL, W_IN_SHARD), g.dtype),
        name="w_in_to_shards", compiler_params=_cparams(("parallel",)))(g)


def _cols_from_shards(w, name):
    _, d, c = w.shape
    tr = 256

    def body(w_ref, o_ref):
        for j in range(N_DEV):
            o_ref[:, j * c:(j + 1) * c] = w_ref[j]

    return pl.pallas_call(
        body, grid=(d // tr,),
        in_specs=[pl.BlockSpec((N_DEV, tr, c), lambda i: (0, i, 0))],
        out_specs=pl.BlockSpec((tr, N_DEV * c), lambda i: (i, 0)), out_shape=_sds((d, N_DEV * c), w.dtype),
        name=name, compiler_params=_cparams(("parallel",)))(w)


def _cols_to_shards(g, name):
    d, n = g.shape
    c = n // N_DEV
    tr = 256

    def body(g_ref, o_ref):
        for j in range(N_DEV):
            o_ref[j] = g_ref[:, j * c:(j + 1) * c]

    return pl.pallas_call(
        body, grid=(d // tr,),
        in_specs=[pl.BlockSpec((tr, n), lambda i: (i, 0))],
        out_specs=pl.BlockSpec((N_DEV, tr, c), lambda i: (0, i, 0)), out_shape=_sds((N_DEV, d, c), g.dtype),
        name=name, compiler_params=_cparams(("parallel",)))(g)


def _block_diag(w):
    n, b, _ = w.shape
    eye = jnp.eye(n, dtype=w.dtype)
    return (w[:, :, None, :] * eye[:, None, :, None]).reshape(n * b, n * b)


def _block_diag_of(w):
    n = D_RNN // 64
    eye = jnp.eye(n, dtype=w.dtype)
    return (w.reshape(n, 64, n, 64) * eye[:, None, :, None]).sum(axis=2)


def _gate_weight(wg, direction):
    lo = direction * GLA_RANK
    return jnp.pad(wg, ((lo, LANES - GLA_RANK - lo), (0, 0)))


def _layer_weights(full, big, l):
    row = lambda v: v.reshape(1, -1)
    lw = dict(
        gpre=row(full["mix_norm_pre"][l]), gpost=row(full["mix_norm_post"][l]),
        w_in=_w_in_from_shards(big["w_in"]),
        cw=full["conv_w"][l], cb=row(full["conv_b"][l]),
        g_rnn=row(full["rnn_out_norm"][l]), g_gla=row(full["gla_out_norm"][l]),
        w_out=big["w_out"].reshape(D_MODEL, D_MODEL),
        fpre=row(full["ffn_norm_pre"][l]), fpost=row(full["ffn_norm_post"][l]),
        wg=_cols_from_shards(big["w_ffn_gate"], "w_ffn_gate_from_shards"),
        wu=_cols_from_shards(big["w_ffn_up"], "w_ffn_up_from_shards"),
        wd=big["w_ffn_down"].reshape(D_FF, D_MODEL))
    for d in (0, 1):
        lw[f"wa{d}"] = _block_diag(full["lru_w_a"][l, d]).astype(MXU_DTYPE)
        lw[f"wx{d}"] = _block_diag(full["lru_w_x"][l, d]).astype(MXU_DTYPE)
        lw[f"ba{d}"] = row(full["lru_b_a"][l, d])
        lw[f"bx{d}"] = row(full["lru_b_x"][l, d])
        lw[f"lam{d}"] = row(full["lru_lambda"][l, d])
        lw[f"gw{d}"] = _gate_weight(full["gla_w_gate"][l, d], d).astype(MXU_DTYPE)
        lw[f"gb{d}"] = row(full["gla_b_gate"][l, d])
    return lw


def _layer_fwd(x, lw, rider=None):
    proj = _mix_in_fwd(x, lw["gpre"], lw["w_in"])
    hs, sts = [], []
    osum = None
    for d in (0, 1):
        hs.append(_rnn_fwd(proj, lw["cw"], lw["cb"], lw[f"wa{d}"], lw[f"ba{d}"], lw[f"wx{d}"], lw[f"bx{d}"],
                           lw[f"lam{d}"], bool(d)))
        osum, st = _gla_fwd(proj, lw[f"gw{d}"], lw[f"gb{d}"], bool(d), osum)
        sts.append(st)
    x1, y = _mix_out_fwd(x, hs[0], hs[1], osum, proj, lw["g_rnn"], lw["g_gla"], lw["w_out"], lw["gpost"])
    (x2, a, u, f), ridden = _ffn_fwd(x1, lw["fpre"], lw["wg"], lw["wu"], lw["wd"], lw["fpost"], rider)
    saved = dict(x=x, proj=proj, hs=hs, osum=osum, sts=sts, y=y, x1=x1, a=a, u=u, f=f)
    return x2, saved, ridden


def _layer_bwd(dx2, sv, lw, rider=None):
    g = {}
    (dx1, df, h2, p, da, du, dfpost, dfpre), ridden = _ffn_bwd(
        dx2, sv["f"], sv["x1"], sv["a"], sv["u"], lw["fpre"], lw["wg"], lw["wu"], lw["wd"], lw["fpost"], rider)
    g["ffn_norm_post"], g["ffn_norm_pre"] = dfpost[0], dfpre[0]
    big = {}
    big["w_ffn_gate"] = _cols_to_shards(_tn_matmul(h2, da, "dw_ffn_gate"), "dw_ffn_gate_to_shards")
    big["w_ffn_up"] = _cols_to_shards(_tn_matmul(h2, du, "dw_ffn_up"), "dw_ffn_up_to_shards")
    big["w_ffn_down"] = _tn_matmul(p, df, "dw_ffn_down").reshape(N_DEV, FF_SHARD, D_MODEL)
    proj = sv["proj"]
    dm, dhs, dgr, dos, dg, dgpost, dgrnn, dggla = _mix_out_bwd(
        dx1, sv["y"], sv["hs"][0], sv["hs"][1], sv["osum"], proj, lw["g_rnn"], lw["g_gla"], lw["w_out"], lw["gpost"])
    g["mix_norm_post"], g["rnn_out_norm"], g["gla_out_norm"] = dgpost[0], dgrnn[0], dggla[0]
    big["w_out"] = _tn_matmul(sv["y"], dm, "dw_out").reshape(N_DEV, D_MODEL // N_DEV, D_MODEL)
    dxc = []
    gla = None
    gw, gb, wa, ba, wx, bx, lam = [], [], [], [], [], [], []
    for d in (0, 1):
        r = _gla_bwd(dos, proj, sv["sts"][d], lw[f"gw{d}"], lw[f"gb{d}"], bool(d), gla)
        gla = r[:4]
        lo = d * GLA_RANK
        gw.append(r[4][lo:lo + GLA_RANK])
        gb.append(r[5][0])
        r = _rnn_bwd(dhs, sv["hs"][d], proj, lw["cw"], lw["cb"], lw[f"wa{d}"], lw[f"ba{d}"], lw[f"wx{d}"],
                     lw[f"bx{d}"], lw[f"lam{d}"], bool(d))
        dxc.append(r[0])
        wa.append(_block_diag_of(r[1])); ba.append(r[2][0]); wx.append(_block_diag_of(r[3])); bx.append(r[4][0])
        lam.append(r[5][0])
    g["gla_w_gate"], g["gla_b_gate"] = jnp.stack(gw), jnp.stack(gb)
    g["lru_w_a"], g["lru_b_a"] = jnp.stack(wa), jnp.stack(ba)
    g["lru_w_x"], g["lru_b_x"], g["lru_lambda"] = jnp.stack(wx), jnp.stack(bx), jnp.stack(lam)
    dxr, dcw, dcb = _conv_bwd(dxc[0], dxc[1], proj, lw["cw"])
    g["conv_w"], g["conv_b"] = dcw, dcb[0]
    dx, dproj, h, dgpre = _mix_in_bwd((dxr, dgr, gla[0], gla[1], gla[2], dg), gla[3], sv["x"], dx1, lw["gpre"],
                                      lw["w_in"])
    g["mix_norm_pre"] = dgpre[0]
    big["w_in"] = _w_in_to_shards(_tn_matmul(h, dproj, "dw_in", rows=1024, acc_bytes=14 * 1024 * 1024))
    return dx, g, [big[n] for n in BIG_WEIGHTS], ridden


WEIGHT_NAMES = ["mix_norm_pre", "mix_norm_post", "w_in", "conv_w", "conv_b", "lru_w_a", "lru_b_a", "lru_w_x", "lru_b_x",
                "lru_lambda", "rnn_out_norm", "gla_w_gate", "gla_b_gate", "gla_out_norm", "w_out", "ffn_norm_pre",
                "ffn_norm_post", "w_ffn_gate", "w_ffn_up", "w_ffn_down"]
BIG_WEIGHTS = ["w_in", "w_out", "w_ffn_gate", "w_ffn_up", "w_ffn_down"]


def _local_step(x, target, full, hooks):
    saved, lws = [], []
    for l in range(DEPTH):
        lws.append(_layer_weights(full, hooks.big_weights(l), l))
        x, sv, ridden = _layer_fwd(x, lws[l], hooks.fwd_rider(l))
        hooks.fwd_ridden(l, ridden)
        saved.append(sv)
    loss, dx = _loss_fwd_bwd(x, target)
    grads = [None] * DEPTH
    for l in reversed(range(DEPTH)):
        dx, grads[l], big, ridden = _layer_bwd(dx, saved[l], lws[l], hooks.bwd_rider(l))
        hooks.bwd_ridden(l, ridden)
        hooks.big_grads(l, big)
    g = {n: jnp.stack([grads[l][n] for l in range(DEPTH)]) for n in WEIGHT_NAMES if n not in BIG_WEIGHTS}
    return loss[0, 0], dx, g


def _all_gather(x, name):
    def body(x_ref, out_ref, send_sems, recv_sems, local_sem):
        mx, my, mc = lax.axis_index("x"), lax.axis_index("y"), lax.axis_index("c")
        me, sibling = (mx, my, mc), (mx, my, 1 - mc)
        chips = [(1 - mx, my), (mx, 1 - my), (1 - mx, 1 - my)]

        def slot(px, py, pc):
            return out_ref.at[4 * px + 2 * py + pc]

        def copy(k, block, to, src=None):
            return pltpu.make_async_remote_copy(
                src_ref=slot(*block) if src is None else src, dst_ref=slot(*block),
                send_sem=send_sems.at[k], recv_sem=recv_sems.at[k], device_id=to, device_id_type=MESH_ID)

        mine = pltpu.make_async_copy(x_ref, slot(*me), local_sem)
        mine.start()
        first = [copy(0, me, sibling, src=x_ref)]
        first += [copy(1 + j, me, (*chip, mc), src=x_ref) for j, chip in enumerate(chips)]
        for cp in first:
            cp.start()
        passed = [copy(4 + j, (*chip, mc), sibling) for j, chip in enumerate(chips)]
        for j, chip in enumerate(chips):
            copy(1 + j, (*chip, mc), me).wait_recv()
            passed[j].start()
        copy(0, sibling, me).wait_recv()
        for j, chip in enumerate(chips):
            copy(4 + j, (*chip, 1 - mc), me).wait_recv()
        for cp in first + passed:
            cp.wait_send()
        mine.wait()

    return pl.pallas_call(
        body, out_shape=_sds((N_DEV,) + x.shape, x.dtype), in_specs=[ANY], out_specs=ANY,
        scratch_shapes=[pltpu.SemaphoreType.DMA((7,)), pltpu.SemaphoreType.DMA((7,)), pltpu.SemaphoreType.DMA],
        name=name)(x)


def _all_to_all(g, name):
    def body(g_ref, out_ref, send_sems, recv_sems, local_sem):
        mx, my, mc = lax.axis_index("x"), lax.axis_index("y"), lax.axis_index("c")
        me = 4 * mx + 2 * my + mc
        mine = pltpu.make_async_copy(g_ref.at[me], out_ref.at[me], local_sem)
        mine.start()
        copies = []
        for r in range(1, N_DEV):
            px = 1 - mx if r & 4 else mx
            py = 1 - my if r & 2 else my
            pc = 1 - mc if r & 1 else mc
            cp = pltpu.make_async_remote_copy(
                src_ref=g_ref.at[4 * px + 2 * py + pc], dst_ref=out_ref.at[me],
                send_sem=send_sems.at[r - 1], recv_sem=recv_sems.at[r - 1],
                device_id=(px, py, pc), device_id_type=MESH_ID)
            cp.start()
            copies.append(cp)
        for cp in copies:
            cp.wait()
        mine.wait()

    return pl.pallas_call(
        body, out_shape=_sds(g.shape, g.dtype), in_specs=[ANY], out_specs=ANY,
        scratch_shapes=[pltpu.SemaphoreType.DMA((7,)), pltpu.SemaphoreType.DMA((7,)), pltpu.SemaphoreType.DMA],
        name=name)(g)


def _sum_adamw(parts, w, m, v, name):
    _, r, c = parts.shape
    tr = _tile(r, ADAM_TILE_ROWS)

    def body(p_ref, w_ref, m_ref, v_ref, g_ref, d_ref, m2_ref, v2_ref):
        g = p_ref[0]
        for k in range(1, N_DEV):
            g = g + p_ref[k]
        g_ref[...] = g
        m2 = ADAM_B1 * m_ref[...] + (1.0 - ADAM_B1) * g
        v2 = ADAM_B2 * v_ref[...] + (1.0 - ADAM_B2) * (g * g)
        m2_ref[...] = m2
        v2_ref[...] = v2
        m_hat = m2 / (1.0 - ADAM_B1 ** ADAM_STEP)
        v_hat = v2 / (1.0 - ADAM_B2 ** ADAM_STEP)
        d_ref[...] = -ADAM_LR * (m_hat / (jnp.sqrt(v_hat) + ADAM_EPS) + ADAM_WD * w_ref[...])

    flat = pl.BlockSpec((tr, c), lambda i: (i, 0))
    return pl.pallas_call(
        body, grid=(r // tr,),
        in_specs=[pl.BlockSpec((N_DEV, tr, c), lambda i: (0, i, 0)), flat, flat, flat],
        out_specs=[flat] * 4, out_shape=[_sds((r, c))] * 4,
        name=name, compiler_params=_cparams(("parallel",)))(parts, w, m, v)


def _gather_big_weights(shards):
    n = len(shards)

    def body(*refs):
        srcs, outs = refs[:n], refs[n:2 * n]
        send_sems, recv_sems, local_sems = refs[2 * n:]
        mx, my, mc = lax.axis_index("x"), lax.axis_index("y"), lax.axis_index("c")
        me, sibling = (mx, my, mc), (mx, my, 1 - mc)
        chips = [(1 - mx, my), (mx, 1 - my), (1 - mx, 1 - my)]

        def slot(a, px, py, pc):
            return outs[a].at[4 * px + 2 * py + pc]

        def copy(k, a, block, to, own=False):
            return pltpu.make_async_remote_copy(
                src_ref=srcs[a] if own else slot(a, *block), dst_ref=slot(a, *block),
                send_sem=send_sems.at[k * n + a], recv_sem=recv_sems.at[k * n + a],
                device_id=to, device_id_type=MESH_ID)

        mine = [pltpu.make_async_copy(srcs[a], slot(a, *me), local_sems.at[a]) for a in range(n)]
        for cp in mine:
            cp.start()
        first = [copy(0, a, me, sibling, own=True) for a in range(n)]
        first += [copy(1 + j, a, me, (*chip, mc), own=True) for j, chip in enumerate(chips) for a in range(n)]
        for cp in first:
            cp.start()
        passed = [[copy(4 + j, a, (*chip, mc), sibling) for a in range(n)] for j, chip in enumerate(chips)]
        for j, chip in enumerate(chips):
            for a in range(n):
                copy(1 + j, a, (*chip, mc), me).wait_recv()
                passed[j][a].start()
        for a in range(n):
            copy(0, a, sibling, me).wait_recv()
        for j, chip in enumerate(chips):
            for a in range(n):
                copy(4 + j, a, (*chip, 1 - mc), me).wait_recv()
        for cp in first + [cp for row in passed for cp in row]:
            cp.wait_send()
        for cp in mine:
            cp.wait()

    return pl.pallas_call(
        body, out_shape=[_sds((N_DEV,) + s.shape, s.dtype) for s in shards],
        in_specs=[ANY] * n, out_specs=[ANY] * n,
        scratch_shapes=[pltpu.SemaphoreType.DMA((7 * n,)), pltpu.SemaphoreType.DMA((7 * n,)),
                        pltpu.SemaphoreType.DMA((n,))],
        name="gather_matmul_weights")(*shards)


def _run_alone(rider, name):
    r_in = len(rider.operands)
    r_out = len(rider.out_shape)

    def body(*refs):
        local, remote = rider.copies(refs[:r_in], refs[r_in:r_in + r_out], *refs[r_in + r_out:])
        for cp in local + remote:
            cp.start()
        for cp in remote:
            cp.wait()
        for cp in local:
            cp.wait()

    n_remote = (N_DEV - 1) * rider.n
    return pl.pallas_call(
        body, out_shape=rider.out_shape, in_specs=[ANY] * r_in, out_specs=[ANY] * r_out,
        input_output_aliases=dict(rider.aliases),
        scratch_shapes=[pltpu.SemaphoreType.DMA((n_remote,)), pltpu.SemaphoreType.DMA((n_remote,)),
                        pltpu.SemaphoreType.DMA((rider.n,))],
        name=name)(*rider.operands)


def _sum_adamw_big(parts, w, m, v, name):
    _, nl, a, b = parts.shape
    ta = _tile(a, 256)

    def body(p_ref, w_ref, m_ref, v_ref, g_ref, d_ref, m2_ref, v2_ref):
        g = p_ref[0].astype(F32)
        for k in range(1, N_DEV):
            g = g + p_ref[k].astype(F32)
        g_ref[...] = g
        m2 = ADAM_B1 * m_ref[...] + (1.0 - ADAM_B1) * g
        v2 = ADAM_B2 * v_ref[...] + (1.0 - ADAM_B2) * (g * g)
        m2_ref[...] = m2
        v2_ref[...] = v2
        m_hat = m2 / (1.0 - ADAM_B1 ** ADAM_STEP)
        v_hat = v2 / (1.0 - ADAM_B2 ** ADAM_STEP)
        d_ref[...] = -ADAM_LR * (m_hat / (jnp.sqrt(v_hat) + ADAM_EPS) + ADAM_WD * w_ref[...])

    blk = pl.BlockSpec((None, ta, b), lambda l, i: (l, i, 0))
    return pl.pallas_call(
        body, grid=(nl, a // ta),
        in_specs=[pl.BlockSpec((N_DEV, None, ta, b), lambda l, i: (0, l, i, 0)), blk, blk, blk],
        out_specs=[blk] * 4, out_shape=[_sds((nl, a, b))] * 4,
        name=name, compiler_params=_cparams(("parallel", "parallel")))(parts, w, m, v)


SMALL_SHARDED = [("conv_w", 2), ("lru_b_a", 2), ("lru_b_x", 2), ("lru_lambda", 2), ("gla_w_gate", 3), ("gla_b_gate", 2)]
REPLICATED = ["mix_norm_pre", "mix_norm_post", "conv_b", "lru_w_a", "lru_w_x", "rnn_out_norm", "gla_out_norm",
              "ffn_norm_pre", "ffn_norm_post"]


def _pack(arrays, cols, row_mult):
    flat = jnp.concatenate([a.reshape(-1) for a in arrays])
    unit = cols * row_mult
    total = -(-flat.shape[0] // unit) * unit
    return jnp.pad(flat, (0, total - flat.shape[0])).reshape(total // cols, cols)


def _pack_slots(arrays, cols, row_mult):
    flat = jnp.concatenate([a.reshape(N_DEV, -1) for a in arrays], axis=1)
    unit = cols * row_mult
    total = -(-flat.shape[1] // unit) * unit
    return jnp.pad(flat, ((0, 0), (0, total - flat.shape[1]))).reshape(N_DEV, total // cols, cols)


def _unpack(flat, shapes):
    flat = flat.reshape(-1)
    out, off = [], 0
    for sh in shapes:
        n = 1
        for d in sh:
            n *= d
        out.append(flat[off:off + n].reshape(sh))
        off += n
    return out


def _unpack_slots(flat, shapes):
    flat = flat.reshape(N_DEV, -1)
    out, off = [], 0
    for sh in shapes:
        n = 1
        for d in sh:
            n *= d
        out.append(flat[:, off:off + n].reshape((N_DEV,) + tuple(sh)))
        off += n
    return out


def _merge_shards(a, axis):
    a = jnp.moveaxis(a, 0, axis)
    sh = a.shape
    return a.reshape(sh[:axis] + (sh[axis] * sh[axis + 1],) + sh[axis + 2:])


def _split_shards(a, axis):
    sh = a.shape
    a = a.reshape(sh[:axis] + (N_DEV, sh[axis] // N_DEV) + sh[axis + 1:])
    return jnp.moveaxis(a, axis, 0)


class _StepHooks:
    def __init__(self, shards, recvs):
        self.shards = shards
        self.recvs = recvs
        self.gathered = {0: _gather_big_weights(shards[0])}
        self.pending = None

    def big_weights(self, l):
        return dict(zip(BIG_WEIGHTS, self.gathered.pop(l)))

    def fwd_rider(self, l):
        return _WeightGather(self.shards[l + 1]) if l + 1 < DEPTH else None

    def fwd_ridden(self, l, outs):
        if outs:
            self.gathered[l + 1] = outs

    def bwd_rider(self, l):
        if self.pending is None:
            return None
        layer, arrays = self.pending
        self.pending = None
        return _GradExchange(arrays, self.recvs, layer)

    def bwd_ridden(self, l, outs):
        if outs:
            self.recvs = list(outs)

    def big_grads(self, l, arrays):
        self.pending = (l, arrays)

    def finish(self):
        layer, arrays = self.pending
        self.pending = None
        self.recvs = list(_run_alone(_GradExchange(arrays, self.recvs, layer), f"exchange_grads_layer{layer}"))
        return self.recvs


def kernel(x, mix_norm_pre, mix_norm_post, w_in, conv_w, conv_b, lru_w_a, lru_b_a, lru_w_x, lru_b_x, lru_lambda, rnn_out_norm, gla_w_gate, gla_b_gate, gla_out_norm, w_out, ffn_norm_pre, ffn_norm_post, w_ffn_gate, w_ffn_up, w_ffn_down, loss_target, m_mix_norm_pre, m_mix_norm_post, m_w_in, m_conv_w, m_conv_b, m_lru_w_a, m_lru_b_a, m_lru_w_x, m_lru_b_x, m_lru_lambda, m_rnn_out_norm, m_gla_w_gate, m_gla_b_gate, m_gla_out_norm, m_w_out, m_ffn_norm_pre, m_ffn_norm_post, m_w_ffn_gate, m_w_ffn_up, m_w_ffn_down, v_mix_norm_pre, v_mix_norm_post, v_w_in, v_conv_w, v_conv_b, v_lru_w_a, v_lru_b_a, v_lru_w_x, v_lru_b_x, v_lru_lambda, v_rnn_out_norm, v_gla_w_gate, v_gla_b_gate, v_gla_out_norm, v_w_out, v_ffn_norm_pre, v_ffn_norm_post, v_w_ffn_gate, v_w_ffn_up, v_w_ffn_down):
    args = dict(locals())
    w = {n: args[n] for n in WEIGHT_NAMES}
    m = {n: args["m_" + n] for n in WEIGHT_NAMES}
    v = {n: args["v_" + n] for n in WEIGHT_NAMES}
    names_s = [n for n, _ in SMALL_SHARDED]
    axis_s = dict(SMALL_SHARDED)
    shapes_s = [w[n].shape for n in names_s]

    small = _pack([w[n] for n in names_s], LANES, 8)
    small_all = _unpack_slots(_all_gather(small, "gather_small_weights"), shapes_s)
    full = {n: w[n] for n in REPLICATED}
    for n, a in zip(names_s, small_all):
        full[n] = _merge_shards(a, axis_s[n])

    hooks = _StepHooks([[w[n][l].astype(MXU_DTYPE) for n in BIG_WEIGHTS] for l in range(DEPTH)],
                       [jnp.zeros((N_DEV,) + w[n].shape, MXU_DTYPE) for n in BIG_WEIGHTS])
    loss, dx, g = _local_step(x[0], loss_target[0], full, hooks)
    loss = lax.psum(loss, MESH_AXES)
    res = {}
    for n, parts in zip(BIG_WEIGHTS, hooks.finish()):
        res[n] = _sum_adamw_big(parts, w[n], m[n], v[n], "adamw_" + n)

    g_slots = _pack_slots([_split_shards(g[n], axis_s[n]) for n in names_s], LANES, 8)
    g_recv = _all_to_all(g_slots, "exchange_small_grads")
    packed = [_pack([t[n] for n in names_s], LANES, 8) for t in (w, m, v)]
    res_s = [_unpack(r, shapes_s) for r in _sum_adamw(g_recv, *packed, "adamw_small")]
    for i, n in enumerate(names_s):
        res[n] = [res_s[k][i] for k in range(4)]

    shapes_r = [w[n].shape for n in REPLICATED]
    g_rep = _all_gather(_pack([g[n] for n in REPLICATED], ADAM_COLS, ADAM_TILE_ROWS), "gather_replicated_grads")
    packed = [_pack([t[n] for n in REPLICATED], ADAM_COLS, ADAM_TILE_ROWS) for t in (w, m, v)]
    res_r = [_unpack(r, shapes_r) for r in _sum_adamw(g_rep, *packed, "adamw_replicated")]
    for i, n in enumerate(REPLICATED):
        res[n] = [res_r[k][i] for k in range(4)]

    outs = [[res[n][k] for n in WEIGHT_NAMES] for k in range(4)]
    return (loss, dx[None], *outs[0], *outs[1], *outs[2], *outs[3])
```

```python
import functools

import jax
import jax.numpy as jnp
from jax import lax
from jax.experimental import pallas as pl
from jax.experimental.pallas import tpu as pltpu

F32 = jnp.float32
MXU_DTYPE = jnp.bfloat16

N_DEV = 8
D_MODEL = 1024
D_RNN = 512
CONV_WIDTH = 4
LRU_C = 8.0
GLA_HEADS = 4
GLA_DK = 64
GLA_PAIRS = 2
PAIR_K = 128
PAIR_V = 256
GLA_DV = 128
GLA_RANK = 16
GLA_TAU = 16.0
GLA_CHUNK = 64
D_FF = 2816
RMS_EPS = 1e-6
TINY = 1e-30
DEPTH = 4

PW = 2688
COL_Q, COL_K, COL_V, COL_G, COL_LR = 1024, 1280, 1536, 2048, 2560
QK_W = GLA_HEADS * GLA_DK
LANES = 128

TILE_S = 512
TILE_F = 256
TILE_TN = 2048
TN_ACC_BYTES = 6 * 1024 * 1024
F_CHUNK = 1408
VMEM_LIMIT = 56 * 1024 * 1024

ADAM_LR = 0.001
ADAM_B1 = 0.9
ADAM_B2 = 0.999
ADAM_EPS = 1e-08
ADAM_WD = 0.01
ADAM_STEP = 10

ADAM_TILE_ROWS = 256
ADAM_COLS = 1024


def _mm(a, b):
    return jnp.dot(a.astype(MXU_DTYPE), b.astype(MXU_DTYPE), preferred_element_type=F32)


def _mm_nt(a, b):
    return lax.dot_general(a.astype(MXU_DTYPE), b.astype(MXU_DTYPE), (((1,), (1,)), ((), ())),
                           preferred_element_type=F32)


def _mm_tn(a, b):
    return lax.dot_general(a.astype(MXU_DTYPE), b.astype(MXU_DTYPE), (((0,), (0,)), ((), ())),
                           preferred_element_type=F32)


def _bmm(a, b):
    return lax.dot_general(a.astype(MXU_DTYPE), b.astype(MXU_DTYPE), (((2,), (1,)), ((0,), (0,))),
                           preferred_element_type=F32)


def _bmm_nt(a, b):
    return lax.dot_general(a.astype(MXU_DTYPE), b.astype(MXU_DTYPE), (((2,), (2,)), ((0,), (0,))),
                           preferred_element_type=F32)


def _bmm_tn(a, b):
    return lax.dot_general(a.astype(MXU_DTYPE), b.astype(MXU_DTYPE), (((1,), (1,)), ((0,), (0,))),
                           preferred_element_type=F32)


def _bmm_tri(tri, x):
    t = jnp.broadcast_to(tri.astype(jnp.bfloat16)[None], (x.shape[0],) + tri.shape)
    hi = x.astype(jnp.bfloat16)
    r1 = x - hi.astype(F32)
    mid = r1.astype(jnp.bfloat16)
    lo = (r1 - mid.astype(F32)).astype(jnp.bfloat16)
    dot = lambda v: lax.dot_general(t, v, (((2,), (1,)), ((0,), (0,))), preferred_element_type=F32)
    return dot(hi) + dot(mid) + dot(lo)


def _sigmoid(x):
    return 0.5 * jnp.tanh(0.5 * x) + 0.5


def _log1p_pos(e):
    series = e * (1.0 - e * (0.5 - e * (1.0 / 3.0 - e * 0.25)))
    return jnp.where(e < 0.01, series, jnp.log(1.0 + e))


def _softplus(x):
    return jnp.maximum(x, 0.0) + _log1p_pos(jnp.exp(-jnp.abs(x)))


def _softplus_coarse(x):
    return jnp.maximum(x, 0.0) + jnp.log(1.0 + jnp.exp(-jnp.abs(x)))


GELU_C = 0.7978845608028654
GELU_K = 0.044715


def _gelu_and_grad(x):
    t = jnp.tanh(GELU_C * (x + GELU_K * x * x * x))
    y = 0.5 * x * (1.0 + t)
    dy = 0.5 * (1.0 + t) + 0.5 * x * (1.0 - t * t) * GELU_C * (1.0 + 3.0 * GELU_K * x * x)
    return y, dy


def _rms_fwd(x, g):
    rs = lax.rsqrt(jnp.mean(x * x, axis=-1, keepdims=True) + RMS_EPS)
    n = x * rs
    return n * g, n, rs


def _rms_bwd(dy, n, rs, g):
    dn = dy * g
    dx = rs * (dn - n * jnp.mean(dn * n, axis=-1, keepdims=True))
    dg = jnp.sum(dy * n, axis=0, keepdims=True)
    return dx, dg


def _cparams(sem=None):
    kw = dict(vmem_limit_bytes=VMEM_LIMIT)
    if sem is not None:
        kw["dimension_semantics"] = sem
    return pltpu.CompilerParams(**kw)


def _tile(n, pref):
    return pref if n % pref == 0 else n


def _const(shape):
    nd = len(shape)
    return pl.BlockSpec(shape, lambda *_: (0,) * nd, pipeline_mode=pl.Buffered(1))


def _acc(shape):
    nd = len(shape)
    return pl.BlockSpec(shape, lambda *_: (0,) * nd)


def _rows(ts, w, col=0, order=None):
    if order is None:
        return pl.BlockSpec((ts, w), lambda i: (i, col))
    return pl.BlockSpec((ts, w), lambda i: (order(i), col))


def _sds(shape, dtype=F32):
    return jax.ShapeDtypeStruct(shape, dtype)


MESH_ID = pl.DeviceIdType.MESH
ANY = pl.BlockSpec(memory_space=pl.ANY)
MESH_AXES = ("x", "y", "c")


def _peers():
    mx, my, mc = lax.axis_index("x"), lax.axis_index("y"), lax.axis_index("c")
    peers = []
    for r in range(1, N_DEV):
        px = 1 - mx if r & 4 else mx
        py = 1 - my if r & 2 else my
        pc = 1 - mc if r & 1 else mc
        peers.append((4 * px + 2 * py + pc, (px, py, pc)))
    return 4 * mx + 2 * my + mc, peers


class _GradExchange:
    def __init__(self, names, srcs, recvs, layers):
        self.names = list(names)
        self.n = len(srcs)
        self.layers = list(layers)
        self.operands = list(srcs) + list(recvs)
        self.out_shape = [_sds(r.shape, r.dtype) for r in recvs]
        self.aliases = {self.n + a: a for a in range(self.n)}

    def copies(self, ins, outs, send_sems, recv_sems, local_sems):
        n, layers = self.n, self.layers
        me, peers = _peers()
        local = [pltpu.make_async_copy(ins[a].at[me], outs[a].at[me, layers[a]], local_sems.at[a]) for a in range(n)]
        remote = [pltpu.make_async_remote_copy(
            src_ref=ins[a].at[slot], dst_ref=outs[a].at[me, layers[a]],
            send_sem=send_sems.at[r * n + a], recv_sem=recv_sems.at[r * n + a],
            device_id=dev, device_id_type=MESH_ID) for r, (slot, dev) in enumerate(peers) for a in range(n)]
        return local, remote


class _WeightGather:
    def __init__(self, shards):
        self.n = len(shards)
        self.operands = list(shards)
        self.out_shape = [_sds((N_DEV,) + s.shape, s.dtype) for s in shards]
        self.aliases = {}

    def copies(self, ins, outs, send_sems, recv_sems, local_sems):
        n = self.n
        me, peers = _peers()
        local = [pltpu.make_async_copy(ins[a], outs[a].at[me], local_sems.at[a]) for a in range(n)]
        remote = [pltpu.make_async_remote_copy(
            src_ref=ins[a], dst_ref=outs[a].at[me],
            send_sem=send_sems.at[r * n + a], recv_sem=recv_sems.at[r * n + a],
            device_id=dev, device_id_type=MESH_ID) for r, (_, dev) in enumerate(peers) for a in range(n)]
        return local, remote


def _call_with_rider(body, rider, operands, *, steps, in_specs, out_specs, out_shape, scratch_shapes=(), name,
                     semantics):
    if rider is None:
        outs = pl.pallas_call(body, grid=(steps,), in_specs=in_specs, out_specs=out_specs, out_shape=out_shape,
                              scratch_shapes=list(scratch_shapes), name=name,
                              compiler_params=_cparams((semantics,)))(*operands)
        return outs, []
    n_in, n_out, n_scr = len(in_specs), len(out_specs), len(scratch_shapes)
    r_in, r_out = len(rider.operands), len(rider.out_shape)

    def riding(*refs):
        own_in, ride_in = refs[:n_in], refs[n_in:n_in + r_in]
        refs = refs[n_in + r_in:]
        own_out, ride_out = refs[:n_out], refs[n_out:n_out + r_out]
        refs = refs[n_out + r_out:]
        own_scr, sems = refs[:n_scr], refs[n_scr:]
        i = pl.program_id(0)

        @pl.when(i == 0)
        def _():
            local, remote = rider.copies(ride_in, ride_out, *sems)
            for cp in local + remote:
                cp.start()

        body(*own_in, *own_out, *own_scr)

        @pl.when(i == steps - 1)
        def _():
            local, remote = rider.copies(ride_in, ride_out, *sems)
            for cp in remote:
                cp.wait()
            for cp in local:
                cp.wait()

    n_remote = (N_DEV - 1) * rider.n
    outs = pl.pallas_call(
        riding, grid=(steps,), in_specs=list(in_specs) + [ANY] * r_in, out_specs=list(out_specs) + [ANY] * r_out,
        out_shape=list(out_shape) + rider.out_shape,
        input_output_aliases={n_in + i: n_out + o for i, o in rider.aliases.items()},
        scratch_shapes=list(scratch_shapes) + [pltpu.SemaphoreType.DMA((n_remote,)),
                                               pltpu.SemaphoreType.DMA((n_remote,)),
                                               pltpu.SemaphoreType.DMA((rider.n,))],
        name=name, compiler_params=_cparams(("arbitrary",)))(*operands, *rider.operands)
    return outs[:n_out], outs[n_out:]


def _mix_in_fwd(x, gpre, w_in_p):
    s = x.shape[0]
    ts = _tile(s, TILE_S)

    def body(x_ref, g_ref, w_ref, o_ref):
        h, _, _ = _rms_fwd(x_ref[...], g_ref[...])
        o_ref[...] = _mm(h, w_ref[...])

    return pl.pallas_call(
        body, grid=(s // ts,),
        in_specs=[_rows(ts, D_MODEL), _const((1, D_MODEL)), _const((D_MODEL, PW))],
        out_specs=_rows(ts, PW), out_shape=_sds((s, PW)),
        name="mix_in_fwd", compiler_params=_cparams(("parallel",)))(x, gpre, w_in_p)


def _conv_taps(xr, hp, hn, first, last):
    ts = xr.shape[0]
    hp = jnp.where(first, 0.0, hp)
    hn = jnp.where(last, 0.0, hn)
    xe = jnp.concatenate([hp, xr, hn], axis=0)
    return xe[6:6 + ts], xe[7:7 + ts], xr, xe[9:9 + ts]


def _conv_fwd(xr, hp, hn, cw, cb, first, last):
    t0, t1, t2, t3 = _conv_taps(xr, hp, hn, first, last)
    return cw[0:1] * t0 + cw[1:2] * t1 + cw[2:3] * t2 + cw[3:4] * t3 + cb


def _rnn_gates(xc, wa, ba, wx, bx, lam):
    r = _sigmoid(_mm(xc, wa) + ba)
    i = _sigmoid(_mm(xc, wx) + bx)
    sp = _softplus(-lam)
    la = (-LRU_C) * r * sp
    a = jnp.exp(la)
    one_minus_a2 = -jnp.tanh(la) * (a * a + 1.0)
    inv_mult = lax.rsqrt(jnp.maximum(one_minus_a2, TINY))
    return r, i, sp, a, one_minus_a2 * inv_mult, inv_mult


def _scan_tile(a_scr, u_scr, h_ref, c0, reverse):
    ts = a_scr.shape[0]
    a = a_scr[...]
    u = u_scr[...]
    row = lax.broadcasted_iota(jnp.int32, a.shape, 0) % 8
    for k in (1, 2, 4):
        if reverse:
            a_sh = pltpu.roll(a, ts - k, 0)
            u_sh = pltpu.roll(u, ts - k, 0)
            ok = row < 8 - k
        else:
            a_sh = pltpu.roll(a, k, 0)
            u_sh = pltpu.roll(u, k, 0)
            ok = row >= k
        u = jnp.where(ok, u + a * u_sh, u)
        a = jnp.where(ok, a * a_sh, a)
    a_scr[...] = a
    u_scr[...] = u
    ng = ts // 8

    def body(j, c):
        g = (ng - 1 - j) if reverse else j
        sl = pl.ds(pl.multiple_of(g * 8, 8), 8)
        hh = u_scr[sl, :] + a_scr[sl, :] * c
        h_ref[sl, :] = hh
        return hh[0:1, :] if reverse else hh[7:8, :]

    return lax.fori_loop(0, ng, body, c0)


def _halo_specs(s, ts, w, col, order):
    n8 = s // 8
    per = ts // 8
    prev = pl.BlockSpec((8, w), lambda i: (jnp.maximum(order(i) * per - 1, 0), col))
    nxt = pl.BlockSpec((8, w), lambda i: (jnp.minimum((order(i) + 1) * per, n8 - 1), col))
    return prev, nxt


def _rnn_fwd(proj, cw, cb, wa, ba, wx, bx, lam, reverse):
    s = proj.shape[0]
    ts = _tile(s, TILE_S)
    nt = s // ts
    order = (lambda i: nt - 1 - i) if reverse else (lambda i: i)

    def body(xr_ref, hp_ref, hn_ref, cw_ref, cb_ref, wa_ref, ba_ref, wx_ref, bx_ref, lam_ref,
             h_ref, a_scr, u_scr, c_scr):
        i = pl.program_id(0)
        t = order(i)

        @pl.when(i == 0)
        def _():
            c_scr[...] = jnp.zeros_like(c_scr)

        xc = _conv_fwd(xr_ref[...], hp_ref[...], hn_ref[...], cw_ref[...], cb_ref[...], t == 0, t == nt - 1)
        _, gi, _, a, mult, _ = _rnn_gates(xc, wa_ref[...], ba_ref[...], wx_ref[...], bx_ref[...], lam_ref[...])
        a_scr[...] = a
        u_scr[...] = xc * gi * mult
        c_scr[0:1, :] = _scan_tile(a_scr, u_scr, h_ref, c_scr[0:1, :], reverse)

    hp, hn = _halo_specs(s, ts, D_RNN, 0, order)
    return pl.pallas_call(
        body, grid=(nt,),
        in_specs=[_rows(ts, D_RNN, 0, order), hp, hn, _const((CONV_WIDTH, D_RNN)), _const((1, D_RNN)),
                  _const((D_RNN, D_RNN)), _const((1, D_RNN)), _const((D_RNN, D_RNN)), _const((1, D_RNN)),
                  _const((1, D_RNN))],
        out_specs=_rows(ts, D_RNN, 0, order), out_shape=_sds((s, D_RNN)),
        scratch_shapes=[pltpu.VMEM((ts, D_RNN), F32), pltpu.VMEM((ts, D_RNN), F32), pltpu.VMEM((8, D_RNN), F32)],
        name="rnn_fwd_rev" if reverse else "rnn_fwd",
        compiler_params=_cparams(("arbitrary",)))(proj, proj, proj, cw, cb, wa, ba, wx, bx, lam)


def _tri(reverse, transpose=False):
    r = lax.broadcasted_iota(jnp.int32, (GLA_CHUNK, GLA_CHUNK), 0)
    c = lax.broadcasted_iota(jnp.int32, (GLA_CHUNK, GLA_CHUNK), 1)
    if transpose:
        r, c = c, r
    return ((r <= c) if reverse else (r >= c)).astype(F32)


def _gla_chunk_terms(q, k, la, tri, reverse):
    b = _bmm_tri(tri, la)
    bl = b[:, 0:1] if reverse else b[:, GLA_CHUNK - 1:GLA_CHUNK]
    eb = jnp.exp(b)
    enb = jnp.exp(-b)
    ebl = jnp.exp(bl - b)
    d = jnp.exp(bl)
    return eb, enb, ebl, d, q * (GLA_DK ** -0.5) * eb, k * enb, k * ebl


def _gla_gate(lr, wg, bg):
    z = _mm(lr, wg) + bg
    return z, -_softplus_coarse(-z) * (1.0 / GLA_TAU)


def _pair_masks():
    first_head = lax.broadcasted_iota(jnp.int32, (1, PAIR_K), 1) < GLA_DK
    row_first = lax.broadcasted_iota(jnp.int32, (PAIR_V, PAIR_K), 0) < GLA_DV
    lane_first = lax.broadcasted_iota(jnp.int32, (PAIR_V, PAIR_K), 1) < GLA_DK
    return first_head, row_first == lane_first


def _gla_specs(ts, order):
    return [_rows(ts, QK_W, COL_Q // QK_W, order), _rows(ts, QK_W, COL_K // QK_W, order),
            _rows(ts, GLA_HEADS * GLA_DV, COL_V // (GLA_HEADS * GLA_DV), order),
            _rows(ts, LANES, COL_LR // LANES, order)]


def _gla_fwd(proj, wg, bg, reverse, o_add=None):
    s = proj.shape[0]
    ts = _tile(s, TILE_S)
    nt = s // ts
    ch = ts // GLA_CHUNK
    vw = GLA_HEADS * GLA_DV
    order = (lambda i: nt - 1 - i) if reverse else (lambda i: i)
    extra = [] if o_add is None else [o_add]

    def body(q_ref, k_ref, v_ref, lr_ref, wg_ref, bg_ref, *rest):
        add_ref = None if o_add is None else rest[0]
        o_ref, st_ref, s_scr = rest[len(extra):]

        @pl.when(pl.program_id(0) == 0)
        def _():
            s_scr[...] = jnp.zeros_like(s_scr)

        _, la = _gla_gate(lr_ref[...], wg_ref[...], bg_ref[...])
        tri = _tri(reverse)
        keep = tri > 0.5
        first_head, own = _pair_masks()
        chunks = lambda a: a.reshape(ch, GLA_CHUNK, a.shape[-1])
        _, _, _, d, qe, ke, kd = _gla_chunk_terms(chunks(q_ref[...]), chunks(k_ref[...]), chunks(la), tri, reverse)
        v = chunks(v_ref[...])
        outs = []
        for p in range(GLA_PAIRS):
            ln = slice(p * PAIR_K, (p + 1) * PAIR_K)
            v_p = v[:, :, p * PAIR_V:(p + 1) * PAIR_V]
            qe_p, ke_p = qe[:, :, ln], ke[:, :, ln]
            grow = jnp.where(own, _bmm_tn(v_p, kd[:, :, ln]), 0.0)
            st = s_scr[p]
            for cc in range(ch):
                c = (ch - 1 - cc) if reverse else cc
                st_ref[c, p] = st
                st = d[c, :, ln] * st + grow[c]
            s_scr[p] = st
            intra = []
            for h in range(2):
                q_h = jnp.where(first_head if h == 0 else ~first_head, qe_p, 0.0)
                a_m = jnp.where(keep, _bmm_nt(q_h, ke_p), 0.0)
                intra.append(_bmm(a_m, v_p[:, :, h * GLA_DV:(h + 1) * GLA_DV]))
            outs.append(_bmm_nt(qe_p, st_ref[:, p]) + jnp.concatenate(intra, axis=2))
        o = jnp.concatenate(outs, axis=2).reshape(ts, vw)
        o_ref[...] = o if add_ref is None else o + add_ref[...]

    return pl.pallas_call(
        body, grid=(nt,),
        in_specs=_gla_specs(ts, order) + [_const((LANES, QK_W)), _const((1, QK_W))]
                 + [_rows(ts, vw, 0, order)] * len(extra),
        out_specs=[_rows(ts, vw, 0, order),
                   pl.BlockSpec((ch, GLA_PAIRS, PAIR_V, PAIR_K), lambda i: (order(i), 0, 0, 0))],
        out_shape=[_sds((s, vw)), _sds((s // GLA_CHUNK, GLA_PAIRS, PAIR_V, PAIR_K))],
        scratch_shapes=[pltpu.VMEM((GLA_PAIRS, PAIR_V, PAIR_K), F32)],
        name="gla_fwd_rev" if reverse else "gla_fwd",
        compiler_params=_cparams(("arbitrary",)))(proj, proj, proj, proj, wg, bg, *extra)


def _mix_out_terms(hf, hb, gate_r, osum, g, g_rnn, g_gla):
    hs = hf + hb
    gl, dgl = _gelu_and_grad(gate_r)
    z = hs * gl
    y_rnn, n_rnn, rs_rnn = _rms_fwd(z, g_rnn)
    sg_lin = _sigmoid(g)
    sg = g * sg_lin
    dsg = sg_lin * (1.0 + g * (1.0 - sg_lin))
    ons, ns, rss = [], [], []
    for h in range(GLA_HEADS):
        ln = slice(h * LANES, (h + 1) * LANES)
        on, n, rs = _rms_fwd(osum[:, ln], g_gla)
        ons.append(on)
        ns.append(n)
        rss.append(rs)
    on = jnp.concatenate(ons, axis=1)
    return hs, gl, dgl, y_rnn, n_rnn, rs_rnn, sg, dsg, on, ns, rss


def _mix_out_fwd(x, hf, hb, osum, proj, g_rnn, g_gla, w_out, gpost):
    s = x.shape[0]
    ts = _tile(s, TILE_S)

    def body(x_ref, hf_ref, hb_ref, gr_ref, os_ref, g_ref, grnn_ref, ggla_ref, w_ref, gp_ref, x1_ref, y_ref):
        _, _, _, y_rnn, _, _, sg, _, on, _, _ = _mix_out_terms(
            hf_ref[...], hb_ref[...], gr_ref[...], os_ref[...], g_ref[...], grnn_ref[...], ggla_ref[...])
        y = jnp.concatenate([y_rnn, on * sg], axis=1).astype(MXU_DTYPE)
        y_ref[...] = y
        out, _, _ = _rms_fwd(_mm(y, w_ref[...]), gp_ref[...])
        x1_ref[...] = x_ref[...] + out

    return pl.pallas_call(
        body, grid=(s // ts,),
        in_specs=[_rows(ts, D_MODEL), _rows(ts, D_RNN), _rows(ts, D_RNN), _rows(ts, D_RNN, 1), _rows(ts, 512),
                  _rows(ts, 512, COL_G // 512), _const((1, D_RNN)), _const((1, GLA_DV)),
                  _const((D_MODEL, D_MODEL)), _const((1, D_MODEL))],
        out_specs=[_rows(ts, D_MODEL), _rows(ts, D_MODEL)],
        out_shape=[_sds((s, D_MODEL)), _sds((s, D_MODEL), MXU_DTYPE)],
        name="mix_out_fwd", compiler_params=_cparams(("parallel",)))(
            x, hf, hb, proj, osum, proj, g_rnn, g_gla, w_out, gpost)


def _f_chunks():
    return [(c0, min(c0 + F_CHUNK, D_FF)) for c0 in range(0, D_FF, F_CHUNK)]


def _ffn_fwd(x1, gpre, wg, wu, wd, gpost, rider=None):
    s = x1.shape[0]
    ts = _tile(s, TILE_F)

    def body(x_ref, gpre_ref, wg_ref, wu_ref, wd_ref, gpost_ref, x2_ref, a_ref, u_ref, f_ref):
        x = x_ref[...]
        h, _, _ = _rms_fwd(x, gpre_ref[...])
        h = h.astype(MXU_DTYPE)
        f = jnp.zeros((ts, D_MODEL), F32)
        for c0, c1 in _f_chunks():
            a = _mm(h, wg_ref[:, c0:c1])
            u = _mm(h, wu_ref[:, c0:c1])
            a_ref[:, c0:c1] = a.astype(MXU_DTYPE)
            u_ref[:, c0:c1] = u.astype(MXU_DTYPE)
            f = f + _mm(a * _sigmoid(a) * u, wd_ref[c0:c1, :])
        f_ref[...] = f
        out, _, _ = _rms_fwd(f, gpost_ref[...])
        x2_ref[...] = x + out

    return _call_with_rider(
        body, rider, (x1, gpre, wg, wu, wd, gpost), steps=s // ts,
        in_specs=[_rows(ts, D_MODEL), _const((1, D_MODEL)), _const((D_MODEL, D_FF)), _const((D_MODEL, D_FF)),
                  _const((D_FF, D_MODEL)), _const((1, D_MODEL))],
        out_specs=[_rows(ts, D_MODEL), _rows(ts, D_FF), _rows(ts, D_FF), _rows(ts, D_MODEL)],
        out_shape=[_sds((s, D_MODEL)), _sds((s, D_FF), MXU_DTYPE), _sds((s, D_FF), MXU_DTYPE), _sds((s, D_MODEL))],
        name="ffn_fwd", semantics="parallel")


def _loss_fwd_bwd(y, target):
    s = y.shape[0]
    ts = _tile(s, TILE_S)

    def body(y_ref, t_ref, loss_ref, dy_ref):
        @pl.when(pl.program_id(0) == 0)
        def _():
            loss_ref[...] = jnp.zeros_like(loss_ref)

        e = y_ref[...] - t_ref[...]
        dy_ref[...] = e * (1.0 / D_MODEL)
        part = jnp.sum(jnp.sum(e * e, axis=1, keepdims=True), axis=0, keepdims=True) * (0.5 / D_MODEL)
        loss_ref[...] += jnp.broadcast_to(part, loss_ref.shape)

    return pl.pallas_call(
        body, grid=(s // ts,),
        in_specs=[_rows(ts, D_MODEL), _rows(ts, D_MODEL)],
        out_specs=[_acc((8, LANES)), _rows(ts, D_MODEL)],
        out_shape=[_sds((8, LANES)), _sds((s, D_MODEL))],
        name="loss", compiler_params=_cparams(("arbitrary",)))(y, target)


def _tn_matmul(a, b, name, rows=TILE_TN, acc_bytes=TN_ACC_BYTES):
    s, k = a.shape
    n = b.shape[1]
    ts = _tile(s, rows)
    tn = max(t for t in range(LANES, n + 1, LANES) if n % t == 0 and (k * t * 4 <= acc_bytes or t == LANES))
    ns = s // ts

    def body(a_ref, b_ref, o_ref, acc):
        i = pl.program_id(1)

        @pl.when(i == 0)
        def _():
            acc[...] = jnp.zeros_like(acc)

        acc[...] += _mm_tn(a_ref[...], b_ref[...])

        @pl.when(i == ns - 1)
        def _():
            o_ref[...] = acc[...].astype(o_ref.dtype)

    return pl.pallas_call(
        body, grid=(n // tn, ns),
        in_specs=[pl.BlockSpec((ts, k), lambda j, i: (i, 0)), pl.BlockSpec((ts, tn), lambda j, i: (i, j))],
        out_specs=pl.BlockSpec((k, tn), lambda j, i: (0, j)), out_shape=_sds((k, n), MXU_DTYPE),
        scratch_shapes=[pltpu.VMEM((k, tn), F32)],
        name=name, compiler_params=_cparams(("parallel", "arbitrary")))(a, b)


def _ffn_bwd(dx2, f, x1, a, u, gpre, wg, wu, wd, gpost, rider=None):
    s = x1.shape[0]
    ts = _tile(s, TILE_F)

    def body(dx2_ref, f_ref, x1_ref, a_ref, u_ref, gpre_ref, wg_ref, wu_ref, wd_ref, gpost_ref,
             dx1_ref, df_ref, h_ref, p_ref, da_ref, du_ref, dgpost_ref, dgpre_ref):
        @pl.when(pl.program_id(0) == 0)
        def _():
            dgpost_ref[...] = jnp.zeros_like(dgpost_ref)
            dgpre_ref[...] = jnp.zeros_like(dgpre_ref)

        dx2 = dx2_ref[...]
        _, nf, rsf = _rms_fwd(f_ref[...], gpost_ref[...])
        df, dgpost = _rms_bwd(dx2, nf, rsf, gpost_ref[...])
        dgpost_ref[...] += dgpost
        df = df.astype(MXU_DTYPE)
        df_ref[...] = df
        h, n1, rs1 = _rms_fwd(x1_ref[...], gpre_ref[...])
        h_ref[...] = h.astype(MXU_DTYPE)
        dh = jnp.zeros((ts, D_MODEL), F32)
        for c0, c1 in _f_chunks():
            av = a_ref[:, c0:c1].astype(F32)
            uv = u_ref[:, c0:c1].astype(F32)
            sg = _sigmoid(av)
            dp = _mm_nt(df, wd_ref[c0:c1, :])
            p_ref[:, c0:c1] = (av * sg * uv).astype(MXU_DTYPE)
            da = (dp * uv * sg * (1.0 + av * (1.0 - sg))).astype(MXU_DTYPE)
            du = (dp * av * sg).astype(MXU_DTYPE)
            da_ref[:, c0:c1] = da
            du_ref[:, c0:c1] = du
            dh = dh + _mm_nt(da, wg_ref[:, c0:c1]) + _mm_nt(du, wu_ref[:, c0:c1])
        dx, dgpre = _rms_bwd(dh, n1, rs1, gpre_ref[...])
        dgpre_ref[...] += dgpre
        dx1_ref[...] = dx2 + dx

    return _call_with_rider(
        body, rider, (dx2, f, x1, a, u, gpre, wg, wu, wd, gpost), steps=s // ts,
        in_specs=[_rows(ts, D_MODEL), _rows(ts, D_MODEL), _rows(ts, D_MODEL), _rows(ts, D_FF), _rows(ts, D_FF),
                  _const((1, D_MODEL)), _const((D_MODEL, D_FF)), _const((D_MODEL, D_FF)), _const((D_FF, D_MODEL)),
                  _const((1, D_MODEL))],
        out_specs=[_rows(ts, D_MODEL), _rows(ts, D_MODEL), _rows(ts, D_MODEL), _rows(ts, D_FF), _rows(ts, D_FF),
                   _rows(ts, D_FF), _acc((1, D_MODEL)), _acc((1, D_MODEL))],
        out_shape=[_sds((s, D_MODEL)), _sds((s, D_MODEL), MXU_DTYPE), _sds((s, D_MODEL), MXU_DTYPE),
                   _sds((s, D_FF), MXU_DTYPE), _sds((s, D_FF), MXU_DTYPE), _sds((s, D_FF), MXU_DTYPE),
                   _sds((1, D_MODEL)), _sds((1, D_MODEL))],
        name="ffn_bwd", semantics="arbitrary")


def _mix_out_bwd(dx1, y, hf, hb, osum, proj, g_rnn, g_gla, w_out, gpost):
    s = y.shape[0]
    ts = _tile(s, TILE_S)

    def body(dx1_ref, y_ref, hf_ref, hb_ref, gr_ref, os_ref, g_ref, grnn_ref, ggla_ref, w_ref, gp_ref,
             dm_ref, dhs_ref, dgr_ref, dos_ref, dg_ref, dgpost_ref, dgrnn_ref, dggla_ref):
        @pl.when(pl.program_id(0) == 0)
        def _():
            dgpost_ref[...] = jnp.zeros_like(dgpost_ref)
            dgrnn_ref[...] = jnp.zeros_like(dgrnn_ref)
            dggla_ref[...] = jnp.zeros_like(dggla_ref)

        _, nm, rsm = _rms_fwd(_mm(y_ref[...], w_ref[...]), gp_ref[...])
        dm, dgpost = _rms_bwd(dx1_ref[...], nm, rsm, gp_ref[...])
        dgpost_ref[...] += dgpost
        dm = dm.astype(MXU_DTYPE)
        dm_ref[...] = dm
        dy = _mm_nt(dm, w_ref[...])
        hs, gl, dgl, _, n_rnn, rs_rnn, sg, dsg, on, ns, rss = _mix_out_terms(
            hf_ref[...], hb_ref[...], gr_ref[...], os_ref[...], g_ref[...], grnn_ref[...], ggla_ref[...])
        dz, dgrnn = _rms_bwd(dy[:, :D_RNN], n_rnn, rs_rnn, grnn_ref[...])
        dgrnn_ref[...] += dgrnn
        dhs_ref[...] = dz * gl
        dgr_ref[...] = (dz * hs * dgl).astype(MXU_DTYPE)
        dyg = dy[:, D_RNN:]
        dg_ref[...] = (dyg * on * dsg).astype(MXU_DTYPE)
        don = dyg * sg
        dggla = jnp.zeros((1, GLA_DV), F32)
        for h in range(GLA_HEADS):
            ln = slice(h * LANES, (h + 1) * LANES)
            dos, dgh = _rms_bwd(don[:, ln], ns[h], rss[h], ggla_ref[...])
            dos_ref[:, ln] = dos.astype(MXU_DTYPE)
            dggla = dggla + dgh
        dggla_ref[...] += dggla

    return pl.pallas_call(
        body, grid=(s // ts,),
        in_specs=[_rows(ts, D_MODEL), _rows(ts, D_MODEL), _rows(ts, D_RNN), _rows(ts, D_RNN), _rows(ts, D_RNN, 1),
                  _rows(ts, 512), _rows(ts, 512, COL_G // 512), _const((1, D_RNN)), _const((1, GLA_DV)),
                  _const((D_MODEL, D_MODEL)), _const((1, D_MODEL))],
        out_specs=[_rows(ts, D_MODEL), _rows(ts, D_RNN), _rows(ts, D_RNN), _rows(ts, 512), _rows(ts, 512),
                   _acc((1, D_MODEL)), _acc((1, D_RNN)), _acc((1, GLA_DV))],
        out_shape=[_sds((s, D_MODEL), MXU_DTYPE), _sds((s, D_RNN)), _sds((s, D_RNN), MXU_DTYPE),
                   _sds((s, 512), MXU_DTYPE), _sds((s, 512), MXU_DTYPE),
                   _sds((1, D_MODEL)), _sds((1, D_RNN)), _sds((1, GLA_DV))],
        name="mix_out_bwd", compiler_params=_cparams(("arbitrary",)))(
            dx1, y, hf, hb, proj, osum, proj, g_rnn, g_gla, w_out, gpost)


def _gla_bwd(dos, proj, st, wg, bg, reverse, prev=None):
    s = proj.shape[0]
    ts = _tile(s, TILE_S)
    nt = s // ts
    ch = ts // GLA_CHUNK
    vw = GLA_HEADS * GLA_DV
    order = (lambda i: i) if reverse else (lambda i: nt - 1 - i)

    n_prev = 0 if prev is None else 4

    def body(do_ref, q_ref, k_ref, v_ref, lr_ref, st_ref, wg_ref, bg_ref, *rest):
        pq_ref, pk_ref, pv_ref, plr_ref = rest[:n_prev] if n_prev else (None,) * 4
        dq_ref, dk_ref, dv_ref, dlr_ref, dwg_ref, dbg_ref, ds_scr, dsa_scr = rest[n_prev:]

        def put(ref, p_ref, val):
            if p_ref is not None:
                val = val + p_ref[...].astype(F32)
            ref[...] = val.astype(ref.dtype)

        @pl.when(pl.program_id(0) == 0)
        def _():
            ds_scr[...] = jnp.zeros_like(ds_scr)
            dwg_ref[...] = jnp.zeros_like(dwg_ref)
            dbg_ref[...] = jnp.zeros_like(dbg_ref)

        z, la = _gla_gate(lr_ref[...], wg_ref[...], bg_ref[...])
        tri = _tri(reverse)
        tri_t = _tri(reverse, transpose=True)
        keep = tri > 0.5
        last_row = 0 if reverse else GLA_CHUNK - 1
        is_last = lax.broadcasted_iota(jnp.int32, (ch, GLA_CHUNK, QK_W), 1) == last_row
        first_head, own = _pair_masks()
        chunks = lambda a: a.reshape(ch, GLA_CHUNK, a.shape[-1])
        eb, enb, ebl, d, qe, ke, kd = _gla_chunk_terms(chunks(q_ref[...]), chunks(k_ref[...]), chunks(la), tri, reverse)
        v = chunks(v_ref[...])
        do = chunks(do_ref[...])
        dqe, dke, dkd, dd, dv = [], [], [], [], []
        for p in range(GLA_PAIRS):
            ln = slice(p * PAIR_K, (p + 1) * PAIR_K)
            lv = slice(p * PAIR_V, (p + 1) * PAIR_V)
            v_p, do_p = v[:, :, lv], do[:, :, lv]
            qe_p, ke_p, kd_p = qe[:, :, ln], ke[:, :, ln], kd[:, :, ln]
            grow = jnp.where(own, _bmm_tn(do_p, qe_p), 0.0)
            dst = ds_scr[p]
            for cc in range(ch):
                c = cc if reverse else (ch - 1 - cc)
                dsa_scr[c, p] = dst
                dst = grow[c] + d[c, :, ln] * dst
            ds_scr[p] = dst
            st_p = st_ref[:, p]
            dst_p = dsa_scr[:, p]
            dv_intra, dqe_intra, dke_intra = [], [], None
            for h in range(2):
                mine = first_head if h == 0 else ~first_head
                hv = slice(h * GLA_DV, (h + 1) * GLA_DV)
                q_h = jnp.where(mine, qe_p, 0.0)
                a_m = jnp.where(keep, _bmm_nt(q_h, ke_p), 0.0)
                da_m = jnp.where(keep, _bmm_nt(do_p[:, :, hv], v_p[:, :, hv]), 0.0)
                dv_intra.append(_bmm_tn(a_m, do_p[:, :, hv]))
                dqe_intra.append(_bmm(da_m, ke_p))
                dk_h = _bmm_tn(da_m, q_h)
                dke_intra = dk_h if dke_intra is None else dke_intra + dk_h
            dv.append(jnp.concatenate(dv_intra, axis=2) + _bmm_nt(kd_p, dst_p))
            dqe.append(jnp.where(first_head, dqe_intra[0], dqe_intra[1]) + _bmm(do_p, st_p))
            dke.append(dke_intra)
            dkd.append(_bmm(v_p, dst_p))
            dd.append(jnp.sum(dst_p * st_p, axis=1, keepdims=True))
        dqe = jnp.concatenate(dqe, axis=2)
        dke = jnp.concatenate(dke, axis=2)
        dkd = jnp.concatenate(dkd, axis=2)
        dd = jnp.concatenate(dd, axis=2)
        dbl = dd * d + jnp.sum(dkd * kd, axis=1, keepdims=True)
        db = dqe * qe - dke * ke - dkd * kd
        db = jnp.where(is_last, db + dbl, db)
        put(dv_ref, pv_ref, jnp.concatenate(dv, axis=2).reshape(ts, vw))
        put(dq_ref, pq_ref, (dqe * eb * (GLA_DK ** -0.5)).reshape(ts, QK_W))
        put(dk_ref, pk_ref, (dke * enb + dkd * ebl).reshape(ts, QK_W))
        dz = (_bmm_tri(tri_t, db) * (1.0 / GLA_TAU)).reshape(ts, QK_W) * _sigmoid(-z)
        put(dlr_ref, plr_ref, _mm_nt(dz, wg_ref[...]))
        dwg_ref[...] += _mm_tn(lr_ref[...], dz)
        dbg_ref[...] += jnp.sum(dz, axis=0, keepdims=True)

    wide, mid, narrow = _rows(ts, vw, 0, order), _rows(ts, QK_W, 0, order), _rows(ts, LANES, 0, order)
    return pl.pallas_call(
        body, grid=(nt,),
        in_specs=[wide] + _gla_specs(ts, order)
                 + [pl.BlockSpec((ch, GLA_PAIRS, PAIR_V, PAIR_K), lambda i: (order(i), 0, 0, 0)),
                    _const((LANES, QK_W)), _const((1, QK_W))] + ([mid, mid, wide, narrow] if n_prev else []),
        out_specs=[mid, mid, wide, narrow, _acc((LANES, QK_W)), _acc((1, QK_W))],
        out_shape=[_sds((s, QK_W), MXU_DTYPE), _sds((s, QK_W), MXU_DTYPE), _sds((s, vw), MXU_DTYPE),
                   _sds((s, LANES), MXU_DTYPE), _sds((LANES, QK_W)), _sds((1, QK_W))],
        scratch_shapes=[pltpu.VMEM((GLA_PAIRS, PAIR_V, PAIR_K), F32),
                        pltpu.VMEM((ch, GLA_PAIRS, PAIR_V, PAIR_K), F32)],
        name="gla_bwd_rev" if reverse else "gla_bwd",
        compiler_params=_cparams(("arbitrary",)))(dos, proj, proj, proj, proj, st, wg, bg, *(prev or ()))


def _rnn_bwd(dhs, h, proj, cw, cb, wa, ba, wx, bx, lam, reverse, rider=None):
    s = proj.shape[0]
    ts = _tile(s, TILE_S)
    nt = s // ts
    order = (lambda i: i) if reverse else (lambda i: nt - 1 - i)
    back = not reverse

    def body(dh_ref, h_ref, hh_ref, xr_ref, hp_ref, hn_ref, cw_ref, cb_ref, wa_ref, ba_ref, wx_ref, bx_ref, lam_ref,
             dxc_ref, dwa_ref, dba_ref, dwx_ref, dbx_ref, dlam_ref, a_scr, u_scr, g_scr, c_scr):
        i = pl.program_id(0)
        t = order(i)

        @pl.when(i == 0)
        def _():
            c_scr[...] = jnp.zeros_like(c_scr)
            dwa_ref[...] = jnp.zeros_like(dwa_ref)
            dba_ref[...] = jnp.zeros_like(dba_ref)
            dwx_ref[...] = jnp.zeros_like(dwx_ref)
            dbx_ref[...] = jnp.zeros_like(dbx_ref)
            dlam_ref[...] = jnp.zeros_like(dlam_ref)

        xc = _conv_fwd(xr_ref[...], hp_ref[...], hn_ref[...], cw_ref[...], cb_ref[...], t == 0, t == nt - 1)
        r, gi, sp, a, mult, inv_mult = _rnn_gates(xc, wa_ref[...], ba_ref[...], wx_ref[...], bx_ref[...], lam_ref[...])
        row = lax.broadcasted_iota(jnp.int32, (ts, D_RNN), 0)
        hv = h_ref[...]
        if reverse:
            edge = jnp.where(t == nt - 1, 0.0, hh_ref[0:1, :])
            h_prev = jnp.where(row == ts - 1, edge, pltpu.roll(hv, ts - 1, 0))
            a_nxt = jnp.where(row == 0, 1.0, pltpu.roll(a, 1, 0))
        else:
            edge = jnp.where(t == 0, 0.0, hh_ref[7:8, :])
            h_prev = jnp.where(row == 0, edge, pltpu.roll(hv, 1, 0))
            a_nxt = jnp.where(row == ts - 1, 1.0, pltpu.roll(a, ts - 1, 0))
        a_scr[...] = a_nxt
        u_scr[...] = dh_ref[...]
        _scan_tile(a_scr, u_scr, g_scr, c_scr[0:1, :], back)
        dh = g_scr[...]
        if reverse:
            c_scr[0:1, :] = a[ts - 1:ts, :] * dh[ts - 1:ts, :]
        else:
            c_scr[0:1, :] = a[0:1, :] * dh[0:1, :]
        dmult = dh * xc * gi
        dla = dh * h_prev * a - dmult * a * a * inv_mult
        dza = dla * (-LRU_C) * sp * r * (1.0 - r)
        dzx = dh * xc * mult * gi * (1.0 - gi)
        dsp = jnp.sum(dla * (-LRU_C) * r, axis=0, keepdims=True)
        dlam_ref[...] += dsp * (-_sigmoid(-lam_ref[...]))
        dxc_ref[...] = dh * gi * mult + _mm_nt(dza, wa_ref[...]) + _mm_nt(dzx, wx_ref[...])
        dwa_ref[...] += _mm_tn(xc, dza)
        dwx_ref[...] += _mm_tn(xc, dzx)
        dba_ref[...] += jnp.sum(dza, axis=0, keepdims=True)
        dbx_ref[...] += jnp.sum(dzx, axis=0, keepdims=True)

    hp, hn = _halo_specs(s, ts, D_RNN, 0, order)
    hhp, hhn = _halo_specs(s, ts, D_RNN, 0, order)
    sq = (D_RNN, D_RNN)
    vec = (1, D_RNN)
    return _call_with_rider(
        body, rider, (dhs, h, h, proj, proj, proj, cw, cb, wa, ba, wx, bx, lam), steps=nt,
        in_specs=[_rows(ts, D_RNN, 0, order), _rows(ts, D_RNN, 0, order), hhn if reverse else hhp,
                  _rows(ts, D_RNN, 0, order), hp, hn, _const((CONV_WIDTH, D_RNN)), _const(vec),
                  _const(sq), _const(vec), _const(sq), _const(vec), _const(vec)],
        out_specs=[_rows(ts, D_RNN, 0, order), _acc(sq), _acc(vec), _acc(sq), _acc(vec), _acc(vec)],
        out_shape=[_sds((s, D_RNN)), _sds(sq), _sds(vec), _sds(sq), _sds(vec), _sds(vec)],
        scratch_shapes=[pltpu.VMEM((ts, D_RNN), F32), pltpu.VMEM((ts, D_RNN), F32), pltpu.VMEM((ts, D_RNN), F32),
                        pltpu.VMEM((8, D_RNN), F32)],
        name="rnn_bwd_rev" if reverse else "rnn_bwd", semantics="arbitrary")


def _conv_bwd(dxc_f, dxc_b, proj, cw):
    s = proj.shape[0]
    ts = _tile(s, TILE_S)
    nt = s // ts
    ident = lambda i: i

    def body(df_ref, dfp_ref, dfn_ref, db_ref, dbp_ref, dbn_ref, xr_ref, xp_ref, xn_ref, cw_ref,
             dxr_ref, dcw_ref, dcb_ref):
        t = pl.program_id(0)

        @pl.when(t == 0)
        def _():
            dcw_ref[...] = jnp.zeros_like(dcw_ref)
            dcb_ref[...] = jnp.zeros_like(dcb_ref)

        first = t == 0
        last = t == nt - 1
        d = df_ref[...] + db_ref[...]
        d_m2, d_m1, _, d_p1 = _conv_taps(d, dfp_ref[...] + dbp_ref[...], dfn_ref[...] + dbn_ref[...], first, last)
        dn = jnp.where(last, 0.0, dfn_ref[...] + dbn_ref[...])
        d_p2 = jnp.concatenate([d, dn], axis=0)[2:2 + ts]
        del d_m2
        cw = cw_ref[...]
        dxr_ref[...] = (cw[0:1] * d_p2 + cw[1:2] * d_p1 + cw[2:3] * d + cw[3:4] * d_m1).astype(dxr_ref.dtype)
        taps = _conv_taps(xr_ref[...], xp_ref[...], xn_ref[...], first, last)
        dcw_ref[...] += jnp.concatenate([jnp.sum(d * tp, axis=0, keepdims=True) for tp in taps], axis=0)
        dcb_ref[...] += jnp.sum(d, axis=0, keepdims=True)

    hp, hn = _halo_specs(s, ts, D_RNN, 0, ident)
    return pl.pallas_call(
        body, grid=(nt,),
        in_specs=[_rows(ts, D_RNN), hp, hn, _rows(ts, D_RNN), hp, hn, _rows(ts, D_RNN), hp, hn,
                  _const((CONV_WIDTH, D_RNN))],
        out_specs=[_rows(ts, D_RNN), _acc((CONV_WIDTH, D_RNN)), _acc((1, D_RNN))],
        out_shape=[_sds((s, D_RNN), MXU_DTYPE), _sds((CONV_WIDTH, D_RNN)), _sds((1, D_RNN))],
        name="conv_bwd", compiler_params=_cparams(("arbitrary",)))(
            dxc_f, dxc_f, dxc_f, dxc_b, dxc_b, dxc_b, proj, proj, proj, cw)


def _mix_in_bwd(parts, dlr, x, dx1, gpre, w_in_p):
    s = x.shape[0]
    ts = _tile(s, TILE_S)
    n_parts = len(parts)
    assert sum(p.shape[1] for p in parts) + LANES == PW

    def body(*refs):
        part_refs = refs[:n_parts + 1]
        x_ref, dx1_ref, g_ref, w_ref, dx_ref, dp_ref, h_ref, dgpre_ref = refs[n_parts + 1:]

        @pl.when(pl.program_id(0) == 0)
        def _():
            dgpre_ref[...] = jnp.zeros_like(dgpre_ref)

        dp = jnp.concatenate([r[...] for r in part_refs], axis=1)
        dp_ref[...] = dp
        h, n, rs = _rms_fwd(x_ref[...], g_ref[...])
        h_ref[...] = h.astype(MXU_DTYPE)
        dh = _mm_nt(dp, w_ref[...])
        dx, dgpre = _rms_bwd(dh, n, rs, g_ref[...])
        dgpre_ref[...] += dgpre
        dx_ref[...] = dx1_ref[...] + dx

    return pl.pallas_call(
        body, grid=(s // ts,),
        in_specs=[_rows(ts, p.shape[1]) for p in parts] + [_rows(ts, LANES), _rows(ts, D_MODEL), _rows(ts, D_MODEL),
                                                             _const((1, D_MODEL)), _const((D_MODEL, PW))],
        out_specs=[_rows(ts, D_MODEL), _rows(ts, PW), _rows(ts, D_MODEL), _acc((1, D_MODEL))],
        out_shape=[_sds((s, D_MODEL)), _sds((s, PW), MXU_DTYPE), _sds((s, D_MODEL), MXU_DTYPE), _sds((1, D_MODEL))],
        name="mix_in_bwd", compiler_params=_cparams(("arbitrary",)))(*parts, dlr, x, dx1, gpre, w_in_p)


W_IN_COLS = 2592
W_IN_SHARD = W_IN_COLS // N_DEV
FF_SHARD = D_FF // N_DEV


def _w_in_pieces():
    return [(j, 0, W_IN_SHARD, j * W_IN_SHARD) for j in range(N_DEV)]


def _w_in_from_shards(w):
    tr = 256

    def body(w_ref, o_ref):
        o_ref[...] = jnp.zeros_like(o_ref)
        for j, src, width, dst in _w_in_pieces():
            o_ref[:, dst:dst + width] = w_ref[j, :, src:src + width]

    return pl.pallas_call(
        body, grid=(D_MODEL // tr,),
        in_specs=[pl.BlockSpec((N_DEV, tr, W_IN_SHARD), lambda i: (0, i, 0))],
        out_specs=pl.BlockSpec((tr, PW), lambda i: (i, 0)), out_shape=_sds((D_MODEL, PW), w.dtype),
        name="w_in_from_shards", compiler_params=_cparams(("parallel",)))(w)


def _w_in_to_shards(g):
    tr = 256

    def body(g_ref, o_ref):
        for j, src, width, dst in _w_in_pieces():
            o_ref[j, :, src:src + width] = g_ref[:, dst:dst + width]

    return pl.pallas_call(
        body, grid=(D_MODEL // tr,),
        in_specs=[pl.BlockSpec((tr, PW), lambda i: (i, 0))],
        out_specs=pl.BlockSpec((N_DEV, tr, W_IN_SHARD), lambda i: (0, i, 0)),
        out_shape=_sds((N_DEV, D_MODEL, W_IN_SHARD), g.dtype),
        name="w_in_to_shards", compiler_params=_cparams(("parallel",)))(g)


def _cols_from_shards(w, name):
    _, d, c = w.shape
    tr = 256

    def body(w_ref, o_ref):
        for j in range(N_DEV):
            o_ref[:, j * c:(j + 1) * c] = w_ref[j]

    return pl.pallas_call(
        body, grid=(d // tr,),
        in_specs=[pl.BlockSpec((N_DEV, tr, c), lambda i: (0, i, 0))],
        out_specs=pl.BlockSpec((tr, N_DEV * c), lambda i: (i, 0)), out_shape=_sds((d, N_DEV * c), w.dtype),
        name=name, compiler_params=_cparams(("parallel",)))(w)


def _cols_to_shards(g, name):
    d, n = g.shape
    c = n // N_DEV
    tr = 256

    def body(g_ref, o_ref):
        for j in range(N_DEV):
            o_ref[j] = g_ref[:, j * c:(j + 1) * c]

    return pl.pallas_call(
        body, grid=(d // tr,),
        in_specs=[pl.BlockSpec((tr, n), lambda i: (i, 0))],
        out_specs=pl.BlockSpec((N_DEV, tr, c), lambda i: (0, i, 0)), out_shape=_sds((N_DEV, d, c), g.dtype),
        name=name, compiler_params=_cparams(("parallel",)))(g)


def _block_diag(w):
    n, b, _ = w.shape
    eye = jnp.eye(n, dtype=w.dtype)
    return (w[:, :, None, :] * eye[:, None, :, None]).reshape(n * b, n * b)


def _block_diag_of(w):
    n = D_RNN // 64
    eye = jnp.eye(n, dtype=w.dtype)
    return (w.reshape(n, 64, n, 64) * eye[:, None, :, None]).sum(axis=2)


def _gate_weight(wg, direction):
    lo = direction * GLA_RANK
    return jnp.pad(wg, ((lo, LANES - GLA_RANK - lo), (0, 0)))


def _layer_weights(full, big, l):
    row = lambda v: v.reshape(1, -1)
    lw = dict(
        gpre=row(full["mix_norm_pre"][l]), gpost=row(full["mix_norm_post"][l]),
        w_in=_w_in_from_shards(big["w_in"]),
        cw=full["conv_w"][l], cb=row(full["conv_b"][l]),
        g_rnn=row(full["rnn_out_norm"][l]), g_gla=row(full["gla_out_norm"][l]),
        w_out=big["w_out"].reshape(D_MODEL, D_MODEL),
        fpre=row(full["ffn_norm_pre"][l]), fpost=row(full["ffn_norm_post"][l]),
        wg=_cols_from_shards(big["w_ffn_gate"], "w_ffn_gate_from_shards"),
        wu=_cols_from_shards(big["w_ffn_up"], "w_ffn_up_from_shards"),
        wd=big["w_ffn_down"].reshape(D_FF, D_MODEL))
    for d in (0, 1):
        lw[f"wa{d}"] = _block_diag(full["lru_w_a"][l, d]).astype(MXU_DTYPE)
        lw[f"wx{d}"] = _block_diag(full["lru_w_x"][l, d]).astype(MXU_DTYPE)
        lw[f"ba{d}"] = row(full["lru_b_a"][l, d])
        lw[f"bx{d}"] = row(full["lru_b_x"][l, d])
        lw[f"lam{d}"] = row(full["lru_lambda"][l, d])
        lw[f"gw{d}"] = _gate_weight(full["gla_w_gate"][l, d], d).astype(MXU_DTYPE)
        lw[f"gb{d}"] = row(full["gla_b_gate"][l, d])
    return lw


def _layer_fwd(x, lw, rider=None):
    proj = _mix_in_fwd(x, lw["gpre"], lw["w_in"])
    hs, sts = [], []
    osum = None
    for d in (0, 1):
        hs.append(_rnn_fwd(proj, lw["cw"], lw["cb"], lw[f"wa{d}"], lw[f"ba{d}"], lw[f"wx{d}"], lw[f"bx{d}"],
                           lw[f"lam{d}"], bool(d)))
        osum, st = _gla_fwd(proj, lw[f"gw{d}"], lw[f"gb{d}"], bool(d), osum)
        sts.append(st)
    x1, y = _mix_out_fwd(x, hs[0], hs[1], osum, proj, lw["g_rnn"], lw["g_gla"], lw["w_out"], lw["gpost"])
    (x2, a, u, f), ridden = _ffn_fwd(x1, lw["fpre"], lw["wg"], lw["wu"], lw["wd"], lw["fpost"], rider)
    saved = dict(x=x, proj=proj, hs=hs, osum=osum, sts=sts, y=y, x1=x1, a=a, u=u, f=f)
    return x2, saved, ridden


def _layer_bwd(dx2, sv, lw, l, hooks):
    g = {}
    rider = hooks.pending_rider()
    (dx1, df, h2, p, da, du, dfpost, dfpre), ridden = _ffn_bwd(
        dx2, sv["f"], sv["x1"], sv["a"], sv["u"], lw["fpre"], lw["wg"], lw["wu"], lw["wd"], lw["fpost"], rider)
    hooks.ridden(rider, ridden)
    g["ffn_norm_post"], g["ffn_norm_pre"] = dfpost[0], dfpre[0]
    big = {}
    big["w_ffn_gate"] = _cols_to_shards(_tn_matmul(h2, da, "dw_ffn_gate"), "dw_ffn_gate_to_shards")
    big["w_ffn_up"] = _cols_to_shards(_tn_matmul(h2, du, "dw_ffn_up"), "dw_ffn_up_to_shards")
    big["w_ffn_down"] = _tn_matmul(p, df, "dw_ffn_down").reshape(N_DEV, FF_SHARD, D_MODEL)
    proj = sv["proj"]
    dm, dhs, dgr, dos, dg, dgpost, dgrnn, dggla = _mix_out_bwd(
        dx1, sv["y"], sv["hs"][0], sv["hs"][1], sv["osum"], proj, lw["g_rnn"], lw["g_gla"], lw["w_out"], lw["gpost"])
    g["mix_norm_post"], g["rnn_out_norm"], g["gla_out_norm"] = dgpost[0], dgrnn[0], dggla[0]
    big["w_out"] = _tn_matmul(sv["y"], dm, "dw_out").reshape(N_DEV, D_MODEL // N_DEV, D_MODEL)
    dxc = []
    gla = None
    gw, gb, wa, ba, wx, bx, lam = [], [], [], [], [], [], []
    for d in (0, 1):
        r = _gla_bwd(dos, proj, sv["sts"][d], lw[f"gw{d}"], lw[f"gb{d}"], bool(d), gla)
        gla = r[:4]
        lo = d * GLA_RANK
        gw.append(r[4][lo:lo + GLA_RANK])
        gb.append(r[5][0])
        early = ("w_ffn_gate", "w_ffn_up") if d == 0 else ("w_ffn_down", "w_out")
        hooks.offer(l, early, [big[n] for n in early])
        rider = hooks.pending_rider()
        r, ridden = _rnn_bwd(dhs, sv["hs"][d], proj, lw["cw"], lw["cb"], lw[f"wa{d}"], lw[f"ba{d}"], lw[f"wx{d}"],
                             lw[f"bx{d}"], lw[f"lam{d}"], bool(d), rider)
        hooks.ridden(rider, ridden)
        dxc.append(r[0])
        wa.append(_block_diag_of(r[1])); ba.append(r[2][0]); wx.append(_block_diag_of(r[3])); bx.append(r[4][0])
        lam.append(r[5][0])
    g["gla_w_gate"], g["gla_b_gate"] = jnp.stack(gw), jnp.stack(gb)
    g["lru_w_a"], g["lru_b_a"] = jnp.stack(wa), jnp.stack(ba)
    g["lru_w_x"], g["lru_b_x"], g["lru_lambda"] = jnp.stack(wx), jnp.stack(bx), jnp.stack(lam)
    dxr, dcw, dcb = _conv_bwd(dxc[0], dxc[1], proj, lw["cw"])
    g["conv_w"], g["conv_b"] = dcw, dcb[0]
    dx, dproj, h, dgpre = _mix_in_bwd((dxr, dgr, gla[0], gla[1], gla[2], dg), gla[3], sv["x"], dx1, lw["gpre"],
                                      lw["w_in"])
    g["mix_norm_pre"] = dgpre[0]
    hooks.offer(l, ("w_in",), [_w_in_to_shards(_tn_matmul(h, dproj, "dw_in", rows=1024, acc_bytes=14 * 1024 * 1024))])
    return dx, g


WEIGHT_NAMES = ["mix_norm_pre", "mix_norm_post", "w_in", "conv_w", "conv_b", "lru_w_a", "lru_b_a", "lru_w_x", "lru_b_x",
                "lru_lambda", "rnn_out_norm", "gla_w_gate", "gla_b_gate", "gla_out_norm", "w_out", "ffn_norm_pre",
                "ffn_norm_post", "w_ffn_gate", "w_ffn_up", "w_ffn_down"]
BIG_WEIGHTS = ["w_in", "w_out", "w_ffn_gate", "w_ffn_up", "w_ffn_down"]


def _local_step(x, target, full, hooks):
    saved, lws = [], []
    for l in range(DEPTH):
        lws.append(_layer_weights(full, hooks.big_weights(l), l))
        x, sv, ridden = _layer_fwd(x, lws[l], hooks.fwd_rider(l))
        hooks.fwd_ridden(l, ridden)
        saved.append(sv)
    loss, dx = _loss_fwd_bwd(x, target)
    grads = [None] * DEPTH
    for l in reversed(range(DEPTH)):
        dx, grads[l] = _layer_bwd(dx, saved[l], lws[l], l, hooks)
    g = {n: jnp.stack([grads[l][n] for l in range(DEPTH)]) for n in WEIGHT_NAMES if n not in BIG_WEIGHTS}
    return loss[0, 0], dx, g


def _all_gather(x, name):
    def body(x_ref, out_ref, send_sems, recv_sems, local_sem):
        mx, my, mc = lax.axis_index("x"), lax.axis_index("y"), lax.axis_index("c")
        me, sibling = (mx, my, mc), (mx, my, 1 - mc)
        chips = [(1 - mx, my), (mx, 1 - my), (1 - mx, 1 - my)]

        def slot(px, py, pc):
            return out_ref.at[4 * px + 2 * py + pc]

        def copy(k, block, to, src=None):
            return pltpu.make_async_remote_copy(
                src_ref=slot(*block) if src is None else src, dst_ref=slot(*block),
                send_sem=send_sems.at[k], recv_sem=recv_sems.at[k], device_id=to, device_id_type=MESH_ID)

        mine = pltpu.make_async_copy(x_ref, slot(*me), local_sem)
        mine.start()
        first = [copy(0, me, sibling, src=x_ref)]
        first += [copy(1 + j, me, (*chip, mc), src=x_ref) for j, chip in enumerate(chips)]
        for cp in first:
            cp.start()
        passed = [copy(4 + j, (*chip, mc), sibling) for j, chip in enumerate(chips)]
        for j, chip in enumerate(chips):
            copy(1 + j, (*chip, mc), me).wait_recv()
            passed[j].start()
        copy(0, sibling, me).wait_recv()
        for j, chip in enumerate(chips):
            copy(4 + j, (*chip, 1 - mc), me).wait_recv()
        for cp in first + passed:
            cp.wait_send()
        mine.wait()

    return pl.pallas_call(
        body, out_shape=_sds((N_DEV,) + x.shape, x.dtype), in_specs=[ANY], out_specs=ANY,
        scratch_shapes=[pltpu.SemaphoreType.DMA((7,)), pltpu.SemaphoreType.DMA((7,)), pltpu.SemaphoreType.DMA],
        name=name)(x)


def _all_to_all(g, name):
    def body(g_ref, out_ref, send_sems, recv_sems, local_sem):
        mx, my, mc = lax.axis_index("x"), lax.axis_index("y"), lax.axis_index("c")
        me = 4 * mx + 2 * my + mc
        mine = pltpu.make_async_copy(g_ref.at[me], out_ref.at[me], local_sem)
        mine.start()
        copies = []
        for r in range(1, N_DEV):
            px = 1 - mx if r & 4 else mx
            py = 1 - my if r & 2 else my
            pc = 1 - mc if r & 1 else mc
            cp = pltpu.make_async_remote_copy(
                src_ref=g_ref.at[4 * px + 2 * py + pc], dst_ref=out_ref.at[me],
                send_sem=send_sems.at[r - 1], recv_sem=recv_sems.at[r - 1],
                device_id=(px, py, pc), device_id_type=MESH_ID)
            cp.start()
            copies.append(cp)
        for cp in copies:
            cp.wait()
        mine.wait()

    return pl.pallas_call(
        body, out_shape=_sds(g.shape, g.dtype), in_specs=[ANY], out_specs=ANY,
        scratch_shapes=[pltpu.SemaphoreType.DMA((7,)), pltpu.SemaphoreType.DMA((7,)), pltpu.SemaphoreType.DMA],
        name=name)(g)


def _sum_slots(parts, name):
    n, r, c = parts.shape
    tr = _tile(r, ADAM_TILE_ROWS)

    def body(p_ref, g_ref):
        g = p_ref[0]
        for k in range(1, n):
            g = g + p_ref[k]
        g_ref[...] = g

    return pl.pallas_call(
        body, grid=(r // tr,), in_specs=[pl.BlockSpec((n, tr, c), lambda i: (0, i, 0))],
        out_specs=pl.BlockSpec((tr, c), lambda i: (i, 0)), out_shape=_sds((r, c)),
        name=name, compiler_params=_cparams(("parallel",)))(parts)


def _sum_adamw(parts, w, m, v, name):
    n_slots, r, c = parts.shape
    tr = _tile(r, ADAM_TILE_ROWS)

    def body(p_ref, w_ref, m_ref, v_ref, g_ref, d_ref, m2_ref, v2_ref):
        g = p_ref[0]
        for k in range(1, n_slots):
            g = g + p_ref[k]
        g_ref[...] = g
        m2 = ADAM_B1 * m_ref[...] + (1.0 - ADAM_B1) * g
        v2 = ADAM_B2 * v_ref[...] + (1.0 - ADAM_B2) * (g * g)
        m2_ref[...] = m2
        v2_ref[...] = v2
        m_hat = m2 / (1.0 - ADAM_B1 ** ADAM_STEP)
        v_hat = v2 / (1.0 - ADAM_B2 ** ADAM_STEP)
        d_ref[...] = -ADAM_LR * (m_hat / (jnp.sqrt(v_hat) + ADAM_EPS) + ADAM_WD * w_ref[...])

    flat = pl.BlockSpec((tr, c), lambda i: (i, 0))
    return pl.pallas_call(
        body, grid=(r // tr,),
        in_specs=[pl.BlockSpec((n_slots, tr, c), lambda i: (0, i, 0)), flat, flat, flat],
        out_specs=[flat] * 4, out_shape=[_sds((r, c))] * 4,
        name=name, compiler_params=_cparams(("parallel",)))(parts, w, m, v)


def _gather_big_weights(shards):
    n = len(shards)

    def body(*refs):
        srcs, outs = refs[:n], refs[n:2 * n]
        send_sems, recv_sems, local_sems = refs[2 * n:]
        mx, my, mc = lax.axis_index("x"), lax.axis_index("y"), lax.axis_index("c")
        me, sibling = (mx, my, mc), (mx, my, 1 - mc)
        chips = [(1 - mx, my), (mx, 1 - my), (1 - mx, 1 - my)]

        def slot(a, px, py, pc):
            return outs[a].at[4 * px + 2 * py + pc]

        def copy(k, a, block, to, own=False):
            return pltpu.make_async_remote_copy(
                src_ref=srcs[a] if own else slot(a, *block), dst_ref=slot(a, *block),
                send_sem=send_sems.at[k * n + a], recv_sem=recv_sems.at[k * n + a],
                device_id=to, device_id_type=MESH_ID)

        mine = [pltpu.make_async_copy(srcs[a], slot(a, *me), local_sems.at[a]) for a in range(n)]
        for cp in mine:
            cp.start()
        first = [copy(0, a, me, sibling, own=True) for a in range(n)]
        first += [copy(1 + j, a, me, (*chip, mc), own=True) for j, chip in enumerate(chips) for a in range(n)]
        for cp in first:
            cp.start()
        passed = [[copy(4 + j, a, (*chip, mc), sibling) for a in range(n)] for j, chip in enumerate(chips)]
        for j, chip in enumerate(chips):
            for a in range(n):
                copy(1 + j, a, (*chip, mc), me).wait_recv()
                passed[j][a].start()
        for a in range(n):
            copy(0, a, sibling, me).wait_recv()
        for j, chip in enumerate(chips):
            for a in range(n):
                copy(4 + j, a, (*chip, 1 - mc), me).wait_recv()
        for cp in first + [cp for row in passed for cp in row]:
            cp.wait_send()
        for cp in mine:
            cp.wait()

    return pl.pallas_call(
        body, out_shape=[_sds((N_DEV,) + s.shape, s.dtype) for s in shards],
        in_specs=[ANY] * n, out_specs=[ANY] * n,
        scratch_shapes=[pltpu.SemaphoreType.DMA((7 * n,)), pltpu.SemaphoreType.DMA((7 * n,)),
                        pltpu.SemaphoreType.DMA((n,))],
        name="gather_matmul_weights")(*shards)


def _run_alone(rider, name):
    r_in = len(rider.operands)
    r_out = len(rider.out_shape)

    def body(*refs):
        local, remote = rider.copies(refs[:r_in], refs[r_in:r_in + r_out], *refs[r_in + r_out:])
        for cp in local + remote:
            cp.start()
        for cp in remote:
            cp.wait()
        for cp in local:
            cp.wait()

    n_remote = (N_DEV - 1) * rider.n
    return pl.pallas_call(
        body, out_shape=rider.out_shape, in_specs=[ANY] * r_in, out_specs=[ANY] * r_out,
        input_output_aliases=dict(rider.aliases),
        scratch_shapes=[pltpu.SemaphoreType.DMA((n_remote,)), pltpu.SemaphoreType.DMA((n_remote,)),
                        pltpu.SemaphoreType.DMA((rider.n,))],
        name=name)(*rider.operands)


def _sum_adamw_big(parts, w, m, v, name):
    _, nl, a, b = parts.shape
    ta = _tile(a, 256)

    def body(p_ref, w_ref, m_ref, v_ref, g_ref, d_ref, m2_ref, v2_ref):
        g = p_ref[0].astype(F32)
        for k in range(1, N_DEV):
            g = g + p_ref[k].astype(F32)
        g_ref[...] = g
        m2 = ADAM_B1 * m_ref[...] + (1.0 - ADAM_B1) * g
        v2 = ADAM_B2 * v_ref[...] + (1.0 - ADAM_B2) * (g * g)
        m2_ref[...] = m2
        v2_ref[...] = v2
        m_hat = m2 / (1.0 - ADAM_B1 ** ADAM_STEP)
        v_hat = v2 / (1.0 - ADAM_B2 ** ADAM_STEP)
        d_ref[...] = -ADAM_LR * (m_hat / (jnp.sqrt(v_hat) + ADAM_EPS) + ADAM_WD * w_ref[...])

    blk = pl.BlockSpec((None, ta, b), lambda l, i: (l, i, 0))
    return pl.pallas_call(
        body, grid=(nl, a // ta),
        in_specs=[pl.BlockSpec((N_DEV, None, ta, b), lambda l, i: (0, l, i, 0)), blk, blk, blk],
        out_specs=[blk] * 4, out_shape=[_sds((nl, a, b))] * 4,
        name=name, compiler_params=_cparams(("parallel", "parallel")))(parts, w, m, v)


SMALL_SHARDED = [("conv_w", 2), ("lru_b_a", 2), ("lru_b_x", 2), ("lru_lambda", 2), ("gla_w_gate", 3), ("gla_b_gate", 2)]
REPLICATED = ["mix_norm_pre", "mix_norm_post", "conv_b", "lru_w_a", "lru_w_x", "rnn_out_norm", "gla_out_norm",
              "ffn_norm_pre", "ffn_norm_post"]


def _pack(arrays, cols, row_mult):
    flat = jnp.concatenate([a.reshape(-1) for a in arrays])
    unit = cols * row_mult
    total = -(-flat.shape[0] // unit) * unit
    return jnp.pad(flat, (0, total - flat.shape[0])).reshape(total // cols, cols)


def _pack_slots(arrays, cols, row_mult):
    flat = jnp.concatenate([a.reshape(N_DEV, -1) for a in arrays], axis=1)
    unit = cols * row_mult
    total = -(-flat.shape[1] // unit) * unit
    return jnp.pad(flat, ((0, 0), (0, total - flat.shape[1]))).reshape(N_DEV, total // cols, cols)


def _unpack(flat, shapes):
    flat = flat.reshape(-1)
    out, off = [], 0
    for sh in shapes:
        n = 1
        for d in sh:
            n *= d
        out.append(flat[off:off + n].reshape(sh))
        off += n
    return out


def _unpack_slots(flat, shapes):
    flat = flat.reshape(N_DEV, -1)
    out, off = [], 0
    for sh in shapes:
        n = 1
        for d in sh:
            n *= d
        out.append(flat[:, off:off + n].reshape((N_DEV,) + tuple(sh)))
        off += n
    return out


def _merge_shards(a, axis):
    a = jnp.moveaxis(a, 0, axis)
    sh = a.shape
    return a.reshape(sh[:axis] + (sh[axis] * sh[axis + 1],) + sh[axis + 2:])


def _split_shards(a, axis):
    sh = a.shape
    a = a.reshape(sh[:axis] + (N_DEV, sh[axis] // N_DEV) + sh[axis + 1:])
    return jnp.moveaxis(a, axis, 0)


class _StepHooks:
    def __init__(self, shards, recvs):
        self.shards = shards
        self.recvs = recvs
        self.gathered = {0: _gather_big_weights(shards[0])}
        self.pending = []

    def big_weights(self, l):
        return dict(zip(BIG_WEIGHTS, self.gathered.pop(l)))

    def fwd_rider(self, l):
        return _WeightGather(self.shards[l + 1]) if l + 1 < DEPTH else None

    def fwd_ridden(self, l, outs):
        if outs:
            self.gathered[l + 1] = outs

    def offer(self, l, names, arrays):
        self.pending += [(l, n, a) for n, a in zip(names, arrays)]

    def pending_rider(self):
        if not self.pending:
            return None
        layers, names, arrays = zip(*self.pending)
        self.pending = []
        return _GradExchange(names, arrays, [self.recvs[n] for n in names], layers)

    def ridden(self, rider, outs):
        if rider is not None:
            self.recvs.update(zip(rider.names, outs))

    def finish(self):
        rider = self.pending_rider()
        self.ridden(rider, _run_alone(rider, "exchange_last_grads"))
        return self.recvs


def kernel(x, mix_norm_pre, mix_norm_post, w_in, conv_w, conv_b, lru_w_a, lru_b_a, lru_w_x, lru_b_x, lru_lambda, rnn_out_norm, gla_w_gate, gla_b_gate, gla_out_norm, w_out, ffn_norm_pre, ffn_norm_post, w_ffn_gate, w_ffn_up, w_ffn_down, loss_target, m_mix_norm_pre, m_mix_norm_post, m_w_in, m_conv_w, m_conv_b, m_lru_w_a, m_lru_b_a, m_lru_w_x, m_lru_b_x, m_lru_lambda, m_rnn_out_norm, m_gla_w_gate, m_gla_b_gate, m_gla_out_norm, m_w_out, m_ffn_norm_pre, m_ffn_norm_post, m_w_ffn_gate, m_w_ffn_up, m_w_ffn_down, v_mix_norm_pre, v_mix_norm_post, v_w_in, v_conv_w, v_conv_b, v_lru_w_a, v_lru_b_a, v_lru_w_x, v_lru_b_x, v_lru_lambda, v_rnn_out_norm, v_gla_w_gate, v_gla_b_gate, v_gla_out_norm, v_w_out, v_ffn_norm_pre, v_ffn_norm_post, v_w_ffn_gate, v_w_ffn_up, v_w_ffn_down):
    args = dict(locals())
    w = {n: args[n] for n in WEIGHT_NAMES}
    m = {n: args["m_" + n] for n in WEIGHT_NAMES}
    v = {n: args["v_" + n] for n in WEIGHT_NAMES}
    names_s = [n for n, _ in SMALL_SHARDED]
    axis_s = dict(SMALL_SHARDED)
    shapes_s = [w[n].shape for n in names_s]

    small = _pack([w[n] for n in names_s], LANES, 8)
    small_all = _unpack_slots(_all_gather(small, "gather_small_weights"), shapes_s)
    full = {n: w[n] for n in REPLICATED}
    for n, a in zip(names_s, small_all):
        full[n] = _merge_shards(a, axis_s[n])

    hooks = _StepHooks([[w[n][l].astype(MXU_DTYPE) for n in BIG_WEIGHTS] for l in range(DEPTH)],
                       {n: jnp.zeros((N_DEV,) + w[n].shape, MXU_DTYPE) for n in BIG_WEIGHTS})
    loss, dx, g = _local_step(x[0], loss_target[0], full, hooks)
    loss = lax.psum(loss, MESH_AXES)
    res = {}
    for n, parts in hooks.finish().items():
        res[n] = _sum_adamw_big(parts, w[n], m[n], v[n], "adamw_" + n)

    g_slots = _pack_slots([_split_shards(g[n], axis_s[n]) for n in names_s], LANES, 8)
    g_recv = _all_to_all(g_slots, "exchange_small_grads")
    packed = [_pack([t[n] for n in names_s], LANES, 8) for t in (w, m, v)]
    res_s = [_unpack(r, shapes_s) for r in _sum_adamw(g_recv, *packed, "adamw_small")]
    for i, n in enumerate(names_s):
        res[n] = [res_s[k][i] for k in range(4)]

    shapes_r = [w[n].shape for n in REPLICATED]
    g_rep = _pack([g[n] for n in REPLICATED], ADAM_COLS, ADAM_TILE_ROWS)
    rows_r = g_rep.shape[0]
    g_rep = _all_to_all(g_rep.reshape(N_DEV, rows_r // N_DEV, ADAM_COLS), "exchange_replicated_grads")
    g_rep = _all_gather(_sum_slots(g_rep, "sum_replicated_grads"), "gather_replicated_grads")
    packed = [_pack([t[n] for n in REPLICATED], ADAM_COLS, ADAM_TILE_ROWS) for t in (w, m, v)]
    res_r = [_unpack(r, shapes_r)
             for r in _sum_adamw(g_rep.reshape(1, rows_r, ADAM_COLS), *packed, "adamw_replicated")]
    for i, n in enumerate(REPLICATED):
        res[n] = [res_r[k][i] for k in range(4)]

    outs = [[res[n][k] for n in WEIGHT_NAMES] for k in range(4)]
    return (loss, dx[None], *outs[0], *outs[1], *outs[2], *outs[3])
```

```python
import functools

import jax
import jax.numpy as jnp
from jax import lax
from jax.experimental import pallas as pl
from jax.experimental.pallas import tpu as pltpu

F32 = jnp.float32
MXU_DTYPE = jnp.bfloat16

N_DEV = 8
D_MODEL = 1024
D_RNN = 512
CONV_WIDTH = 4
LRU_C = 8.0
GLA_HEADS = 4
GLA_DK = 64
GLA_PAIRS = 2
PAIR_K = 128
PAIR_V = 256
GLA_DV = 128
GLA_RANK = 16
GLA_TAU = 16.0
GLA_CHUNK = 64
D_FF = 2816
RMS_EPS = 1e-6
TINY = 1e-30
DEPTH = 4

PW = 2688
COL_Q, COL_K, COL_V, COL_G, COL_LR = 1024, 1280, 1536, 2048, 2560
QK_W = GLA_HEADS * GLA_DK
LANES = 128

TILE_S = 1024
TILE_WIDE = 512
TILE_F = 256
TILE_TN = 2048
TN_ACC_BYTES = 6 * 1024 * 1024
F_CHUNK = 1408
VMEM_LIMIT = 56 * 1024 * 1024

ADAM_LR = 0.001
ADAM_B1 = 0.9
ADAM_B2 = 0.999
ADAM_EPS = 1e-08
ADAM_WD = 0.01
ADAM_STEP = 10

ADAM_TILE_ROWS = 256
ADAM_COLS = 1024


def _mm(a, b):
    return jnp.dot(a.astype(MXU_DTYPE), b.astype(MXU_DTYPE), preferred_element_type=F32)


def _mm_nt(a, b):
    return lax.dot_general(a.astype(MXU_DTYPE), b.astype(MXU_DTYPE), (((1,), (1,)), ((), ())),
                           preferred_element_type=F32)


def _mm_tn(a, b):
    return lax.dot_general(a.astype(MXU_DTYPE), b.astype(MXU_DTYPE), (((0,), (0,)), ((), ())),
                           preferred_element_type=F32)


def _bmm(a, b):
    return lax.dot_general(a.astype(MXU_DTYPE), b.astype(MXU_DTYPE), (((2,), (1,)), ((0,), (0,))),
                           preferred_element_type=F32)


def _bmm_nt(a, b):
    return lax.dot_general(a.astype(MXU_DTYPE), b.astype(MXU_DTYPE), (((2,), (2,)), ((0,), (0,))),
                           preferred_element_type=F32)


def _bmm_tn(a, b):
    return lax.dot_general(a.astype(MXU_DTYPE), b.astype(MXU_DTYPE), (((1,), (1,)), ((0,), (0,))),
                           preferred_element_type=F32)


def _bmm_tri(tri, x):
    t = jnp.broadcast_to(tri.astype(jnp.bfloat16)[None], (x.shape[0],) + tri.shape)
    hi = x.astype(jnp.bfloat16)
    r1 = x - hi.astype(F32)
    mid = r1.astype(jnp.bfloat16)
    lo = (r1 - mid.astype(F32)).astype(jnp.bfloat16)
    dot = lambda v: lax.dot_general(t, v, (((2,), (1,)), ((0,), (0,))), preferred_element_type=F32)
    return dot(hi) + dot(mid) + dot(lo)


def _sigmoid(x):
    return 0.5 * jnp.tanh(0.5 * x) + 0.5


def _log1p_pos(e):
    series = e * (1.0 - e * (0.5 - e * (1.0 / 3.0 - e * 0.25)))
    return jnp.where(e < 0.01, series, jnp.log(1.0 + e))


def _softplus(x):
    return jnp.maximum(x, 0.0) + _log1p_pos(jnp.exp(-jnp.abs(x)))


def _softplus_coarse(x):
    return jnp.maximum(x, 0.0) + jnp.log(1.0 + jnp.exp(-jnp.abs(x)))


GELU_C = 0.7978845608028654
GELU_K = 0.044715


def _gelu_and_grad(x):
    t = jnp.tanh(GELU_C * (x + GELU_K * x * x * x))
    y = 0.5 * x * (1.0 + t)
    dy = 0.5 * (1.0 + t) + 0.5 * x * (1.0 - t * t) * GELU_C * (1.0 + 3.0 * GELU_K * x * x)
    return y, dy


def _rms_fwd(x, g):
    rs = lax.rsqrt(jnp.mean(x * x, axis=-1, keepdims=True) + RMS_EPS)
    n = x * rs
    return n * g, n, rs


def _rms_bwd(dy, n, rs, g):
    dn = dy * g
    dx = rs * (dn - n * jnp.mean(dn * n, axis=-1, keepdims=True))
    dg = jnp.sum(dy * n, axis=0, keepdims=True)
    return dx, dg


def _cparams(sem=None):
    kw = dict(vmem_limit_bytes=VMEM_LIMIT)
    if sem is not None:
        kw["dimension_semantics"] = sem
    return pltpu.CompilerParams(**kw)


def _tile(n, pref):
    return pref if n % pref == 0 else n


def _const(shape):
    nd = len(shape)
    return pl.BlockSpec(shape, lambda *_: (0,) * nd, pipeline_mode=pl.Buffered(1))


def _acc(shape):
    nd = len(shape)
    return pl.BlockSpec(shape, lambda *_: (0,) * nd)


def _rows(ts, w, col=0, order=None):
    if order is None:
        return pl.BlockSpec((ts, w), lambda i: (i, col))
    return pl.BlockSpec((ts, w), lambda i: (order(i), col))


def _sds(shape, dtype=F32):
    return jax.ShapeDtypeStruct(shape, dtype)


MESH_ID = pl.DeviceIdType.MESH
ANY = pl.BlockSpec(memory_space=pl.ANY)
MESH_AXES = ("x", "y", "c")


def _peers():
    mx, my, mc = lax.axis_index("x"), lax.axis_index("y"), lax.axis_index("c")
    peers = []
    for r in range(1, N_DEV):
        px = 1 - mx if r & 4 else mx
        py = 1 - my if r & 2 else my
        pc = 1 - mc if r & 1 else mc
        peers.append((4 * px + 2 * py + pc, (px, py, pc)))
    return 4 * mx + 2 * my + mc, peers


class _GradExchange:
    def __init__(self, names, srcs, recvs, layers):
        self.names = list(names)
        self.n = len(srcs)
        self.layers = list(layers)
        self.operands = list(srcs) + list(recvs)
        self.out_shape = [_sds(r.shape, r.dtype) for r in recvs]
        self.aliases = {self.n + a: a for a in range(self.n)}

    def copies(self, ins, outs, send_sems, recv_sems, local_sems):
        n, layers = self.n, self.layers
        me, peers = _peers()
        local = [pltpu.make_async_copy(ins[a].at[me], outs[a].at[me, layers[a]], local_sems.at[a]) for a in range(n)]
        remote = [pltpu.make_async_remote_copy(
            src_ref=ins[a].at[slot], dst_ref=outs[a].at[me, layers[a]],
            send_sem=send_sems.at[r * n + a], recv_sem=recv_sems.at[r * n + a],
            device_id=dev, device_id_type=MESH_ID) for r, (slot, dev) in enumerate(peers) for a in range(n)]
        return local, remote


class _WeightGather:
    def __init__(self, shards):
        self.n = len(shards)
        self.operands = list(shards)
        self.out_shape = [_sds((N_DEV,) + s.shape, s.dtype) for s in shards]
        self.aliases = {}

    def copies(self, ins, outs, send_sems, recv_sems, local_sems):
        n = self.n
        me, peers = _peers()
        local = [pltpu.make_async_copy(ins[a], outs[a].at[me], local_sems.at[a]) for a in range(n)]
        remote = [pltpu.make_async_remote_copy(
            src_ref=ins[a], dst_ref=outs[a].at[me],
            send_sem=send_sems.at[r * n + a], recv_sem=recv_sems.at[r * n + a],
            device_id=dev, device_id_type=MESH_ID) for r, (_, dev) in enumerate(peers) for a in range(n)]
        return local, remote


def _call_with_rider(body, rider, operands, *, steps, in_specs, out_specs, out_shape, scratch_shapes=(), name,
                     semantics):
    if rider is None:
        outs = pl.pallas_call(body, grid=(steps,), in_specs=in_specs, out_specs=out_specs, out_shape=out_shape,
                              scratch_shapes=list(scratch_shapes), name=name,
                              compiler_params=_cparams((semantics,)))(*operands)
        return outs, []
    n_in, n_out, n_scr = len(in_specs), len(out_specs), len(scratch_shapes)
    r_in, r_out = len(rider.operands), len(rider.out_shape)

    def riding(*refs):
        own_in, ride_in = refs[:n_in], refs[n_in:n_in + r_in]
        refs = refs[n_in + r_in:]
        own_out, ride_out = refs[:n_out], refs[n_out:n_out + r_out]
        refs = refs[n_out + r_out:]
        own_scr, sems = refs[:n_scr], refs[n_scr:]
        i = pl.program_id(0)

        @pl.when(i == 0)
        def _():
            local, remote = rider.copies(ride_in, ride_out, *sems)
            for cp in local + remote:
                cp.start()

        body(*own_in, *own_out, *own_scr)

        @pl.when(i == steps - 1)
        def _():
            local, remote = rider.copies(ride_in, ride_out, *sems)
            for cp in remote:
                cp.wait()
            for cp in local:
                cp.wait()

    n_remote = (N_DEV - 1) * rider.n
    outs = pl.pallas_call(
        riding, grid=(steps,), in_specs=list(in_specs) + [ANY] * r_in, out_specs=list(out_specs) + [ANY] * r_out,
        out_shape=list(out_shape) + rider.out_shape,
        input_output_aliases={n_in + i: n_out + o for i, o in rider.aliases.items()},
        scratch_shapes=list(scratch_shapes) + [pltpu.SemaphoreType.DMA((n_remote,)),
                                               pltpu.SemaphoreType.DMA((n_remote,)),
                                               pltpu.SemaphoreType.DMA((rider.n,))],
        name=name, compiler_params=_cparams(("arbitrary",)))(*operands, *rider.operands)
    return outs[:n_out], outs[n_out:]


def _mix_in_fwd(x, gpre, w_in_p):
    s = x.shape[0]
    ts = _tile(s, TILE_S)

    def body(x_ref, g_ref, w_ref, o_ref):
        h, _, _ = _rms_fwd(x_ref[...], g_ref[...])
        o_ref[...] = _mm(h, w_ref[...])

    return pl.pallas_call(
        body, grid=(s // ts,),
        in_specs=[_rows(ts, D_MODEL), _const((1, D_MODEL)), _const((D_MODEL, PW))],
        out_specs=_rows(ts, PW), out_shape=_sds((s, PW)),
        name="mix_in_fwd", compiler_params=_cparams(("parallel",)))(x, gpre, w_in_p)


def _conv_taps(xr, hp, hn, first, last):
    ts = xr.shape[0]
    hp = jnp.where(first, 0.0, hp)
    hn = jnp.where(last, 0.0, hn)
    xe = jnp.concatenate([hp, xr, hn], axis=0)
    return xe[6:6 + ts], xe[7:7 + ts], xr, xe[9:9 + ts]


def _conv_fwd(xr, hp, hn, cw, cb, first, last):
    t0, t1, t2, t3 = _conv_taps(xr, hp, hn, first, last)
    return cw[0:1] * t0 + cw[1:2] * t1 + cw[2:3] * t2 + cw[3:4] * t3 + cb


def _rnn_gates(xc, wa, ba, wx, bx, lam):
    r = _sigmoid(_mm(xc, wa) + ba)
    i = _sigmoid(_mm(xc, wx) + bx)
    sp = _softplus(-lam)
    la = (-LRU_C) * r * sp
    a = jnp.exp(la)
    one_minus_a2 = -jnp.tanh(la) * (a * a + 1.0)
    inv_mult = lax.rsqrt(jnp.maximum(one_minus_a2, TINY))
    return r, i, sp, a, one_minus_a2 * inv_mult, inv_mult


def _scan_tile(a_scr, u_scr, h_ref, c0, reverse):
    ts = a_scr.shape[0]
    a = a_scr[...]
    u = u_scr[...]
    row = lax.broadcasted_iota(jnp.int32, a.shape, 0) % 8
    for k in (1, 2, 4):
        if reverse:
            a_sh = pltpu.roll(a, ts - k, 0)
            u_sh = pltpu.roll(u, ts - k, 0)
            ok = row < 8 - k
        else:
            a_sh = pltpu.roll(a, k, 0)
            u_sh = pltpu.roll(u, k, 0)
            ok = row >= k
        u = jnp.where(ok, u + a * u_sh, u)
        a = jnp.where(ok, a * a_sh, a)
    a_scr[...] = a
    u_scr[...] = u
    ng = ts // 8

    def body(j, c):
        g = (ng - 1 - j) if reverse else j
        sl = pl.ds(pl.multiple_of(g * 8, 8), 8)
        hh = u_scr[sl, :] + a_scr[sl, :] * c
        h_ref[sl, :] = hh
        return hh[0:1, :] if reverse else hh[7:8, :]

    return lax.fori_loop(0, ng, body, c0)


def _halo_specs(s, ts, w, col, order):
    n8 = s // 8
    per = ts // 8
    prev = pl.BlockSpec((8, w), lambda i: (jnp.maximum(order(i) * per - 1, 0), col))
    nxt = pl.BlockSpec((8, w), lambda i: (jnp.minimum((order(i) + 1) * per, n8 - 1), col))
    return prev, nxt


def _rnn_fwd(proj, cw, cb, wa, ba, wx, bx, lam, reverse):
    s = proj.shape[0]
    ts = _tile(s, TILE_S)
    nt = s // ts
    order = (lambda i: nt - 1 - i) if reverse else (lambda i: i)

    def body(xr_ref, hp_ref, hn_ref, cw_ref, cb_ref, wa_ref, ba_ref, wx_ref, bx_ref, lam_ref,
             h_ref, a_scr, u_scr, c_scr):
        i = pl.program_id(0)
        t = order(i)

        @pl.when(i == 0)
        def _():
            c_scr[...] = jnp.zeros_like(c_scr)

        xc = _conv_fwd(xr_ref[...], hp_ref[...], hn_ref[...], cw_ref[...], cb_ref[...], t == 0, t == nt - 1)
        _, gi, _, a, mult, _ = _rnn_gates(xc, wa_ref[...], ba_ref[...], wx_ref[...], bx_ref[...], lam_ref[...])
        a_scr[...] = a
        u_scr[...] = xc * gi * mult
        c_scr[0:1, :] = _scan_tile(a_scr, u_scr, h_ref, c_scr[0:1, :], reverse)

    hp, hn = _halo_specs(s, ts, D_RNN, 0, order)
    return pl.pallas_call(
        body, grid=(nt,),
        in_specs=[_rows(ts, D_RNN, 0, order), hp, hn, _const((CONV_WIDTH, D_RNN)), _const((1, D_RNN)),
                  _const((D_RNN, D_RNN)), _const((1, D_RNN)), _const((D_RNN, D_RNN)), _const((1, D_RNN)),
                  _const((1, D_RNN))],
        out_specs=_rows(ts, D_RNN, 0, order), out_shape=_sds((s, D_RNN)),
        scratch_shapes=[pltpu.VMEM((ts, D_RNN), F32), pltpu.VMEM((ts, D_RNN), F32), pltpu.VMEM((8, D_RNN), F32)],
        name="rnn_fwd_rev" if reverse else "rnn_fwd",
        compiler_params=_cparams(("arbitrary",)))(proj, proj, proj, cw, cb, wa, ba, wx, bx, lam)


def _tri(reverse, transpose=False):
    r = lax.broadcasted_iota(jnp.int32, (GLA_CHUNK, GLA_CHUNK), 0)
    c = lax.broadcasted_iota(jnp.int32, (GLA_CHUNK, GLA_CHUNK), 1)
    if transpose:
        r, c = c, r
    return ((r <= c) if reverse else (r >= c)).astype(F32)


def _gla_chunk_terms(q, k, la, tri, reverse):
    b = _bmm_tri(tri, la)
    bl = b[:, 0:1] if reverse else b[:, GLA_CHUNK - 1:GLA_CHUNK]
    eb = jnp.exp(b)
    enb = jnp.exp(-b)
    ebl = jnp.exp(bl - b)
    d = jnp.exp(bl)
    return eb, enb, ebl, d, q * (GLA_DK ** -0.5) * eb, k * enb, k * ebl


def _gla_gate(lr, wg, bg):
    z = _mm(lr, wg) + bg
    return z, -_softplus_coarse(-z) * (1.0 / GLA_TAU)


def _pair_masks():
    first_head = lax.broadcasted_iota(jnp.int32, (1, PAIR_K), 1) < GLA_DK
    row_first = lax.broadcasted_iota(jnp.int32, (PAIR_V, PAIR_K), 0) < GLA_DV
    lane_first = lax.broadcasted_iota(jnp.int32, (PAIR_V, PAIR_K), 1) < GLA_DK
    return first_head, row_first == lane_first


def _gla_specs(ts, order):
    return [_rows(ts, QK_W, COL_Q // QK_W, order), _rows(ts, QK_W, COL_K // QK_W, order),
            _rows(ts, GLA_HEADS * GLA_DV, COL_V // (GLA_HEADS * GLA_DV), order),
            _rows(ts, LANES, COL_LR // LANES, order)]


def _gla_fwd(proj, wg, bg, reverse, o_add=None):
    s = proj.shape[0]
    ts = _tile(s, TILE_S)
    nt = s // ts
    ch = ts // GLA_CHUNK
    vw = GLA_HEADS * GLA_DV
    order = (lambda i: nt - 1 - i) if reverse else (lambda i: i)
    extra = [] if o_add is None else [o_add]

    def body(q_ref, k_ref, v_ref, lr_ref, wg_ref, bg_ref, *rest):
        add_ref = None if o_add is None else rest[0]
        o_ref, st_ref, s_scr = rest[len(extra):]

        @pl.when(pl.program_id(0) == 0)
        def _():
            s_scr[...] = jnp.zeros_like(s_scr)

        _, la = _gla_gate(lr_ref[...], wg_ref[...], bg_ref[...])
        tri = _tri(reverse)
        keep = tri > 0.5
        first_head, own = _pair_masks()
        chunks = lambda a: a.reshape(ch, GLA_CHUNK, a.shape[-1])
        _, _, _, d, qe, ke, kd = _gla_chunk_terms(chunks(q_ref[...]), chunks(k_ref[...]), chunks(la), tri, reverse)
        v = chunks(v_ref[...])
        outs = []
        for p in range(GLA_PAIRS):
            ln = slice(p * PAIR_K, (p + 1) * PAIR_K)
            v_p = v[:, :, p * PAIR_V:(p + 1) * PAIR_V]
            qe_p, ke_p = qe[:, :, ln], ke[:, :, ln]
            grow = jnp.where(own, _bmm_tn(v_p, kd[:, :, ln]), 0.0)
            st = s_scr[p]
            for cc in range(ch):
                c = (ch - 1 - cc) if reverse else cc
                st_ref[c, p] = st
                st = d[c, :, ln] * st + grow[c]
            s_scr[p] = st
            intra = []
            for h in range(2):
                q_h = jnp.where(first_head if h == 0 else ~first_head, qe_p, 0.0)
                a_m = jnp.where(keep, _bmm_nt(q_h, ke_p), 0.0)
                intra.append(_bmm(a_m, v_p[:, :, h * GLA_DV:(h + 1) * GLA_DV]))
            outs.append(_bmm_nt(qe_p, st_ref[:, p]) + jnp.concatenate(intra, axis=2))
        o = jnp.concatenate(outs, axis=2).reshape(ts, vw)
        o_ref[...] = o if add_ref is None else o + add_ref[...]

    return pl.pallas_call(
        body, grid=(nt,),
        in_specs=_gla_specs(ts, order) + [_const((LANES, QK_W)), _const((1, QK_W))]
                 + [_rows(ts, vw, 0, order)] * len(extra),
        out_specs=[_rows(ts, vw, 0, order),
                   pl.BlockSpec((ch, GLA_PAIRS, PAIR_V, PAIR_K), lambda i: (order(i), 0, 0, 0))],
        out_shape=[_sds((s, vw)), _sds((s // GLA_CHUNK, GLA_PAIRS, PAIR_V, PAIR_K))],
        scratch_shapes=[pltpu.VMEM((GLA_PAIRS, PAIR_V, PAIR_K), F32)],
        name="gla_fwd_rev" if reverse else "gla_fwd",
        compiler_params=_cparams(("arbitrary",)))(proj, proj, proj, proj, wg, bg, *extra)


def _mix_out_terms(hf, hb, gate_r, osum, g, g_rnn, g_gla):
    hs = hf + hb
    gl, dgl = _gelu_and_grad(gate_r)
    z = hs * gl
    y_rnn, n_rnn, rs_rnn = _rms_fwd(z, g_rnn)
    sg_lin = _sigmoid(g)
    sg = g * sg_lin
    dsg = sg_lin * (1.0 + g * (1.0 - sg_lin))
    ons, ns, rss = [], [], []
    for h in range(GLA_HEADS):
        ln = slice(h * LANES, (h + 1) * LANES)
        on, n, rs = _rms_fwd(osum[:, ln], g_gla)
        ons.append(on)
        ns.append(n)
        rss.append(rs)
    on = jnp.concatenate(ons, axis=1)
    return hs, gl, dgl, y_rnn, n_rnn, rs_rnn, sg, dsg, on, ns, rss


def _mix_out_fwd(x, hf, hb, osum, proj, g_rnn, g_gla, w_out, gpost):
    s = x.shape[0]
    ts = _tile(s, TILE_S)

    def body(x_ref, hf_ref, hb_ref, gr_ref, os_ref, g_ref, grnn_ref, ggla_ref, w_ref, gp_ref, x1_ref, y_ref):
        _, _, _, y_rnn, _, _, sg, _, on, _, _ = _mix_out_terms(
            hf_ref[...], hb_ref[...], gr_ref[...], os_ref[...], g_ref[...], grnn_ref[...], ggla_ref[...])
        y = jnp.concatenate([y_rnn, on * sg], axis=1).astype(MXU_DTYPE)
        y_ref[...] = y
        out, _, _ = _rms_fwd(_mm(y, w_ref[...]), gp_ref[...])
        x1_ref[...] = x_ref[...] + out

    return pl.pallas_call(
        body, grid=(s // ts,),
        in_specs=[_rows(ts, D_MODEL), _rows(ts, D_RNN), _rows(ts, D_RNN), _rows(ts, D_RNN, 1), _rows(ts, 512),
                  _rows(ts, 512, COL_G // 512), _const((1, D_RNN)), _const((1, GLA_DV)),
                  _const((D_MODEL, D_MODEL)), _const((1, D_MODEL))],
        out_specs=[_rows(ts, D_MODEL), _rows(ts, D_MODEL)],
        out_shape=[_sds((s, D_MODEL)), _sds((s, D_MODEL), MXU_DTYPE)],
        name="mix_out_fwd", compiler_params=_cparams(("parallel",)))(
            x, hf, hb, proj, osum, proj, g_rnn, g_gla, w_out, gpost)


def _f_chunks():
    return [(c0, min(c0 + F_CHUNK, D_FF)) for c0 in range(0, D_FF, F_CHUNK)]


def _ffn_fwd(x1, gpre, wg, wu, wd, gpost, rider=None):
    s = x1.shape[0]
    ts = _tile(s, TILE_F)

    def body(x_ref, gpre_ref, wg_ref, wu_ref, wd_ref, gpost_ref, x2_ref, a_ref, u_ref, f_ref):
        x = x_ref[...]
        h, _, _ = _rms_fwd(x, gpre_ref[...])
        h = h.astype(MXU_DTYPE)
        f = jnp.zeros((ts, D_MODEL), F32)
        for c0, c1 in _f_chunks():
            a = _mm(h, wg_ref[:, c0:c1])
            u = _mm(h, wu_ref[:, c0:c1])
            a_ref[:, c0:c1] = a.astype(MXU_DTYPE)
            u_ref[:, c0:c1] = u.astype(MXU_DTYPE)
            f = f + _mm(a * _sigmoid(a) * u, wd_ref[c0:c1, :])
        f_ref[...] = f
        out, _, _ = _rms_fwd(f, gpost_ref[...])
        x2_ref[...] = x + out

    return _call_with_rider(
        body, rider, (x1, gpre, wg, wu, wd, gpost), steps=s // ts,
        in_specs=[_rows(ts, D_MODEL), _const((1, D_MODEL)), _const((D_MODEL, D_FF)), _const((D_MODEL, D_FF)),
                  _const((D_FF, D_MODEL)), _const((1, D_MODEL))],
        out_specs=[_rows(ts, D_MODEL), _rows(ts, D_FF), _rows(ts, D_FF), _rows(ts, D_MODEL)],
        out_shape=[_sds((s, D_MODEL)), _sds((s, D_FF), MXU_DTYPE), _sds((s, D_FF), MXU_DTYPE), _sds((s, D_MODEL))],
        name="ffn_fwd", semantics="parallel")


def _loss_fwd_bwd(y, target):
    s = y.shape[0]
    ts = _tile(s, TILE_S)

    def body(y_ref, t_ref, loss_ref, dy_ref):
        @pl.when(pl.program_id(0) == 0)
        def _():
            loss_ref[...] = jnp.zeros_like(loss_ref)

        e = y_ref[...] - t_ref[...]
        dy_ref[...] = e * (1.0 / D_MODEL)
        part = jnp.sum(jnp.sum(e * e, axis=1, keepdims=True), axis=0, keepdims=True) * (0.5 / D_MODEL)
        loss_ref[...] += jnp.broadcast_to(part, loss_ref.shape)

    return pl.pallas_call(
        body, grid=(s // ts,),
        in_specs=[_rows(ts, D_MODEL), _rows(ts, D_MODEL)],
        out_specs=[_acc((8, LANES)), _rows(ts, D_MODEL)],
        out_shape=[_sds((8, LANES)), _sds((s, D_MODEL))],
        name="loss", compiler_params=_cparams(("arbitrary",)))(y, target)


def _tn_matmul(a, b, name, rows=TILE_TN, acc_bytes=TN_ACC_BYTES):
    s, k = a.shape
    n = b.shape[1]
    ts = _tile(s, rows)
    tn = max(t for t in range(LANES, n + 1, LANES) if n % t == 0 and (k * t * 4 <= acc_bytes or t == LANES))
    ns = s // ts

    def body(a_ref, b_ref, o_ref, acc):
        i = pl.program_id(1)

        @pl.when(i == 0)
        def _():
            acc[...] = jnp.zeros_like(acc)

        acc[...] += _mm_tn(a_ref[...], b_ref[...])

        @pl.when(i == ns - 1)
        def _():
            o_ref[...] = acc[...].astype(o_ref.dtype)

    return pl.pallas_call(
        body, grid=(n // tn, ns),
        in_specs=[pl.BlockSpec((ts, k), lambda j, i: (i, 0)), pl.BlockSpec((ts, tn), lambda j, i: (i, j))],
        out_specs=pl.BlockSpec((k, tn), lambda j, i: (0, j)), out_shape=_sds((k, n), MXU_DTYPE),
        scratch_shapes=[pltpu.VMEM((k, tn), F32)],
        name=name, compiler_params=_cparams(("parallel", "arbitrary")))(a, b)


def _ffn_bwd(dx2, f, x1, a, u, gpre, wg, wu, wd, gpost, rider=None):
    s = x1.shape[0]
    ts = _tile(s, TILE_F)

    def body(dx2_ref, f_ref, x1_ref, a_ref, u_ref, gpre_ref, wg_ref, wu_ref, wd_ref, gpost_ref,
             dx1_ref, df_ref, h_ref, p_ref, da_ref, du_ref, dgpost_ref, dgpre_ref):
        @pl.when(pl.program_id(0) == 0)
        def _():
            dgpost_ref[...] = jnp.zeros_like(dgpost_ref)
            dgpre_ref[...] = jnp.zeros_like(dgpre_ref)

        dx2 = dx2_ref[...]
        _, nf, rsf = _rms_fwd(f_ref[...], gpost_ref[...])
        df, dgpost = _rms_bwd(dx2, nf, rsf, gpost_ref[...])
        dgpost_ref[...] += dgpost
        df = df.astype(MXU_DTYPE)
        df_ref[...] = df
        h, n1, rs1 = _rms_fwd(x1_ref[...], gpre_ref[...])
        h_ref[...] = h.astype(MXU_DTYPE)
        dh = jnp.zeros((ts, D_MODEL), F32)
        for c0, c1 in _f_chunks():
            av = a_ref[:, c0:c1].astype(F32)
            uv = u_ref[:, c0:c1].astype(F32)
            sg = _sigmoid(av)
            dp = _mm_nt(df, wd_ref[c0:c1, :])
            p_ref[:, c0:c1] = (av * sg * uv).astype(MXU_DTYPE)
            da = (dp * uv * sg * (1.0 + av * (1.0 - sg))).astype(MXU_DTYPE)
            du = (dp * av * sg).astype(MXU_DTYPE)
            da_ref[:, c0:c1] = da
            du_ref[:, c0:c1] = du
            dh = dh + _mm_nt(da, wg_ref[:, c0:c1]) + _mm_nt(du, wu_ref[:, c0:c1])
        dx, dgpre = _rms_bwd(dh, n1, rs1, gpre_ref[...])
        dgpre_ref[...] += dgpre
        dx1_ref[...] = dx2 + dx

    return _call_with_rider(
        body, rider, (dx2, f, x1, a, u, gpre, wg, wu, wd, gpost), steps=s // ts,
        in_specs=[_rows(ts, D_MODEL), _rows(ts, D_MODEL), _rows(ts, D_MODEL), _rows(ts, D_FF), _rows(ts, D_FF),
                  _const((1, D_MODEL)), _const((D_MODEL, D_FF)), _const((D_MODEL, D_FF)), _const((D_FF, D_MODEL)),
                  _const((1, D_MODEL))],
        out_specs=[_rows(ts, D_MODEL), _rows(ts, D_MODEL), _rows(ts, D_MODEL), _rows(ts, D_FF), _rows(ts, D_FF),
                   _rows(ts, D_FF), _acc((1, D_MODEL)), _acc((1, D_MODEL))],
        out_shape=[_sds((s, D_MODEL)), _sds((s, D_MODEL), MXU_DTYPE), _sds((s, D_MODEL), MXU_DTYPE),
                   _sds((s, D_FF), MXU_DTYPE), _sds((s, D_FF), MXU_DTYPE), _sds((s, D_FF), MXU_DTYPE),
                   _sds((1, D_MODEL)), _sds((1, D_MODEL))],
        name="ffn_bwd", semantics="arbitrary")


def _mix_out_bwd(dx1, y, hf, hb, osum, proj, g_rnn, g_gla, w_out, gpost):
    s = y.shape[0]
    ts = _tile(s, TILE_WIDE)

    def body(dx1_ref, y_ref, hf_ref, hb_ref, gr_ref, os_ref, g_ref, grnn_ref, ggla_ref, w_ref, gp_ref,
             dm_ref, dhs_ref, dgr_ref, dos_ref, dg_ref, dgpost_ref, dgrnn_ref, dggla_ref):
        @pl.when(pl.program_id(0) == 0)
        def _():
            dgpost_ref[...] = jnp.zeros_like(dgpost_ref)
            dgrnn_ref[...] = jnp.zeros_like(dgrnn_ref)
            dggla_ref[...] = jnp.zeros_like(dggla_ref)

        _, nm, rsm = _rms_fwd(_mm(y_ref[...], w_ref[...]), gp_ref[...])
        dm, dgpost = _rms_bwd(dx1_ref[...], nm, rsm, gp_ref[...])
        dgpost_ref[...] += dgpost
        dm = dm.astype(MXU_DTYPE)
        dm_ref[...] = dm
        dy = _mm_nt(dm, w_ref[...])
        hs, gl, dgl, _, n_rnn, rs_rnn, sg, dsg, on, ns, rss = _mix_out_terms(
            hf_ref[...], hb_ref[...], gr_ref[...], os_ref[...], g_ref[...], grnn_ref[...], ggla_ref[...])
        dz, dgrnn = _rms_bwd(dy[:, :D_RNN], n_rnn, rs_rnn, grnn_ref[...])
        dgrnn_ref[...] += dgrnn
        dhs_ref[...] = dz * gl
        dgr_ref[...] = (dz * hs * dgl).astype(MXU_DTYPE)
        dyg = dy[:, D_RNN:]
        dg_ref[...] = (dyg * on * dsg).astype(MXU_DTYPE)
        don = dyg * sg
        dggla = jnp.zeros((1, GLA_DV), F32)
        for h in range(GLA_HEADS):
            ln = slice(h * LANES, (h + 1) * LANES)
            dos, dgh = _rms_bwd(don[:, ln], ns[h], rss[h], ggla_ref[...])
            dos_ref[:, ln] = dos.astype(MXU_DTYPE)
            dggla = dggla + dgh
        dggla_ref[...] += dggla

    return pl.pallas_call(
        body, grid=(s // ts,),
        in_specs=[_rows(ts, D_MODEL), _rows(ts, D_MODEL), _rows(ts, D_RNN), _rows(ts, D_RNN), _rows(ts, D_RNN, 1),
                  _rows(ts, 512), _rows(ts, 512, COL_G // 512), _const((1, D_RNN)), _const((1, GLA_DV)),
                  _const((D_MODEL, D_MODEL)), _const((1, D_MODEL))],
        out_specs=[_rows(ts, D_MODEL), _rows(ts, D_RNN), _rows(ts, D_RNN), _rows(ts, 512), _rows(ts, 512),
                   _acc((1, D_MODEL)), _acc((1, D_RNN)), _acc((1, GLA_DV))],
        out_shape=[_sds((s, D_MODEL), MXU_DTYPE), _sds((s, D_RNN)), _sds((s, D_RNN), MXU_DTYPE),
                   _sds((s, 512), MXU_DTYPE), _sds((s, 512), MXU_DTYPE),
                   _sds((1, D_MODEL)), _sds((1, D_RNN)), _sds((1, GLA_DV))],
        name="mix_out_bwd", compiler_params=_cparams(("arbitrary",)))(
            dx1, y, hf, hb, proj, osum, proj, g_rnn, g_gla, w_out, gpost)


def _gla_bwd(dos, proj, st, wg, bg, reverse, prev=None):
    s = proj.shape[0]
    ts = _tile(s, TILE_S)
    nt = s // ts
    ch = ts // GLA_CHUNK
    vw = GLA_HEADS * GLA_DV
    order = (lambda i: i) if reverse else (lambda i: nt - 1 - i)

    n_prev = 0 if prev is None else 4

    def body(do_ref, q_ref, k_ref, v_ref, lr_ref, st_ref, wg_ref, bg_ref, *rest):
        pq_ref, pk_ref, pv_ref, plr_ref = rest[:n_prev] if n_prev else (None,) * 4
        dq_ref, dk_ref, dv_ref, dlr_ref, dwg_ref, dbg_ref, ds_scr, dsa_scr = rest[n_prev:]

        def put(ref, p_ref, val):
            if p_ref is not None:
                val = val + p_ref[...].astype(F32)
            ref[...] = val.astype(ref.dtype)

        @pl.when(pl.program_id(0) == 0)
        def _():
            ds_scr[...] = jnp.zeros_like(ds_scr)
            dwg_ref[...] = jnp.zeros_like(dwg_ref)
            dbg_ref[...] = jnp.zeros_like(dbg_ref)

        z, la = _gla_gate(lr_ref[...], wg_ref[...], bg_ref[...])
        tri = _tri(reverse)
        tri_t = _tri(reverse, transpose=True)
        keep = tri > 0.5
        last_row = 0 if reverse else GLA_CHUNK - 1
        is_last = lax.broadcasted_iota(jnp.int32, (ch, GLA_CHUNK, QK_W), 1) == last_row
        first_head, own = _pair_masks()
        chunks = lambda a: a.reshape(ch, GLA_CHUNK, a.shape[-1])
        eb, enb, ebl, d, qe, ke, kd = _gla_chunk_terms(chunks(q_ref[...]), chunks(k_ref[...]), chunks(la), tri, reverse)
        v = chunks(v_ref[...])
        do = chunks(do_ref[...])
        dqe, dke, dkd, dd, dv = [], [], [], [], []
        for p in range(GLA_PAIRS):
            ln = slice(p * PAIR_K, (p + 1) * PAIR_K)
            lv = slice(p * PAIR_V, (p + 1) * PAIR_V)
            v_p, do_p = v[:, :, lv], do[:, :, lv]
            qe_p, ke_p, kd_p = qe[:, :, ln], ke[:, :, ln], kd[:, :, ln]
            grow = jnp.where(own, _bmm_tn(do_p, qe_p), 0.0)
            dst = ds_scr[p]
            for cc in range(ch):
                c = cc if reverse else (ch - 1 - cc)
                dsa_scr[c, p] = dst
                dst = grow[c] + d[c, :, ln] * dst
            ds_scr[p] = dst
            st_p = st_ref[:, p]
            dst_p = dsa_scr[:, p]
            dv_intra, dqe_intra, dke_intra = [], [], None
            for h in range(2):
                mine = first_head if h == 0 else ~first_head
                hv = slice(h * GLA_DV, (h + 1) * GLA_DV)
                q_h = jnp.where(mine, qe_p, 0.0)
                a_m = jnp.where(keep, _bmm_nt(q_h, ke_p), 0.0)
                da_m = jnp.where(keep, _bmm_nt(do_p[:, :, hv], v_p[:, :, hv]), 0.0)
                dv_intra.append(_bmm_tn(a_m, do_p[:, :, hv]))
                dqe_intra.append(_bmm(da_m, ke_p))
                dk_h = _bmm_tn(da_m, q_h)
                dke_intra = dk_h if dke_intra is None else dke_intra + dk_h
            dv.append(jnp.concatenate(dv_intra, axis=2) + _bmm_nt(kd_p, dst_p))
            dqe.append(jnp.where(first_head, dqe_intra[0], dqe_intra[1]) + _bmm(do_p, st_p))
            dke.append(dke_intra)
            dkd.append(_bmm(v_p, dst_p))
            dd.append(jnp.sum(dst_p * st_p, axis=1, keepdims=True))
        dqe = jnp.concatenate(dqe, axis=2)
        dke = jnp.concatenate(dke, axis=2)
        dkd = jnp.concatenate(dkd, axis=2)
        dd = jnp.concatenate(dd, axis=2)
        dbl = dd * d + jnp.sum(dkd * kd, axis=1, keepdims=True)
        db = dqe * qe - dke * ke - dkd * kd
        db = jnp.where(is_last, db + dbl, db)
        put(dv_ref, pv_ref, jnp.concatenate(dv, axis=2).reshape(ts, vw))
        put(dq_ref, pq_ref, (dqe * eb * (GLA_DK ** -0.5)).reshape(ts, QK_W))
        put(dk_ref, pk_ref, (dke * enb + dkd * ebl).reshape(ts, QK_W))
        dz = (_bmm_tri(tri_t, db) * (1.0 / GLA_TAU)).reshape(ts, QK_W) * _sigmoid(-z)
        put(dlr_ref, plr_ref, _mm_nt(dz, wg_ref[...]))
        dwg_ref[...] += _mm_tn(lr_ref[...], dz)
        dbg_ref[...] += jnp.sum(dz, axis=0, keepdims=True)

    wide, mid, narrow = _rows(ts, vw, 0, order), _rows(ts, QK_W, 0, order), _rows(ts, LANES, 0, order)
    return pl.pallas_call(
        body, grid=(nt,),
        in_specs=[wide] + _gla_specs(ts, order)
                 + [pl.BlockSpec((ch, GLA_PAIRS, PAIR_V, PAIR_K), lambda i: (order(i), 0, 0, 0)),
                    _const((LANES, QK_W)), _const((1, QK_W))] + ([mid, mid, wide, narrow] if n_prev else []),
        out_specs=[mid, mid, wide, narrow, _acc((LANES, QK_W)), _acc((1, QK_W))],
        out_shape=[_sds((s, QK_W), MXU_DTYPE), _sds((s, QK_W), MXU_DTYPE), _sds((s, vw), MXU_DTYPE),
                   _sds((s, LANES), MXU_DTYPE), _sds((LANES, QK_W)), _sds((1, QK_W))],
        scratch_shapes=[pltpu.VMEM((GLA_PAIRS, PAIR_V, PAIR_K), F32),
                        pltpu.VMEM((ch, GLA_PAIRS, PAIR_V, PAIR_K), F32)],
        name="gla_bwd_rev" if reverse else "gla_bwd",
        compiler_params=_cparams(("arbitrary",)))(dos, proj, proj, proj, proj, st, wg, bg, *(prev or ()))


def _rnn_bwd(dhs, h, proj, cw, cb, wa, ba, wx, bx, lam, reverse, rider=None):
    s = proj.shape[0]
    ts = _tile(s, TILE_S)
    nt = s // ts
    order = (lambda i: i) if reverse else (lambda i: nt - 1 - i)
    back = not reverse

    def body(dh_ref, h_ref, hh_ref, xr_ref, hp_ref, hn_ref, cw_ref, cb_ref, wa_ref, ba_ref, wx_ref, bx_ref, lam_ref,
             dxc_ref, dwa_ref, dba_ref, dwx_ref, dbx_ref, dlam_ref, a_scr, u_scr, g_scr, c_scr):
        i = pl.program_id(0)
        t = order(i)

        @pl.when(i == 0)
        def _():
            c_scr[...] = jnp.zeros_like(c_scr)
            dwa_ref[...] = jnp.zeros_like(dwa_ref)
            dba_ref[...] = jnp.zeros_like(dba_ref)
            dwx_ref[...] = jnp.zeros_like(dwx_ref)
            dbx_ref[...] = jnp.zeros_like(dbx_ref)
            dlam_ref[...] = jnp.zeros_like(dlam_ref)

        xc = _conv_fwd(xr_ref[...], hp_ref[...], hn_ref[...], cw_ref[...], cb_ref[...], t == 0, t == nt - 1)
        r, gi, sp, a, mult, inv_mult = _rnn_gates(xc, wa_ref[...], ba_ref[...], wx_ref[...], bx_ref[...], lam_ref[...])
        row = lax.broadcasted_iota(jnp.int32, (ts, D_RNN), 0)
        hv = h_ref[...]
        if reverse:
            edge = jnp.where(t == nt - 1, 0.0, hh_ref[0:1, :])
            h_prev = jnp.where(row == ts - 1, edge, pltpu.roll(hv, ts - 1, 0))
            a_nxt = jnp.where(row == 0, 1.0, pltpu.roll(a, 1, 0))
        else:
            edge = jnp.where(t == 0, 0.0, hh_ref[7:8, :])
            h_prev = jnp.where(row == 0, edge, pltpu.roll(hv, 1, 0))
            a_nxt = jnp.where(row == ts - 1, 1.0, pltpu.roll(a, ts - 1, 0))
        a_scr[...] = a_nxt
        u_scr[...] = dh_ref[...]
        _scan_tile(a_scr, u_scr, g_scr, c_scr[0:1, :], back)
        dh = g_scr[...]
        if reverse:
            c_scr[0:1, :] = a[ts - 1:ts, :] * dh[ts - 1:ts, :]
        else:
            c_scr[0:1, :] = a[0:1, :] * dh[0:1, :]
        dmult = dh * xc * gi
        dla = dh * h_prev * a - dmult * a * a * inv_mult
        dza = dla * (-LRU_C) * sp * r * (1.0 - r)
        dzx = dh * xc * mult * gi * (1.0 - gi)
        dsp = jnp.sum(dla * (-LRU_C) * r, axis=0, keepdims=True)
        dlam_ref[...] += dsp * (-_sigmoid(-lam_ref[...]))
        dxc_ref[...] = dh * gi * mult + _mm_nt(dza, wa_ref[...]) + _mm_nt(dzx, wx_ref[...])
        dwa_ref[...] += _mm_tn(xc, dza)
        dwx_ref[...] += _mm_tn(xc, dzx)
        dba_ref[...] += jnp.sum(dza, axis=0, keepdims=True)
        dbx_ref[...] += jnp.sum(dzx, axis=0, keepdims=True)

    hp, hn = _halo_specs(s, ts, D_RNN, 0, order)
    hhp, hhn = _halo_specs(s, ts, D_RNN, 0, order)
    sq = (D_RNN, D_RNN)
    vec = (1, D_RNN)
    return _call_with_rider(
        body, rider, (dhs, h, h, proj, proj, proj, cw, cb, wa, ba, wx, bx, lam), steps=nt,
        in_specs=[_rows(ts, D_RNN, 0, order), _rows(ts, D_RNN, 0, order), hhn if reverse else hhp,
                  _rows(ts, D_RNN, 0, order), hp, hn, _const((CONV_WIDTH, D_RNN)), _const(vec),
                  _const(sq), _const(vec), _const(sq), _const(vec), _const(vec)],
        out_specs=[_rows(ts, D_RNN, 0, order), _acc(sq), _acc(vec), _acc(sq), _acc(vec), _acc(vec)],
        out_shape=[_sds((s, D_RNN)), _sds(sq), _sds(vec), _sds(sq), _sds(vec), _sds(vec)],
        scratch_shapes=[pltpu.VMEM((ts, D_RNN), F32), pltpu.VMEM((ts, D_RNN), F32), pltpu.VMEM((ts, D_RNN), F32),
                        pltpu.VMEM((8, D_RNN), F32)],
        name="rnn_bwd_rev" if reverse else "rnn_bwd", semantics="arbitrary")


def _conv_bwd(dxc_f, dxc_b, proj, cw):
    s = proj.shape[0]
    ts = _tile(s, TILE_S)
    nt = s // ts
    ident = lambda i: i

    def body(df_ref, dfp_ref, dfn_ref, db_ref, dbp_ref, dbn_ref, xr_ref, xp_ref, xn_ref, cw_ref,
             dxr_ref, dcw_ref, dcb_ref):
        t = pl.program_id(0)

        @pl.when(t == 0)
        def _():
            dcw_ref[...] = jnp.zeros_like(dcw_ref)
            dcb_ref[...] = jnp.zeros_like(dcb_ref)

        first = t == 0
        last = t == nt - 1
        d = df_ref[...] + db_ref[...]
        d_m2, d_m1, _, d_p1 = _conv_taps(d, dfp_ref[...] + dbp_ref[...], dfn_ref[...] + dbn_ref[...], first, last)
        dn = jnp.where(last, 0.0, dfn_ref[...] + dbn_ref[...])
        d_p2 = jnp.concatenate([d, dn], axis=0)[2:2 + ts]
        del d_m2
        cw = cw_ref[...]
        dxr_ref[...] = (cw[0:1] * d_p2 + cw[1:2] * d_p1 + cw[2:3] * d + cw[3:4] * d_m1).astype(dxr_ref.dtype)
        taps = _conv_taps(xr_ref[...], xp_ref[...], xn_ref[...], first, last)
        dcw_ref[...] += jnp.concatenate([jnp.sum(d * tp, axis=0, keepdims=True) for tp in taps], axis=0)
        dcb_ref[...] += jnp.sum(d, axis=0, keepdims=True)

    hp, hn = _halo_specs(s, ts, D_RNN, 0, ident)
    return pl.pallas_call(
        body, grid=(nt,),
        in_specs=[_rows(ts, D_RNN), hp, hn, _rows(ts, D_RNN), hp, hn, _rows(ts, D_RNN), hp, hn,
                  _const((CONV_WIDTH, D_RNN))],
        out_specs=[_rows(ts, D_RNN), _acc((CONV_WIDTH, D_RNN)), _acc((1, D_RNN))],
        out_shape=[_sds((s, D_RNN), MXU_DTYPE), _sds((CONV_WIDTH, D_RNN)), _sds((1, D_RNN))],
        name="conv_bwd", compiler_params=_cparams(("arbitrary",)))(
            dxc_f, dxc_f, dxc_f, dxc_b, dxc_b, dxc_b, proj, proj, proj, cw)


def _mix_in_bwd(parts, dlr, x, dx1, gpre, w_in_p):
    s = x.shape[0]
    ts = _tile(s, TILE_WIDE)
    n_parts = len(parts)
    assert sum(p.shape[1] for p in parts) + LANES == PW

    def body(*refs):
        part_refs = refs[:n_parts + 1]
        x_ref, dx1_ref, g_ref, w_ref, dx_ref, dp_ref, h_ref, dgpre_ref = refs[n_parts + 1:]

        @pl.when(pl.program_id(0) == 0)
        def _():
            dgpre_ref[...] = jnp.zeros_like(dgpre_ref)

        dp = jnp.concatenate([r[...] for r in part_refs], axis=1)
        dp_ref[...] = dp
        h, n, rs = _rms_fwd(x_ref[...], g_ref[...])
        h_ref[...] = h.astype(MXU_DTYPE)
        dh = _mm_nt(dp, w_ref[...])
        dx, dgpre = _rms_bwd(dh, n, rs, g_ref[...])
        dgpre_ref[...] += dgpre
        dx_ref[...] = dx1_ref[...] + dx

    return pl.pallas_call(
        body, grid=(s // ts,),
        in_specs=[_rows(ts, p.shape[1]) for p in parts] + [_rows(ts, LANES), _rows(ts, D_MODEL), _rows(ts, D_MODEL),
                                                             _const((1, D_MODEL)), _const((D_MODEL, PW))],
        out_specs=[_rows(ts, D_MODEL), _rows(ts, PW), _rows(ts, D_MODEL), _acc((1, D_MODEL))],
        out_shape=[_sds((s, D_MODEL)), _sds((s, PW), MXU_DTYPE), _sds((s, D_MODEL), MXU_DTYPE), _sds((1, D_MODEL))],
        name="mix_in_bwd", compiler_params=_cparams(("arbitrary",)))(*parts, dlr, x, dx1, gpre, w_in_p)


W_IN_COLS = 2592
W_IN_SHARD = W_IN_COLS // N_DEV
FF_SHARD = D_FF // N_DEV


def _w_in_pieces():
    return [(j, 0, W_IN_SHARD, j * W_IN_SHARD) for j in range(N_DEV)]


def _w_in_from_shards(w):
    tr = 256

    def body(w_ref, o_ref):
        o_ref[...] = jnp.zeros_like(o_ref)
        for j, src, width, dst in _w_in_pieces():
            o_ref[:, dst:dst + width] = w_ref[j, :, src:src + width]

    return pl.pallas_call(
        body, grid=(D_MODEL // tr,),
        in_specs=[pl.BlockSpec((N_DEV, tr, W_IN_SHARD), lambda i: (0, i, 0))],
        out_specs=pl.BlockSpec((tr, PW), lambda i: (i, 0)), out_shape=_sds((D_MODEL, PW), w.dtype),
        name="w_in_from_shards", compiler_params=_cparams(("parallel",)))(w)


def _w_in_to_shards(g):
    tr = 256

    def body(g_ref, o_ref):
        for j, src, width, dst in _w_in_pieces():
            o_ref[j, :, src:src + width] = g_ref[:, dst:dst + width]

    return pl.pallas_call(
        body, grid=(D_MODEL // tr,),
        in_specs=[pl.BlockSpec((tr, PW), lambda i: (i, 0))],
        out_specs=pl.BlockSpec((N_DEV, tr, W_IN_SHARD), lambda i: (0, i, 0)),
        out_shape=_sds((N_DEV, D_MODEL, W_IN_SHARD), g.dtype),
        name="w_in_to_shards", compiler_params=_cparams(("parallel",)))(g)


def _cols_from_shards(w, name):
    _, d, c = w.shape
    tr = 256

    def body(w_ref, o_ref):
        for j in range(N_DEV):
            o_ref[:, j * c:(j + 1) * c] = w_ref[j]

    return pl.pallas_call(
        body, grid=(d // tr,),
        in_specs=[pl.BlockSpec((N_DEV, tr, c), lambda i: (0, i, 0))],
        out_specs=pl.BlockSpec((tr, N_DEV * c), lambda i: (i, 0)), out_shape=_sds((d, N_DEV * c), w.dtype),
        name=name, compiler_params=_cparams(("parallel",)))(w)


def _cols_to_shards(g, name):
    d, n = g.shape
    c = n // N_DEV
    tr = 256

    def body(g_ref, o_ref):
        for j in range(N_DEV):
            o_ref[j] = g_ref[:, j * c:(j + 1) * c]

    return pl.pallas_call(
        body, grid=(d // tr,),
        in_specs=[pl.BlockSpec((tr, n), lambda i: (i, 0))],
        out_specs=pl.BlockSpec((N_DEV, tr, c), lambda i: (0, i, 0)), out_shape=_sds((N_DEV, d, c), g.dtype),
        name=name, compiler_params=_cparams(("parallel",)))(g)


def _block_diag(w):
    n, b, _ = w.shape
    eye = jnp.eye(n, dtype=w.dtype)
    return (w[:, :, None, :] * eye[:, None, :, None]).reshape(n * b, n * b)


def _block_diag_of(w):
    n = D_RNN // 64
    eye = jnp.eye(n, dtype=w.dtype)
    return (w.reshape(n, 64, n, 64) * eye[:, None, :, None]).sum(axis=2)


def _gate_weight(wg, direction):
    lo = direction * GLA_RANK
    return jnp.pad(wg, ((lo, LANES - GLA_RANK - lo), (0, 0)))


def _layer_weights(full, big, l):
    row = lambda v: v.reshape(1, -1)
    lw = dict(
        gpre=row(full["mix_norm_pre"][l]), gpost=row(full["mix_norm_post"][l]),
        w_in=_w_in_from_shards(big["w_in"]),
        cw=full["conv_w"][l], cb=row(full["conv_b"][l]),
        g_rnn=row(full["rnn_out_norm"][l]), g_gla=row(full["gla_out_norm"][l]),
        w_out=big["w_out"].reshape(D_MODEL, D_MODEL),
        fpre=row(full["ffn_norm_pre"][l]), fpost=row(full["ffn_norm_post"][l]),
        wg=_cols_from_shards(big["w_ffn_gate"], "w_ffn_gate_from_shards"),
        wu=_cols_from_shards(big["w_ffn_up"], "w_ffn_up_from_shards"),
        wd=big["w_ffn_down"].reshape(D_FF, D_MODEL))
    for d in (0, 1):
        lw[f"wa{d}"] = _block_diag(full["lru_w_a"][l, d]).astype(MXU_DTYPE)
        lw[f"wx{d}"] = _block_diag(full["lru_w_x"][l, d]).astype(MXU_DTYPE)
        lw[f"ba{d}"] = row(full["lru_b_a"][l, d])
        lw[f"bx{d}"] = row(full["lru_b_x"][l, d])
        lw[f"lam{d}"] = row(full["lru_lambda"][l, d])
        lw[f"gw{d}"] = _gate_weight(full["gla_w_gate"][l, d], d).astype(MXU_DTYPE)
        lw[f"gb{d}"] = row(full["gla_b_gate"][l, d])
    return lw


def _layer_fwd(x, lw, rider=None):
    proj = _mix_in_fwd(x, lw["gpre"], lw["w_in"])
    hs, sts = [], []
    osum = None
    for d in (0, 1):
        hs.append(_rnn_fwd(proj, lw["cw"], lw["cb"], lw[f"wa{d}"], lw[f"ba{d}"], lw[f"wx{d}"], lw[f"bx{d}"],
                           lw[f"lam{d}"], bool(d)))
        osum, st = _gla_fwd(proj, lw[f"gw{d}"], lw[f"gb{d}"], bool(d), osum)
        sts.append(st)
    x1, y = _mix_out_fwd(x, hs[0], hs[1], osum, proj, lw["g_rnn"], lw["g_gla"], lw["w_out"], lw["gpost"])
    (x2, a, u, f), ridden = _ffn_fwd(x1, lw["fpre"], lw["wg"], lw["wu"], lw["wd"], lw["fpost"], rider)
    saved = dict(x=x, proj=proj, hs=hs, osum=osum, sts=sts, y=y, x1=x1, a=a, u=u, f=f)
    return x2, saved, ridden


def _layer_bwd(dx2, sv, lw, l, hooks):
    g = {}
    rider = hooks.pending_rider()
    (dx1, df, h2, p, da, du, dfpost, dfpre), ridden = _ffn_bwd(
        dx2, sv["f"], sv["x1"], sv["a"], sv["u"], lw["fpre"], lw["wg"], lw["wu"], lw["wd"], lw["fpost"], rider)
    hooks.ridden(rider, ridden)
    g["ffn_norm_post"], g["ffn_norm_pre"] = dfpost[0], dfpre[0]
    big = {}
    big["w_ffn_gate"] = _cols_to_shards(_tn_matmul(h2, da, "dw_ffn_gate"), "dw_ffn_gate_to_shards")
    big["w_ffn_up"] = _cols_to_shards(_tn_matmul(h2, du, "dw_ffn_up"), "dw_ffn_up_to_shards")
    big["w_ffn_down"] = _tn_matmul(p, df, "dw_ffn_down").reshape(N_DEV, FF_SHARD, D_MODEL)
    proj = sv["proj"]
    dm, dhs, dgr, dos, dg, dgpost, dgrnn, dggla = _mix_out_bwd(
        dx1, sv["y"], sv["hs"][0], sv["hs"][1], sv["osum"], proj, lw["g_rnn"], lw["g_gla"], lw["w_out"], lw["gpost"])
    g["mix_norm_post"], g["rnn_out_norm"], g["gla_out_norm"] = dgpost[0], dgrnn[0], dggla[0]
    big["w_out"] = _tn_matmul(sv["y"], dm, "dw_out").reshape(N_DEV, D_MODEL // N_DEV, D_MODEL)
    dxc = []
    gla = None
    gw, gb, wa, ba, wx, bx, lam = [], [], [], [], [], [], []
    for d in (0, 1):
        r = _gla_bwd(dos, proj, sv["sts"][d], lw[f"gw{d}"], lw[f"gb{d}"], bool(d), gla)
        gla = r[:4]
        lo = d * GLA_RANK
        gw.append(r[4][lo:lo + GLA_RANK])
        gb.append(r[5][0])
        early = ("w_ffn_gate", "w_ffn_up") if d == 0 else ("w_ffn_down", "w_out")
        hooks.offer(l, early, [big[n] for n in early])
        rider = hooks.pending_rider()
        r, ridden = _rnn_bwd(dhs, sv["hs"][d], proj, lw["cw"], lw["cb"], lw[f"wa{d}"], lw[f"ba{d}"], lw[f"wx{d}"],
                             lw[f"bx{d}"], lw[f"lam{d}"], bool(d), rider)
        hooks.ridden(rider, ridden)
        dxc.append(r[0])
        wa.append(_block_diag_of(r[1])); ba.append(r[2][0]); wx.append(_block_diag_of(r[3])); bx.append(r[4][0])
        lam.append(r[5][0])
    g["gla_w_gate"], g["gla_b_gate"] = jnp.stack(gw), jnp.stack(gb)
    g["lru_w_a"], g["lru_b_a"] = jnp.stack(wa), jnp.stack(ba)
    g["lru_w_x"], g["lru_b_x"], g["lru_lambda"] = jnp.stack(wx), jnp.stack(bx), jnp.stack(lam)
    dxr, dcw, dcb = _conv_bwd(dxc[0], dxc[1], proj, lw["cw"])
    g["conv_w"], g["conv_b"] = dcw, dcb[0]
    dx, dproj, h, dgpre = _mix_in_bwd((dxr, dgr, gla[0], gla[1], gla[2], dg), gla[3], sv["x"], dx1, lw["gpre"],
                                      lw["w_in"])
    g["mix_norm_pre"] = dgpre[0]
    hooks.offer(l, ("w_in",), [_w_in_to_shards(_tn_matmul(h, dproj, "dw_in", rows=1024, acc_bytes=14 * 1024 * 1024))])
    return dx, g


WEIGHT_NAMES = ["mix_norm_pre", "mix_norm_post", "w_in", "conv_w", "conv_b", "lru_w_a", "lru_b_a", "lru_w_x", "lru_b_x",
                "lru_lambda", "rnn_out_norm", "gla_w_gate", "gla_b_gate", "gla_out_norm", "w_out", "ffn_norm_pre",
                "ffn_norm_post", "w_ffn_gate", "w_ffn_up", "w_ffn_down"]
BIG_WEIGHTS = ["w_in", "w_out", "w_ffn_gate", "w_ffn_up", "w_ffn_down"]


def _local_step(x, target, full, hooks):
    saved, lws = [], []
    for l in range(DEPTH):
        lws.append(_layer_weights(full, hooks.big_weights(l), l))
        x, sv, ridden = _layer_fwd(x, lws[l], hooks.fwd_rider(l))
        hooks.fwd_ridden(l, ridden)
        saved.append(sv)
    loss, dx = _loss_fwd_bwd(x, target)
    grads = [None] * DEPTH
    for l in reversed(range(DEPTH)):
        dx, grads[l] = _layer_bwd(dx, saved[l], lws[l], l, hooks)
    g = {n: jnp.stack([grads[l][n] for l in range(DEPTH)]) for n in WEIGHT_NAMES if n not in BIG_WEIGHTS}
    return loss[0, 0], dx, g


def _all_gather(x, name):
    def body(x_ref, out_ref, send_sems, recv_sems, local_sem):
        mx, my, mc = lax.axis_index("x"), lax.axis_index("y"), lax.axis_index("c")
        me, sibling = (mx, my, mc), (mx, my, 1 - mc)
        chips = [(1 - mx, my), (mx, 1 - my), (1 - mx, 1 - my)]

        def slot(px, py, pc):
            return out_ref.at[4 * px + 2 * py + pc]

        def copy(k, block, to, src=None):
            return pltpu.make_async_remote_copy(
                src_ref=slot(*block) if src is None else src, dst_ref=slot(*block),
                send_sem=send_sems.at[k], recv_sem=recv_sems.at[k], device_id=to, device_id_type=MESH_ID)

        mine = pltpu.make_async_copy(x_ref, slot(*me), local_sem)
        mine.start()
        first = [copy(0, me, sibling, src=x_ref)]
        first += [copy(1 + j, me, (*chip, mc), src=x_ref) for j, chip in enumerate(chips)]
        for cp in first:
            cp.start()
        passed = [copy(4 + j, (*chip, mc), sibling) for j, chip in enumerate(chips)]
        for j, chip in enumerate(chips):
            copy(1 + j, (*chip, mc), me).wait_recv()
            passed[j].start()
        copy(0, sibling, me).wait_recv()
        for j, chip in enumerate(chips):
            copy(4 + j, (*chip, 1 - mc), me).wait_recv()
        for cp in first + passed:
            cp.wait_send()
        mine.wait()

    return pl.pallas_call(
        body, out_shape=_sds((N_DEV,) + x.shape, x.dtype), in_specs=[ANY], out_specs=ANY,
        scratch_shapes=[pltpu.SemaphoreType.DMA((7,)), pltpu.SemaphoreType.DMA((7,)), pltpu.SemaphoreType.DMA],
        name=name)(x)


def _all_to_all(g, name):
    def body(g_ref, out_ref, send_sems, recv_sems, local_sem):
        mx, my, mc = lax.axis_index("x"), lax.axis_index("y"), lax.axis_index("c")
        me = 4 * mx + 2 * my + mc
        mine = pltpu.make_async_copy(g_ref.at[me], out_ref.at[me], local_sem)
        mine.start()
        copies = []
        for r in range(1, N_DEV):
            px = 1 - mx if r & 4 else mx
            py = 1 - my if r & 2 else my
            pc = 1 - mc if r & 1 else mc
            cp = pltpu.make_async_remote_copy(
                src_ref=g_ref.at[4 * px + 2 * py + pc], dst_ref=out_ref.at[me],
                send_sem=send_sems.at[r - 1], recv_sem=recv_sems.at[r - 1],
                device_id=(px, py, pc), device_id_type=MESH_ID)
            cp.start()
            copies.append(cp)
        for cp in copies:
            cp.wait()
        mine.wait()

    return pl.pallas_call(
        body, out_shape=_sds(g.shape, g.dtype), in_specs=[ANY], out_specs=ANY,
        scratch_shapes=[pltpu.SemaphoreType.DMA((7,)), pltpu.SemaphoreType.DMA((7,)), pltpu.SemaphoreType.DMA],
        name=name)(g)


def _sum_slots(parts, name):
    n, r, c = parts.shape
    tr = _tile(r, ADAM_TILE_ROWS)

    def body(p_ref, g_ref):
        g = p_ref[0]
        for k in range(1, n):
            g = g + p_ref[k]
        g_ref[...] = g

    return pl.pallas_call(
        body, grid=(r // tr,), in_specs=[pl.BlockSpec((n, tr, c), lambda i: (0, i, 0))],
        out_specs=pl.BlockSpec((tr, c), lambda i: (i, 0)), out_shape=_sds((r, c)),
        name=name, compiler_params=_cparams(("parallel",)))(parts)


def _sum_adamw(parts, w, m, v, name):
    n_slots, r, c = parts.shape
    tr = _tile(r, ADAM_TILE_ROWS)

    def body(p_ref, w_ref, m_ref, v_ref, g_ref, d_ref, m2_ref, v2_ref):
        g = p_ref[0]
        for k in range(1, n_slots):
            g = g + p_ref[k]
        g_ref[...] = g
        m2 = ADAM_B1 * m_ref[...] + (1.0 - ADAM_B1) * g
        v2 = ADAM_B2 * v_ref[...] + (1.0 - ADAM_B2) * (g * g)
        m2_ref[...] = m2
        v2_ref[...] = v2
        m_hat = m2 / (1.0 - ADAM_B1 ** ADAM_STEP)
        v_hat = v2 / (1.0 - ADAM_B2 ** ADAM_STEP)
        d_ref[...] = -ADAM_LR * (m_hat / (jnp.sqrt(v_hat) + ADAM_EPS) + ADAM_WD * w_ref[...])

    flat = pl.BlockSpec((tr, c), lambda i: (i, 0))
    return pl.pallas_call(
        body, grid=(r // tr,),
        in_specs=[pl.BlockSpec((n_slots, tr, c), lambda i: (0, i, 0)), flat, flat, flat],
        out_specs=[flat] * 4, out_shape=[_sds((r, c))] * 4,
        name=name, compiler_params=_cparams(("parallel",)))(parts, w, m, v)


def _gather_big_weights(shards):
    n = len(shards)

    def body(*refs):
        srcs, outs = refs[:n], refs[n:2 * n]
        send_sems, recv_sems, local_sems = refs[2 * n:]
        mx, my, mc = lax.axis_index("x"), lax.axis_index("y"), lax.axis_index("c")
        me, sibling = (mx, my, mc), (mx, my, 1 - mc)
        chips = [(1 - mx, my), (mx, 1 - my), (1 - mx, 1 - my)]

        def slot(a, px, py, pc):
            return outs[a].at[4 * px + 2 * py + pc]

        def copy(k, a, block, to, own=False):
            return pltpu.make_async_remote_copy(
                src_ref=srcs[a] if own else slot(a, *block), dst_ref=slot(a, *block),
                send_sem=send_sems.at[k * n + a], recv_sem=recv_sems.at[k * n + a],
                device_id=to, device_id_type=MESH_ID)

        mine = [pltpu.make_async_copy(srcs[a], slot(a, *me), local_sems.at[a]) for a in range(n)]
        for cp in mine:
            cp.start()
        first = [copy(0, a, me, sibling, own=True) for a in range(n)]
        first += [copy(1 + j, a, me, (*chip, mc), own=True) for j, chip in enumerate(chips) for a in range(n)]
        for cp in first:
            cp.start()
        passed = [[copy(4 + j, a, (*chip, mc), sibling) for a in range(n)] for j, chip in enumerate(chips)]
        for j, chip in enumerate(chips):
            for a in range(n):
                copy(1 + j, a, (*chip, mc), me).wait_recv()
                passed[j][a].start()
        for a in range(n):
            copy(0, a, sibling, me).wait_recv()
        for j, chip in enumerate(chips):
            for a in range(n):
                copy(4 + j, a, (*chip, 1 - mc), me).wait_recv()
        for cp in first + [cp for row in passed for cp in row]:
            cp.wait_send()
        for cp in mine:
            cp.wait()

    return pl.pallas_call(
        body, out_shape=[_sds((N_DEV,) + s.shape, s.dtype) for s in shards],
        in_specs=[ANY] * n, out_specs=[ANY] * n,
        scratch_shapes=[pltpu.SemaphoreType.DMA((7 * n,)), pltpu.SemaphoreType.DMA((7 * n,)),
                        pltpu.SemaphoreType.DMA((n,))],
        name="gather_matmul_weights")(*shards)


def _run_alone(rider, name):
    r_in = len(rider.operands)
    r_out = len(rider.out_shape)

    def body(*refs):
        local, remote = rider.copies(refs[:r_in], refs[r_in:r_in + r_out], *refs[r_in + r_out:])
        for cp in local + remote:
            cp.start()
        for cp in remote:
            cp.wait()
        for cp in local:
            cp.wait()

    n_remote = (N_DEV - 1) * rider.n
    return pl.pallas_call(
        body, out_shape=rider.out_shape, in_specs=[ANY] * r_in, out_specs=[ANY] * r_out,
        input_output_aliases=dict(rider.aliases),
        scratch_shapes=[pltpu.SemaphoreType.DMA((n_remote,)), pltpu.SemaphoreType.DMA((n_remote,)),
                        pltpu.SemaphoreType.DMA((rider.n,))],
        name=name)(*rider.operands)


def _sum_adamw_big(parts, w, m, v, name):
    _, nl, a, b = parts.shape
    ta = _tile(a, 256)

    def body(p_ref, w_ref, m_ref, v_ref, g_ref, d_ref, m2_ref, v2_ref):
        g = p_ref[0].astype(F32)
        for k in range(1, N_DEV):
            g = g + p_ref[k].astype(F32)
        g_ref[...] = g
        m2 = ADAM_B1 * m_ref[...] + (1.0 - ADAM_B1) * g
        v2 = ADAM_B2 * v_ref[...] + (1.0 - ADAM_B2) * (g * g)
        m2_ref[...] = m2
        v2_ref[...] = v2
        m_hat = m2 / (1.0 - ADAM_B1 ** ADAM_STEP)
        v_hat = v2 / (1.0 - ADAM_B2 ** ADAM_STEP)
        d_ref[...] = -ADAM_LR * (m_hat / (jnp.sqrt(v_hat) + ADAM_EPS) + ADAM_WD * w_ref[...])

    blk = pl.BlockSpec((None, ta, b), lambda l, i: (l, i, 0))
    return pl.pallas_call(
        body, grid=(nl, a // ta),
        in_specs=[pl.BlockSpec((N_DEV, None, ta, b), lambda l, i: (0, l, i, 0)), blk, blk, blk],
        out_specs=[blk] * 4, out_shape=[_sds((nl, a, b))] * 4,
        name=name, compiler_params=_cparams(("parallel", "parallel")))(parts, w, m, v)


SMALL_SHARDED = [("conv_w", 2), ("lru_b_a", 2), ("lru_b_x", 2), ("lru_lambda", 2), ("gla_w_gate", 3), ("gla_b_gate", 2)]
REPLICATED = ["mix_norm_pre", "mix_norm_post", "conv_b", "lru_w_a", "lru_w_x", "rnn_out_norm", "gla_out_norm",
              "ffn_norm_pre", "ffn_norm_post"]


def _pack(arrays, cols, row_mult):
    flat = jnp.concatenate([a.reshape(-1) for a in arrays])
    unit = cols * row_mult
    total = -(-flat.shape[0] // unit) * unit
    return jnp.pad(flat, (0, total - flat.shape[0])).reshape(total // cols, cols)


def _pack_slots(arrays, cols, row_mult):
    flat = jnp.concatenate([a.reshape(N_DEV, -1) for a in arrays], axis=1)
    unit = cols * row_mult
    total = -(-flat.shape[1] // unit) * unit
    return jnp.pad(flat, ((0, 0), (0, total - flat.shape[1]))).reshape(N_DEV, total // cols, cols)


def _unpack(flat, shapes):
    flat = flat.reshape(-1)
    out, off = [], 0
    for sh in shapes:
        n = 1
        for d in sh:
            n *= d
        out.append(flat[off:off + n].reshape(sh))
        off += n
    return out


def _unpack_slots(flat, shapes):
    flat = flat.reshape(N_DEV, -1)
    out, off = [], 0
    for sh in shapes:
        n = 1
        for d in sh:
            n *= d
        out.append(flat[:, off:off + n].reshape((N_DEV,) + tuple(sh)))
        off += n
    return out


def _merge_shards(a, axis):
    a = jnp.moveaxis(a, 0, axis)
    sh = a.shape
    return a.reshape(sh[:axis] + (sh[axis] * sh[axis + 1],) + sh[axis + 2:])


def _split_shards(a, axis):
    sh = a.shape
    a = a.reshape(sh[:axis] + (N_DEV, sh[axis] // N_DEV) + sh[axis + 1:])
    return jnp.moveaxis(a, axis, 0)


class _StepHooks:
    def __init__(self, shards, recvs):
        self.shards = shards
        self.recvs = recvs
        self.gathered = {0: _gather_big_weights(shards[0])}
        self.pending = []

    def big_weights(self, l):
        return dict(zip(BIG_WEIGHTS, self.gathered.pop(l)))

    def fwd_rider(self, l):
        return _WeightGather(self.shards[l + 1]) if l + 1 < DEPTH else None

    def fwd_ridden(self, l, outs):
        if outs:
            self.gathered[l + 1] = outs

    def offer(self, l, names, arrays):
        self.pending += [(l, n, a) for n, a in zip(names, arrays)]

    def pending_rider(self):
        if not self.pending:
            return None
        layers, names, arrays = zip(*self.pending)
        self.pending = []
        return _GradExchange(names, arrays, [self.recvs[n] for n in names], layers)

    def ridden(self, rider, outs):
        if rider is not None:
            self.recvs.update(zip(rider.names, outs))

    def finish(self):
        rider = self.pending_rider()
        self.ridden(rider, _run_alone(rider, "exchange_last_grads"))
        return self.recvs


def kernel(x, mix_norm_pre, mix_norm_post, w_in, conv_w, conv_b, lru_w_a, lru_b_a, lru_w_x, lru_b_x, lru_lambda, rnn_out_norm, gla_w_gate, gla_b_gate, gla_out_norm, w_out, ffn_norm_pre, ffn_norm_post, w_ffn_gate, w_ffn_up, w_ffn_down, loss_target, m_mix_norm_pre, m_mix_norm_post, m_w_in, m_conv_w, m_conv_b, m_lru_w_a, m_lru_b_a, m_lru_w_x, m_lru_b_x, m_lru_lambda, m_rnn_out_norm, m_gla_w_gate, m_gla_b_gate, m_gla_out_norm, m_w_out, m_ffn_norm_pre, m_ffn_norm_post, m_w_ffn_gate, m_w_ffn_up, m_w_ffn_down, v_mix_norm_pre, v_mix_norm_post, v_w_in, v_conv_w, v_conv_b, v_lru_w_a, v_lru_b_a, v_lru_w_x, v_lru_b_x, v_lru_lambda, v_rnn_out_norm, v_gla_w_gate, v_gla_b_gate, v_gla_out_norm, v_w_out, v_ffn_norm_pre, v_ffn_norm_post, v_w_ffn_gate, v_w_ffn_up, v_w_ffn_down):
    args = dict(locals())
    w = {n: args[n] for n in WEIGHT_NAMES}
    m = {n: args["m_" + n] for n in WEIGHT_NAMES}
    v = {n: args["v_" + n] for n in WEIGHT_NAMES}
    names_s = [n for n, _ in SMALL_SHARDED]
    axis_s = dict(SMALL_SHARDED)
    shapes_s = [w[n].shape for n in names_s]

    small = _pack([w[n] for n in names_s], LANES, 8)
    small_all = _unpack_slots(_all_gather(small, "gather_small_weights"), shapes_s)
    full = {n: w[n] for n in REPLICATED}
    for n, a in zip(names_s, small_all):
        full[n] = _merge_shards(a, axis_s[n])

    hooks = _StepHooks([[w[n][l].astype(MXU_DTYPE) for n in BIG_WEIGHTS] for l in range(DEPTH)],
                       {n: jnp.zeros((N_DEV,) + w[n].shape, MXU_DTYPE) for n in BIG_WEIGHTS})
    loss, dx, g = _local_step(x[0], loss_target[0], full, hooks)
    loss = lax.psum(loss, MESH_AXES)
    res = {}
    for n, parts in hooks.finish().items():
        res[n] = _sum_adamw_big(parts, w[n], m[n], v[n], "adamw_" + n)

    g_slots = _pack_slots([_split_shards(g[n], axis_s[n]) for n in names_s], LANES, 8)
    g_recv = _all_to_all(g_slots, "exchange_small_grads")
    packed = [_pack([t[n] for n in names_s], LANES, 8) for t in (w, m, v)]
    res_s = [_unpack(r, shapes_s) for r in _sum_adamw(g_recv, *packed, "adamw_small")]
    for i, n in enumerate(names_s):
        res[n] = [res_s[k][i] for k in range(4)]

    shapes_r = [w[n].shape for n in REPLICATED]
    g_rep = _pack([g[n] for n in REPLICATED], ADAM_COLS, ADAM_TILE_ROWS)
    rows_r = g_rep.shape[0]
    g_rep = _all_to_all(g_rep.reshape(N_DEV, rows_r // N_DEV, ADAM_COLS), "exchange_replicated_grads")
    g_rep = _all_gather(_sum_slots(g_rep, "sum_replicated_grads"), "gather_replicated_grads")
    packed = [_pack([t[n] for n in REPLICATED], ADAM_COLS, ADAM_TILE_ROWS) for t in (w, m, v)]
    res_r = [_unpack(r, shapes_r)
             for r in _sum_adamw(g_rep.reshape(1, rows_r, ADAM_COLS), *packed, "adamw_replicated")]
    for i, n in enumerate(REPLICATED):
        res[n] = [res_r[k][i] for k in range(4)]

    outs = [[res[n][k] for n in WEIGHT_NAMES] for k in range(4)]
    return (loss, dx[None], *outs[0], *outs[1], *outs[2], *outs[3])
```

```python
import functools

import jax
import jax.numpy as jnp
from jax import lax
from jax.experimental import pallas as pl
from jax.experimental.pallas import tpu as pltpu

F32 = jnp.float32
MXU_DTYPE = jnp.bfloat16

N_DEV = 8
D_MODEL = 1024
D_RNN = 512
CONV_WIDTH = 4
LRU_C = 8.0
GLA_HEADS = 4
GLA_DK = 64
GLA_PAIRS = 2
PAIR_K = 128
PAIR_V = 256
GLA_DV = 128
GLA_RANK = 16
GLA_TAU = 16.0
GLA_CHUNK = 64
D_FF = 2816
RMS_EPS = 1e-6
TINY = 1e-30
DEPTH = 4

PW = 2688
COL_Q, COL_K, COL_V, COL_G, COL_LR = 1024, 1280, 1536, 2048, 2560
QK_W = GLA_HEADS * GLA_DK
LANES = 128

TILE_S = 1024
TILE_WIDE = 512
TILE_F = 256
TILE_F_FWD = 512
TILE_TN = 2048
TN_ACC_BYTES = 6 * 1024 * 1024
F_CHUNK = 1408
VMEM_LIMIT = 56 * 1024 * 1024

ADAM_LR = 0.001
ADAM_B1 = 0.9
ADAM_B2 = 0.999
ADAM_EPS = 1e-08
ADAM_WD = 0.01
ADAM_STEP = 10

ADAM_TILE_ROWS = 256
ADAM_COLS = 1024


def _mm(a, b):
    return jnp.dot(a.astype(MXU_DTYPE), b.astype(MXU_DTYPE), preferred_element_type=F32)


def _mm_nt(a, b):
    return lax.dot_general(a.astype(MXU_DTYPE), b.astype(MXU_DTYPE), (((1,), (1,)), ((), ())),
                           preferred_element_type=F32)


def _mm_tn(a, b):
    return lax.dot_general(a.astype(MXU_DTYPE), b.astype(MXU_DTYPE), (((0,), (0,)), ((), ())),
                           preferred_element_type=F32)


def _bmm(a, b):
    return lax.dot_general(a.astype(MXU_DTYPE), b.astype(MXU_DTYPE), (((2,), (1,)), ((0,), (0,))),
                           preferred_element_type=F32)


def _bmm_nt(a, b):
    return lax.dot_general(a.astype(MXU_DTYPE), b.astype(MXU_DTYPE), (((2,), (2,)), ((0,), (0,))),
                           preferred_element_type=F32)


def _bmm_tn(a, b):
    return lax.dot_general(a.astype(MXU_DTYPE), b.astype(MXU_DTYPE), (((1,), (1,)), ((0,), (0,))),
                           preferred_element_type=F32)


def _bmm_tri(tri, x):
    t = jnp.broadcast_to(tri.astype(jnp.bfloat16)[None], (x.shape[0],) + tri.shape)
    hi = x.astype(jnp.bfloat16)
    r1 = x - hi.astype(F32)
    mid = r1.astype(jnp.bfloat16)
    lo = (r1 - mid.astype(F32)).astype(jnp.bfloat16)
    dot = lambda v: lax.dot_general(t, v, (((2,), (1,)), ((0,), (0,))), preferred_element_type=F32)
    return dot(hi) + dot(mid) + dot(lo)


def _sigmoid(x):
    return 0.5 * jnp.tanh(0.5 * x) + 0.5


def _log1p_pos(e):
    series = e * (1.0 - e * (0.5 - e * (1.0 / 3.0 - e * 0.25)))
    return jnp.where(e < 0.01, series, jnp.log(1.0 + e))


def _softplus(x):
    return jnp.maximum(x, 0.0) + _log1p_pos(jnp.exp(-jnp.abs(x)))


def _softplus_coarse(x):
    return jnp.maximum(x, 0.0) + jnp.log(1.0 + jnp.exp(-jnp.abs(x)))


GELU_C = 0.7978845608028654
GELU_K = 0.044715


def _gelu_and_grad(x):
    t = jnp.tanh(GELU_C * (x + GELU_K * x * x * x))
    y = 0.5 * x * (1.0 + t)
    dy = 0.5 * (1.0 + t) + 0.5 * x * (1.0 - t * t) * GELU_C * (1.0 + 3.0 * GELU_K * x * x)
    return y, dy


def _rms_fwd(x, g):
    rs = lax.rsqrt(jnp.mean(x * x, axis=-1, keepdims=True) + RMS_EPS)
    n = x * rs
    return n * g, n, rs


def _rms_bwd(dy, n, rs, g):
    dn = dy * g
    dx = rs * (dn - n * jnp.mean(dn * n, axis=-1, keepdims=True))
    dg = jnp.sum(dy * n, axis=0, keepdims=True)
    return dx, dg


def _cparams(sem=None):
    kw = dict(vmem_limit_bytes=VMEM_LIMIT)
    if sem is not None:
        kw["dimension_semantics"] = sem
    return pltpu.CompilerParams(**kw)


def _tile(n, pref):
    return pref if n % pref == 0 else n


def _const(shape):
    nd = len(shape)
    return pl.BlockSpec(shape, lambda *_: (0,) * nd, pipeline_mode=pl.Buffered(1))


def _acc(shape):
    nd = len(shape)
    return pl.BlockSpec(shape, lambda *_: (0,) * nd)


def _rows(ts, w, col=0, order=None):
    if order is None:
        return pl.BlockSpec((ts, w), lambda i: (i, col))
    return pl.BlockSpec((ts, w), lambda i: (order(i), col))


def _sds(shape, dtype=F32):
    return jax.ShapeDtypeStruct(shape, dtype)


MESH_ID = pl.DeviceIdType.MESH
ANY = pl.BlockSpec(memory_space=pl.ANY)
MESH_AXES = ("x", "y", "c")


def _peers():
    mx, my, mc = lax.axis_index("x"), lax.axis_index("y"), lax.axis_index("c")
    peers = []
    for r in range(1, N_DEV):
        px = 1 - mx if r & 4 else mx
        py = 1 - my if r & 2 else my
        pc = 1 - mc if r & 1 else mc
        peers.append((4 * px + 2 * py + pc, (px, py, pc)))
    return 4 * mx + 2 * my + mc, peers


class _GradExchange:
    def __init__(self, names, srcs, recvs, layers):
        self.names = list(names)
        self.n = len(srcs)
        self.layers = list(layers)
        self.operands = list(srcs) + list(recvs)
        self.out_shape = [_sds(r.shape, r.dtype) for r in recvs]
        self.aliases = {self.n + a: a for a in range(self.n)}

    def copies(self, ins, outs, send_sems, recv_sems, local_sems):
        n, layers = self.n, self.layers
        me, peers = _peers()
        local = [pltpu.make_async_copy(ins[a].at[me], outs[a].at[me, layers[a]], local_sems.at[a]) for a in range(n)]
        remote = [pltpu.make_async_remote_copy(
            src_ref=ins[a].at[slot], dst_ref=outs[a].at[me, layers[a]],
            send_sem=send_sems.at[r * n + a], recv_sem=recv_sems.at[r * n + a],
            device_id=dev, device_id_type=MESH_ID) for r, (slot, dev) in enumerate(peers) for a in range(n)]
        return local, remote


class _WeightGather:
    def __init__(self, shards):
        self.n = len(shards)
        self.operands = list(shards)
        self.out_shape = [_sds((N_DEV,) + s.shape, s.dtype) for s in shards]
        self.aliases = {}

    def copies(self, ins, outs, send_sems, recv_sems, local_sems):
        n = self.n
        me, peers = _peers()
        local = [pltpu.make_async_copy(ins[a], outs[a].at[me], local_sems.at[a]) for a in range(n)]
        remote = [pltpu.make_async_remote_copy(
            src_ref=ins[a], dst_ref=outs[a].at[me],
            send_sem=send_sems.at[r * n + a], recv_sem=recv_sems.at[r * n + a],
            device_id=dev, device_id_type=MESH_ID) for r, (_, dev) in enumerate(peers) for a in range(n)]
        return local, remote


def _call_with_rider(body, rider, operands, *, steps, in_specs, out_specs, out_shape, scratch_shapes=(), name,
                     semantics):
    if rider is None:
        outs = pl.pallas_call(body, grid=(steps,), in_specs=in_specs, out_specs=out_specs, out_shape=out_shape,
                              scratch_shapes=list(scratch_shapes), name=name,
                              compiler_params=_cparams((semantics,)))(*operands)
        return outs, []
    n_in, n_out, n_scr = len(in_specs), len(out_specs), len(scratch_shapes)
    r_in, r_out = len(rider.operands), len(rider.out_shape)

    def riding(*refs):
        own_in, ride_in = refs[:n_in], refs[n_in:n_in + r_in]
        refs = refs[n_in + r_in:]
        own_out, ride_out = refs[:n_out], refs[n_out:n_out + r_out]
        refs = refs[n_out + r_out:]
        own_scr, sems = refs[:n_scr], refs[n_scr:]
        i = pl.program_id(0)

        @pl.when(i == 0)
        def _():
            local, remote = rider.copies(ride_in, ride_out, *sems)
            for cp in local + remote:
                cp.start()

        body(*own_in, *own_out, *own_scr)

        @pl.when(i == steps - 1)
        def _():
            local, remote = rider.copies(ride_in, ride_out, *sems)
            for cp in remote:
                cp.wait()
            for cp in local:
                cp.wait()

    n_remote = (N_DEV - 1) * rider.n
    outs = pl.pallas_call(
        riding, grid=(steps,), in_specs=list(in_specs) + [ANY] * r_in, out_specs=list(out_specs) + [ANY] * r_out,
        out_shape=list(out_shape) + rider.out_shape,
        input_output_aliases={n_in + i: n_out + o for i, o in rider.aliases.items()},
        scratch_shapes=list(scratch_shapes) + [pltpu.SemaphoreType.DMA((n_remote,)),
                                               pltpu.SemaphoreType.DMA((n_remote,)),
                                               pltpu.SemaphoreType.DMA((rider.n,))],
        name=name, compiler_params=_cparams(("arbitrary",)))(*operands, *rider.operands)
    return outs[:n_out], outs[n_out:]


def _mix_in_fwd(x, gpre, w_in_p, rider=None):
    s = x.shape[0]
    ts = _tile(s, TILE_S)

    def body(x_ref, g_ref, w_ref, o_ref):
        h, _, _ = _rms_fwd(x_ref[...], g_ref[...])
        o_ref[...] = _mm(h, w_ref[...])

    (proj,), ridden = _call_with_rider(
        body, rider, (x, gpre, w_in_p), steps=s // ts,
        in_specs=[_rows(ts, D_MODEL), _const((1, D_MODEL)), _const((D_MODEL, PW))],
        out_specs=[_rows(ts, PW)], out_shape=[_sds((s, PW))], name="mix_in_fwd", semantics="parallel")
    return proj, ridden


def _conv_taps(xr, hp, hn, first, last):
    ts = xr.shape[0]
    hp = jnp.where(first, 0.0, hp)
    hn = jnp.where(last, 0.0, hn)
    xe = jnp.concatenate([hp, xr, hn], axis=0)
    return xe[6:6 + ts], xe[7:7 + ts], xr, xe[9:9 + ts]


def _conv_fwd(xr, hp, hn, cw, cb, first, last):
    t0, t1, t2, t3 = _conv_taps(xr, hp, hn, first, last)
    return cw[0:1] * t0 + cw[1:2] * t1 + cw[2:3] * t2 + cw[3:4] * t3 + cb


def _rnn_gates(xc, wa, ba, wx, bx, lam):
    r = _sigmoid(_mm(xc, wa) + ba)
    i = _sigmoid(_mm(xc, wx) + bx)
    sp = _softplus(-lam)
    la = (-LRU_C) * r * sp
    a = jnp.exp(la)
    one_minus_a2 = -jnp.tanh(la) * (a * a + 1.0)
    inv_mult = lax.rsqrt(jnp.maximum(one_minus_a2, TINY))
    return r, i, sp, a, one_minus_a2 * inv_mult, inv_mult


def _scan_tile(a_scr, u_scr, h_ref, c0, reverse):
    ts = a_scr.shape[0]
    a = a_scr[...]
    u = u_scr[...]
    row = lax.broadcasted_iota(jnp.int32, a.shape, 0) % 8
    for k in (1, 2, 4):
        if reverse:
            a_sh = pltpu.roll(a, ts - k, 0)
            u_sh = pltpu.roll(u, ts - k, 0)
            ok = row < 8 - k
        else:
            a_sh = pltpu.roll(a, k, 0)
            u_sh = pltpu.roll(u, k, 0)
            ok = row >= k
        u = jnp.where(ok, u + a * u_sh, u)
        a = jnp.where(ok, a * a_sh, a)
    a_scr[...] = a
    u_scr[...] = u
    ng = ts // 8

    def body(j, c):
        g = (ng - 1 - j) if reverse else j
        sl = pl.ds(pl.multiple_of(g * 8, 8), 8)
        hh = u_scr[sl, :] + a_scr[sl, :] * c
        h_ref[sl, :] = hh
        return hh[0:1, :] if reverse else hh[7:8, :]

    return lax.fori_loop(0, ng, body, c0)


def _halo_specs(s, ts, w, col, order):
    n8 = s // 8
    per = ts // 8
    prev = pl.BlockSpec((8, w), lambda i: (jnp.maximum(order(i) * per - 1, 0), col))
    nxt = pl.BlockSpec((8, w), lambda i: (jnp.minimum((order(i) + 1) * per, n8 - 1), col))
    return prev, nxt


def _rnn_fwd(proj, cw, cb, wa, ba, wx, bx, lam, reverse, rider=None):
    s = proj.shape[0]
    ts = _tile(s, TILE_S)
    nt = s // ts
    order = (lambda i: nt - 1 - i) if reverse else (lambda i: i)

    def body(xr_ref, hp_ref, hn_ref, cw_ref, cb_ref, wa_ref, ba_ref, wx_ref, bx_ref, lam_ref,
             h_ref, a_scr, u_scr, c_scr):
        i = pl.program_id(0)
        t = order(i)

        @pl.when(i == 0)
        def _():
            c_scr[...] = jnp.zeros_like(c_scr)

        xc = _conv_fwd(xr_ref[...], hp_ref[...], hn_ref[...], cw_ref[...], cb_ref[...], t == 0, t == nt - 1)
        _, gi, _, a, mult, _ = _rnn_gates(xc, wa_ref[...], ba_ref[...], wx_ref[...], bx_ref[...], lam_ref[...])
        a_scr[...] = a
        u_scr[...] = xc * gi * mult
        c_scr[0:1, :] = _scan_tile(a_scr, u_scr, h_ref, c_scr[0:1, :], reverse)

    hp, hn = _halo_specs(s, ts, D_RNN, 0, order)
    (h,), ridden = _call_with_rider(
        body, rider, (proj, proj, proj, cw, cb, wa, ba, wx, bx, lam), steps=nt,
        in_specs=[_rows(ts, D_RNN, 0, order), hp, hn, _const((CONV_WIDTH, D_RNN)), _const((1, D_RNN)),
                  _const((D_RNN, D_RNN)), _const((1, D_RNN)), _const((D_RNN, D_RNN)), _const((1, D_RNN)),
                  _const((1, D_RNN))],
        out_specs=[_rows(ts, D_RNN, 0, order)], out_shape=[_sds((s, D_RNN))],
        scratch_shapes=[pltpu.VMEM((ts, D_RNN), F32), pltpu.VMEM((ts, D_RNN), F32), pltpu.VMEM((8, D_RNN), F32)],
        name="rnn_fwd_rev" if reverse else "rnn_fwd", semantics="arbitrary")
    return h, ridden


def _tri(reverse, transpose=False):
    r = lax.broadcasted_iota(jnp.int32, (GLA_CHUNK, GLA_CHUNK), 0)
    c = lax.broadcasted_iota(jnp.int32, (GLA_CHUNK, GLA_CHUNK), 1)
    if transpose:
        r, c = c, r
    return ((r <= c) if reverse else (r >= c)).astype(F32)


def _gla_chunk_terms(q, k, la, tri, reverse):
    b = _bmm_tri(tri, la)
    bl = b[:, 0:1] if reverse else b[:, GLA_CHUNK - 1:GLA_CHUNK]
    eb = jnp.exp(b)
    enb = jnp.exp(-b)
    ebl = jnp.exp(bl - b)
    d = jnp.exp(bl)
    return eb, enb, ebl, d, q * (GLA_DK ** -0.5) * eb, k * enb, k * ebl


def _gla_gate(lr, wg, bg):
    z = _mm(lr, wg) + bg
    return z, -_softplus_coarse(-z) * (1.0 / GLA_TAU)


def _pair_masks():
    first_head = lax.broadcasted_iota(jnp.int32, (1, PAIR_K), 1) < GLA_DK
    row_first = lax.broadcasted_iota(jnp.int32, (PAIR_V, PAIR_K), 0) < GLA_DV
    lane_first = lax.broadcasted_iota(jnp.int32, (PAIR_V, PAIR_K), 1) < GLA_DK
    return first_head, row_first == lane_first


def _gla_specs(ts, order):
    return [_rows(ts, QK_W, COL_Q // QK_W, order), _rows(ts, QK_W, COL_K // QK_W, order),
            _rows(ts, GLA_HEADS * GLA_DV, COL_V // (GLA_HEADS * GLA_DV), order),
            _rows(ts, LANES, COL_LR // LANES, order)]


def _gla_fwd(proj, wg, bg, reverse, o_add=None):
    s = proj.shape[0]
    ts = _tile(s, TILE_S)
    nt = s // ts
    ch = ts // GLA_CHUNK
    vw = GLA_HEADS * GLA_DV
    order = (lambda i: nt - 1 - i) if reverse else (lambda i: i)
    extra = [] if o_add is None else [o_add]

    def body(q_ref, k_ref, v_ref, lr_ref, wg_ref, bg_ref, *rest):
        add_ref = None if o_add is None else rest[0]
        o_ref, st_ref, s_scr = rest[len(extra):]

        @pl.when(pl.program_id(0) == 0)
        def _():
            s_scr[...] = jnp.zeros_like(s_scr)

        _, la = _gla_gate(lr_ref[...], wg_ref[...], bg_ref[...])
        tri = _tri(reverse)
        keep = tri > 0.5
        first_head, own = _pair_masks()
        chunks = lambda a: a.reshape(ch, GLA_CHUNK, a.shape[-1])
        _, _, _, d, qe, ke, kd = _gla_chunk_terms(chunks(q_ref[...]), chunks(k_ref[...]), chunks(la), tri, reverse)
        v = chunks(v_ref[...])
        outs = []
        for p in range(GLA_PAIRS):
            ln = slice(p * PAIR_K, (p + 1) * PAIR_K)
            v_p = v[:, :, p * PAIR_V:(p + 1) * PAIR_V]
            qe_p, ke_p = qe[:, :, ln], ke[:, :, ln]
            grow = jnp.where(own, _bmm_tn(v_p, kd[:, :, ln]), 0.0)
            st = s_scr[p]
            for cc in range(ch):
                c = (ch - 1 - cc) if reverse else cc
                st_ref[c, p] = st
                st = d[c, :, ln] * st + grow[c]
            s_scr[p] = st
            intra = []
            for h in range(2):
                q_h = jnp.where(first_head if h == 0 else ~first_head, qe_p, 0.0)
                a_m = jnp.where(keep, _bmm_nt(q_h, ke_p), 0.0)
                intra.append(_bmm(a_m, v_p[:, :, h * GLA_DV:(h + 1) * GLA_DV]))
            outs.append(_bmm_nt(qe_p, st_ref[:, p]) + jnp.concatenate(intra, axis=2))
        o = jnp.concatenate(outs, axis=2).reshape(ts, vw)
        o_ref[...] = o if add_ref is None else o + add_ref[...]

    return pl.pallas_call(
        body, grid=(nt,),
        in_specs=_gla_specs(ts, order) + [_const((LANES, QK_W)), _const((1, QK_W))]
                 + [_rows(ts, vw, 0, order)] * len(extra),
        out_specs=[_rows(ts, vw, 0, order),
                   pl.BlockSpec((ch, GLA_PAIRS, PAIR_V, PAIR_K), lambda i: (order(i), 0, 0, 0))],
        out_shape=[_sds((s, vw)), _sds((s // GLA_CHUNK, GLA_PAIRS, PAIR_V, PAIR_K))],
        scratch_shapes=[pltpu.VMEM((GLA_PAIRS, PAIR_V, PAIR_K), F32)],
        name="gla_fwd_rev" if reverse else "gla_fwd",
        compiler_params=_cparams(("arbitrary",)))(proj, proj, proj, proj, wg, bg, *extra)


def _mix_out_terms(hf, hb, gate_r, osum, g, g_rnn, g_gla):
    hs = hf + hb
    gl, dgl = _gelu_and_grad(gate_r)
    z = hs * gl
    y_rnn, n_rnn, rs_rnn = _rms_fwd(z, g_rnn)
    sg_lin = _sigmoid(g)
    sg = g * sg_lin
    dsg = sg_lin * (1.0 + g * (1.0 - sg_lin))
    ons, ns, rss = [], [], []
    for h in range(GLA_HEADS):
        ln = slice(h * LANES, (h + 1) * LANES)
        on, n, rs = _rms_fwd(osum[:, ln], g_gla)
        ons.append(on)
        ns.append(n)
        rss.append(rs)
    on = jnp.concatenate(ons, axis=1)
    return hs, gl, dgl, y_rnn, n_rnn, rs_rnn, sg, dsg, on, ns, rss


def _mix_out_fwd(x, hf, hb, osum, proj, g_rnn, g_gla, w_out, gpost):
    s = x.shape[0]
    ts = _tile(s, TILE_S)

    def body(x_ref, hf_ref, hb_ref, gr_ref, os_ref, g_ref, grnn_ref, ggla_ref, w_ref, gp_ref, x1_ref, y_ref):
        _, _, _, y_rnn, _, _, sg, _, on, _, _ = _mix_out_terms(
            hf_ref[...], hb_ref[...], gr_ref[...], os_ref[...], g_ref[...], grnn_ref[...], ggla_ref[...])
        y = jnp.concatenate([y_rnn, on * sg], axis=1).astype(MXU_DTYPE)
        y_ref[...] = y
        out, _, _ = _rms_fwd(_mm(y, w_ref[...]), gp_ref[...])
        x1_ref[...] = x_ref[...] + out

    return pl.pallas_call(
        body, grid=(s // ts,),
        in_specs=[_rows(ts, D_MODEL), _rows(ts, D_RNN), _rows(ts, D_RNN), _rows(ts, D_RNN, 1), _rows(ts, 512),
                  _rows(ts, 512, COL_G // 512), _const((1, D_RNN)), _const((1, GLA_DV)),
                  _const((D_MODEL, D_MODEL)), _const((1, D_MODEL))],
        out_specs=[_rows(ts, D_MODEL), _rows(ts, D_MODEL)],
        out_shape=[_sds((s, D_MODEL)), _sds((s, D_MODEL), MXU_DTYPE)],
        name="mix_out_fwd", compiler_params=_cparams(("parallel",)))(
            x, hf, hb, proj, osum, proj, g_rnn, g_gla, w_out, gpost)


def _f_chunks():
    return [(c0, min(c0 + F_CHUNK, D_FF)) for c0 in range(0, D_FF, F_CHUNK)]


def _ffn_fwd(x1, gpre, wg, wu, wd, gpost, rider=None):
    s = x1.shape[0]
    ts = _tile(s, TILE_F_FWD)

    def body(x_ref, gpre_ref, wg_ref, wu_ref, wd_ref, gpost_ref, x2_ref, a_ref, u_ref, f_ref):
        x = x_ref[...]
        h, _, _ = _rms_fwd(x, gpre_ref[...])
        h = h.astype(MXU_DTYPE)
        f = jnp.zeros((ts, D_MODEL), F32)
        for c0, c1 in _f_chunks():
            a = _mm(h, wg_ref[:, c0:c1])
            u = _mm(h, wu_ref[:, c0:c1])
            a_ref[:, c0:c1] = a.astype(MXU_DTYPE)
            u_ref[:, c0:c1] = u.astype(MXU_DTYPE)
            f = f + _mm(a * _sigmoid(a) * u, wd_ref[c0:c1, :])
        f_ref[...] = f
        out, _, _ = _rms_fwd(f, gpost_ref[...])
        x2_ref[...] = x + out

    return _call_with_rider(
        body, rider, (x1, gpre, wg, wu, wd, gpost), steps=s // ts,
        in_specs=[_rows(ts, D_MODEL), _const((1, D_MODEL)), _const((D_MODEL, D_FF)), _const((D_MODEL, D_FF)),
                  _const((D_FF, D_MODEL)), _const((1, D_MODEL))],
        out_specs=[_rows(ts, D_MODEL), _rows(ts, D_FF), _rows(ts, D_FF), _rows(ts, D_MODEL)],
        out_shape=[_sds((s, D_MODEL)), _sds((s, D_FF), MXU_DTYPE), _sds((s, D_FF), MXU_DTYPE), _sds((s, D_MODEL))],
        name="ffn_fwd", semantics="parallel")


def _loss_fwd_bwd(y, target):
    s = y.shape[0]
    ts = _tile(s, TILE_S)

    def body(y_ref, t_ref, loss_ref, dy_ref):
        @pl.when(pl.program_id(0) == 0)
        def _():
            loss_ref[...] = jnp.zeros_like(loss_ref)

        e = y_ref[...] - t_ref[...]
        dy_ref[...] = e * (1.0 / D_MODEL)
        part = jnp.sum(jnp.sum(e * e, axis=1, keepdims=True), axis=0, keepdims=True) * (0.5 / D_MODEL)
        loss_ref[...] += jnp.broadcast_to(part, loss_ref.shape)

    return pl.pallas_call(
        body, grid=(s // ts,),
        in_specs=[_rows(ts, D_MODEL), _rows(ts, D_MODEL)],
        out_specs=[_acc((8, LANES)), _rows(ts, D_MODEL)],
        out_shape=[_sds((8, LANES)), _sds((s, D_MODEL))],
        name="loss", compiler_params=_cparams(("arbitrary",)))(y, target)


def _tn_matmul(a, b, name, rows=TILE_TN, acc_bytes=TN_ACC_BYTES):
    s, k = a.shape
    n = b.shape[1]
    ts = _tile(s, rows)
    tn = max(t for t in range(LANES, n + 1, LANES) if n % t == 0 and (k * t * 4 <= acc_bytes or t == LANES))
    ns = s // ts

    def body(a_ref, b_ref, o_ref, acc):
        i = pl.program_id(1)

        @pl.when(i == 0)
        def _():
            acc[...] = jnp.zeros_like(acc)

        acc[...] += _mm_tn(a_ref[...], b_ref[...])

        @pl.when(i == ns - 1)
        def _():
            o_ref[...] = acc[...].astype(o_ref.dtype)

    return pl.pallas_call(
        body, grid=(n // tn, ns),
        in_specs=[pl.BlockSpec((ts, k), lambda j, i: (i, 0)), pl.BlockSpec((ts, tn), lambda j, i: (i, j))],
        out_specs=pl.BlockSpec((k, tn), lambda j, i: (0, j)), out_shape=_sds((k, n), MXU_DTYPE),
        scratch_shapes=[pltpu.VMEM((k, tn), F32)],
        name=name, compiler_params=_cparams(("parallel", "arbitrary")))(a, b)


def _ffn_bwd(dx2, f, x1, a, u, gpre, wg, wu, wd, gpost, rider=None):
    s = x1.shape[0]
    ts = _tile(s, TILE_F)

    def body(dx2_ref, f_ref, x1_ref, a_ref, u_ref, gpre_ref, wg_ref, wu_ref, wd_ref, gpost_ref,
             dx1_ref, df_ref, h_ref, p_ref, da_ref, du_ref, dgpost_ref, dgpre_ref):
        @pl.when(pl.program_id(0) == 0)
        def _():
            dgpost_ref[...] = jnp.zeros_like(dgpost_ref)
            dgpre_ref[...] = jnp.zeros_like(dgpre_ref)

        dx2 = dx2_ref[...]
        _, nf, rsf = _rms_fwd(f_ref[...], gpost_ref[...])
        df, dgpost = _rms_bwd(dx2, nf, rsf, gpost_ref[...])
        dgpost_ref[...] += dgpost
        df = df.astype(MXU_DTYPE)
        df_ref[...] = df
        h, n1, rs1 = _rms_fwd(x1_ref[...], gpre_ref[...])
        h_ref[...] = h.astype(MXU_DTYPE)
        dh = jnp.zeros((ts, D_MODEL), F32)
        for c0, c1 in _f_chunks():
            av = a_ref[:, c0:c1].astype(F32)
            uv = u_ref[:, c0:c1].astype(F32)
            sg = _sigmoid(av)
            dp = _mm_nt(df, wd_ref[c0:c1, :])
            p_ref[:, c0:c1] = (av * sg * uv).astype(MXU_DTYPE)
            da = (dp * uv * sg * (1.0 + av * (1.0 - sg))).astype(MXU_DTYPE)
            du = (dp * av * sg).astype(MXU_DTYPE)
            da_ref[:, c0:c1] = da
            du_ref[:, c0:c1] = du
            dh = dh + _mm_nt(da, wg_ref[:, c0:c1]) + _mm_nt(du, wu_ref[:, c0:c1])
        dx, dgpre = _rms_bwd(dh, n1, rs1, gpre_ref[...])
        dgpre_ref[...] += dgpre
        dx1_ref[...] = dx2 + dx

    return _call_with_rider(
        body, rider, (dx2, f, x1, a, u, gpre, wg, wu, wd, gpost), steps=s // ts,
        in_specs=[_rows(ts, D_MODEL), _rows(ts, D_MODEL), _rows(ts, D_MODEL), _rows(ts, D_FF), _rows(ts, D_FF),
                  _const((1, D_MODEL)), _const((D_MODEL, D_FF)), _const((D_MODEL, D_FF)), _const((D_FF, D_MODEL)),
                  _const((1, D_MODEL))],
        out_specs=[_rows(ts, D_MODEL), _rows(ts, D_MODEL), _rows(ts, D_MODEL), _rows(ts, D_FF), _rows(ts, D_FF),
                   _rows(ts, D_FF), _acc((1, D_MODEL)), _acc((1, D_MODEL))],
        out_shape=[_sds((s, D_MODEL)), _sds((s, D_MODEL), MXU_DTYPE), _sds((s, D_MODEL), MXU_DTYPE),
                   _sds((s, D_FF), MXU_DTYPE), _sds((s, D_FF), MXU_DTYPE), _sds((s, D_FF), MXU_DTYPE),
                   _sds((1, D_MODEL)), _sds((1, D_MODEL))],
        name="ffn_bwd", semantics="arbitrary")


def _mix_out_bwd(dx1, y, hf, hb, osum, proj, g_rnn, g_gla, w_out, gpost):
    s = y.shape[0]
    ts = _tile(s, TILE_WIDE)

    def body(dx1_ref, y_ref, hf_ref, hb_ref, gr_ref, os_ref, g_ref, grnn_ref, ggla_ref, w_ref, gp_ref,
             dm_ref, dhs_ref, dgr_ref, dos_ref, dg_ref, dgpost_ref, dgrnn_ref, dggla_ref):
        @pl.when(pl.program_id(0) == 0)
        def _():
            dgpost_ref[...] = jnp.zeros_like(dgpost_ref)
            dgrnn_ref[...] = jnp.zeros_like(dgrnn_ref)
            dggla_ref[...] = jnp.zeros_like(dggla_ref)

        _, nm, rsm = _rms_fwd(_mm(y_ref[...], w_ref[...]), gp_ref[...])
        dm, dgpost = _rms_bwd(dx1_ref[...], nm, rsm, gp_ref[...])
        dgpost_ref[...] += dgpost
        dm = dm.astype(MXU_DTYPE)
        dm_ref[...] = dm
        dy = _mm_nt(dm, w_ref[...])
        hs, gl, dgl, _, n_rnn, rs_rnn, sg, dsg, on, ns, rss = _mix_out_terms(
            hf_ref[...], hb_ref[...], gr_ref[...], os_ref[...], g_ref[...], grnn_ref[...], ggla_ref[...])
        dz, dgrnn = _rms_bwd(dy[:, :D_RNN], n_rnn, rs_rnn, grnn_ref[...])
        dgrnn_ref[...] += dgrnn
        dhs_ref[...] = dz * gl
        dgr_ref[...] = (dz * hs * dgl).astype(MXU_DTYPE)
        dyg = dy[:, D_RNN:]
        dg_ref[...] = (dyg * on * dsg).astype(MXU_DTYPE)
        don = dyg * sg
        dggla = jnp.zeros((1, GLA_DV), F32)
        for h in range(GLA_HEADS):
            ln = slice(h * LANES, (h + 1) * LANES)
            dos, dgh = _rms_bwd(don[:, ln], ns[h], rss[h], ggla_ref[...])
            dos_ref[:, ln] = dos.astype(MXU_DTYPE)
            dggla = dggla + dgh
        dggla_ref[...] += dggla

    return pl.pallas_call(
        body, grid=(s // ts,),
        in_specs=[_rows(ts, D_MODEL), _rows(ts, D_MODEL), _rows(ts, D_RNN), _rows(ts, D_RNN), _rows(ts, D_RNN, 1),
                  _rows(ts, 512), _rows(ts, 512, COL_G // 512), _const((1, D_RNN)), _const((1, GLA_DV)),
                  _const((D_MODEL, D_MODEL)), _const((1, D_MODEL))],
        out_specs=[_rows(ts, D_MODEL), _rows(ts, D_RNN), _rows(ts, D_RNN), _rows(ts, 512), _rows(ts, 512),
                   _acc((1, D_MODEL)), _acc((1, D_RNN)), _acc((1, GLA_DV))],
        out_shape=[_sds((s, D_MODEL), MXU_DTYPE), _sds((s, D_RNN)), _sds((s, D_RNN), MXU_DTYPE),
                   _sds((s, 512), MXU_DTYPE), _sds((s, 512), MXU_DTYPE),
                   _sds((1, D_MODEL)), _sds((1, D_RNN)), _sds((1, GLA_DV))],
        name="mix_out_bwd", compiler_params=_cparams(("arbitrary",)))(
            dx1, y, hf, hb, proj, osum, proj, g_rnn, g_gla, w_out, gpost)


def _gla_bwd(dos, proj, st, wg, bg, reverse, prev=None):
    s = proj.shape[0]
    ts = _tile(s, TILE_S)
    nt = s // ts
    ch = ts // GLA_CHUNK
    vw = GLA_HEADS * GLA_DV
    order = (lambda i: i) if reverse else (lambda i: nt - 1 - i)

    n_prev = 0 if prev is None else 4

    def body(do_ref, q_ref, k_ref, v_ref, lr_ref, st_ref, wg_ref, bg_ref, *rest):
        pq_ref, pk_ref, pv_ref, plr_ref = rest[:n_prev] if n_prev else (None,) * 4
        dq_ref, dk_ref, dv_ref, dlr_ref, dwg_ref, dbg_ref, ds_scr, dsa_scr = rest[n_prev:]

        def put(ref, p_ref, val):
            if p_ref is not None:
                val = val + p_ref[...].astype(F32)
            ref[...] = val.astype(ref.dtype)

        @pl.when(pl.program_id(0) == 0)
        def _():
            ds_scr[...] = jnp.zeros_like(ds_scr)
            dwg_ref[...] = jnp.zeros_like(dwg_ref)
            dbg_ref[...] = jnp.zeros_like(dbg_ref)

        z, la = _gla_gate(lr_ref[...], wg_ref[...], bg_ref[...])
        tri = _tri(reverse)
        tri_t = _tri(reverse, transpose=True)
        keep = tri > 0.5
        last_row = 0 if reverse else GLA_CHUNK - 1
        is_last = lax.broadcasted_iota(jnp.int32, (ch, GLA_CHUNK, QK_W), 1) == last_row
        first_head, own = _pair_masks()
        chunks = lambda a: a.reshape(ch, GLA_CHUNK, a.shape[-1])
        eb, enb, ebl, d, qe, ke, kd = _gla_chunk_terms(chunks(q_ref[...]), chunks(k_ref[...]), chunks(la), tri, reverse)
        v = chunks(v_ref[...])
        do = chunks(do_ref[...])
        dqe, dke, dkd, dd, dv = [], [], [], [], []
        for p in range(GLA_PAIRS):
            ln = slice(p * PAIR_K, (p + 1) * PAIR_K)
            lv = slice(p * PAIR_V, (p + 1) * PAIR_V)
            v_p, do_p = v[:, :, lv], do[:, :, lv]
            qe_p, ke_p, kd_p = qe[:, :, ln], ke[:, :, ln], kd[:, :, ln]
            grow = jnp.where(own, _bmm_tn(do_p, qe_p), 0.0)
            dst = ds_scr[p]
            for cc in range(ch):
                c = cc if reverse else (ch - 1 - cc)
                dsa_scr[c, p] = dst
                dst = grow[c] + d[c, :, ln] * dst
            ds_scr[p] = dst
            st_p = st_ref[:, p]
            dst_p = dsa_scr[:, p]
            dv_intra, dqe_intra, dke_intra = [], [], None
            for h in range(2):
                mine = first_head if h == 0 else ~first_head
                hv = slice(h * GLA_DV, (h + 1) * GLA_DV)
                q_h = jnp.where(mine, qe_p, 0.0)
                a_m = jnp.where(keep, _bmm_nt(q_h, ke_p), 0.0)
                da_m = jnp.where(keep, _bmm_nt(do_p[:, :, hv], v_p[:, :, hv]), 0.0)
                dv_intra.append(_bmm_tn(a_m, do_p[:, :, hv]))
                dqe_intra.append(_bmm(da_m, ke_p))
                dk_h = _bmm_tn(da_m, q_h)
                dke_intra = dk_h if dke_intra is None else dke_intra + dk_h
            dv.append(jnp.concatenate(dv_intra, axis=2) + _bmm_nt(kd_p, dst_p))
            dqe.append(jnp.where(first_head, dqe_intra[0], dqe_intra[1]) + _bmm(do_p, st_p))
            dke.append(dke_intra)
            dkd.append(_bmm(v_p, dst_p))
            dd.append(jnp.sum(dst_p * st_p, axis=1, keepdims=True))
        dqe = jnp.concatenate(dqe, axis=2)
        dke = jnp.concatenate(dke, axis=2)
        dkd = jnp.concatenate(dkd, axis=2)
        dd = jnp.concatenate(dd, axis=2)
        dbl = dd * d + jnp.sum(dkd * kd, axis=1, keepdims=True)
        db = dqe * qe - dke * ke - dkd * kd
        db = jnp.where(is_last, db + dbl, db)
        put(dv_ref, pv_ref, jnp.concatenate(dv, axis=2).reshape(ts, vw))
        put(dq_ref, pq_ref, (dqe * eb * (GLA_DK ** -0.5)).reshape(ts, QK_W))
        put(dk_ref, pk_ref, (dke * enb + dkd * ebl).reshape(ts, QK_W))
        dz = (_bmm_tri(tri_t, db) * (1.0 / GLA_TAU)).reshape(ts, QK_W) * _sigmoid(-z)
        put(dlr_ref, plr_ref, _mm_nt(dz, wg_ref[...]))
        dwg_ref[...] += _mm_tn(lr_ref[...], dz)
        dbg_ref[...] += jnp.sum(dz, axis=0, keepdims=True)

    wide, mid, narrow = _rows(ts, vw, 0, order), _rows(ts, QK_W, 0, order), _rows(ts, LANES, 0, order)
    return pl.pallas_call(
        body, grid=(nt,),
        in_specs=[wide] + _gla_specs(ts, order)
                 + [pl.BlockSpec((ch, GLA_PAIRS, PAIR_V, PAIR_K), lambda i: (order(i), 0, 0, 0)),
                    _const((LANES, QK_W)), _const((1, QK_W))] + ([mid, mid, wide, narrow] if n_prev else []),
        out_specs=[mid, mid, wide, narrow, _acc((LANES, QK_W)), _acc((1, QK_W))],
        out_shape=[_sds((s, QK_W), MXU_DTYPE), _sds((s, QK_W), MXU_DTYPE), _sds((s, vw), MXU_DTYPE),
                   _sds((s, LANES), MXU_DTYPE), _sds((LANES, QK_W)), _sds((1, QK_W))],
        scratch_shapes=[pltpu.VMEM((GLA_PAIRS, PAIR_V, PAIR_K), F32),
                        pltpu.VMEM((ch, GLA_PAIRS, PAIR_V, PAIR_K), F32)],
        name="gla_bwd_rev" if reverse else "gla_bwd",
        compiler_params=_cparams(("arbitrary",)))(dos, proj, proj, proj, proj, st, wg, bg, *(prev or ()))


def _rnn_bwd(dhs, h, proj, cw, cb, wa, ba, wx, bx, lam, reverse, rider=None):
    s = proj.shape[0]
    ts = _tile(s, TILE_S)
    nt = s // ts
    order = (lambda i: i) if reverse else (lambda i: nt - 1 - i)
    back = not reverse

    def body(dh_ref, h_ref, hh_ref, xr_ref, hp_ref, hn_ref, cw_ref, cb_ref, wa_ref, ba_ref, wx_ref, bx_ref, lam_ref,
             dxc_ref, dwa_ref, dba_ref, dwx_ref, dbx_ref, dlam_ref, a_scr, u_scr, g_scr, c_scr):
        i = pl.program_id(0)
        t = order(i)

        @pl.when(i == 0)
        def _():
            c_scr[...] = jnp.zeros_like(c_scr)
            dwa_ref[...] = jnp.zeros_like(dwa_ref)
            dba_ref[...] = jnp.zeros_like(dba_ref)
            dwx_ref[...] = jnp.zeros_like(dwx_ref)
            dbx_ref[...] = jnp.zeros_like(dbx_ref)
            dlam_ref[...] = jnp.zeros_like(dlam_ref)

        xc = _conv_fwd(xr_ref[...], hp_ref[...], hn_ref[...], cw_ref[...], cb_ref[...], t == 0, t == nt - 1)
        r, gi, sp, a, mult, inv_mult = _rnn_gates(xc, wa_ref[...], ba_ref[...], wx_ref[...], bx_ref[...], lam_ref[...])
        row = lax.broadcasted_iota(jnp.int32, (ts, D_RNN), 0)
        hv = h_ref[...]
        if reverse:
            edge = jnp.where(t == nt - 1, 0.0, hh_ref[0:1, :])
            h_prev = jnp.where(row == ts - 1, edge, pltpu.roll(hv, ts - 1, 0))
            a_nxt = jnp.where(row == 0, 1.0, pltpu.roll(a, 1, 0))
        else:
            edge = jnp.where(t == 0, 0.0, hh_ref[7:8, :])
            h_prev = jnp.where(row == 0, edge, pltpu.roll(hv, 1, 0))
            a_nxt = jnp.where(row == ts - 1, 1.0, pltpu.roll(a, ts - 1, 0))
        a_scr[...] = a_nxt
        u_scr[...] = dh_ref[...]
        _scan_tile(a_scr, u_scr, g_scr, c_scr[0:1, :], back)
        dh = g_scr[...]
        if reverse:
            c_scr[0:1, :] = a[ts - 1:ts, :] * dh[ts - 1:ts, :]
        else:
            c_scr[0:1, :] = a[0:1, :] * dh[0:1, :]
        dmult = dh * xc * gi
        dla = dh * h_prev * a - dmult * a * a * inv_mult
        dza = dla * (-LRU_C) * sp * r * (1.0 - r)
        dzx = dh * xc * mult * gi * (1.0 - gi)
        dsp = jnp.sum(dla * (-LRU_C) * r, axis=0, keepdims=True)
        dlam_ref[...] += dsp * (-_sigmoid(-lam_ref[...]))
        dxc_ref[...] = dh * gi * mult + _mm_nt(dza, wa_ref[...]) + _mm_nt(dzx, wx_ref[...])
        dwa_ref[...] += _mm_tn(xc, dza)
        dwx_ref[...] += _mm_tn(xc, dzx)
        dba_ref[...] += jnp.sum(dza, axis=0, keepdims=True)
        dbx_ref[...] += jnp.sum(dzx, axis=0, keepdims=True)

    hp, hn = _halo_specs(s, ts, D_RNN, 0, order)
    hhp, hhn = _halo_specs(s, ts, D_RNN, 0, order)
    sq = (D_RNN, D_RNN)
    vec = (1, D_RNN)
    return _call_with_rider(
        body, rider, (dhs, h, h, proj, proj, proj, cw, cb, wa, ba, wx, bx, lam), steps=nt,
        in_specs=[_rows(ts, D_RNN, 0, order), _rows(ts, D_RNN, 0, order), hhn if reverse else hhp,
                  _rows(ts, D_RNN, 0, order), hp, hn, _const((CONV_WIDTH, D_RNN)), _const(vec),
                  _const(sq), _const(vec), _const(sq), _const(vec), _const(vec)],
        out_specs=[_rows(ts, D_RNN, 0, order), _acc(sq), _acc(vec), _acc(sq), _acc(vec), _acc(vec)],
        out_shape=[_sds((s, D_RNN)), _sds(sq), _sds(vec), _sds(sq), _sds(vec), _sds(vec)],
        scratch_shapes=[pltpu.VMEM((ts, D_RNN), F32), pltpu.VMEM((ts, D_RNN), F32), pltpu.VMEM((ts, D_RNN), F32),
                        pltpu.VMEM((8, D_RNN), F32)],
        name="rnn_bwd_rev" if reverse else "rnn_bwd", semantics="arbitrary")


def _conv_bwd(dxc_f, dxc_b, proj, cw):
    s = proj.shape[0]
    ts = _tile(s, TILE_S)
    nt = s // ts
    ident = lambda i: i

    def body(df_ref, dfp_ref, dfn_ref, db_ref, dbp_ref, dbn_ref, xr_ref, xp_ref, xn_ref, cw_ref,
             dxr_ref, dcw_ref, dcb_ref):
        t = pl.program_id(0)

        @pl.when(t == 0)
        def _():
            dcw_ref[...] = jnp.zeros_like(dcw_ref)
            dcb_ref[...] = jnp.zeros_like(dcb_ref)

        first = t == 0
        last = t == nt - 1
        d = df_ref[...] + db_ref[...]
        d_m2, d_m1, _, d_p1 = _conv_taps(d, dfp_ref[...] + dbp_ref[...], dfn_ref[...] + dbn_ref[...], first, last)
        dn = jnp.where(last, 0.0, dfn_ref[...] + dbn_ref[...])
        d_p2 = jnp.concatenate([d, dn], axis=0)[2:2 + ts]
        del d_m2
        cw = cw_ref[...]
        dxr_ref[...] = (cw[0:1] * d_p2 + cw[1:2] * d_p1 + cw[2:3] * d + cw[3:4] * d_m1).astype(dxr_ref.dtype)
        taps = _conv_taps(xr_ref[...], xp_ref[...], xn_ref[...], first, last)
        dcw_ref[...] += jnp.concatenate([jnp.sum(d * tp, axis=0, keepdims=True) for tp in taps], axis=0)
        dcb_ref[...] += jnp.sum(d, axis=0, keepdims=True)

    hp, hn = _halo_specs(s, ts, D_RNN, 0, ident)
    return pl.pallas_call(
        body, grid=(nt,),
        in_specs=[_rows(ts, D_RNN), hp, hn, _rows(ts, D_RNN), hp, hn, _rows(ts, D_RNN), hp, hn,
                  _const((CONV_WIDTH, D_RNN))],
        out_specs=[_rows(ts, D_RNN), _acc((CONV_WIDTH, D_RNN)), _acc((1, D_RNN))],
        out_shape=[_sds((s, D_RNN), MXU_DTYPE), _sds((CONV_WIDTH, D_RNN)), _sds((1, D_RNN))],
        name="conv_bwd", compiler_params=_cparams(("arbitrary",)))(
            dxc_f, dxc_f, dxc_f, dxc_b, dxc_b, dxc_b, proj, proj, proj, cw)


def _mix_in_bwd(parts, dlr, x, dx1, gpre, w_in_p):
    s = x.shape[0]
    ts = _tile(s, TILE_WIDE)
    n_parts = len(parts)
    assert sum(p.shape[1] for p in parts) + LANES == PW

    def body(*refs):
        part_refs = refs[:n_parts + 1]
        x_ref, dx1_ref, g_ref, w_ref, dx_ref, dp_ref, h_ref, dgpre_ref = refs[n_parts + 1:]

        @pl.when(pl.program_id(0) == 0)
        def _():
            dgpre_ref[...] = jnp.zeros_like(dgpre_ref)

        dp = jnp.concatenate([r[...] for r in part_refs], axis=1)
        dp_ref[...] = dp
        h, n, rs = _rms_fwd(x_ref[...], g_ref[...])
        h_ref[...] = h.astype(MXU_DTYPE)
        dh = _mm_nt(dp, w_ref[...])
        dx, dgpre = _rms_bwd(dh, n, rs, g_ref[...])
        dgpre_ref[...] += dgpre
        dx_ref[...] = dx1_ref[...] + dx

    return pl.pallas_call(
        body, grid=(s // ts,),
        in_specs=[_rows(ts, p.shape[1]) for p in parts] + [_rows(ts, LANES), _rows(ts, D_MODEL), _rows(ts, D_MODEL),
                                                             _const((1, D_MODEL)), _const((D_MODEL, PW))],
        out_specs=[_rows(ts, D_MODEL), _rows(ts, PW), _rows(ts, D_MODEL), _acc((1, D_MODEL))],
        out_shape=[_sds((s, D_MODEL)), _sds((s, PW), MXU_DTYPE), _sds((s, D_MODEL), MXU_DTYPE), _sds((1, D_MODEL))],
        name="mix_in_bwd", compiler_params=_cparams(("arbitrary",)))(*parts, dlr, x, dx1, gpre, w_in_p)


W_IN_COLS = 2592
W_IN_SHARD = W_IN_COLS // N_DEV
FF_SHARD = D_FF // N_DEV


def _w_in_pieces():
    return [(j, 0, W_IN_SHARD, j * W_IN_SHARD) for j in range(N_DEV)]


def _w_in_from_shards(w):
    tr = 256

    def body(w_ref, o_ref):
        o_ref[...] = jnp.zeros_like(o_ref)
        for j, src, width, dst in _w_in_pieces():
            o_ref[:, dst:dst + width] = w_ref[j, :, src:src + width]

    return pl.pallas_call(
        body, grid=(D_MODEL // tr,),
        in_specs=[pl.BlockSpec((N_DEV, tr, W_IN_SHARD), lambda i: (0, i, 0))],
        out_specs=pl.BlockSpec((tr, PW), lambda i: (i, 0)), out_shape=_sds((D_MODEL, PW), w.dtype),
        name="w_in_from_shards", compiler_params=_cparams(("parallel",)))(w)


def _w_in_to_shards(g):
    tr = 256

    def body(g_ref, o_ref):
        for j, src, width, dst in _w_in_pieces():
            o_ref[j, :, src:src + width] = g_ref[:, dst:dst + width]

    return pl.pallas_call(
        body, grid=(D_MODEL // tr,),
        in_specs=[pl.BlockSpec((tr, PW), lambda i: (i, 0))],
        out_specs=pl.BlockSpec((N_DEV, tr, W_IN_SHARD), lambda i: (0, i, 0)),
        out_shape=_sds((N_DEV, D_MODEL, W_IN_SHARD), g.dtype),
        name="w_in_to_shards", compiler_params=_cparams(("parallel",)))(g)


def _cols_from_shards(w, name):
    _, d, c = w.shape
    tr = 256

    def body(w_ref, o_ref):
        for j in range(N_DEV):
            o_ref[:, j * c:(j + 1) * c] = w_ref[j]

    return pl.pallas_call(
        body, grid=(d // tr,),
        in_specs=[pl.BlockSpec((N_DEV, tr, c), lambda i: (0, i, 0))],
        out_specs=pl.BlockSpec((tr, N_DEV * c), lambda i: (i, 0)), out_shape=_sds((d, N_DEV * c), w.dtype),
        name=name, compiler_params=_cparams(("parallel",)))(w)


def _cols_to_shards(g, name):
    d, n = g.shape
    c = n // N_DEV
    tr = 256

    def body(g_ref, o_ref):
        for j in range(N_DEV):
            o_ref[j] = g_ref[:, j * c:(j + 1) * c]

    return pl.pallas_call(
        body, grid=(d // tr,),
        in_specs=[pl.BlockSpec((tr, n), lambda i: (i, 0))],
        out_specs=pl.BlockSpec((N_DEV, tr, c), lambda i: (0, i, 0)), out_shape=_sds((N_DEV, d, c), g.dtype),
        name=name, compiler_params=_cparams(("parallel",)))(g)


def _block_diag(w):
    n, b, _ = w.shape
    eye = jnp.eye(n, dtype=w.dtype)
    return (w[:, :, None, :] * eye[:, None, :, None]).reshape(n * b, n * b)


def _block_diag_of(w):
    n = D_RNN // 64
    eye = jnp.eye(n, dtype=w.dtype)
    return (w.reshape(n, 64, n, 64) * eye[:, None, :, None]).sum(axis=2)


def _gate_weight(wg, direction):
    lo = direction * GLA_RANK
    return jnp.pad(wg, ((lo, LANES - GLA_RANK - lo), (0, 0)))


def _late_weights(big):
    return dict(
        w_out=big["w_out"].reshape(D_MODEL, D_MODEL),
        wg=_cols_from_shards(big["w_ffn_gate"], "w_ffn_gate_from_shards"),
        wu=_cols_from_shards(big["w_ffn_up"], "w_ffn_up_from_shards"),
        wd=big["w_ffn_down"].reshape(D_FF, D_MODEL))


def _layer_weights(full, w_in_shards, l):
    row = lambda v: v.reshape(1, -1)
    lw = dict(
        gpre=row(full["mix_norm_pre"][l]), gpost=row(full["mix_norm_post"][l]),
        w_in=_w_in_from_shards(w_in_shards),
        cw=full["conv_w"][l], cb=row(full["conv_b"][l]),
        g_rnn=row(full["rnn_out_norm"][l]), g_gla=row(full["gla_out_norm"][l]),
        fpre=row(full["ffn_norm_pre"][l]), fpost=row(full["ffn_norm_post"][l]))
    for d in (0, 1):
        lw[f"wa{d}"] = _block_diag(full["lru_w_a"][l, d]).astype(MXU_DTYPE)
        lw[f"wx{d}"] = _block_diag(full["lru_w_x"][l, d]).astype(MXU_DTYPE)
        lw[f"ba{d}"] = row(full["lru_b_a"][l, d])
        lw[f"bx{d}"] = row(full["lru_b_x"][l, d])
        lw[f"lam{d}"] = row(full["lru_lambda"][l, d])
        lw[f"gw{d}"] = _gate_weight(full["gla_w_gate"][l, d], d).astype(MXU_DTYPE)
        lw[f"gb{d}"] = row(full["gla_b_gate"][l, d])
    return lw


def _layer_fwd(x, lw, l, hooks):
    proj, ridden = _mix_in_fwd(x, lw["gpre"], lw["w_in"], hooks.early_rider(l, 0))
    hooks.early_ridden(l, 0, ridden)
    hs, sts = [], []
    osum = None
    for d in (0, 1):
        h, ridden = _rnn_fwd(proj, lw["cw"], lw["cb"], lw[f"wa{d}"], lw[f"ba{d}"], lw[f"wx{d}"], lw[f"bx{d}"],
                             lw[f"lam{d}"], bool(d), hooks.early_rider(l, 1 + d))
        hooks.early_ridden(l, 1 + d, ridden)
        hs.append(h)
        osum, st = _gla_fwd(proj, lw[f"gw{d}"], lw[f"gb{d}"], bool(d), osum)
        sts.append(st)
    lw.update(_late_weights(hooks.late_weights(l)))
    x1, y = _mix_out_fwd(x, hs[0], hs[1], osum, proj, lw["g_rnn"], lw["g_gla"], lw["w_out"], lw["gpost"])
    (x2, a, u, f), ridden = _ffn_fwd(x1, lw["fpre"], lw["wg"], lw["wu"], lw["wd"], lw["fpost"], hooks.fwd_rider(l))
    hooks.fwd_ridden(l, ridden)
    return x2, dict(x=x, proj=proj, hs=hs, osum=osum, sts=sts, y=y, x1=x1, a=a, u=u, f=f)


def _layer_bwd(dx2, sv, lw, l, hooks):
    g = {}
    rider = hooks.pending_rider()
    (dx1, df, h2, p, da, du, dfpost, dfpre), ridden = _ffn_bwd(
        dx2, sv["f"], sv["x1"], sv["a"], sv["u"], lw["fpre"], lw["wg"], lw["wu"], lw["wd"], lw["fpost"], rider)
    hooks.ridden(rider, ridden)
    g["ffn_norm_post"], g["ffn_norm_pre"] = dfpost[0], dfpre[0]
    big = {}
    big["w_ffn_gate"] = _cols_to_shards(_tn_matmul(h2, da, "dw_ffn_gate"), "dw_ffn_gate_to_shards")
    big["w_ffn_up"] = _cols_to_shards(_tn_matmul(h2, du, "dw_ffn_up"), "dw_ffn_up_to_shards")
    big["w_ffn_down"] = _tn_matmul(p, df, "dw_ffn_down").reshape(N_DEV, FF_SHARD, D_MODEL)
    proj = sv["proj"]
    dm, dhs, dgr, dos, dg, dgpost, dgrnn, dggla = _mix_out_bwd(
        dx1, sv["y"], sv["hs"][0], sv["hs"][1], sv["osum"], proj, lw["g_rnn"], lw["g_gla"], lw["w_out"], lw["gpost"])
    g["mix_norm_post"], g["rnn_out_norm"], g["gla_out_norm"] = dgpost[0], dgrnn[0], dggla[0]
    big["w_out"] = _tn_matmul(sv["y"], dm, "dw_out").reshape(N_DEV, D_MODEL // N_DEV, D_MODEL)
    dxc = []
    gla = None
    gw, gb, wa, ba, wx, bx, lam = [], [], [], [], [], [], []
    for d in (0, 1):
        r = _gla_bwd(dos, proj, sv["sts"][d], lw[f"gw{d}"], lw[f"gb{d}"], bool(d), gla)
        gla = r[:4]
        lo = d * GLA_RANK
        gw.append(r[4][lo:lo + GLA_RANK])
        gb.append(r[5][0])
        early = ("w_ffn_gate", "w_ffn_up") if d == 0 else ("w_ffn_down", "w_out")
        hooks.offer(l, early, [big[n] for n in early])
        rider = hooks.pending_rider()
        r, ridden = _rnn_bwd(dhs, sv["hs"][d], proj, lw["cw"], lw["cb"], lw[f"wa{d}"], lw[f"ba{d}"], lw[f"wx{d}"],
                             lw[f"bx{d}"], lw[f"lam{d}"], bool(d), rider)
        hooks.ridden(rider, ridden)
        dxc.append(r[0])
        wa.append(_block_diag_of(r[1])); ba.append(r[2][0]); wx.append(_block_diag_of(r[3])); bx.append(r[4][0])
        lam.append(r[5][0])
    g["gla_w_gate"], g["gla_b_gate"] = jnp.stack(gw), jnp.stack(gb)
    g["lru_w_a"], g["lru_b_a"] = jnp.stack(wa), jnp.stack(ba)
    g["lru_w_x"], g["lru_b_x"], g["lru_lambda"] = jnp.stack(wx), jnp.stack(bx), jnp.stack(lam)
    dxr, dcw, dcb = _conv_bwd(dxc[0], dxc[1], proj, lw["cw"])
    g["conv_w"], g["conv_b"] = dcw, dcb[0]
    dx, dproj, h, dgpre = _mix_in_bwd((dxr, dgr, gla[0], gla[1], gla[2], dg), gla[3], sv["x"], dx1, lw["gpre"],
                                      lw["w_in"])
    g["mix_norm_pre"] = dgpre[0]
    hooks.offer(l, ("w_in",), [_w_in_to_shards(_tn_matmul(h, dproj, "dw_in", rows=1024, acc_bytes=14 * 1024 * 1024))])
    return dx, g


WEIGHT_NAMES = ["mix_norm_pre", "mix_norm_post", "w_in", "conv_w", "conv_b", "lru_w_a", "lru_b_a", "lru_w_x", "lru_b_x",
                "lru_lambda", "rnn_out_norm", "gla_w_gate", "gla_b_gate", "gla_out_norm", "w_out", "ffn_norm_pre",
                "ffn_norm_post", "w_ffn_gate", "w_ffn_up", "w_ffn_down"]
BIG_WEIGHTS = ["w_in", "w_out", "w_ffn_gate", "w_ffn_up", "w_ffn_down"]


def _local_step(x, target, full, hooks):
    saved, lws = [], []
    for l in range(DEPTH):
        lws.append(_layer_weights(full, hooks.w_in_shards(l), l))
        x, sv = _layer_fwd(x, lws[l], l, hooks)
        saved.append(sv)
    loss, dx = _loss_fwd_bwd(x, target)
    grads = [None] * DEPTH
    for l in reversed(range(DEPTH)):
        dx, grads[l] = _layer_bwd(dx, saved[l], lws[l], l, hooks)
    g = {n: jnp.stack([grads[l][n] for l in range(DEPTH)]) for n in WEIGHT_NAMES if n not in BIG_WEIGHTS}
    return loss[0, 0], dx, g


def _all_gather(x, name):
    def body(x_ref, out_ref, send_sems, recv_sems, local_sem):
        mx, my, mc = lax.axis_index("x"), lax.axis_index("y"), lax.axis_index("c")
        me, sibling = (mx, my, mc), (mx, my, 1 - mc)
        chips = [(1 - mx, my), (mx, 1 - my), (1 - mx, 1 - my)]

        def slot(px, py, pc):
            return out_ref.at[4 * px + 2 * py + pc]

        def copy(k, block, to, src=None):
            return pltpu.make_async_remote_copy(
                src_ref=slot(*block) if src is None else src, dst_ref=slot(*block),
                send_sem=send_sems.at[k], recv_sem=recv_sems.at[k], device_id=to, device_id_type=MESH_ID)

        mine = pltpu.make_async_copy(x_ref, slot(*me), local_sem)
        mine.start()
        first = [copy(0, me, sibling, src=x_ref)]
        first += [copy(1 + j, me, (*chip, mc), src=x_ref) for j, chip in enumerate(chips)]
        for cp in first:
            cp.start()
        passed = [copy(4 + j, (*chip, mc), sibling) for j, chip in enumerate(chips)]
        for j, chip in enumerate(chips):
            copy(1 + j, (*chip, mc), me).wait_recv()
            passed[j].start()
        copy(0, sibling, me).wait_recv()
        for j, chip in enumerate(chips):
            copy(4 + j, (*chip, 1 - mc), me).wait_recv()
        for cp in first + passed:
            cp.wait_send()
        mine.wait()

    return pl.pallas_call(
        body, out_shape=_sds((N_DEV,) + x.shape, x.dtype), in_specs=[ANY], out_specs=ANY,
        scratch_shapes=[pltpu.SemaphoreType.DMA((7,)), pltpu.SemaphoreType.DMA((7,)), pltpu.SemaphoreType.DMA],
        name=name)(x)


def _all_to_all(g, name):
    def body(g_ref, out_ref, send_sems, recv_sems, local_sem):
        mx, my, mc = lax.axis_index("x"), lax.axis_index("y"), lax.axis_index("c")
        me = 4 * mx + 2 * my + mc
        mine = pltpu.make_async_copy(g_ref.at[me], out_ref.at[me], local_sem)
        mine.start()
        copies = []
        for r in range(1, N_DEV):
            px = 1 - mx if r & 4 else mx
            py = 1 - my if r & 2 else my
            pc = 1 - mc if r & 1 else mc
            cp = pltpu.make_async_remote_copy(
                src_ref=g_ref.at[4 * px + 2 * py + pc], dst_ref=out_ref.at[me],
                send_sem=send_sems.at[r - 1], recv_sem=recv_sems.at[r - 1],
                device_id=(px, py, pc), device_id_type=MESH_ID)
            cp.start()
            copies.append(cp)
        for cp in copies:
            cp.wait()
        mine.wait()

    return pl.pallas_call(
        body, out_shape=_sds(g.shape, g.dtype), in_specs=[ANY], out_specs=ANY,
        scratch_shapes=[pltpu.SemaphoreType.DMA((7,)), pltpu.SemaphoreType.DMA((7,)), pltpu.SemaphoreType.DMA],
        name=name)(g)


def _sum_slots(parts, name):
    n, r, c = parts.shape
    tr = _tile(r, ADAM_TILE_ROWS)

    def body(p_ref, g_ref):
        g = p_ref[0]
        for k in range(1, n):
            g = g + p_ref[k]
        g_ref[...] = g

    return pl.pallas_call(
        body, grid=(r // tr,), in_specs=[pl.BlockSpec((n, tr, c), lambda i: (0, i, 0))],
        out_specs=pl.BlockSpec((tr, c), lambda i: (i, 0)), out_shape=_sds((r, c)),
        name=name, compiler_params=_cparams(("parallel",)))(parts)


def _sum_adamw(parts, w, m, v, name):
    n_slots, r, c = parts.shape
    tr = _tile(r, ADAM_TILE_ROWS)

    def body(p_ref, w_ref, m_ref, v_ref, g_ref, d_ref, m2_ref, v2_ref):
        g = p_ref[0]
        for k in range(1, n_slots):
            g = g + p_ref[k]
        g_ref[...] = g
        m2 = ADAM_B1 * m_ref[...] + (1.0 - ADAM_B1) * g
        v2 = ADAM_B2 * v_ref[...] + (1.0 - ADAM_B2) * (g * g)
        m2_ref[...] = m2
        v2_ref[...] = v2
        m_hat = m2 / (1.0 - ADAM_B1 ** ADAM_STEP)
        v_hat = v2 / (1.0 - ADAM_B2 ** ADAM_STEP)
        d_ref[...] = -ADAM_LR * (m_hat / (jnp.sqrt(v_hat) + ADAM_EPS) + ADAM_WD * w_ref[...])

    flat = pl.BlockSpec((tr, c), lambda i: (i, 0))
    return pl.pallas_call(
        body, grid=(r // tr,),
        in_specs=[pl.BlockSpec((n_slots, tr, c), lambda i: (0, i, 0)), flat, flat, flat],
        out_specs=[flat] * 4, out_shape=[_sds((r, c))] * 4,
        name=name, compiler_params=_cparams(("parallel",)))(parts, w, m, v)


def _gather_big_weights(shards):
    n = len(shards)

    def body(*refs):
        srcs, outs = refs[:n], refs[n:2 * n]
        send_sems, recv_sems, local_sems = refs[2 * n:]
        mx, my, mc = lax.axis_index("x"), lax.axis_index("y"), lax.axis_index("c")
        me, sibling = (mx, my, mc), (mx, my, 1 - mc)
        chips = [(1 - mx, my), (mx, 1 - my), (1 - mx, 1 - my)]

        def slot(a, px, py, pc):
            return outs[a].at[4 * px + 2 * py + pc]

        def copy(k, a, block, to, own=False):
            return pltpu.make_async_remote_copy(
                src_ref=srcs[a] if own else slot(a, *block), dst_ref=slot(a, *block),
                send_sem=send_sems.at[k * n + a], recv_sem=recv_sems.at[k * n + a],
                device_id=to, device_id_type=MESH_ID)

        mine = [pltpu.make_async_copy(srcs[a], slot(a, *me), local_sems.at[a]) for a in range(n)]
        for cp in mine:
            cp.start()
        first = [copy(0, a, me, sibling, own=True) for a in range(n)]
        first += [copy(1 + j, a, me, (*chip, mc), own=True) for j, chip in enumerate(chips) for a in range(n)]
        for cp in first:
            cp.start()
        passed = [[copy(4 + j, a, (*chip, mc), sibling) for a in range(n)] for j, chip in enumerate(chips)]
        for j, chip in enumerate(chips):
            for a in range(n):
                copy(1 + j, a, (*chip, mc), me).wait_recv()
                passed[j][a].start()
        for a in range(n):
            copy(0, a, sibling, me).wait_recv()
        for j, chip in enumerate(chips):
            for a in range(n):
                copy(4 + j, a, (*chip, 1 - mc), me).wait_recv()
        for cp in first + [cp for row in passed for cp in row]:
            cp.wait_send()
        for cp in mine:
            cp.wait()

    return pl.pallas_call(
        body, out_shape=[_sds((N_DEV,) + s.shape, s.dtype) for s in shards],
        in_specs=[ANY] * n, out_specs=[ANY] * n,
        scratch_shapes=[pltpu.SemaphoreType.DMA((7 * n,)), pltpu.SemaphoreType.DMA((7 * n,)),
                        pltpu.SemaphoreType.DMA((n,))],
        name="gather_matmul_weights")(*shards)


def _run_alone(rider, name):
    r_in = len(rider.operands)
    r_out = len(rider.out_shape)

    def body(*refs):
        local, remote = rider.copies(refs[:r_in], refs[r_in:r_in + r_out], *refs[r_in + r_out:])
        for cp in local + remote:
            cp.start()
        for cp in remote:
            cp.wait()
        for cp in local:
            cp.wait()

    n_remote = (N_DEV - 1) * rider.n
    return pl.pallas_call(
        body, out_shape=rider.out_shape, in_specs=[ANY] * r_in, out_specs=[ANY] * r_out,
        input_output_aliases=dict(rider.aliases),
        scratch_shapes=[pltpu.SemaphoreType.DMA((n_remote,)), pltpu.SemaphoreType.DMA((n_remote,)),
                        pltpu.SemaphoreType.DMA((rider.n,))],
        name=name)(*rider.operands)


def _sum_adamw_big(parts, w, m, v, name):
    _, nl, a, b = parts.shape
    ta = _tile(a, 256)

    def body(p_ref, w_ref, m_ref, v_ref, g_ref, d_ref, m2_ref, v2_ref):
        g = p_ref[0].astype(F32)
        for k in range(1, N_DEV):
            g = g + p_ref[k].astype(F32)
        g_ref[...] = g
        m2 = ADAM_B1 * m_ref[...] + (1.0 - ADAM_B1) * g
        v2 = ADAM_B2 * v_ref[...] + (1.0 - ADAM_B2) * (g * g)
        m2_ref[...] = m2
        v2_ref[...] = v2
        m_hat = m2 / (1.0 - ADAM_B1 ** ADAM_STEP)
        v_hat = v2 / (1.0 - ADAM_B2 ** ADAM_STEP)
        d_ref[...] = -ADAM_LR * (m_hat / (jnp.sqrt(v_hat) + ADAM_EPS) + ADAM_WD * w_ref[...])

    blk = pl.BlockSpec((None, ta, b), lambda l, i: (l, i, 0))
    return pl.pallas_call(
        body, grid=(nl, a // ta),
        in_specs=[pl.BlockSpec((N_DEV, None, ta, b), lambda l, i: (0, l, i, 0)), blk, blk, blk],
        out_specs=[blk] * 4, out_shape=[_sds((nl, a, b))] * 4,
        name=name, compiler_params=_cparams(("parallel", "parallel")))(parts, w, m, v)


SMALL_SHARDED = [("conv_w", 2), ("lru_b_a", 2), ("lru_b_x", 2), ("lru_lambda", 2), ("gla_w_gate", 3), ("gla_b_gate", 2)]
REPLICATED = ["mix_norm_pre", "mix_norm_post", "conv_b", "lru_w_a", "lru_w_x", "rnn_out_norm", "gla_out_norm",
              "ffn_norm_pre", "ffn_norm_post"]


def _pack(arrays, cols, row_mult):
    flat = jnp.concatenate([a.reshape(-1) for a in arrays])
    unit = cols * row_mult
    total = -(-flat.shape[0] // unit) * unit
    return jnp.pad(flat, (0, total - flat.shape[0])).reshape(total // cols, cols)


def _pack_slots(arrays, cols, row_mult):
    flat = jnp.concatenate([a.reshape(N_DEV, -1) for a in arrays], axis=1)
    unit = cols * row_mult
    total = -(-flat.shape[1] // unit) * unit
    return jnp.pad(flat, ((0, 0), (0, total - flat.shape[1]))).reshape(N_DEV, total // cols, cols)


def _unpack(flat, shapes):
    flat = flat.reshape(-1)
    out, off = [], 0
    for sh in shapes:
        n = 1
        for d in sh:
            n *= d
        out.append(flat[off:off + n].reshape(sh))
        off += n
    return out


def _unpack_slots(flat, shapes):
    flat = flat.reshape(N_DEV, -1)
    out, off = [], 0
    for sh in shapes:
        n = 1
        for d in sh:
            n *= d
        out.append(flat[:, off:off + n].reshape((N_DEV,) + tuple(sh)))
        off += n
    return out


def _merge_shards(a, axis):
    a = jnp.moveaxis(a, 0, axis)
    sh = a.shape
    return a.reshape(sh[:axis] + (sh[axis] * sh[axis + 1],) + sh[axis + 2:])


def _split_shards(a, axis):
    sh = a.shape
    a = a.reshape(sh[:axis] + (N_DEV, sh[axis] // N_DEV) + sh[axis + 1:])
    return jnp.moveaxis(a, axis, 0)


class _StepHooks:
    EARLY = {0: ("w_out", "w_ffn_down"), 1: ("w_ffn_gate", "w_ffn_up")}

    def __init__(self, shards, recvs):
        self.shards = [dict(zip(BIG_WEIGHTS, s)) for s in shards]
        self.recvs = recvs
        self.gathered = {0: {"w_in": _gather_big_weights([self.shards[0]["w_in"]])[0]}}
        self.pending = []

    def w_in_shards(self, l):
        return self.gathered[l]["w_in"]

    def early_rider(self, l, k):
        if l == 0 and k in self.EARLY:
            return _WeightGather([self.shards[0][n] for n in self.EARLY[k]])
        return None

    def early_ridden(self, l, k, outs):
        if outs:
            self.gathered[l].update(zip(self.EARLY[k], outs))

    def late_weights(self, l):
        return self.gathered.pop(l)

    def fwd_rider(self, l):
        return _WeightGather([self.shards[l + 1][n] for n in BIG_WEIGHTS]) if l + 1 < DEPTH else None

    def fwd_ridden(self, l, outs):
        if outs:
            self.gathered[l + 1] = dict(zip(BIG_WEIGHTS, outs))

    def offer(self, l, names, arrays):
        self.pending += [(l, n, a) for n, a in zip(names, arrays)]

    def pending_rider(self):
        if not self.pending:
            return None
        layers, names, arrays = zip(*self.pending)
        self.pending = []
        return _GradExchange(names, arrays, [self.recvs[n] for n in names], layers)

    def ridden(self, rider, outs):
        if rider is not None:
            self.recvs.update(zip(rider.names, outs))

    def finish(self):
        rider = self.pending_rider()
        self.ridden(rider, _run_alone(rider, "exchange_last_grads"))
        return self.recvs


def kernel(x, mix_norm_pre, mix_norm_post, w_in, conv_w, conv_b, lru_w_a, lru_b_a, lru_w_x, lru_b_x, lru_lambda, rnn_out_norm, gla_w_gate, gla_b_gate, gla_out_norm, w_out, ffn_norm_pre, ffn_norm_post, w_ffn_gate, w_ffn_up, w_ffn_down, loss_target, m_mix_norm_pre, m_mix_norm_post, m_w_in, m_conv_w, m_conv_b, m_lru_w_a, m_lru_b_a, m_lru_w_x, m_lru_b_x, m_lru_lambda, m_rnn_out_norm, m_gla_w_gate, m_gla_b_gate, m_gla_out_norm, m_w_out, m_ffn_norm_pre, m_ffn_norm_post, m_w_ffn_gate, m_w_ffn_up, m_w_ffn_down, v_mix_norm_pre, v_mix_norm_post, v_w_in, v_conv_w, v_conv_b, v_lru_w_a, v_lru_b_a, v_lru_w_x, v_lru_b_x, v_lru_lambda, v_rnn_out_norm, v_gla_w_gate, v_gla_b_gate, v_gla_out_norm, v_w_out, v_ffn_norm_pre, v_ffn_norm_post, v_w_ffn_gate, v_w_ffn_up, v_w_ffn_down):
    args = dict(locals())
    w = {n: args[n] for n in WEIGHT_NAMES}
    m = {n: args["m_" + n] for n in WEIGHT_NAMES}
    v = {n: args["v_" + n] for n in WEIGHT_NAMES}
    names_s = [n for n, _ in SMALL_SHARDED]
    axis_s = dict(SMALL_SHARDED)
    shapes_s = [w[n].shape for n in names_s]

    small = _pack([w[n] for n in names_s], LANES, 8)
    small_all = _unpack_slots(_all_gather(small, "gather_small_weights"), shapes_s)
    full = {n: w[n] for n in REPLICATED}
    for n, a in zip(names_s, small_all):
        full[n] = _merge_shards(a, axis_s[n])

    hooks = _StepHooks([[w[n][l].astype(MXU_DTYPE) for n in BIG_WEIGHTS] for l in range(DEPTH)],
                       {n: jnp.zeros((N_DEV,) + w[n].shape, MXU_DTYPE) for n in BIG_WEIGHTS})
    loss, dx, g = _local_step(x[0], loss_target[0], full, hooks)
    loss = lax.psum(loss, MESH_AXES)
    res = {}
    for n, parts in hooks.finish().items():
        res[n] = _sum_adamw_big(parts, w[n], m[n], v[n], "adamw_" + n)

    g_slots = _pack_slots([_split_shards(g[n], axis_s[n]) for n in names_s], LANES, 8)
    g_recv = _all_to_all(g_slots, "exchange_small_grads")
    packed = [_pack([t[n] for n in names_s], LANES, 8) for t in (w, m, v)]
    res_s = [_unpack(r, shapes_s) for r in _sum_adamw(g_recv, *packed, "adamw_small")]
    for i, n in enumerate(names_s):
        res[n] = [res_s[k][i] for k in range(4)]

    shapes_r = [w[n].shape for n in REPLICATED]
    g_rep = _pack([g[n] for n in REPLICATED], ADAM_COLS, ADAM_TILE_ROWS)
    rows_r = g_rep.shape[0]
    g_rep = _all_to_all(g_rep.reshape(N_DEV, rows_r // N_DEV, ADAM_COLS), "exchange_replicated_grads")
    g_rep = _all_gather(_sum_slots(g_rep, "sum_replicated_grads"), "gather_replicated_grads")
    packed = [_pack([t[n] for n in REPLICATED], ADAM_COLS, ADAM_TILE_ROWS) for t in (w, m, v)]
    res_r = [_unpack(r, shapes_r)
             for r in _sum_adamw(g_rep.reshape(1, rows_r, ADAM_COLS), *packed, "adamw_replicated")]
    for i, n in enumerate(REPLICATED):
        res[n] = [res_r[k][i] for k in range(4)]

    outs = [[res[n][k] for n in WEIGHT_NAMES] for k in range(4)]
    return (loss, dx[None], *outs[0], *outs[1], *outs[2], *outs[3])
```

```python
import functools

import jax
import jax.numpy as jnp
from jax import lax
from jax.experimental import pallas as pl
from jax.experimental.pallas import tpu as pltpu

F32 = jnp.float32
MXU_DTYPE = jnp.bfloat16

N_DEV = 8
D_MODEL = 1024
D_RNN = 512
CONV_WIDTH = 4
LRU_C = 8.0
GLA_HEADS = 4
GLA_DK = 64
GLA_PAIRS = 2
PAIR_K = 128
PAIR_V = 256
GLA_DV = 128
GLA_RANK = 16
GLA_TAU = 16.0
GLA_CHUNK = 64
D_FF = 2816
RMS_EPS = 1e-6
TINY = 1e-30
DEPTH = 4

PW = 2688
COL_Q, COL_K, COL_V, COL_G, COL_LR = 1024, 1280, 1536, 2048, 2560
QK_W = GLA_HEADS * GLA_DK
LANES = 128

TILE_S = 1024
TILE_WIDE = 512
TILE_LIGHT = 2048
TILE_F = 256
TILE_F_FWD = 512
TILE_TN = 2048
TN_ACC_BYTES = 6 * 1024 * 1024
F_CHUNK = 1408
VMEM_LIMIT = 56 * 1024 * 1024

ADAM_LR = 0.001
ADAM_B1 = 0.9
ADAM_B2 = 0.999
ADAM_EPS = 1e-08
ADAM_WD = 0.01
ADAM_STEP = 10

ADAM_TILE_ROWS = 256
ADAM_COLS = 1024


def _mm(a, b):
    return jnp.dot(a.astype(MXU_DTYPE), b.astype(MXU_DTYPE), preferred_element_type=F32)


def _mm_nt(a, b):
    return lax.dot_general(a.astype(MXU_DTYPE), b.astype(MXU_DTYPE), (((1,), (1,)), ((), ())),
                           preferred_element_type=F32)


def _mm_tn(a, b):
    return lax.dot_general(a.astype(MXU_DTYPE), b.astype(MXU_DTYPE), (((0,), (0,)), ((), ())),
                           preferred_element_type=F32)


def _bmm(a, b):
    return lax.dot_general(a.astype(MXU_DTYPE), b.astype(MXU_DTYPE), (((2,), (1,)), ((0,), (0,))),
                           preferred_element_type=F32)


def _bmm_nt(a, b):
    return lax.dot_general(a.astype(MXU_DTYPE), b.astype(MXU_DTYPE), (((2,), (2,)), ((0,), (0,))),
                           preferred_element_type=F32)


def _bmm_tn(a, b):
    return lax.dot_general(a.astype(MXU_DTYPE), b.astype(MXU_DTYPE), (((1,), (1,)), ((0,), (0,))),
                           preferred_element_type=F32)


def _bmm_tri(tri, x):
    t = jnp.broadcast_to(tri.astype(jnp.bfloat16)[None], (x.shape[0],) + tri.shape)
    hi = x.astype(jnp.bfloat16)
    r1 = x - hi.astype(F32)
    mid = r1.astype(jnp.bfloat16)
    lo = (r1 - mid.astype(F32)).astype(jnp.bfloat16)
    dot = lambda v: lax.dot_general(t, v, (((2,), (1,)), ((0,), (0,))), preferred_element_type=F32)
    return dot(hi) + dot(mid) + dot(lo)


def _sigmoid(x):
    return 0.5 * jnp.tanh(0.5 * x) + 0.5


def _log1p_pos(e):
    series = e * (1.0 - e * (0.5 - e * (1.0 / 3.0 - e * 0.25)))
    return jnp.where(e < 0.01, series, jnp.log(1.0 + e))


def _softplus(x):
    return jnp.maximum(x, 0.0) + _log1p_pos(jnp.exp(-jnp.abs(x)))


def _softplus_coarse(x):
    return jnp.maximum(x, 0.0) + jnp.log(1.0 + jnp.exp(-jnp.abs(x)))


GELU_C = 0.7978845608028654
GELU_K = 0.044715


def _gelu_and_grad(x):
    t = jnp.tanh(GELU_C * (x + GELU_K * x * x * x))
    y = 0.5 * x * (1.0 + t)
    dy = 0.5 * (1.0 + t) + 0.5 * x * (1.0 - t * t) * GELU_C * (1.0 + 3.0 * GELU_K * x * x)
    return y, dy


def _rms_fwd(x, g):
    rs = lax.rsqrt(jnp.mean(x * x, axis=-1, keepdims=True) + RMS_EPS)
    n = x * rs
    return n * g, n, rs


def _rms_bwd(dy, n, rs, g):
    dn = dy * g
    dx = rs * (dn - n * jnp.mean(dn * n, axis=-1, keepdims=True))
    dg = jnp.sum(dy * n, axis=0, keepdims=True)
    return dx, dg


def _cparams(sem=None):
    kw = dict(vmem_limit_bytes=VMEM_LIMIT)
    if sem is not None:
        kw["dimension_semantics"] = sem
    return pltpu.CompilerParams(**kw)


def _tile(n, pref):
    return pref if n % pref == 0 else n


def _const(shape):
    nd = len(shape)
    return pl.BlockSpec(shape, lambda *_: (0,) * nd, pipeline_mode=pl.Buffered(1))


def _acc(shape):
    nd = len(shape)
    return pl.BlockSpec(shape, lambda *_: (0,) * nd)


def _rows(ts, w, col=0, order=None):
    if order is None:
        return pl.BlockSpec((ts, w), lambda i: (i, col))
    return pl.BlockSpec((ts, w), lambda i: (order(i), col))


def _sds(shape, dtype=F32):
    return jax.ShapeDtypeStruct(shape, dtype)


MESH_ID = pl.DeviceIdType.MESH
ANY = pl.BlockSpec(memory_space=pl.ANY)
MESH_AXES = ("x", "y", "c")


def _peers():
    mx, my, mc = lax.axis_index("x"), lax.axis_index("y"), lax.axis_index("c")
    peers = []
    for r in range(1, N_DEV):
        px = 1 - mx if r & 4 else mx
        py = 1 - my if r & 2 else my
        pc = 1 - mc if r & 1 else mc
        peers.append((4 * px + 2 * py + pc, (px, py, pc)))
    return 4 * mx + 2 * my + mc, peers


class _GradExchange:
    def __init__(self, names, srcs, recvs, layers):
        self.names = list(names)
        self.n = len(srcs)
        self.layers = list(layers)
        self.operands = list(srcs) + list(recvs)
        self.out_shape = [_sds(r.shape, r.dtype) for r in recvs]
        self.aliases = {self.n + a: a for a in range(self.n)}

    def copies(self, ins, outs, send_sems, recv_sems, local_sems):
        n, layers = self.n, self.layers
        me, peers = _peers()
        local = [pltpu.make_async_copy(ins[a].at[me], outs[a].at[me, layers[a]], local_sems.at[a]) for a in range(n)]
        remote = [pltpu.make_async_remote_copy(
            src_ref=ins[a].at[slot], dst_ref=outs[a].at[me, layers[a]],
            send_sem=send_sems.at[r * n + a], recv_sem=recv_sems.at[r * n + a],
            device_id=dev, device_id_type=MESH_ID) for r, (slot, dev) in enumerate(peers) for a in range(n)]
        return local, remote


class _WeightGather:
    def __init__(self, shards):
        self.n = len(shards)
        self.operands = list(shards)
        self.out_shape = [_sds((N_DEV,) + s.shape, s.dtype) for s in shards]
        self.aliases = {}

    def copies(self, ins, outs, send_sems, recv_sems, local_sems):
        n = self.n
        me, peers = _peers()
        local = [pltpu.make_async_copy(ins[a], outs[a].at[me], local_sems.at[a]) for a in range(n)]
        remote = [pltpu.make_async_remote_copy(
            src_ref=ins[a], dst_ref=outs[a].at[me],
            send_sem=send_sems.at[r * n + a], recv_sem=recv_sems.at[r * n + a],
            device_id=dev, device_id_type=MESH_ID) for r, (_, dev) in enumerate(peers) for a in range(n)]
        return local, remote


def _call_with_rider(body, rider, operands, *, steps, in_specs, out_specs, out_shape, scratch_shapes=(), name,
                     semantics):
    if rider is None:
        outs = pl.pallas_call(body, grid=(steps,), in_specs=in_specs, out_specs=out_specs, out_shape=out_shape,
                              scratch_shapes=list(scratch_shapes), name=name,
                              compiler_params=_cparams((semantics,)))(*operands)
        return outs, []
    n_in, n_out, n_scr = len(in_specs), len(out_specs), len(scratch_shapes)
    r_in, r_out = len(rider.operands), len(rider.out_shape)

    def riding(*refs):
        own_in, ride_in = refs[:n_in], refs[n_in:n_in + r_in]
        refs = refs[n_in + r_in:]
        own_out, ride_out = refs[:n_out], refs[n_out:n_out + r_out]
        refs = refs[n_out + r_out:]
        own_scr, sems = refs[:n_scr], refs[n_scr:]
        i = pl.program_id(0)

        @pl.when(i == 0)
        def _():
            local, remote = rider.copies(ride_in, ride_out, *sems)
            for cp in local + remote:
                cp.start()

        body(*own_in, *own_out, *own_scr)

        @pl.when(i == steps - 1)
        def _():
            local, remote = rider.copies(ride_in, ride_out, *sems)
            for cp in remote:
                cp.wait()
            for cp in local:
                cp.wait()

    n_remote = (N_DEV - 1) * rider.n
    outs = pl.pallas_call(
        riding, grid=(steps,), in_specs=list(in_specs) + [ANY] * r_in, out_specs=list(out_specs) + [ANY] * r_out,
        out_shape=list(out_shape) + rider.out_shape,
        input_output_aliases={n_in + i: n_out + o for i, o in rider.aliases.items()},
        scratch_shapes=list(scratch_shapes) + [pltpu.SemaphoreType.DMA((n_remote,)),
                                               pltpu.SemaphoreType.DMA((n_remote,)),
                                               pltpu.SemaphoreType.DMA((rider.n,))],
        name=name, compiler_params=_cparams(("arbitrary",)))(*operands, *rider.operands)
    return outs[:n_out], outs[n_out:]


def _mix_in_fwd(x, gpre, w_in_p, rider=None):
    s = x.shape[0]
    ts = _tile(s, TILE_S)

    def body(x_ref, g_ref, w_ref, o_ref):
        h, _, _ = _rms_fwd(x_ref[...], g_ref[...])
        o_ref[...] = _mm(h, w_ref[...])

    (proj,), ridden = _call_with_rider(
        body, rider, (x, gpre, w_in_p), steps=s // ts,
        in_specs=[_rows(ts, D_MODEL), _const((1, D_MODEL)), _const((D_MODEL, PW))],
        out_specs=[_rows(ts, PW)], out_shape=[_sds((s, PW))], name="mix_in_fwd", semantics="parallel")
    return proj, ridden


def _conv_taps(xr, hp, hn, first, last):
    ts = xr.shape[0]
    hp = jnp.where(first, 0.0, hp)
    hn = jnp.where(last, 0.0, hn)
    xe = jnp.concatenate([hp, xr, hn], axis=0)
    return xe[6:6 + ts], xe[7:7 + ts], xr, xe[9:9 + ts]


def _conv_fwd(xr, hp, hn, cw, cb, first, last):
    t0, t1, t2, t3 = _conv_taps(xr, hp, hn, first, last)
    return cw[0:1] * t0 + cw[1:2] * t1 + cw[2:3] * t2 + cw[3:4] * t3 + cb


def _rnn_gates(xc, wa, ba, wx, bx, lam):
    r = _sigmoid(_mm(xc, wa) + ba)
    i = _sigmoid(_mm(xc, wx) + bx)
    sp = _softplus(-lam)
    la = (-LRU_C) * r * sp
    a = jnp.exp(la)
    one_minus_a2 = -jnp.tanh(la) * (a * a + 1.0)
    inv_mult = lax.rsqrt(jnp.maximum(one_minus_a2, TINY))
    return r, i, sp, a, one_minus_a2 * inv_mult, inv_mult


def _scan_tile(a_scr, u_scr, h_ref, c0, reverse):
    ts = a_scr.shape[0]
    a = a_scr[...]
    u = u_scr[...]
    row = lax.broadcasted_iota(jnp.int32, a.shape, 0) % 8
    for k in (1, 2, 4):
        if reverse:
            a_sh = pltpu.roll(a, ts - k, 0)
            u_sh = pltpu.roll(u, ts - k, 0)
            ok = row < 8 - k
        else:
            a_sh = pltpu.roll(a, k, 0)
            u_sh = pltpu.roll(u, k, 0)
            ok = row >= k
        u = jnp.where(ok, u + a * u_sh, u)
        a = jnp.where(ok, a * a_sh, a)
    a_scr[...] = a
    u_scr[...] = u
    ng = ts // 8

    def body(j, c):
        g = (ng - 1 - j) if reverse else j
        sl = pl.ds(pl.multiple_of(g * 8, 8), 8)
        hh = u_scr[sl, :] + a_scr[sl, :] * c
        h_ref[sl, :] = hh
        return hh[0:1, :] if reverse else hh[7:8, :]

    return lax.fori_loop(0, ng, body, c0)


def _halo_specs(s, ts, w, col, order):
    n8 = s // 8
    per = ts // 8
    prev = pl.BlockSpec((8, w), lambda i: (jnp.maximum(order(i) * per - 1, 0), col))
    nxt = pl.BlockSpec((8, w), lambda i: (jnp.minimum((order(i) + 1) * per, n8 - 1), col))
    return prev, nxt


def _rnn_fwd(proj, cw, cb, wa, ba, wx, bx, lam, reverse, rider=None):
    s = proj.shape[0]
    ts = _tile(s, TILE_LIGHT)
    nt = s // ts
    order = (lambda i: nt - 1 - i) if reverse else (lambda i: i)

    def body(xr_ref, hp_ref, hn_ref, cw_ref, cb_ref, wa_ref, ba_ref, wx_ref, bx_ref, lam_ref,
             h_ref, a_scr, u_scr, c_scr):
        i = pl.program_id(0)
        t = order(i)

        @pl.when(i == 0)
        def _():
            c_scr[...] = jnp.zeros_like(c_scr)

        xc = _conv_fwd(xr_ref[...], hp_ref[...], hn_ref[...], cw_ref[...], cb_ref[...], t == 0, t == nt - 1)
        _, gi, _, a, mult, _ = _rnn_gates(xc, wa_ref[...], ba_ref[...], wx_ref[...], bx_ref[...], lam_ref[...])
        a_scr[...] = a
        u_scr[...] = xc * gi * mult
        c_scr[0:1, :] = _scan_tile(a_scr, u_scr, h_ref, c_scr[0:1, :], reverse)

    hp, hn = _halo_specs(s, ts, D_RNN, 0, order)
    (h,), ridden = _call_with_rider(
        body, rider, (proj, proj, proj, cw, cb, wa, ba, wx, bx, lam), steps=nt,
        in_specs=[_rows(ts, D_RNN, 0, order), hp, hn, _const((CONV_WIDTH, D_RNN)), _const((1, D_RNN)),
                  _const((D_RNN, D_RNN)), _const((1, D_RNN)), _const((D_RNN, D_RNN)), _const((1, D_RNN)),
                  _const((1, D_RNN))],
        out_specs=[_rows(ts, D_RNN, 0, order)], out_shape=[_sds((s, D_RNN))],
        scratch_shapes=[pltpu.VMEM((ts, D_RNN), F32), pltpu.VMEM((ts, D_RNN), F32), pltpu.VMEM((8, D_RNN), F32)],
        name="rnn_fwd_rev" if reverse else "rnn_fwd", semantics="arbitrary")
    return h, ridden


def _tri(reverse, transpose=False):
    r = lax.broadcasted_iota(jnp.int32, (GLA_CHUNK, GLA_CHUNK), 0)
    c = lax.broadcasted_iota(jnp.int32, (GLA_CHUNK, GLA_CHUNK), 1)
    if transpose:
        r, c = c, r
    return ((r <= c) if reverse else (r >= c)).astype(F32)


def _gla_chunk_terms(q, k, la, tri, reverse):
    b = _bmm_tri(tri, la)
    bl = b[:, 0:1] if reverse else b[:, GLA_CHUNK - 1:GLA_CHUNK]
    eb = jnp.exp(b)
    enb = jnp.exp(-b)
    ebl = jnp.exp(bl - b)
    d = jnp.exp(bl)
    return eb, enb, ebl, d, q * (GLA_DK ** -0.5) * eb, k * enb, k * ebl


def _gla_gate(lr, wg, bg):
    z = _mm(lr, wg) + bg
    return z, -_softplus_coarse(-z) * (1.0 / GLA_TAU)


def _pair_masks():
    first_head = lax.broadcasted_iota(jnp.int32, (1, PAIR_K), 1) < GLA_DK
    row_first = lax.broadcasted_iota(jnp.int32, (PAIR_V, PAIR_K), 0) < GLA_DV
    lane_first = lax.broadcasted_iota(jnp.int32, (PAIR_V, PAIR_K), 1) < GLA_DK
    return first_head, row_first == lane_first


def _gla_specs(ts, order):
    return [_rows(ts, QK_W, COL_Q // QK_W, order), _rows(ts, QK_W, COL_K // QK_W, order),
            _rows(ts, GLA_HEADS * GLA_DV, COL_V // (GLA_HEADS * GLA_DV), order),
            _rows(ts, LANES, COL_LR // LANES, order)]


def _gla_fwd(proj, wg, bg, reverse, o_add=None):
    s = proj.shape[0]
    ts = _tile(s, TILE_S)
    nt = s // ts
    ch = ts // GLA_CHUNK
    vw = GLA_HEADS * GLA_DV
    order = (lambda i: nt - 1 - i) if reverse else (lambda i: i)
    extra = [] if o_add is None else [o_add]

    def body(q_ref, k_ref, v_ref, lr_ref, wg_ref, bg_ref, *rest):
        add_ref = None if o_add is None else rest[0]
        o_ref, st_ref, s_scr = rest[len(extra):]

        @pl.when(pl.program_id(0) == 0)
        def _():
            s_scr[...] = jnp.zeros_like(s_scr)

        _, la = _gla_gate(lr_ref[...], wg_ref[...], bg_ref[...])
        tri = _tri(reverse)
        keep = tri > 0.5
        first_head, own = _pair_masks()
        chunks = lambda a: a.reshape(ch, GLA_CHUNK, a.shape[-1])
        _, _, _, d, qe, ke, kd = _gla_chunk_terms(chunks(q_ref[...]), chunks(k_ref[...]), chunks(la), tri, reverse)
        v = chunks(v_ref[...])
        outs = []
        for p in range(GLA_PAIRS):
            ln = slice(p * PAIR_K, (p + 1) * PAIR_K)
            v_p = v[:, :, p * PAIR_V:(p + 1) * PAIR_V]
            qe_p, ke_p = qe[:, :, ln], ke[:, :, ln]
            grow = jnp.where(own, _bmm_tn(v_p, kd[:, :, ln]), 0.0)
            st = s_scr[p]
            for cc in range(ch):
                c = (ch - 1 - cc) if reverse else cc
                st_ref[c, p] = st
                st = d[c, :, ln] * st + grow[c]
            s_scr[p] = st
            intra = []
            for h in range(2):
                q_h = jnp.where(first_head if h == 0 else ~first_head, qe_p, 0.0)
                a_m = jnp.where(keep, _bmm_nt(q_h, ke_p), 0.0)
                intra.append(_bmm(a_m, v_p[:, :, h * GLA_DV:(h + 1) * GLA_DV]))
            outs.append(_bmm_nt(qe_p, st_ref[:, p]) + jnp.concatenate(intra, axis=2))
        o = jnp.concatenate(outs, axis=2).reshape(ts, vw)
        o_ref[...] = o if add_ref is None else o + add_ref[...]

    return pl.pallas_call(
        body, grid=(nt,),
        in_specs=_gla_specs(ts, order) + [_const((LANES, QK_W)), _const((1, QK_W))]
                 + [_rows(ts, vw, 0, order)] * len(extra),
        out_specs=[_rows(ts, vw, 0, order),
                   pl.BlockSpec((ch, GLA_PAIRS, PAIR_V, PAIR_K), lambda i: (order(i), 0, 0, 0))],
        out_shape=[_sds((s, vw)), _sds((s // GLA_CHUNK, GLA_PAIRS, PAIR_V, PAIR_K))],
        scratch_shapes=[pltpu.VMEM((GLA_PAIRS, PAIR_V, PAIR_K), F32)],
        name="gla_fwd_rev" if reverse else "gla_fwd",
        compiler_params=_cparams(("arbitrary",)))(proj, proj, proj, proj, wg, bg, *extra)


def _mix_out_terms(hf, hb, gate_r, osum, g, g_rnn, g_gla):
    hs = hf + hb
    gl, dgl = _gelu_and_grad(gate_r)
    z = hs * gl
    y_rnn, n_rnn, rs_rnn = _rms_fwd(z, g_rnn)
    sg_lin = _sigmoid(g)
    sg = g * sg_lin
    dsg = sg_lin * (1.0 + g * (1.0 - sg_lin))
    ons, ns, rss = [], [], []
    for h in range(GLA_HEADS):
        ln = slice(h * LANES, (h + 1) * LANES)
        on, n, rs = _rms_fwd(osum[:, ln], g_gla)
        ons.append(on)
        ns.append(n)
        rss.append(rs)
    on = jnp.concatenate(ons, axis=1)
    return hs, gl, dgl, y_rnn, n_rnn, rs_rnn, sg, dsg, on, ns, rss


def _mix_out_fwd(x, hf, hb, osum, proj, g_rnn, g_gla, w_out, gpost):
    s = x.shape[0]
    ts = _tile(s, TILE_S)

    def body(x_ref, hf_ref, hb_ref, gr_ref, os_ref, g_ref, grnn_ref, ggla_ref, w_ref, gp_ref, x1_ref, y_ref):
        _, _, _, y_rnn, _, _, sg, _, on, _, _ = _mix_out_terms(
            hf_ref[...], hb_ref[...], gr_ref[...], os_ref[...], g_ref[...], grnn_ref[...], ggla_ref[...])
        y = jnp.concatenate([y_rnn, on * sg], axis=1).astype(MXU_DTYPE)
        y_ref[...] = y
        out, _, _ = _rms_fwd(_mm(y, w_ref[...]), gp_ref[...])
        x1_ref[...] = x_ref[...] + out

    return pl.pallas_call(
        body, grid=(s // ts,),
        in_specs=[_rows(ts, D_MODEL), _rows(ts, D_RNN), _rows(ts, D_RNN), _rows(ts, D_RNN, 1), _rows(ts, 512),
                  _rows(ts, 512, COL_G // 512), _const((1, D_RNN)), _const((1, GLA_DV)),
                  _const((D_MODEL, D_MODEL)), _const((1, D_MODEL))],
        out_specs=[_rows(ts, D_MODEL), _rows(ts, D_MODEL)],
        out_shape=[_sds((s, D_MODEL)), _sds((s, D_MODEL), MXU_DTYPE)],
        name="mix_out_fwd", compiler_params=_cparams(("parallel",)))(
            x, hf, hb, proj, osum, proj, g_rnn, g_gla, w_out, gpost)


def _f_chunks():
    return [(c0, min(c0 + F_CHUNK, D_FF)) for c0 in range(0, D_FF, F_CHUNK)]


def _ffn_fwd(x1, gpre, wg, wu, wd, gpost, rider=None):
    s = x1.shape[0]
    ts = _tile(s, TILE_F_FWD)

    def body(x_ref, gpre_ref, wg_ref, wu_ref, wd_ref, gpost_ref, x2_ref, a_ref, u_ref, f_ref):
        x = x_ref[...]
        h, _, _ = _rms_fwd(x, gpre_ref[...])
        h = h.astype(MXU_DTYPE)
        f = jnp.zeros((ts, D_MODEL), F32)
        for c0, c1 in _f_chunks():
            a = _mm(h, wg_ref[:, c0:c1])
            u = _mm(h, wu_ref[:, c0:c1])
            a_ref[:, c0:c1] = a.astype(MXU_DTYPE)
            u_ref[:, c0:c1] = u.astype(MXU_DTYPE)
            f = f + _mm(a * _sigmoid(a) * u, wd_ref[c0:c1, :])
        f_ref[...] = f
        out, _, _ = _rms_fwd(f, gpost_ref[...])
        x2_ref[...] = x + out

    return _call_with_rider(
        body, rider, (x1, gpre, wg, wu, wd, gpost), steps=s // ts,
        in_specs=[_rows(ts, D_MODEL), _const((1, D_MODEL)), _const((D_MODEL, D_FF)), _const((D_MODEL, D_FF)),
                  _const((D_FF, D_MODEL)), _const((1, D_MODEL))],
        out_specs=[_rows(ts, D_MODEL), _rows(ts, D_FF), _rows(ts, D_FF), _rows(ts, D_MODEL)],
        out_shape=[_sds((s, D_MODEL)), _sds((s, D_FF), MXU_DTYPE), _sds((s, D_FF), MXU_DTYPE), _sds((s, D_MODEL))],
        name="ffn_fwd", semantics="parallel")


def _loss_fwd_bwd(y, target):
    s = y.shape[0]
    ts = _tile(s, TILE_LIGHT)

    def body(y_ref, t_ref, loss_ref, dy_ref):
        @pl.when(pl.program_id(0) == 0)
        def _():
            loss_ref[...] = jnp.zeros_like(loss_ref)

        e = y_ref[...] - t_ref[...]
        dy_ref[...] = e * (1.0 / D_MODEL)
        part = jnp.sum(jnp.sum(e * e, axis=1, keepdims=True), axis=0, keepdims=True) * (0.5 / D_MODEL)
        loss_ref[...] += jnp.broadcast_to(part, loss_ref.shape)

    return pl.pallas_call(
        body, grid=(s // ts,),
        in_specs=[_rows(ts, D_MODEL), _rows(ts, D_MODEL)],
        out_specs=[_acc((8, LANES)), _rows(ts, D_MODEL)],
        out_shape=[_sds((8, LANES)), _sds((s, D_MODEL))],
        name="loss", compiler_params=_cparams(("arbitrary",)))(y, target)


def _tn_matmul(a, b, name, rows=TILE_TN, acc_bytes=TN_ACC_BYTES):
    s, k = a.shape
    n = b.shape[1]
    ts = _tile(s, rows)
    tn = max(t for t in range(LANES, n + 1, LANES) if n % t == 0 and (k * t * 4 <= acc_bytes or t == LANES))
    ns = s // ts

    def body(a_ref, b_ref, o_ref, acc):
        i = pl.program_id(1)

        @pl.when(i == 0)
        def _():
            acc[...] = jnp.zeros_like(acc)

        acc[...] += _mm_tn(a_ref[...], b_ref[...])

        @pl.when(i == ns - 1)
        def _():
            o_ref[...] = acc[...].astype(o_ref.dtype)

    return pl.pallas_call(
        body, grid=(n // tn, ns),
        in_specs=[pl.BlockSpec((ts, k), lambda j, i: (i, 0)), pl.BlockSpec((ts, tn), lambda j, i: (i, j))],
        out_specs=pl.BlockSpec((k, tn), lambda j, i: (0, j)), out_shape=_sds((k, n), MXU_DTYPE),
        scratch_shapes=[pltpu.VMEM((k, tn), F32)],
        name=name, compiler_params=_cparams(("parallel", "arbitrary")))(a, b)


def _ffn_bwd(dx2, f, x1, a, u, gpre, wg, wu, wd, gpost, rider=None):
    s = x1.shape[0]
    ts = _tile(s, TILE_F)

    def body(dx2_ref, f_ref, x1_ref, a_ref, u_ref, gpre_ref, wg_ref, wu_ref, wd_ref, gpost_ref,
             dx1_ref, df_ref, h_ref, p_ref, da_ref, du_ref, dgpost_ref, dgpre_ref):
        @pl.when(pl.program_id(0) == 0)
        def _():
            dgpost_ref[...] = jnp.zeros_like(dgpost_ref)
            dgpre_ref[...] = jnp.zeros_like(dgpre_ref)

        dx2 = dx2_ref[...]
        _, nf, rsf = _rms_fwd(f_ref[...], gpost_ref[...])
        df, dgpost = _rms_bwd(dx2, nf, rsf, gpost_ref[...])
        dgpost_ref[...] += dgpost
        df = df.astype(MXU_DTYPE)
        df_ref[...] = df
        h, n1, rs1 = _rms_fwd(x1_ref[...], gpre_ref[...])
        h_ref[...] = h.astype(MXU_DTYPE)
        dh = jnp.zeros((ts, D_MODEL), F32)
        for c0, c1 in _f_chunks():
            av = a_ref[:, c0:c1].astype(F32)
            uv = u_ref[:, c0:c1].astype(F32)
            sg = _sigmoid(av)
            dp = _mm_nt(df, wd_ref[c0:c1, :])
            p_ref[:, c0:c1] = (av * sg * uv).astype(MXU_DTYPE)
            da = (dp * uv * sg * (1.0 + av * (1.0 - sg))).astype(MXU_DTYPE)
            du = (dp * av * sg).astype(MXU_DTYPE)
            da_ref[:, c0:c1] = da
            du_ref[:, c0:c1] = du
            dh = dh + _mm_nt(da, wg_ref[:, c0:c1]) + _mm_nt(du, wu_ref[:, c0:c1])
        dx, dgpre = _rms_bwd(dh, n1, rs1, gpre_ref[...])
        dgpre_ref[...] += dgpre
        dx1_ref[...] = dx2 + dx

    return _call_with_rider(
        body, rider, (dx2, f, x1, a, u, gpre, wg, wu, wd, gpost), steps=s // ts,
        in_specs=[_rows(ts, D_MODEL), _rows(ts, D_MODEL), _rows(ts, D_MODEL), _rows(ts, D_FF), _rows(ts, D_FF),
                  _const((1, D_MODEL)), _const((D_MODEL, D_FF)), _const((D_MODEL, D_FF)), _const((D_FF, D_MODEL)),
                  _const((1, D_MODEL))],
        out_specs=[_rows(ts, D_MODEL), _rows(ts, D_MODEL), _rows(ts, D_MODEL), _rows(ts, D_FF), _rows(ts, D_FF),
                   _rows(ts, D_FF), _acc((1, D_MODEL)), _acc((1, D_MODEL))],
        out_shape=[_sds((s, D_MODEL)), _sds((s, D_MODEL), MXU_DTYPE), _sds((s, D_MODEL), MXU_DTYPE),
                   _sds((s, D_FF), MXU_DTYPE), _sds((s, D_FF), MXU_DTYPE), _sds((s, D_FF), MXU_DTYPE),
                   _sds((1, D_MODEL)), _sds((1, D_MODEL))],
        name="ffn_bwd", semantics="arbitrary")


def _mix_out_bwd(dx1, y, hf, hb, osum, proj, g_rnn, g_gla, w_out, gpost):
    s = y.shape[0]
    ts = _tile(s, TILE_WIDE)

    def body(dx1_ref, y_ref, hf_ref, hb_ref, gr_ref, os_ref, g_ref, grnn_ref, ggla_ref, w_ref, gp_ref,
             dm_ref, dhs_ref, dgr_ref, dos_ref, dg_ref, dgpost_ref, dgrnn_ref, dggla_ref):
        @pl.when(pl.program_id(0) == 0)
        def _():
            dgpost_ref[...] = jnp.zeros_like(dgpost_ref)
            dgrnn_ref[...] = jnp.zeros_like(dgrnn_ref)
            dggla_ref[...] = jnp.zeros_like(dggla_ref)

        _, nm, rsm = _rms_fwd(_mm(y_ref[...], w_ref[...]), gp_ref[...])
        dm, dgpost = _rms_bwd(dx1_ref[...], nm, rsm, gp_ref[...])
        dgpost_ref[...] += dgpost
        dm = dm.astype(MXU_DTYPE)
        dm_ref[...] = dm
        dy = _mm_nt(dm, w_ref[...])
        hs, gl, dgl, _, n_rnn, rs_rnn, sg, dsg, on, ns, rss = _mix_out_terms(
            hf_ref[...], hb_ref[...], gr_ref[...], os_ref[...], g_ref[...], grnn_ref[...], ggla_ref[...])
        dz, dgrnn = _rms_bwd(dy[:, :D_RNN], n_rnn, rs_rnn, grnn_ref[...])
        dgrnn_ref[...] += dgrnn
        dhs_ref[...] = dz * gl
        dgr_ref[...] = (dz * hs * dgl).astype(MXU_DTYPE)
        dyg = dy[:, D_RNN:]
        dg_ref[...] = (dyg * on * dsg).astype(MXU_DTYPE)
        don = dyg * sg
        dggla = jnp.zeros((1, GLA_DV), F32)
        for h in range(GLA_HEADS):
            ln = slice(h * LANES, (h + 1) * LANES)
            dos, dgh = _rms_bwd(don[:, ln], ns[h], rss[h], ggla_ref[...])
            dos_ref[:, ln] = dos.astype(MXU_DTYPE)
            dggla = dggla + dgh
        dggla_ref[...] += dggla

    return pl.pallas_call(
        body, grid=(s // ts,),
        in_specs=[_rows(ts, D_MODEL), _rows(ts, D_MODEL), _rows(ts, D_RNN), _rows(ts, D_RNN), _rows(ts, D_RNN, 1),
                  _rows(ts, 512), _rows(ts, 512, COL_G // 512), _const((1, D_RNN)), _const((1, GLA_DV)),
                  _const((D_MODEL, D_MODEL)), _const((1, D_MODEL))],
        out_specs=[_rows(ts, D_MODEL), _rows(ts, D_RNN), _rows(ts, D_RNN), _rows(ts, 512), _rows(ts, 512),
                   _acc((1, D_MODEL)), _acc((1, D_RNN)), _acc((1, GLA_DV))],
        out_shape=[_sds((s, D_MODEL), MXU_DTYPE), _sds((s, D_RNN)), _sds((s, D_RNN), MXU_DTYPE),
                   _sds((s, 512), MXU_DTYPE), _sds((s, 512), MXU_DTYPE),
                   _sds((1, D_MODEL)), _sds((1, D_RNN)), _sds((1, GLA_DV))],
        name="mix_out_bwd", compiler_params=_cparams(("arbitrary",)))(
            dx1, y, hf, hb, proj, osum, proj, g_rnn, g_gla, w_out, gpost)


def _gla_bwd(dos, proj, st, wg, bg, reverse, prev=None):
    s = proj.shape[0]
    ts = _tile(s, TILE_S)
    nt = s // ts
    ch = ts // GLA_CHUNK
    vw = GLA_HEADS * GLA_DV
    order = (lambda i: i) if reverse else (lambda i: nt - 1 - i)

    n_prev = 0 if prev is None else 4

    def body(do_ref, q_ref, k_ref, v_ref, lr_ref, st_ref, wg_ref, bg_ref, *rest):
        pq_ref, pk_ref, pv_ref, plr_ref = rest[:n_prev] if n_prev else (None,) * 4
        dq_ref, dk_ref, dv_ref, dlr_ref, dwg_ref, dbg_ref, ds_scr, dsa_scr = rest[n_prev:]

        def put(ref, p_ref, val):
            if p_ref is not None:
                val = val + p_ref[...].astype(F32)
            ref[...] = val.astype(ref.dtype)

        @pl.when(pl.program_id(0) == 0)
        def _():
            ds_scr[...] = jnp.zeros_like(ds_scr)
            dwg_ref[...] = jnp.zeros_like(dwg_ref)
            dbg_ref[...] = jnp.zeros_like(dbg_ref)

        z, la = _gla_gate(lr_ref[...], wg_ref[...], bg_ref[...])
        tri = _tri(reverse)
        tri_t = _tri(reverse, transpose=True)
        keep = tri > 0.5
        last_row = 0 if reverse else GLA_CHUNK - 1
        is_last = lax.broadcasted_iota(jnp.int32, (ch, GLA_CHUNK, QK_W), 1) == last_row
        first_head, own = _pair_masks()
        chunks = lambda a: a.reshape(ch, GLA_CHUNK, a.shape[-1])
        eb, enb, ebl, d, qe, ke, kd = _gla_chunk_terms(chunks(q_ref[...]), chunks(k_ref[...]), chunks(la), tri, reverse)
        v = chunks(v_ref[...])
        do = chunks(do_ref[...])
        dqe, dke, dkd, dd, dv = [], [], [], [], []
        for p in range(GLA_PAIRS):
            ln = slice(p * PAIR_K, (p + 1) * PAIR_K)
            lv = slice(p * PAIR_V, (p + 1) * PAIR_V)
            v_p, do_p = v[:, :, lv], do[:, :, lv]
            qe_p, ke_p, kd_p = qe[:, :, ln], ke[:, :, ln], kd[:, :, ln]
            grow = jnp.where(own, _bmm_tn(do_p, qe_p), 0.0)
            dst = ds_scr[p]
            for cc in range(ch):
                c = cc if reverse else (ch - 1 - cc)
                dsa_scr[c, p] = dst
                dst = grow[c] + d[c, :, ln] * dst
            ds_scr[p] = dst
            st_p = st_ref[:, p]
            dst_p = dsa_scr[:, p]
            dv_intra, dqe_intra, dke_intra = [], [], None
            for h in range(2):
                mine = first_head if h == 0 else ~first_head
                hv = slice(h * GLA_DV, (h + 1) * GLA_DV)
                q_h = jnp.where(mine, qe_p, 0.0)
                a_m = jnp.where(keep, _bmm_nt(q_h, ke_p), 0.0)
                da_m = jnp.where(keep, _bmm_nt(do_p[:, :, hv], v_p[:, :, hv]), 0.0)
                dv_intra.append(_bmm_tn(a_m, do_p[:, :, hv]))
                dqe_intra.append(_bmm(da_m, ke_p))
                dk_h = _bmm_tn(da_m, q_h)
                dke_intra = dk_h if dke_intra is None else dke_intra + dk_h
            dv.append(jnp.concatenate(dv_intra, axis=2) + _bmm_nt(kd_p, dst_p))
            dqe.append(jnp.where(first_head, dqe_intra[0], dqe_intra[1]) + _bmm(do_p, st_p))
            dke.append(dke_intra)
            dkd.append(_bmm(v_p, dst_p))
            dd.append(jnp.sum(dst_p * st_p, axis=1, keepdims=True))
        dqe = jnp.concatenate(dqe, axis=2)
        dke = jnp.concatenate(dke, axis=2)
        dkd = jnp.concatenate(dkd, axis=2)
        dd = jnp.concatenate(dd, axis=2)
        dbl = dd * d + jnp.sum(dkd * kd, axis=1, keepdims=True)
        db = dqe * qe - dke * ke - dkd * kd
        db = jnp.where(is_last, db + dbl, db)
        put(dv_ref, pv_ref, jnp.concatenate(dv, axis=2).reshape(ts, vw))
        put(dq_ref, pq_ref, (dqe * eb * (GLA_DK ** -0.5)).reshape(ts, QK_W))
        put(dk_ref, pk_ref, (dke * enb + dkd * ebl).reshape(ts, QK_W))
        dz = (_bmm_tri(tri_t, db) * (1.0 / GLA_TAU)).reshape(ts, QK_W) * _sigmoid(-z)
        put(dlr_ref, plr_ref, _mm_nt(dz, wg_ref[...]))
        dwg_ref[...] += _mm_tn(lr_ref[...], dz)
        dbg_ref[...] += jnp.sum(dz, axis=0, keepdims=True)

    wide, mid, narrow = _rows(ts, vw, 0, order), _rows(ts, QK_W, 0, order), _rows(ts, LANES, 0, order)
    return pl.pallas_call(
        body, grid=(nt,),
        in_specs=[wide] + _gla_specs(ts, order)
                 + [pl.BlockSpec((ch, GLA_PAIRS, PAIR_V, PAIR_K), lambda i: (order(i), 0, 0, 0)),
                    _const((LANES, QK_W)), _const((1, QK_W))] + ([mid, mid, wide, narrow] if n_prev else []),
        out_specs=[mid, mid, wide, narrow, _acc((LANES, QK_W)), _acc((1, QK_W))],
        out_shape=[_sds((s, QK_W), MXU_DTYPE), _sds((s, QK_W), MXU_DTYPE), _sds((s, vw), MXU_DTYPE),
                   _sds((s, LANES), MXU_DTYPE), _sds((LANES, QK_W)), _sds((1, QK_W))],
        scratch_shapes=[pltpu.VMEM((GLA_PAIRS, PAIR_V, PAIR_K), F32),
                        pltpu.VMEM((ch, GLA_PAIRS, PAIR_V, PAIR_K), F32)],
        name="gla_bwd_rev" if reverse else "gla_bwd",
        compiler_params=_cparams(("arbitrary",)))(dos, proj, proj, proj, proj, st, wg, bg, *(prev or ()))


def _rnn_bwd(dhs, h, proj, cw, cb, wa, ba, wx, bx, lam, reverse, rider=None):
    s = proj.shape[0]
    ts = _tile(s, TILE_S)
    nt = s // ts
    order = (lambda i: i) if reverse else (lambda i: nt - 1 - i)
    back = not reverse

    def body(dh_ref, h_ref, hh_ref, xr_ref, hp_ref, hn_ref, cw_ref, cb_ref, wa_ref, ba_ref, wx_ref, bx_ref, lam_ref,
             dxc_ref, dwa_ref, dba_ref, dwx_ref, dbx_ref, dlam_ref, a_scr, u_scr, g_scr, c_scr):
        i = pl.program_id(0)
        t = order(i)

        @pl.when(i == 0)
        def _():
            c_scr[...] = jnp.zeros_like(c_scr)
            dwa_ref[...] = jnp.zeros_like(dwa_ref)
            dba_ref[...] = jnp.zeros_like(dba_ref)
            dwx_ref[...] = jnp.zeros_like(dwx_ref)
            dbx_ref[...] = jnp.zeros_like(dbx_ref)
            dlam_ref[...] = jnp.zeros_like(dlam_ref)

        xc = _conv_fwd(xr_ref[...], hp_ref[...], hn_ref[...], cw_ref[...], cb_ref[...], t == 0, t == nt - 1)
        r, gi, sp, a, mult, inv_mult = _rnn_gates(xc, wa_ref[...], ba_ref[...], wx_ref[...], bx_ref[...], lam_ref[...])
        row = lax.broadcasted_iota(jnp.int32, (ts, D_RNN), 0)
        hv = h_ref[...]
        if reverse:
            edge = jnp.where(t == nt - 1, 0.0, hh_ref[0:1, :])
            h_prev = jnp.where(row == ts - 1, edge, pltpu.roll(hv, ts - 1, 0))
            a_nxt = jnp.where(row == 0, 1.0, pltpu.roll(a, 1, 0))
        else:
            edge = jnp.where(t == 0, 0.0, hh_ref[7:8, :])
            h_prev = jnp.where(row == 0, edge, pltpu.roll(hv, 1, 0))
            a_nxt = jnp.where(row == ts - 1, 1.0, pltpu.roll(a, ts - 1, 0))
        a_scr[...] = a_nxt
        u_scr[...] = dh_ref[...]
        _scan_tile(a_scr, u_scr, g_scr, c_scr[0:1, :], back)
        dh = g_scr[...]
        if reverse:
            c_scr[0:1, :] = a[ts - 1:ts, :] * dh[ts - 1:ts, :]
        else:
            c_scr[0:1, :] = a[0:1, :] * dh[0:1, :]
        dmult = dh * xc * gi
        dla = dh * h_prev * a - dmult * a * a * inv_mult
        dza = dla * (-LRU_C) * sp * r * (1.0 - r)
        dzx = dh * xc * mult * gi * (1.0 - gi)
        dsp = jnp.sum(dla * (-LRU_C) * r, axis=0, keepdims=True)
        dlam_ref[...] += dsp * (-_sigmoid(-lam_ref[...]))
        dxc_ref[...] = dh * gi * mult + _mm_nt(dza, wa_ref[...]) + _mm_nt(dzx, wx_ref[...])
        dwa_ref[...] += _mm_tn(xc, dza)
        dwx_ref[...] += _mm_tn(xc, dzx)
        dba_ref[...] += jnp.sum(dza, axis=0, keepdims=True)
        dbx_ref[...] += jnp.sum(dzx, axis=0, keepdims=True)

    hp, hn = _halo_specs(s, ts, D_RNN, 0, order)
    hhp, hhn = _halo_specs(s, ts, D_RNN, 0, order)
    sq = (D_RNN, D_RNN)
    vec = (1, D_RNN)
    return _call_with_rider(
        body, rider, (dhs, h, h, proj, proj, proj, cw, cb, wa, ba, wx, bx, lam), steps=nt,
        in_specs=[_rows(ts, D_RNN, 0, order), _rows(ts, D_RNN, 0, order), hhn if reverse else hhp,
                  _rows(ts, D_RNN, 0, order), hp, hn, _const((CONV_WIDTH, D_RNN)), _const(vec),
                  _const(sq), _const(vec), _const(sq), _const(vec), _const(vec)],
        out_specs=[_rows(ts, D_RNN, 0, order), _acc(sq), _acc(vec), _acc(sq), _acc(vec), _acc(vec)],
        out_shape=[_sds((s, D_RNN)), _sds(sq), _sds(vec), _sds(sq), _sds(vec), _sds(vec)],
        scratch_shapes=[pltpu.VMEM((ts, D_RNN), F32), pltpu.VMEM((ts, D_RNN), F32), pltpu.VMEM((ts, D_RNN), F32),
                        pltpu.VMEM((8, D_RNN), F32)],
        name="rnn_bwd_rev" if reverse else "rnn_bwd", semantics="arbitrary")


def _conv_bwd(dxc_f, dxc_b, proj, cw):
    s = proj.shape[0]
    ts = _tile(s, TILE_LIGHT)
    nt = s // ts
    ident = lambda i: i

    def body(df_ref, dfp_ref, dfn_ref, db_ref, dbp_ref, dbn_ref, xr_ref, xp_ref, xn_ref, cw_ref,
             dxr_ref, dcw_ref, dcb_ref):
        t = pl.program_id(0)

        @pl.when(t == 0)
        def _():
            dcw_ref[...] = jnp.zeros_like(dcw_ref)
            dcb_ref[...] = jnp.zeros_like(dcb_ref)

        first = t == 0
        last = t == nt - 1
        d = df_ref[...] + db_ref[...]
        d_m2, d_m1, _, d_p1 = _conv_taps(d, dfp_ref[...] + dbp_ref[...], dfn_ref[...] + dbn_ref[...], first, last)
        dn = jnp.where(last, 0.0, dfn_ref[...] + dbn_ref[...])
        d_p2 = jnp.concatenate([d, dn], axis=0)[2:2 + ts]
        del d_m2
        cw = cw_ref[...]
        dxr_ref[...] = (cw[0:1] * d_p2 + cw[1:2] * d_p1 + cw[2:3] * d + cw[3:4] * d_m1).astype(dxr_ref.dtype)
        taps = _conv_taps(xr_ref[...], xp_ref[...], xn_ref[...], first, last)
        dcw_ref[...] += jnp.concatenate([jnp.sum(d * tp, axis=0, keepdims=True) for tp in taps], axis=0)
        dcb_ref[...] += jnp.sum(d, axis=0, keepdims=True)

    hp, hn = _halo_specs(s, ts, D_RNN, 0, ident)
    return pl.pallas_call(
        body, grid=(nt,),
        in_specs=[_rows(ts, D_RNN), hp, hn, _rows(ts, D_RNN), hp, hn, _rows(ts, D_RNN), hp, hn,
                  _const((CONV_WIDTH, D_RNN))],
        out_specs=[_rows(ts, D_RNN), _acc((CONV_WIDTH, D_RNN)), _acc((1, D_RNN))],
        out_shape=[_sds((s, D_RNN), MXU_DTYPE), _sds((CONV_WIDTH, D_RNN)), _sds((1, D_RNN))],
        name="conv_bwd", compiler_params=_cparams(("arbitrary",)))(
            dxc_f, dxc_f, dxc_f, dxc_b, dxc_b, dxc_b, proj, proj, proj, cw)


def _mix_in_bwd(parts, dlr, x, dx1, gpre, w_in_p):
    s = x.shape[0]
    ts = _tile(s, TILE_WIDE)
    n_parts = len(parts)
    assert sum(p.shape[1] for p in parts) + LANES == PW

    def body(*refs):
        part_refs = refs[:n_parts + 1]
        x_ref, dx1_ref, g_ref, w_ref, dx_ref, dp_ref, h_ref, dgpre_ref = refs[n_parts + 1:]

        @pl.when(pl.program_id(0) == 0)
        def _():
            dgpre_ref[...] = jnp.zeros_like(dgpre_ref)

        dp = jnp.concatenate([r[...] for r in part_refs], axis=1)
        dp_ref[...] = dp
        h, n, rs = _rms_fwd(x_ref[...], g_ref[...])
        h_ref[...] = h.astype(MXU_DTYPE)
        dh = _mm_nt(dp, w_ref[...])
        dx, dgpre = _rms_bwd(dh, n, rs, g_ref[...])
        dgpre_ref[...] += dgpre
        dx_ref[...] = dx1_ref[...] + dx

    return pl.pallas_call(
        body, grid=(s // ts,),
        in_specs=[_rows(ts, p.shape[1]) for p in parts] + [_rows(ts, LANES), _rows(ts, D_MODEL), _rows(ts, D_MODEL),
                                                             _const((1, D_MODEL)), _const((D_MODEL, PW))],
        out_specs=[_rows(ts, D_MODEL), _rows(ts, PW), _rows(ts, D_MODEL), _acc((1, D_MODEL))],
        out_shape=[_sds((s, D_MODEL)), _sds((s, PW), MXU_DTYPE), _sds((s, D_MODEL), MXU_DTYPE), _sds((1, D_MODEL))],
        name="mix_in_bwd", compiler_params=_cparams(("arbitrary",)))(*parts, dlr, x, dx1, gpre, w_in_p)


W_IN_COLS = 2592
W_IN_SHARD = W_IN_COLS // N_DEV
FF_SHARD = D_FF // N_DEV


def _w_in_pieces():
    return [(j, 0, W_IN_SHARD, j * W_IN_SHARD) for j in range(N_DEV)]


def _w_in_from_shards(w):
    tr = 256

    def body(w_ref, o_ref):
        o_ref[...] = jnp.zeros_like(o_ref)
        for j, src, width, dst in _w_in_pieces():
            o_ref[:, dst:dst + width] = w_ref[j, :, src:src + width]

    return pl.pallas_call(
        body, grid=(D_MODEL // tr,),
        in_specs=[pl.BlockSpec((N_DEV, tr, W_IN_SHARD), lambda i: (0, i, 0))],
        out_specs=pl.BlockSpec((tr, PW), lambda i: (i, 0)), out_shape=_sds((D_MODEL, PW), w.dtype),
        name="w_in_from_shards", compiler_params=_cparams(("parallel",)))(w)


def _w_in_to_shards(g):
    tr = 256

    def body(g_ref, o_ref):
        for j, src, width, dst in _w_in_pieces():
            o_ref[j, :, src:src + width] = g_ref[:, dst:dst + width]

    return pl.pallas_call(
        body, grid=(D_MODEL // tr,),
        in_specs=[pl.BlockSpec((tr, PW), lambda i: (i, 0))],
        out_specs=pl.BlockSpec((N_DEV, tr, W_IN_SHARD), lambda i: (0, i, 0)),
        out_shape=_sds((N_DEV, D_MODEL, W_IN_SHARD), g.dtype),
        name="w_in_to_shards", compiler_params=_cparams(("parallel",)))(g)


def _cols_from_shards(w, name):
    _, d, c = w.shape
    tr = 256

    def body(w_ref, o_ref):
        for j in range(N_DEV):
            o_ref[:, j * c:(j + 1) * c] = w_ref[j]

    return pl.pallas_call(
        body, grid=(d // tr,),
        in_specs=[pl.BlockSpec((N_DEV, tr, c), lambda i: (0, i, 0))],
        out_specs=pl.BlockSpec((tr, N_DEV * c), lambda i: (i, 0)), out_shape=_sds((d, N_DEV * c), w.dtype),
        name=name, compiler_params=_cparams(("parallel",)))(w)


def _cols_to_shards(g, name):
    d, n = g.shape
    c = n // N_DEV
    tr = 256

    def body(g_ref, o_ref):
        for j in range(N_DEV):
            o_ref[j] = g_ref[:, j * c:(j + 1) * c]

    return pl.pallas_call(
        body, grid=(d // tr,),
        in_specs=[pl.BlockSpec((tr, n), lambda i: (i, 0))],
        out_specs=pl.BlockSpec((N_DEV, tr, c), lambda i: (0, i, 0)), out_shape=_sds((N_DEV, d, c), g.dtype),
        name=name, compiler_params=_cparams(("parallel",)))(g)


def _block_diag(w):
    n, b, _ = w.shape
    eye = jnp.eye(n, dtype=w.dtype)
    return (w[:, :, None, :] * eye[:, None, :, None]).reshape(n * b, n * b)


def _block_diag_of(w):
    n = D_RNN // 64
    eye = jnp.eye(n, dtype=w.dtype)
    return (w.reshape(n, 64, n, 64) * eye[:, None, :, None]).sum(axis=2)


def _gate_weight(wg, direction):
    lo = direction * GLA_RANK
    return jnp.pad(wg, ((lo, LANES - GLA_RANK - lo), (0, 0)))


def _late_weights(big):
    return dict(
        w_out=big["w_out"].reshape(D_MODEL, D_MODEL),
        wg=_cols_from_shards(big["w_ffn_gate"], "w_ffn_gate_from_shards"),
        wu=_cols_from_shards(big["w_ffn_up"], "w_ffn_up_from_shards"),
        wd=big["w_ffn_down"].reshape(D_FF, D_MODEL))


def _layer_weights(full, w_in_shards, l):
    row = lambda v: v.reshape(1, -1)
    lw = dict(
        gpre=row(full["mix_norm_pre"][l]), gpost=row(full["mix_norm_post"][l]),
        w_in=_w_in_from_shards(w_in_shards),
        cw=full["conv_w"][l], cb=row(full["conv_b"][l]),
        g_rnn=row(full["rnn_out_norm"][l]), g_gla=row(full["gla_out_norm"][l]),
        fpre=row(full["ffn_norm_pre"][l]), fpost=row(full["ffn_norm_post"][l]))
    for d in (0, 1):
        lw[f"wa{d}"] = _block_diag(full["lru_w_a"][l, d]).astype(MXU_DTYPE)
        lw[f"wx{d}"] = _block_diag(full["lru_w_x"][l, d]).astype(MXU_DTYPE)
        lw[f"ba{d}"] = row(full["lru_b_a"][l, d])
        lw[f"bx{d}"] = row(full["lru_b_x"][l, d])
        lw[f"lam{d}"] = row(full["lru_lambda"][l, d])
        lw[f"gw{d}"] = _gate_weight(full["gla_w_gate"][l, d], d).astype(MXU_DTYPE)
        lw[f"gb{d}"] = row(full["gla_b_gate"][l, d])
    return lw


def _layer_fwd(x, lw, l, hooks):
    proj, ridden = _mix_in_fwd(x, lw["gpre"], lw["w_in"], hooks.early_rider(l, 0))
    hooks.early_ridden(l, 0, ridden)
    hs, sts = [], []
    osum = None
    for d in (0, 1):
        h, ridden = _rnn_fwd(proj, lw["cw"], lw["cb"], lw[f"wa{d}"], lw[f"ba{d}"], lw[f"wx{d}"], lw[f"bx{d}"],
                             lw[f"lam{d}"], bool(d), hooks.early_rider(l, 1 + d))
        hooks.early_ridden(l, 1 + d, ridden)
        hs.append(h)
        osum, st = _gla_fwd(proj, lw[f"gw{d}"], lw[f"gb{d}"], bool(d), osum)
        sts.append(st)
    lw.update(_late_weights(hooks.late_weights(l)))
    x1, y = _mix_out_fwd(x, hs[0], hs[1], osum, proj, lw["g_rnn"], lw["g_gla"], lw["w_out"], lw["gpost"])
    (x2, a, u, f), ridden = _ffn_fwd(x1, lw["fpre"], lw["wg"], lw["wu"], lw["wd"], lw["fpost"], hooks.fwd_rider(l))
    hooks.fwd_ridden(l, ridden)
    return x2, dict(x=x, proj=proj, hs=hs, osum=osum, sts=sts, y=y, x1=x1, a=a, u=u, f=f)


def _layer_bwd(dx2, sv, lw, l, hooks):
    g = {}
    rider = hooks.pending_rider()
    (dx1, df, h2, p, da, du, dfpost, dfpre), ridden = _ffn_bwd(
        dx2, sv["f"], sv["x1"], sv["a"], sv["u"], lw["fpre"], lw["wg"], lw["wu"], lw["wd"], lw["fpost"], rider)
    hooks.ridden(rider, ridden)
    g["ffn_norm_post"], g["ffn_norm_pre"] = dfpost[0], dfpre[0]
    big = {}
    big["w_ffn_gate"] = _cols_to_shards(_tn_matmul(h2, da, "dw_ffn_gate"), "dw_ffn_gate_to_shards")
    big["w_ffn_up"] = _cols_to_shards(_tn_matmul(h2, du, "dw_ffn_up"), "dw_ffn_up_to_shards")
    big["w_ffn_down"] = _tn_matmul(p, df, "dw_ffn_down").reshape(N_DEV, FF_SHARD, D_MODEL)
    proj = sv["proj"]
    dm, dhs, dgr, dos, dg, dgpost, dgrnn, dggla = _mix_out_bwd(
        dx1, sv["y"], sv["hs"][0], sv["hs"][1], sv["osum"], proj, lw["g_rnn"], lw["g_gla"], lw["w_out"], lw["gpost"])
    g["mix_norm_post"], g["rnn_out_norm"], g["gla_out_norm"] = dgpost[0], dgrnn[0], dggla[0]
    big["w_out"] = _tn_matmul(sv["y"], dm, "dw_out").reshape(N_DEV, D_MODEL // N_DEV, D_MODEL)
    dxc = []
    gla = None
    gw, gb, wa, ba, wx, bx, lam = [], [], [], [], [], [], []
    for d in (0, 1):
        r = _gla_bwd(dos, proj, sv["sts"][d], lw[f"gw{d}"], lw[f"gb{d}"], bool(d), gla)
        gla = r[:4]
        lo = d * GLA_RANK
        gw.append(r[4][lo:lo + GLA_RANK])
        gb.append(r[5][0])
        early = ("w_ffn_gate", "w_ffn_up") if d == 0 else ("w_ffn_down", "w_out")
        hooks.offer(l, early, [big[n] for n in early])
        rider = hooks.pending_rider()
        r, ridden = _rnn_bwd(dhs, sv["hs"][d], proj, lw["cw"], lw["cb"], lw[f"wa{d}"], lw[f"ba{d}"], lw[f"wx{d}"],
                             lw[f"bx{d}"], lw[f"lam{d}"], bool(d), rider)
        hooks.ridden(rider, ridden)
        dxc.append(r[0])
        wa.append(_block_diag_of(r[1])); ba.append(r[2][0]); wx.append(_block_diag_of(r[3])); bx.append(r[4][0])
        lam.append(r[5][0])
    g["gla_w_gate"], g["gla_b_gate"] = jnp.stack(gw), jnp.stack(gb)
    g["lru_w_a"], g["lru_b_a"] = jnp.stack(wa), jnp.stack(ba)
    g["lru_w_x"], g["lru_b_x"], g["lru_lambda"] = jnp.stack(wx), jnp.stack(bx), jnp.stack(lam)
    dxr, dcw, dcb = _conv_bwd(dxc[0], dxc[1], proj, lw["cw"])
    g["conv_w"], g["conv_b"] = dcw, dcb[0]
    dx, dproj, h, dgpre = _mix_in_bwd((dxr, dgr, gla[0], gla[1], gla[2], dg), gla[3], sv["x"], dx1, lw["gpre"],
                                      lw["w_in"])
    g["mix_norm_pre"] = dgpre[0]
    hooks.offer(l, ("w_in",), [_w_in_to_shards(_tn_matmul(h, dproj, "dw_in", rows=1024, acc_bytes=14 * 1024 * 1024))])
    return dx, g


WEIGHT_NAMES = ["mix_norm_pre", "mix_norm_post", "w_in", "conv_w", "conv_b", "lru_w_a", "lru_b_a", "lru_w_x", "lru_b_x",
                "lru_lambda", "rnn_out_norm", "gla_w_gate", "gla_b_gate", "gla_out_norm", "w_out", "ffn_norm_pre",
                "ffn_norm_post", "w_ffn_gate", "w_ffn_up", "w_ffn_down"]
BIG_WEIGHTS = ["w_in", "w_out", "w_ffn_gate", "w_ffn_up", "w_ffn_down"]


def _local_step(x, target, full, hooks):
    saved, lws = [], []
    for l in range(DEPTH):
        lws.append(_layer_weights(full, hooks.w_in_shards(l), l))
        x, sv = _layer_fwd(x, lws[l], l, hooks)
        saved.append(sv)
    loss, dx = _loss_fwd_bwd(x, target)
    grads = [None] * DEPTH
    for l in reversed(range(DEPTH)):
        dx, grads[l] = _layer_bwd(dx, saved[l], lws[l], l, hooks)
    g = {n: jnp.stack([grads[l][n] for l in range(DEPTH)]) for n in WEIGHT_NAMES if n not in BIG_WEIGHTS}
    return loss[0, 0], dx, g


def _all_gather(x, name):
    def body(x_ref, out_ref, send_sems, recv_sems, local_sem):
        mx, my, mc = lax.axis_index("x"), lax.axis_index("y"), lax.axis_index("c")
        me, sibling = (mx, my, mc), (mx, my, 1 - mc)
        chips = [(1 - mx, my), (mx, 1 - my), (1 - mx, 1 - my)]

        def slot(px, py, pc):
            return out_ref.at[4 * px + 2 * py + pc]

        def copy(k, block, to, src=None):
            return pltpu.make_async_remote_copy(
                src_ref=slot(*block) if src is None else src, dst_ref=slot(*block),
                send_sem=send_sems.at[k], recv_sem=recv_sems.at[k], device_id=to, device_id_type=MESH_ID)

        mine = pltpu.make_async_copy(x_ref, slot(*me), local_sem)
        mine.start()
        first = [copy(0, me, sibling, src=x_ref)]
        first += [copy(1 + j, me, (*chip, mc), src=x_ref) for j, chip in enumerate(chips)]
        for cp in first:
            cp.start()
        passed = [copy(4 + j, (*chip, mc), sibling) for j, chip in enumerate(chips)]
        for j, chip in enumerate(chips):
            copy(1 + j, (*chip, mc), me).wait_recv()
            passed[j].start()
        copy(0, sibling, me).wait_recv()
        for j, chip in enumerate(chips):
            copy(4 + j, (*chip, 1 - mc), me).wait_recv()
        for cp in first + passed:
            cp.wait_send()
        mine.wait()

    return pl.pallas_call(
        body, out_shape=_sds((N_DEV,) + x.shape, x.dtype), in_specs=[ANY], out_specs=ANY,
        scratch_shapes=[pltpu.SemaphoreType.DMA((7,)), pltpu.SemaphoreType.DMA((7,)), pltpu.SemaphoreType.DMA],
        name=name)(x)


def _all_to_all(g, name):
    def body(g_ref, out_ref, send_sems, recv_sems, local_sem):
        mx, my, mc = lax.axis_index("x"), lax.axis_index("y"), lax.axis_index("c")
        me = 4 * mx + 2 * my + mc
        mine = pltpu.make_async_copy(g_ref.at[me], out_ref.at[me], local_sem)
        mine.start()
        copies = []
        for r in range(1, N_DEV):
            px = 1 - mx if r & 4 else mx
            py = 1 - my if r & 2 else my
            pc = 1 - mc if r & 1 else mc
            cp = pltpu.make_async_remote_copy(
                src_ref=g_ref.at[4 * px + 2 * py + pc], dst_ref=out_ref.at[me],
                send_sem=send_sems.at[r - 1], recv_sem=recv_sems.at[r - 1],
                device_id=(px, py, pc), device_id_type=MESH_ID)
            cp.start()
            copies.append(cp)
        for cp in copies:
            cp.wait()
        mine.wait()

    return pl.pallas_call(
        body, out_shape=_sds(g.shape, g.dtype), in_specs=[ANY], out_specs=ANY,
        scratch_shapes=[pltpu.SemaphoreType.DMA((7,)), pltpu.SemaphoreType.DMA((7,)), pltpu.SemaphoreType.DMA],
        name=name)(g)


def _sum_slots(parts, name):
    n, r, c = parts.shape
    tr = _tile(r, ADAM_TILE_ROWS)

    def body(p_ref, g_ref):
        g = p_ref[0]
        for k in range(1, n):
            g = g + p_ref[k]
        g_ref[...] = g

    return pl.pallas_call(
        body, grid=(r // tr,), in_specs=[pl.BlockSpec((n, tr, c), lambda i: (0, i, 0))],
        out_specs=pl.BlockSpec((tr, c), lambda i: (i, 0)), out_shape=_sds((r, c)),
        name=name, compiler_params=_cparams(("parallel",)))(parts)


def _sum_adamw(parts, w, m, v, name):
    n_slots, r, c = parts.shape
    tr = _tile(r, ADAM_TILE_ROWS)

    def body(p_ref, w_ref, m_ref, v_ref, g_ref, d_ref, m2_ref, v2_ref):
        g = p_ref[0]
        for k in range(1, n_slots):
            g = g + p_ref[k]
        g_ref[...] = g
        m2 = ADAM_B1 * m_ref[...] + (1.0 - ADAM_B1) * g
        v2 = ADAM_B2 * v_ref[...] + (1.0 - ADAM_B2) * (g * g)
        m2_ref[...] = m2
        v2_ref[...] = v2
        m_hat = m2 / (1.0 - ADAM_B1 ** ADAM_STEP)
        v_hat = v2 / (1.0 - ADAM_B2 ** ADAM_STEP)
        d_ref[...] = -ADAM_LR * (m_hat / (jnp.sqrt(v_hat) + ADAM_EPS) + ADAM_WD * w_ref[...])

    flat = pl.BlockSpec((tr, c), lambda i: (i, 0))
    return pl.pallas_call(
        body, grid=(r // tr,),
        in_specs=[pl.BlockSpec((n_slots, tr, c), lambda i: (0, i, 0)), flat, flat, flat],
        out_specs=[flat] * 4, out_shape=[_sds((r, c))] * 4,
        name=name, compiler_params=_cparams(("parallel",)))(parts, w, m, v)


def _gather_big_weights(shards):
    n = len(shards)

    def body(*refs):
        srcs, outs = refs[:n], refs[n:2 * n]
        send_sems, recv_sems, local_sems = refs[2 * n:]
        mx, my, mc = lax.axis_index("x"), lax.axis_index("y"), lax.axis_index("c")
        me, sibling = (mx, my, mc), (mx, my, 1 - mc)
        chips = [(1 - mx, my), (mx, 1 - my), (1 - mx, 1 - my)]

        def slot(a, px, py, pc):
            return outs[a].at[4 * px + 2 * py + pc]

        def copy(k, a, block, to, own=False):
            return pltpu.make_async_remote_copy(
                src_ref=srcs[a] if own else slot(a, *block), dst_ref=slot(a, *block),
                send_sem=send_sems.at[k * n + a], recv_sem=recv_sems.at[k * n + a],
                device_id=to, device_id_type=MESH_ID)

        mine = [pltpu.make_async_copy(srcs[a], slot(a, *me), local_sems.at[a]) for a in range(n)]
        for cp in mine:
            cp.start()
        first = [copy(0, a, me, sibling, own=True) for a in range(n)]
        first += [copy(1 + j, a, me, (*chip, mc), own=True) for j, chip in enumerate(chips) for a in range(n)]
        for cp in first:
            cp.start()
        passed = [[copy(4 + j, a, (*chip, mc), sibling) for a in range(n)] for j, chip in enumerate(chips)]
        for j, chip in enumerate(chips):
            for a in range(n):
                copy(1 + j, a, (*chip, mc), me).wait_recv()
                passed[j][a].start()
        for a in range(n):
            copy(0, a, sibling, me).wait_recv()
        for j, chip in enumerate(chips):
            for a in range(n):
                copy(4 + j, a, (*chip, 1 - mc), me).wait_recv()
        for cp in first + [cp for row in passed for cp in row]:
            cp.wait_send()
        for cp in mine:
            cp.wait()

    return pl.pallas_call(
        body, out_shape=[_sds((N_DEV,) + s.shape, s.dtype) for s in shards],
        in_specs=[ANY] * n, out_specs=[ANY] * n,
        scratch_shapes=[pltpu.SemaphoreType.DMA((7 * n,)), pltpu.SemaphoreType.DMA((7 * n,)),
                        pltpu.SemaphoreType.DMA((n,))],
        name="gather_matmul_weights")(*shards)


def _run_alone(rider, name):
    r_in = len(rider.operands)
    r_out = len(rider.out_shape)

    def body(*refs):
        local, remote = rider.copies(refs[:r_in], refs[r_in:r_in + r_out], *refs[r_in + r_out:])
        for cp in local + remote:
            cp.start()
        for cp in remote:
            cp.wait()
        for cp in local:
            cp.wait()

    n_remote = (N_DEV - 1) * rider.n
    return pl.pallas_call(
        body, out_shape=rider.out_shape, in_specs=[ANY] * r_in, out_specs=[ANY] * r_out,
        input_output_aliases=dict(rider.aliases),
        scratch_shapes=[pltpu.SemaphoreType.DMA((n_remote,)), pltpu.SemaphoreType.DMA((n_remote,)),
                        pltpu.SemaphoreType.DMA((rider.n,))],
        name=name)(*rider.operands)


def _sum_adamw_big(parts, w, m, v, name):
    _, nl, a, b = parts.shape
    ta = _tile(a, 256)

    def body(p_ref, w_ref, m_ref, v_ref, g_ref, d_ref, m2_ref, v2_ref):
        g = p_ref[0].astype(F32)
        for k in range(1, N_DEV):
            g = g + p_ref[k].astype(F32)
        g_ref[...] = g
        m2 = ADAM_B1 * m_ref[...] + (1.0 - ADAM_B1) * g
        v2 = ADAM_B2 * v_ref[...] + (1.0 - ADAM_B2) * (g * g)
        m2_ref[...] = m2
        v2_ref[...] = v2
        m_hat = m2 / (1.0 - ADAM_B1 ** ADAM_STEP)
        v_hat = v2 / (1.0 - ADAM_B2 ** ADAM_STEP)
        d_ref[...] = -ADAM_LR * (m_hat / (jnp.sqrt(v_hat) + ADAM_EPS) + ADAM_WD * w_ref[...])

    blk = pl.BlockSpec((None, ta, b), lambda l, i: (l, i, 0))
    return pl.pallas_call(
        body, grid=(nl, a // ta),
        in_specs=[pl.BlockSpec((N_DEV, None, ta, b), lambda l, i: (0, l, i, 0)), blk, blk, blk],
        out_specs=[blk] * 4, out_shape=[_sds((nl, a, b))] * 4,
        name=name, compiler_params=_cparams(("parallel", "parallel")))(parts, w, m, v)


SMALL_SHARDED = [("conv_w", 2), ("lru_b_a", 2), ("lru_b_x", 2), ("lru_lambda", 2), ("gla_w_gate", 3), ("gla_b_gate", 2)]
REPLICATED = ["mix_norm_pre", "mix_norm_post", "conv_b", "lru_w_a", "lru_w_x", "rnn_out_norm", "gla_out_norm",
              "ffn_norm_pre", "ffn_norm_post"]


def _pack(arrays, cols, row_mult):
    flat = jnp.concatenate([a.reshape(-1) for a in arrays])
    unit = cols * row_mult
    total = -(-flat.shape[0] // unit) * unit
    return jnp.pad(flat, (0, total - flat.shape[0])).reshape(total // cols, cols)


def _pack_slots(arrays, cols, row_mult):
    flat = jnp.concatenate([a.reshape(N_DEV, -1) for a in arrays], axis=1)
    unit = cols * row_mult
    total = -(-flat.shape[1] // unit) * unit
    return jnp.pad(flat, ((0, 0), (0, total - flat.shape[1]))).reshape(N_DEV, total // cols, cols)


def _unpack(flat, shapes):
    flat = flat.reshape(-1)
    out, off = [], 0
    for sh in shapes:
        n = 1
        for d in sh:
            n *= d
        out.append(flat[off:off + n].reshape(sh))
        off += n
    return out


def _unpack_slots(flat, shapes):
    flat = flat.reshape(N_DEV, -1)
    out, off = [], 0
    for sh in shapes:
        n = 1
        for d in sh:
            n *= d
        out.append(flat[:, off:off + n].reshape((N_DEV,) + tuple(sh)))
        off += n
    return out


def _merge_shards(a, axis):
    a = jnp.moveaxis(a, 0, axis)
    sh = a.shape
    return a.reshape(sh[:axis] + (sh[axis] * sh[axis + 1],) + sh[axis + 2:])


def _split_shards(a, axis):
    sh = a.shape
    a = a.reshape(sh[:axis] + (N_DEV, sh[axis] // N_DEV) + sh[axis + 1:])
    return jnp.moveaxis(a, axis, 0)


class _StepHooks:
    EARLY = {0: ("w_out", "w_ffn_down"), 1: ("w_ffn_gate", "w_ffn_up")}

    def __init__(self, shards, recvs):
        self.shards = [dict(zip(BIG_WEIGHTS, s)) for s in shards]
        self.recvs = recvs
        self.gathered = {0: {"w_in": _gather_big_weights([self.shards[0]["w_in"]])[0]}}
        self.pending = []

    def w_in_shards(self, l):
        return self.gathered[l]["w_in"]

    def early_rider(self, l, k):
        if l == 0 and k in self.EARLY:
            return _WeightGather([self.shards[0][n] for n in self.EARLY[k]])
        return None

    def early_ridden(self, l, k, outs):
        if outs:
            self.gathered[l].update(zip(self.EARLY[k], outs))

    def late_weights(self, l):
        return self.gathered.pop(l)

    def fwd_rider(self, l):
        return _WeightGather([self.shards[l + 1][n] for n in BIG_WEIGHTS]) if l + 1 < DEPTH else None

    def fwd_ridden(self, l, outs):
        if outs:
            self.gathered[l + 1] = dict(zip(BIG_WEIGHTS, outs))

    def offer(self, l, names, arrays):
        self.pending += [(l, n, a) for n, a in zip(names, arrays)]

    def pending_rider(self):
        if not self.pending:
            return None
        layers, names, arrays = zip(*self.pending)
        self.pending = []
        return _GradExchange(names, arrays, [self.recvs[n] for n in names], layers)

    def ridden(self, rider, outs):
        if rider is not None:
            self.recvs.update(zip(rider.names, outs))

    def finish(self):
        rider = self.pending_rider()
        self.ridden(rider, _run_alone(rider, "exchange_last_grads"))
        return self.recvs


def kernel(x, mix_norm_pre, mix_norm_post, w_in, conv_w, conv_b, lru_w_a, lru_b_a, lru_w_x, lru_b_x, lru_lambda, rnn_out_norm, gla_w_gate, gla_b_gate, gla_out_norm, w_out, ffn_norm_pre, ffn_norm_post, w_ffn_gate, w_ffn_up, w_ffn_down, loss_target, m_mix_norm_pre, m_mix_norm_post, m_w_in, m_conv_w, m_conv_b, m_lru_w_a, m_lru_b_a, m_lru_w_x, m_lru_b_x, m_lru_lambda, m_rnn_out_norm, m_gla_w_gate, m_gla_b_gate, m_gla_out_norm, m_w_out, m_ffn_norm_pre, m_ffn_norm_post, m_w_ffn_gate, m_w_ffn_up, m_w_ffn_down, v_mix_norm_pre, v_mix_norm_post, v_w_in, v_conv_w, v_conv_b, v_lru_w_a, v_lru_b_a, v_lru_w_x, v_lru_b_x, v_lru_lambda, v_rnn_out_norm, v_gla_w_gate, v_gla_b_gate, v_gla_out_norm, v_w_out, v_ffn_norm_pre, v_ffn_norm_post, v_w_ffn_gate, v_w_ffn_up, v_w_ffn_down):
    args = dict(locals())
    w = {n: args[n] for n in WEIGHT_NAMES}
    m = {n: args["m_" + n] for n in WEIGHT_NAMES}
    v = {n: args["v_" + n] for n in WEIGHT_NAMES}
    names_s = [n for n, _ in SMALL_SHARDED]
    axis_s = dict(SMALL_SHARDED)
    shapes_s = [w[n].shape for n in names_s]

    small = _pack([w[n] for n in names_s], LANES, 8)
    small_all = _unpack_slots(_all_gather(small, "gather_small_weights"), shapes_s)
    full = {n: w[n] for n in REPLICATED}
    for n, a in zip(names_s, small_all):
        full[n] = _merge_shards(a, axis_s[n])

    hooks = _StepHooks([[w[n][l].astype(MXU_DTYPE) for n in BIG_WEIGHTS] for l in range(DEPTH)],
                       {n: jnp.zeros((N_DEV,) + w[n].shape, MXU_DTYPE) for n in BIG_WEIGHTS})
    loss, dx, g = _local_step(x[0], loss_target[0], full, hooks)
    loss = lax.psum(loss, MESH_AXES)
    res = {}
    for n, parts in hooks.finish().items():
        res[n] = _sum_adamw_big(parts, w[n], m[n], v[n], "adamw_" + n)

    g_slots = _pack_slots([_split_shards(g[n], axis_s[n]) for n in names_s], LANES, 8)
    g_recv = _all_to_all(g_slots, "exchange_small_grads")
    packed = [_pack([t[n] for n in names_s], LANES, 8) for t in (w, m, v)]
    res_s = [_unpack(r, shapes_s) for r in _sum_adamw(g_recv, *packed, "adamw_small")]
    for i, n in enumerate(names_s):
        res[n] = [res_s[k][i] for k in range(4)]

    shapes_r = [w[n].shape for n in REPLICATED]
    g_rep = _pack([g[n] for n in REPLICATED], ADAM_COLS, ADAM_TILE_ROWS)
    rows_r = g_rep.shape[0]
    g_rep = _all_to_all(g_rep.reshape(N_DEV, rows_r // N_DEV, ADAM_COLS), "exchange_replicated_grads")
    g_rep = _all_gather(_sum_slots(g_rep, "sum_replicated_grads"), "gather_replicated_grads")
    packed = [_pack([t[n] for n in REPLICATED], ADAM_COLS, ADAM_TILE_ROWS) for t in (w, m, v)]
    res_r = [_unpack(r, shapes_r)
             for r in _sum_adamw(g_rep.reshape(1, rows_r, ADAM_COLS), *packed, "adamw_replicated")]
    for i, n in enumerate(REPLICATED):
        res[n] = [res_r[k][i] for k in range(4)]

    outs = [[res[n][k] for n in WEIGHT_NAMES] for k in range(4)]
    return (loss, dx[None], *outs[0], *outs[1], *outs[2], *outs[3])
```
